```python
import jax, jax.numpy as jnp
from jax import lax
import numpy as np

D_MODEL = 1024
BATCH = 4
SEQ = 4096
DEPTH = 1

NSA_HEADS = 8
NSA_KV_GROUPS = 2
NSA_HEADS_PER_GROUP = NSA_HEADS // NSA_KV_GROUPS
NSA_HEAD_DIM = 64
CMP_BLOCK = 32
CMP_STRIDE = 16
CMP_RATIO = CMP_BLOCK // CMP_STRIDE
CMP_HIDDEN = 256
SLC_BLOCK = 64
SLC_TOPK = 16
WINDOW = 512
Q_BLOCK = 128
GLA_HEADS = 4
GLA_KEY_DIM = 64
GLA_VAL_DIM = 128
GLA_RANK = 16
GLA_TAU = 16.0
GLA_CHUNK = 64
MOE_GROUPS = 4
MOE_EXPERTS_PER_GROUP = 8
MOE_N_EXPERTS = MOE_GROUPS * MOE_EXPERTS_PER_GROUP
MOE_TOPK = 2
MOE_D_FF = 512

ROPE_THETA = 10000.0
EPS = 1e-6
NEG_INF = -1e30
FORCED_SCORE = 1e4

NSA_Q_W = NSA_HEADS * NSA_HEAD_DIM
NSA_KV_W = NSA_KV_GROUPS * NSA_HEAD_DIM
GLA_QK_W = GLA_HEADS * GLA_KEY_DIM
GLA_V_W = GLA_HEADS * GLA_VAL_DIM
IN_SPLITS = (NSA_Q_W,) + (NSA_KV_W,) * 6 + (3 * NSA_HEADS, GLA_QK_W, GLA_QK_W, GLA_V_W, GLA_RANK, GLA_V_W, D_MODEL, D_MODEL)
IN_WIDTH = sum(IN_SPLITS)

kernel_name = "hybrid_nsa_gla_hmoe_block"


def rms_norm(x, g):
    xf = x.astype(jnp.float32)
    y = xf * lax.rsqrt(jnp.mean(xf * xf, axis=-1, keepdims=True) + EPS)
    return (y * g.astype(jnp.float32)).astype(x.dtype)


def rope(x, pos):
    half = x.shape[-1] // 2
    inv = 1.0 / (ROPE_THETA ** (jnp.arange(half, dtype=jnp.float32) / half))
    ang = pos.astype(jnp.float32)[..., None] * inv
    cos, sin = jnp.cos(ang), jnp.sin(ang)
    x1 = x[..., :half].astype(jnp.float32)
    x2 = x[..., half:].astype(jnp.float32)
    return jnp.concatenate([x1 * cos - x2 * sin, x2 * cos + x1 * sin], axis=-1).astype(x.dtype)


def _heads(t, n, dh):
    b, s, _ = t.shape
    return t.reshape(b, s, n, dh).transpose(0, 2, 1, 3)


def compress_blocks(t, pos_emb, w1, b1, w2, b2):
    bsz, g, seq, dh = t.shape
    chunks = t.reshape(bsz, g, seq // CMP_STRIDE, CMP_STRIDE, dh)
    n_cmp = seq // CMP_STRIDE - CMP_RATIO + 1
    blocks = jnp.concatenate([chunks[:, :, m:m + n_cmp] for m in range(CMP_RATIO)], axis=3)
    blocks = blocks + pos_emb
    flat = blocks.reshape(bsz, g, n_cmp, CMP_BLOCK * dh)
    return jax.nn.gelu(flat @ w1 + b1) @ w2 + b2


def cmp_to_slc_overlap(n_cmp, n_slc):
    c0 = CMP_STRIDE * jnp.arange(n_cmp)[:, None]
    s0 = SLC_BLOCK * jnp.arange(n_slc)[None, :]
    ov = jnp.clip(jnp.minimum(c0 + CMP_BLOCK, s0 + SLC_BLOCK) - jnp.maximum(c0, s0), 0, None)
    return ov.astype(jnp.float32) / CMP_BLOCK


def nsa_attention(q, k_c, v_c, k_s, v_s, k_w, v_w, gate_logits, positions,
                  cmp_pos_k, cmp_w1_k, cmp_b1_k, cmp_w2_k, cmp_b2_k,
                  cmp_pos_v, cmp_w1_v, cmp_b1_v, cmp_w2_v, cmp_b2_v):
    f32 = jnp.float32
    bsz, seq, _ = q.shape
    G, hpg, dh = NSA_KV_GROUPS, NSA_HEADS_PER_GROUP, NSA_HEAD_DIM
    pos_h = positions[:, None, :]
    q = (rope(_heads(q, NSA_HEADS, dh), pos_h) * dh ** -0.5).reshape(bsz, G, hpg, seq, dh)
    k_cmp = compress_blocks(_heads(k_c, G, dh), cmp_pos_k, cmp_w1_k, cmp_b1_k, cmp_w2_k, cmp_b2_k)
    v_cmp = compress_blocks(_heads(v_c, G, dh), cmp_pos_v, cmp_w1_v, cmp_b1_v, cmp_w2_v, cmp_b2_v)
    n_cmp = k_cmp.shape[2]
    cmp_end = CMP_STRIDE * jnp.arange(n_cmp) + CMP_BLOCK - 1
    k_cmp = rope(k_cmp, jnp.take(positions, cmp_end, axis=1)[:, None, :])
    n_slc = seq // SLC_BLOCK
    top_n = min(SLC_TOPK, n_slc)
    overlap = cmp_to_slc_overlap(n_cmp, n_slc)
    k_blocks = rope(_heads(k_s, G, dh), pos_h).reshape(bsz, G, n_slc, SLC_BLOCK, dh)
    v_blocks = _heads(v_s, G, dh).reshape(bsz, G, n_slc, SLC_BLOCK, dh)
    pad = ((0, 0), (0, 0), (WINDOW, 0), (0, 0))
    k_win = jnp.pad(rope(_heads(k_w, G, dh), pos_h), pad)
    v_win = jnp.pad(_heads(v_w, G, dh), pad)
    gates = jax.nn.sigmoid(gate_logits.astype(f32)).reshape(bsz, seq, NSA_HEADS, 3)
    gates = gates.transpose(0, 2, 1, 3).reshape(bsz, G, hpg, seq, 3)
    b_idx = jnp.arange(bsz)[:, None, None, None]
    g_idx = jnp.arange(G)[None, :, None, None]
    blk_ids = jnp.arange(n_slc)

    def query_block(c):
        q0 = c * Q_BLOCK
        t_idx = q0 + jnp.arange(Q_BLOCK)
        qc = lax.dynamic_slice_in_dim(q, q0, Q_BLOCK, axis=3)
        gc = lax.dynamic_slice_in_dim(gates, q0, Q_BLOCK, axis=3)
        s = jnp.einsum('bghqd,bgnd->bghqn', qc, k_cmp).astype(f32)
        cmp_ok = cmp_end[None, :] <= t_idx[:, None]
        p_cmp = jax.nn.softmax(jnp.where(cmp_ok, s, NEG_INF), axis=-1) * cmp_ok
        o_cmp = jnp.einsum('bghqn,bgnd->bghqd', p_cmp.astype(v_cmp.dtype), v_cmp)
        imp = jnp.einsum('bghqn,nj->bgqj', p_cmp, overlap)
        cur = t_idx // SLC_BLOCK
        forced = (blk_ids[None, :] == 0) | (blk_ids[None, :] == cur[:, None]) | (blk_ids[None, :] == cur[:, None] - 1)
        imp = jnp.where(forced, FORCED_SCORE, imp)
        imp = jnp.where(blk_ids[None, :] <= cur[:, None], imp, NEG_INF)
        _, sel = lax.top_k(imp, top_n)
        k_sel = k_blocks[b_idx, g_idx, sel].reshape(bsz, G, Q_BLOCK, top_n * SLC_BLOCK, dh)
        v_sel = v_blocks[b_idx, g_idx, sel].reshape(bsz, G, Q_BLOCK, top_n * SLC_BLOCK, dh)
        key_pos = (sel[..., None] * SLC_BLOCK + jnp.arange(SLC_BLOCK)).reshape(bsz, G, Q_BLOCK, top_n * SLC_BLOCK)
        sel_ok = (key_pos <= t_idx[:, None])[:, :, None]
        s = jnp.einsum('bghqd,bgqkd->bghqk', qc, k_sel).astype(f32)
        p = jax.nn.softmax(jnp.where(sel_ok, s, NEG_INF), axis=-1)
        o_slc = jnp.einsum('bghqk,bgqkd->bghqd', p.astype(v_sel.dtype), v_sel)
        kw = lax.dynamic_slice_in_dim(k_win, q0, Q_BLOCK + WINDOW, axis=2)
        vw = lax.dynamic_slice_in_dim(v_win, q0, Q_BLOCK + WINDOW, axis=2)
        kpos = q0 - WINDOW + jnp.arange(Q_BLOCK + WINDOW)
        win_ok = (kpos[None, :] >= 0) & (kpos[None, :] <= t_idx[:, None]) & (kpos[None, :] > t_idx[:, None] - WINDOW)
        s = jnp.einsum('bghqd,bgkd->bghqk', qc, kw).astype(f32)
        p = jax.nn.softmax(jnp.where(win_ok, s, NEG_INF), axis=-1)
        o_win = jnp.einsum('bghqk,bgkd->bghqd', p.astype(vw.dtype), vw)
        o = gc[..., 0:1] * o_cmp + gc[..., 1:2] * o_slc + gc[..., 2:3] * o_win
        return o.astype(q.dtype)

    out = lax.map(query_block, jnp.arange(seq // Q_BLOCK))
    return out.transpose(1, 0, 4, 2, 3, 5).reshape(bsz, seq, NSA_HEADS * dh)


def gla_attention(q, k, v, a_low, r, w_a2, b_a, norm_g):
    f32 = jnp.float32
    bsz, seq, _ = q.shape
    H, dk, dv, C = GLA_HEADS, GLA_KEY_DIM, GLA_VAL_DIM, GLA_CHUNK
    q = _heads(q, H, dk).astype(f32) * dk ** -0.5
    k = _heads(k, H, dk).astype(f32)
    v = _heads(v, H, dv).astype(f32)
    log_a = _heads(jax.nn.log_sigmoid((a_low @ w_a2 + b_a).astype(f32)) / GLA_TAU, H, dk)
    nc = seq // C

    def to_chunks(t):
        return t.reshape(bsz, H, nc, C, t.shape[-1]).transpose(2, 0, 1, 3, 4)

    causal = jnp.tril(jnp.ones((C, C), dtype=bool))[:, :, None]

    def step(state, xs):
        qc, kc, vc, lac = xs
        b = jnp.cumsum(lac, axis=2)
        o_inter = jnp.einsum('bhtd,bhde->bhte', qc * jnp.exp(b), state)
        diff = b[:, :, :, None, :] - b[:, :, None, :, :]
        decay = jnp.where(causal, jnp.exp(jnp.where(causal, diff, 0.0)), 0.0)
        attn = jnp.einsum('bhtd,bhsd,bhtsd->bhts', qc, kc, decay)
        o_intra = jnp.einsum('bhts,bhse->bhte', attn, vc)
        b_last = b[:, :, -1, :]
        k_dec = kc * jnp.exp(b_last[:, :, None, :] - b)
        state = jnp.exp(b_last)[..., None] * state + jnp.einsum('bhsd,bhse->bhde', k_dec, vc)
        return state, o_inter + o_intra

    state0 = jnp.zeros((bsz, H, dk, dv), f32)
    _, o = lax.scan(step, state0, (to_chunks(q), to_chunks(k), to_chunks(v), to_chunks(log_a)))
    o = o.transpose(1, 2, 0, 3, 4).reshape(bsz, H, seq, dv)
    o = o * lax.rsqrt(jnp.mean(o * o, axis=-1, keepdims=True) + EPS)
    o = o.transpose(0, 2, 1, 3).reshape(bsz, seq, H * dv) * norm_g.astype(f32)
    return (o * jax.nn.silu(r.astype(f32))).astype(r.dtype)


def hybrid_mixer(u, positions, w_in,
                 cmp_pos_k, cmp_w1_k, cmp_b1_k, cmp_w2_k, cmp_b2_k,
                 cmp_pos_v, cmp_w1_v, cmp_b1_v, cmp_w2_v, cmp_b2_v,
                 gla_w_a2, gla_b_a, gla_norm_g, w_proj_nsa, w_proj_gla, w_out):
    z = u @ w_in
    (nsa_q, k_c, v_c, k_s, v_s, k_w, v_w, nsa_g,
     g_q, g_k, g_v, g_a, g_r, m_a, m_b) = jnp.split(z, np.cumsum(IN_SPLITS)[:-1].tolist(), axis=-1)
    y_a = nsa_attention(nsa_q, k_c, v_c, k_s, v_s, k_w, v_w, nsa_g, positions,
                        cmp_pos_k, cmp_w1_k, cmp_b1_k, cmp_w2_k, cmp_b2_k,
                        cmp_pos_v, cmp_w1_v, cmp_b1_v, cmp_w2_v, cmp_b2_v) @ w_proj_nsa
    y_b = gla_attention(g_q, g_k, g_v, g_a, g_r, gla_w_a2, gla_b_a, gla_norm_g) @ w_proj_gla
    mixed = jax.nn.sigmoid(m_a) * y_a + jax.nn.sigmoid(m_b) * y_b
    return mixed @ w_out


def hier_moe(v, w_grp, b_grp, w_exp, b_exp, w_gate, w_up, w_down):
    f32 = jnp.float32
    bsz, seq, d = v.shape
    vt = v.reshape(-1, d)
    grp_prob = jax.nn.softmax((vt @ w_grp + b_grp).astype(f32), axis=-1)
    p_grp, g_sel = lax.top_k(grp_prob, 1)
    exp_logits = (vt @ w_exp + b_exp).astype(f32).reshape(-1, MOE_GROUPS, MOE_EXPERTS_PER_GROUP)
    in_grp = jnp.take_along_axis(exp_logits, g_sel[:, :, None], axis=1)[:, 0]
    p_in, e_sel = lax.top_k(jax.nn.softmax(in_grp, axis=-1), MOE_TOPK)
    w = p_grp * p_in / jnp.sum(p_in, axis=-1, keepdims=True)
    expert_id = g_sel * MOE_EXPERTS_PER_GROUP + e_sel
    combine = jnp.einsum('tk,tke->te', w, jax.nn.one_hot(expert_id, MOE_N_EXPERTS, dtype=f32))
    y = jnp.zeros(vt.shape, f32)
    for e in range(MOE_N_EXPERTS):
        hdn = jax.nn.silu(vt @ w_gate[e]) * (vt @ w_up[e])
        y = y + (combine[:, e:e + 1].astype(hdn.dtype) * hdn) @ w_down[e]
    return y.reshape(bsz, seq, d).astype(v.dtype)


def setup_inputs(seed: int = 0) -> dict:
    key = jax.random.key(seed)
    ks = jax.random.split(key, 32)
    f32 = jnp.float32
    L, D, dh = DEPTH, D_MODEL, NSA_HEAD_DIM

    def nrm(k, shape, scale):
        return jax.random.normal(k, shape, f32) * scale

    start = jax.random.randint(ks[1], (BATCH,), 0, 1024)
    positions = (start[:, None] + jnp.arange(SEQ)[None, :]).astype(jnp.int32)
    return {
        "x": nrm(ks[0], (BATCH, SEQ, D), 1.0),
        "positions": positions,
        "g_mix": 1.0 + nrm(ks[2], (L, D), 0.02),
        "w_in": nrm(ks[3], (L, D, IN_WIDTH), D ** -0.5),
        "cmp_pos_k": nrm(ks[4], (L, CMP_BLOCK, dh), 0.02),
        "cmp_w1_k": nrm(ks[5], (L, CMP_BLOCK * dh, CMP_HIDDEN), (CMP_BLOCK * dh) ** -0.5),
        "cmp_b1_k": nrm(ks[6], (L, CMP_HIDDEN), 0.01),
        "cmp_w2_k": nrm(ks[7], (L, CMP_HIDDEN, dh), CMP_HIDDEN ** -0.5),
        "cmp_b2_k": nrm(ks[8], (L, dh), 0.01),
        "cmp_pos_v": nrm(ks[9], (L, CMP_BLOCK, dh), 0.02),
        "cmp_w1_v": nrm(ks[10], (L, CMP_BLOCK * dh, CMP_HIDDEN), (CMP_BLOCK * dh) ** -0.5),
        "cmp_b1_v": nrm(ks[11], (L, CMP_HIDDEN), 0.01),
        "cmp_w2_v": nrm(ks[12], (L, CMP_HIDDEN, dh), CMP_HIDDEN ** -0.5),
        "cmp_b2_v": nrm(ks[13], (L, dh), 0.01),
        "gla_w_a2": nrm(ks[14], (L, GLA_RANK, GLA_QK_W), GLA_RANK ** -0.5),
        "gla_b_a": nrm(ks[15], (L, GLA_QK_W), 0.1),
        "gla_norm_g": 1.0 + nrm(ks[16], (L, GLA_V_W), 0.02),
        "w_proj_nsa": nrm(ks[17], (L, NSA_Q_W, D), NSA_Q_W ** -0.5),
        "w_proj_gla": nrm(ks[18], (L, GLA_V_W, D), GLA_V_W ** -0.5),
        "w_out": nrm(ks[19], (L, D, D), D ** -0.5),
        "g_ffn": 1.0 + nrm(ks[20], (L, D), 0.02),
        "w_grp": nrm(ks[21], (L, D, MOE_GROUPS), D ** -0.5),
        "b_grp": nrm(ks[22], (L, MOE_GROUPS), 0.01),
        "w_exp": nrm(ks[23], (L, D, MOE_N_EXPERTS), D ** -0.5),
        "b_exp": nrm(ks[24], (L, MOE_N_EXPERTS), 0.01),
        "w_gate": nrm(ks[25], (L, MOE_N_EXPERTS, D, MOE_D_FF), D ** -0.5),
        "w_up": nrm(ks[26], (L, MOE_N_EXPERTS, D, MOE_D_FF), D ** -0.5),
        "w_down": nrm(ks[27], (L, MOE_N_EXPERTS, MOE_D_FF, D), MOE_D_FF ** -0.5),
        "g_final": 1.0 + nrm(ks[28], (D,), 0.02),
    }


def reference(x, positions, g_mix, w_in,
              cmp_pos_k, cmp_w1_k, cmp_b1_k, cmp_w2_k, cmp_b2_k,
              cmp_pos_v, cmp_w1_v, cmp_b1_v, cmp_w2_v, cmp_b2_v,
              gla_w_a2, gla_b_a, gla_norm_g, w_proj_nsa, w_proj_gla, w_out,
              g_ffn, w_grp, b_grp, w_exp, b_exp, w_gate, w_up, w_down, g_final):
    h = x
    for layer in range(DEPTH):
        u = rms_norm(h, g_mix[layer])
        h = h + hybrid_mixer(u, positions, w_in[layer],
                             cmp_pos_k[layer], cmp_w1_k[layer], cmp_b1_k[layer], cmp_w2_k[layer], cmp_b2_k[layer],
                             cmp_pos_v[layer], cmp_w1_v[layer], cmp_b1_v[layer], cmp_w2_v[layer], cmp_b2_v[layer],
                             gla_w_a2[layer], gla_b_a[layer], gla_norm_g[layer],
                             w_proj_nsa[layer], w_proj_gla[layer], w_out[layer])
        h = h + hier_moe(rms_norm(h, g_ffn[layer]), w_grp[layer], b_grp[layer], w_exp[layer], b_exp[layer],
                         w_gate[layer], w_up[layer], w_down[layer])
    return rms_norm(h, g_final)
```

```python
import functools

import numpy as np
import jax
import jax.numpy as jnp
from jax import lax
from jax.experimental import pallas as pl
from jax.experimental.pallas import tpu as pltpu

F32 = jnp.float32
BF16 = jnp.bfloat16

NSA_HEADS = 8
NSA_KV_GROUPS = 2
NSA_HPG = NSA_HEADS // NSA_KV_GROUPS
NSA_DH = 64
CMP_BLOCK = 32
CMP_STRIDE = 16
CMP_HIDDEN = 256
SLC_BLOCK = 64
SLC_TOPK = 16
WINDOW = 512
GLA_HEADS = 4
GLA_DK = 64
GLA_DV = 128
GLA_RANK = 16
GLA_TAU = 16.0
MOE_GROUPS = 4
MOE_EPG = 8
MOE_EXPERTS = MOE_GROUPS * MOE_EPG
MOE_DFF = 512
ROPE_THETA = 10000.0
EPS = 1e-6
NEG_INF = -1e30
FORCED_SCORE = 1e4

LANES = 128
VMEM_LIMIT = 56 * 1024 * 1024

IN_TM = 512
ATT_TQ = 128
GLA_C = 128
GLA_SUB = 16
MERGE_TM = 512
MOE_TM = 1024


def _dot(a, b):
    return jnp.dot(a, b, preferred_element_type=F32)


def _dot_nt(a, b):
    return lax.dot_general(a, b, (((1,), (1,)), ((), ())), preferred_element_type=F32)


def _split3(x):
    x1 = x.astype(BF16)
    r1 = x - x1.astype(F32)
    x2 = r1.astype(BF16)
    r2 = r1 - x2.astype(F32)
    x3 = r2.astype(BF16)
    return x1, x2, x3


def _dot_exact_lhs(a_bf16, x):
    x1, x2, x3 = _split3(x)
    return _dot(a_bf16, x1) + _dot(a_bf16, x2) + _dot(a_bf16, x3)


def _dot_f32(a, b):
    a1, a2, a3 = _split3(a)
    b1, b2, b3 = _split3(b)
    return (_dot(a1, b1) + (_dot(a1, b2) + _dot(a2, b1))
            + (_dot(a1, b3) + _dot(a2, b2) + _dot(a3, b1)))


def _rope_lanes(z, cs, sn):
    w = z.shape[-1]
    lane = lax.broadcasted_iota(jnp.int32, z.shape, 1)
    first_half = (lane % NSA_DH) < (NSA_DH // 2)
    rot = jnp.where(first_half, pltpu.roll(z, w - NSA_DH // 2, 1), pltpu.roll(z, NSA_DH // 2, 1))
    reps = w // LANES
    if reps > 1:
        cs = jnp.concatenate([cs] * reps, axis=1)
        sn = jnp.concatenate([sn] * reps, axis=1)
    return z * cs + rot * sn


_SEC = {}
_off = 0
for _name, _w in (("q", 512), ("kvc", 256), ("ks", 128), ("vs", 128), ("kw", 128), ("vw", 128),
                  ("gq", 256), ("gk", 256), ("gv", 512), ("gr", 512), ("ma", 1024), ("mb", 1024),
                  ("ng", 128), ("ga", 128)):
    _SEC[_name] = (_off, _off + _w)
    _off += _w
IN_NW = _off


def _in_proj_body(x_ref, g_ref, w_ref, cs_ref, sn_ref,
                  q_ref, kvc_ref, ks_ref, vs_ref, kw_ref, vw_ref,
                  gq_ref, gk_ref, gv_ref, gvt_ref, gr_ref, ma_ref, mb_ref, ng_ref, ga_ref):
    x = x_ref[0]
    var = jnp.mean(x * x, axis=-1, keepdims=True)
    u = (x * lax.rsqrt(var + EPS) * g_ref[...]).astype(BF16)
    cs = cs_ref[0]
    sn = sn_ref[0]

    def proj(name):
        a, b = _SEC[name]
        return _dot(u, w_ref[:, a:b])

    zq = _rope_lanes(proj("q"), cs, sn) * (NSA_DH ** -0.5)
    for h in range(NSA_HEADS):
        q_ref[0, h] = zq[:, h * NSA_DH:(h + 1) * NSA_DH].astype(BF16)
    kvc_ref[0] = proj("kvc")
    zks = _rope_lanes(proj("ks"), cs, sn)
    zkw = _rope_lanes(proj("kw"), cs, sn)
    zvs = proj("vs")
    zvw = proj("vw")
    for g in range(NSA_KV_GROUPS):
        sl = slice(g * NSA_DH, (g + 1) * NSA_DH)
        ks_ref[0, g] = zks[:, sl].astype(BF16)
        kw_ref[0, g] = zkw[:, sl].astype(BF16)
        vs_ref[0, g] = zvs[:, sl].astype(BF16)
        vw_ref[0, g] = zvw[:, sl].astype(BF16)
    zgq = proj("gq") * (GLA_DK ** -0.5)
    zgk = proj("gk")
    for h in range(GLA_HEADS):
        sl = slice(h * GLA_DK, (h + 1) * GLA_DK)
        gq_ref[0, h] = zgq[:, sl]
        gk_ref[0, h] = zgk[:, sl]
    zgv = proj("gv")
    gv_ref[0] = zgv.astype(BF16)
    gvt_ref[0] = zgv.T.astype(BF16)
    gr_ref[0] = proj("gr")
    ma_ref[0] = jax.nn.sigmoid(proj("ma"))
    mb_ref[0] = jax.nn.sigmoid(proj("mb"))
    ng_ref[0] = jax.nn.sigmoid(proj("ng"))
    ga_ref[0] = proj("ga")


def _in_proj(x, g_mix, w_in, cs, sn):
    B, S, D = x.shape
    tm = min(IN_TM, S)
    splits = np.cumsum((512,) + (128,) * 6 + (24, 256, 256, 512, 16, 512, 1024, 1024))
    (wq, wkc, wvc, wks, wvs, wkw, wvw, wng, wgq, wgk, wgv, wga, wgr, wma, wmb) = jnp.split(
        w_in, splits[:-1].tolist(), axis=1)
    pad = lambda w: jnp.pad(w, ((0, 0), (0, LANES - w.shape[1])))
    w_all = jnp.concatenate([wq, wkc, wvc, wks, wvs, wkw, wvw, wgq, wgk, wgv, wgr, wma, wmb,
                             pad(wng), pad(wga)], axis=1).astype(BF16)
    assert w_all.shape[1] == IN_NW
    grid = (B, S // tm)
    tok = lambda w: pl.BlockSpec((1, tm, w), lambda b, i: (b, i, 0))
    head = lambda n, w: pl.BlockSpec((1, n, tm, w), lambda b, i: (b, 0, i, 0))
    sds = jax.ShapeDtypeStruct
    out_shape = (
        sds((B, NSA_HEADS, S, NSA_DH), BF16),
        sds((B, S, 256), F32),
        sds((B, NSA_KV_GROUPS, S, NSA_DH), BF16),
        sds((B, NSA_KV_GROUPS, S, NSA_DH), BF16),
        sds((B, NSA_KV_GROUPS, S, NSA_DH), BF16),
        sds((B, NSA_KV_GROUPS, S, NSA_DH), BF16),
        sds((B, GLA_HEADS, S, GLA_DK), F32),
        sds((B, GLA_HEADS, S, GLA_DK), F32),
        sds((B, S, GLA_HEADS * GLA_DV), BF16),
        sds((B, GLA_HEADS * GLA_DV, S), BF16),
        sds((B, S, GLA_HEADS * GLA_DV), F32),
        sds((B, S, D), F32),
        sds((B, S, D), F32),
        sds((B, S, LANES), F32),
        sds((B, S, LANES), F32),
    )
    out_specs = (
        head(NSA_HEADS, NSA_DH), tok(256),
        head(NSA_KV_GROUPS, NSA_DH), head(NSA_KV_GROUPS, NSA_DH),
        head(NSA_KV_GROUPS, NSA_DH), head(NSA_KV_GROUPS, NSA_DH),
        head(GLA_HEADS, GLA_DK), head(GLA_HEADS, GLA_DK),
        tok(GLA_HEADS * GLA_DV),
        pl.BlockSpec((1, GLA_HEADS * GLA_DV, tm), lambda b, i: (b, 0, i)),
        tok(GLA_HEADS * GLA_DV), tok(D), tok(D), tok(LANES), tok(LANES),
    )
    return pl.pallas_call(
        _in_proj_body,
        grid=grid,
        in_specs=[
            tok(D),
            pl.BlockSpec((1, D), lambda b, i: (0, 0)),
            pl.BlockSpec((D, IN_NW), lambda b, i: (0, 0), pipeline_mode=pl.Buffered(1)),
            tok(LANES), tok(LANES),
        ],
        out_specs=out_specs,
        out_shape=out_shape,
        compiler_params=pltpu.CompilerParams(
            dimension_semantics=("parallel", "parallel"), vmem_limit_bytes=VMEM_LIMIT),
        name="in_proj",
    )(x, g_mix.reshape(1, D), w_all, cs, sn)


def _compress_body(x_ref, p0_ref, p1_ref, w0_ref, w1_ref, b1_ref, w2_ref, b2_ref, cs_ref, sn_ref,
                   kc_ref, vc_ref):
    x = x_ref[0]
    n = x.shape[0]
    y0 = _dot((x + p0_ref[...]).astype(BF16), w0_ref[...])
    y1 = _dot((x + p1_ref[...]).astype(BF16), w1_ref[...])
    h = jax.nn.gelu(y0 + pltpu.roll(y1, n - 1, 0) + b1_ref[...])
    o = _dot(h.astype(BF16), w2_ref[...]) + b2_ref[...]
    k = _rope_lanes(o[:, :LANES], cs_ref[0], sn_ref[0])
    v = o[:, LANES:]
    for g in range(NSA_KV_GROUPS):
        sl = slice(g * NSA_DH, (g + 1) * NSA_DH)
        kc_ref[0, g] = k[:, sl].astype(BF16)
        vc_ref[0, g] = v[:, sl].astype(BF16)


def _compress(kvc, cs_c, sn_c, pos_k, w1_k, b1_k, w2_k, b2_k, pos_v, w1_v, b1_v, w2_v, b2_v):
    B, S, _ = kvc.shape
    n = S // CMP_STRIDE
    ns = 2 * NSA_KV_GROUPS
    x = kvc.reshape(B, n, CMP_STRIDE * ns * NSA_DH)
    eye = jnp.eye(ns, dtype=F32)
    w1s = jnp.stack([w1_k, w1_k, w1_v, w1_v])
    poss = jnp.stack([pos_k, pos_k, pos_v, pos_v])
    wbig, pbig = [], []
    for m in range(CMP_BLOCK // CMP_STRIDE):
        wm = w1s.reshape(ns, CMP_BLOCK, NSA_DH, CMP_HIDDEN)[:, m * CMP_STRIDE:(m + 1) * CMP_STRIDE]
        wb = jnp.einsum('ctdj,ce->tcdej', wm, eye).reshape(CMP_STRIDE * ns * NSA_DH, ns * CMP_HIDDEN)
        wbig.append(wb.astype(BF16))
        pm = poss[:, m * CMP_STRIDE:(m + 1) * CMP_STRIDE]
        pbig.append(pm.transpose(1, 0, 2).reshape(1, CMP_STRIDE * ns * NSA_DH))
    b1 = jnp.concatenate([b1_k, b1_k, b1_v, b1_v]).reshape(1, ns * CMP_HIDDEN)
    w2s = jnp.stack([w2_k, w2_k, w2_v, w2_v])
    w2big = jnp.einsum('cjd,ce->cjed', w2s, eye).reshape(ns * CMP_HIDDEN, ns * NSA_DH).astype(BF16)
    b2 = jnp.concatenate([b2_k, b2_k, b2_v, b2_v]).reshape(1, ns * NSA_DH)
    full = lambda a: pl.BlockSpec(a.shape, lambda b: (0,) * a.ndim)
    sds = jax.ShapeDtypeStruct
    return pl.pallas_call(
        _compress_body,
        grid=(B,),
        in_specs=[pl.BlockSpec((1, n, x.shape[2]), lambda b: (b, 0, 0)),
                  full(pbig[0]), full(pbig[1]), full(wbig[0]), full(wbig[1]), full(b1), full(w2big), full(b2),
                  pl.BlockSpec((1, n, LANES), lambda b: (b, 0, 0)),
                  pl.BlockSpec((1, n, LANES), lambda b: (b, 0, 0))],
        out_specs=(pl.BlockSpec((1, NSA_KV_GROUPS, n, NSA_DH), lambda b: (b, 0, 0, 0)),
                   pl.BlockSpec((1, NSA_KV_GROUPS, n, NSA_DH), lambda b: (b, 0, 0, 0))),
        out_shape=(sds((B, NSA_KV_GROUPS, n, NSA_DH), BF16), sds((B, NSA_KV_GROUPS, n, NSA_DH), BF16)),
        compiler_params=pltpu.CompilerParams(
            dimension_semantics=("parallel",), vmem_limit_bytes=VMEM_LIMIT),
        name="compress",
    )(x, pbig[0], pbig[1], wbig[0], wbig[1], b1, w2big, b2, cs_c, sn_c)


def _flash_step(q, k, v, mask, m, l, acc, tq):
    s = _dot_nt(q, k)
    if mask is not None:
        tk = s.shape[1]
        s = jnp.where(mask[None], s.reshape(NSA_HPG, tq, tk), NEG_INF).reshape(NSA_HPG * tq, tk)
    m_new = jnp.maximum(m, jnp.max(s, axis=-1, keepdims=True))
    alpha = jnp.exp(m - m_new)
    p = jnp.exp(s - m_new)
    l = alpha * l + jnp.sum(p, axis=-1, keepdims=True)
    acc = alpha * acc + _dot(p.astype(BF16), v)
    return m_new, l, acc


def _nsa_body(q_ref, kc_ref, vc_ref, ks_ref, vs_ref, kw_ref, vw_ref, ng_ref, ovt_ref, exp_ref,
              o_ref, *, tq, n_slc):
    qi = pl.program_id(2)
    q0 = qi * tq
    M = NSA_HPG * tq
    q = q_ref[0].reshape(M, NSA_DH)
    n_cmp = kc_ref.shape[2]

    s = _dot_nt(q, kc_ref[0, 0])
    t_row = q0 + lax.broadcasted_iota(jnp.int32, (NSA_HPG, tq, n_cmp), 1).reshape(M, n_cmp)
    cmp_end = CMP_STRIDE * lax.broadcasted_iota(jnp.int32, (M, n_cmp), 1) + (CMP_BLOCK - 1)
    ok = cmp_end <= t_row
    sm = jnp.where(ok, s, NEG_INF)
    e = jnp.exp(sm - jnp.max(sm, axis=-1, keepdims=True))
    p_cmp = jnp.where(ok, e / jnp.sum(e, axis=-1, keepdims=True), 0.0)
    o_cmp = _dot(p_cmp.astype(BF16), vc_ref[0, 0])

    p_sum = p_cmp[0:tq]
    for h in range(1, NSA_HPG):
        p_sum = p_sum + p_cmp[h * tq:(h + 1) * tq]
    p1, p2, p3 = _split3(p_sum)
    ovt = ovt_ref[...]
    imp = _dot_nt(ovt, p1) + _dot_nt(ovt, p2) + _dot_nt(ovt, p3)
    blk = lax.broadcasted_iota(jnp.int32, (n_slc, tq), 0)
    cur = (q0 + lax.broadcasted_iota(jnp.int32, (n_slc, tq), 1)) // SLC_BLOCK
    forced = (blk == 0) | (blk == cur) | (blk == cur - 1)
    imp = jnp.where(forced, FORCED_SCORE, imp)
    imp = jnp.where(blk <= cur, imp, NEG_INF)
    rank = jnp.zeros((n_slc, tq), F32)
    for i in range(n_slc):
        row = imp[i:i + 1, :]
        rank = rank + jnp.where(row > imp, 1.0, 0.0)
        rank = rank + jnp.where(jnp.logical_and(row == imp, blk > i), 1.0, 0.0)
    sel_t = jnp.where(rank < float(min(SLC_TOPK, n_slc)), 1.0, 0.0)
    if n_slc < LANES:
        sel_t = jnp.concatenate([sel_t, jnp.zeros((LANES - n_slc, tq), F32)], axis=0)
    sel = sel_t.T.astype(BF16)

    row_t = q0 + lax.broadcasted_iota(jnp.int32, (tq, tq), 0)
    col_l = lax.broadcasted_iota(jnp.int32, (tq, tq), 1)

    init = (jnp.full((M, 1), NEG_INF, F32), jnp.zeros((M, 1), F32), jnp.zeros((M, NSA_DH), F32))

    def slc_step(jt, carry, causal):
        k0 = pl.multiple_of(jt * tq, tq)
        chosen = _dot(sel, exp_ref[:, pl.ds(k0, tq)]) > 0.5
        if causal:
            chosen = jnp.logical_and(chosen, (k0 + col_l) <= row_t)
        return _flash_step(q, ks_ref[0, 0, pl.ds(k0, tq), :], vs_ref[0, 0, pl.ds(k0, tq), :],
                           chosen, *carry, tq)

    carry = lax.fori_loop(0, qi, lambda jt, c: slc_step(jt, c, False), init)
    _, l_s, acc_s = slc_step(qi, carry, True)
    o_slc = acc_s / l_s

    n_back = WINDOW // tq

    def win_step(jt, carry):
        k0 = pl.multiple_of(jt * tq, tq)
        kpos = k0 + col_l
        okw = jnp.logical_and(kpos <= row_t, kpos > row_t - WINDOW)
        return _flash_step(q, kw_ref[0, 0, pl.ds(k0, tq), :], vw_ref[0, 0, pl.ds(k0, tq), :],
                           okw, *carry, tq)

    _, l_w, acc_w = lax.fori_loop(jnp.maximum(qi - n_back, 0), qi + 1, win_step, init)
    o_win = acc_w / l_w

    g = ng_ref[0]
    gi = pl.program_id(1)
    lane = lax.broadcasted_iota(jnp.int32, g.shape, 1)
    outs = []
    for h in range(NSA_HPG):
        sl = slice(h * tq, (h + 1) * tq)
        base = (gi * NSA_HPG + h) * 3
        gate = lambda j: jnp.sum(jnp.where(lane == base + j, g, 0.0), axis=-1, keepdims=True)
        outs.append(gate(0) * o_cmp[sl] + gate(1) * o_slc[sl] + gate(2) * o_win[sl])
    o_ref[0] = jnp.concatenate(outs, axis=1).astype(o_ref.dtype)


def _nsa_attention(q, kc, vc, ks, vs, kw, vw, ng):
    B, H, S, dh = q.shape
    tq = min(ATT_TQ, S)
    n_cmp = kc.shape[2]
    n_slc = S // SLC_BLOCK
    c0 = CMP_STRIDE * np.arange(n_cmp)[None, :]
    s0 = SLC_BLOCK * np.arange(n_slc)[:, None]
    ov = np.clip(np.minimum(c0 + CMP_BLOCK, s0 + SLC_BLOCK) - np.maximum(c0, s0), 0, None) / CMP_BLOCK
    ovt = jnp.asarray(ov, BF16)
    expand = jnp.asarray(np.arange(max(n_slc, LANES))[:, None] == (np.arange(S) // SLC_BLOCK)[None, :], BF16)
    grid = (B, NSA_KV_GROUPS, S // tq)
    kv_spec = lambda n: pl.BlockSpec((1, 1, n, dh), lambda b, g, i: (b, g, 0, 0))
    return pl.pallas_call(
        functools.partial(_nsa_body, tq=tq, n_slc=n_slc),
        grid=grid,
        in_specs=[
            pl.BlockSpec((1, NSA_HPG, tq, dh), lambda b, g, i: (b, g, i, 0)),
            kv_spec(n_cmp), kv_spec(n_cmp), kv_spec(S), kv_spec(S), kv_spec(S), kv_spec(S),
            pl.BlockSpec((1, tq, LANES), lambda b, g, i: (b, i, 0)),
            pl.BlockSpec(ovt.shape, lambda b, g, i: (0, 0)),
            pl.BlockSpec(expand.shape, lambda b, g, i: (0, 0)),
        ],
        out_specs=pl.BlockSpec((1, tq, NSA_HPG * dh), lambda b, g, i: (b, i, g)),
        out_shape=jax.ShapeDtypeStruct((B, S, H * dh), BF16),
        compiler_params=pltpu.CompilerParams(
            dimension_semantics=("parallel", "parallel", "arbitrary"), vmem_limit_bytes=VMEM_LIMIT),
        name="nsa_attn",
    )(q, kc, vc, ks, vs, kw, vw, ng, ovt, expand)


def _gla_body(q_ref, k_ref, v_ref, vt_ref, ga_ref, r_ref, wa_ref, ba_ref, ng_ref, tri_ref,
              o_ref, state_ref, *, n_chunks):
    C, SUB = GLA_C, GLA_SUB
    state_ref[...] = jnp.zeros_like(state_ref)
    wa = wa_ref[0]
    ba = ba_ref[0]
    tri = tri_ref[...]
    t_loc = lax.broadcasted_iota(jnp.int32, (C, 1), 0)

    def chunk(c, _):
        c0 = pl.multiple_of(c * C, C)
        q = q_ref[0, 0, pl.ds(c0, C), :]
        k = k_ref[0, 0, pl.ds(c0, C), :]
        v = v_ref[0, pl.ds(c0, C), :]
        vt = vt_ref[0, :, pl.ds(c0, C)]
        la = jax.nn.log_sigmoid(_dot_f32(ga_ref[0, pl.ds(c0, C), :], wa) + ba) / GLA_TAU
        b = _dot_exact_lhs(tri, la)
        b_last = b[C - 1:C, :]
        st = state_ref[...]
        o = _dot_nt((q * jnp.exp(b)).astype(BF16), st.astype(BF16))
        o_sub = [o[i * SUB:(i + 1) * SUB] for i in range(C // SUB)]
        for i in range(1, C // SUB):
            r0 = i * SUB
            b_ref0 = b[r0:r0 + 1, :]
            qt = q[r0:r0 + SUB] * jnp.exp(b[r0:r0 + SUB] - b_ref0)
            kt = k[:r0] * jnp.exp(b_ref0 - b[:r0])
            a = _dot_nt(qt.astype(BF16), kt.astype(BF16))
            lt = lax.broadcasted_iota(jnp.int32, (SUB, r0), 0)
            ls = lax.broadcasted_iota(jnp.int32, (SUB, r0), 1)
            a = jnp.where(lt + (r0 - SUB) >= ls, a, 0.0)
            o_sub[i] = o_sub[i] + _dot(a.astype(BF16), v[:r0])
        o = jnp.concatenate(o_sub, axis=0)
        vf = v.astype(F32)
        for d in range(SUB):
            if d == 0:
                kd, bd, vd = k, b, vf
            else:
                kd, bd, vd = pltpu.roll(k, d, 0), pltpu.roll(b, d, 0), pltpu.roll(vf, d, 0)
            valid = t_loc >= d
            w = jnp.exp(jnp.where(valid, b - bd, 0.0))
            a_d = jnp.sum(jnp.where(valid, q * kd * w, 0.0), axis=-1, keepdims=True)
            o = o + a_d * vd
        k_dec = (k * jnp.exp(b_last - b)).astype(BF16)
        state_ref[...] = st * jnp.exp(b_last) + _dot(vt, k_dec)
        o = o * lax.rsqrt(jnp.mean(o * o, axis=-1, keepdims=True) + EPS) * ng_ref[0]
        o_ref[0, pl.ds(c0, C), :] = (o * jax.nn.silu(r_ref[0, pl.ds(c0, C), :])).astype(o_ref.dtype)
        return 0

    lax.fori_loop(0, n_chunks, chunk, 0)


def _gla(gq, gk, gv, gvt, ga, gr, w_a2, b_a, norm_g):
    B, H, S, dk = gq.shape
    dv = GLA_DV
    wa = jnp.pad(w_a2, ((0, LANES - GLA_RANK), (0, 0))).reshape(LANES, H, dk).transpose(1, 0, 2)
    ba = b_a.reshape(H, 1, dk)
    ng = norm_g.reshape(H, 1, dv)
    tri = jnp.asarray(np.tril(np.ones((GLA_C, GLA_C))), BF16)
    qk_spec = pl.BlockSpec((1, 1, S, dk), lambda b, h: (b, h, 0, 0))
    tok_spec = lambda w: pl.BlockSpec((1, S, w), lambda b, h: (b, 0, h))
    return pl.pallas_call(
        functools.partial(_gla_body, n_chunks=S // GLA_C),
        grid=(B, H),
        in_specs=[qk_spec, qk_spec, tok_spec(dv),
                  pl.BlockSpec((1, dv, S), lambda b, h: (b, h, 0)),
                  pl.BlockSpec((1, S, LANES), lambda b, h: (b, 0, 0)),
                  tok_spec(dv),
                  pl.BlockSpec((1, LANES, dk), lambda b, h: (h, 0, 0)),
                  pl.BlockSpec((1, 1, dk), lambda b, h: (h, 0, 0)),
                  pl.BlockSpec((1, 1, dv), lambda b, h: (h, 0, 0)),
                  pl.BlockSpec(tri.shape, lambda b, h: (0, 0))],
        out_specs=tok_spec(dv),
        out_shape=jax.ShapeDtypeStruct((B, S, H * dv), BF16),
        scratch_shapes=[pltpu.VMEM((dv, dk), F32)],
        compiler_params=pltpu.CompilerParams(
            dimension_semantics=("parallel", "parallel"), vmem_limit_bytes=VMEM_LIMIT),
        name="gla",
    )(gq, gk, gv, gvt, ga, gr, wa, ba, ng, tri)


def _merge_body(x_ref, ya_ref, yb_ref, ma_ref, mb_ref, wpa_ref, wpb_ref, wo_ref, gf_ref, wr_ref, br_ref,
                h_ref, v_ref, comb_ref):
    y_a = _dot(ya_ref[...], wpa_ref[...])
    y_b = _dot(yb_ref[...], wpb_ref[...])
    mixed = ma_ref[...] * y_a + mb_ref[...] * y_b
    h = x_ref[...] + _dot(mixed.astype(BF16), wo_ref[...])
    h_ref[...] = h
    v = h * lax.rsqrt(jnp.mean(h * h, axis=-1, keepdims=True) + EPS) * gf_ref[...]
    v_ref[...] = v.astype(BF16)
    logits = _dot_f32(v, wr_ref[...]) + br_ref[...]
    lane = lax.broadcasted_iota(jnp.int32, logits.shape, 1)
    is_grp = jnp.logical_and(lane >= MOE_EXPERTS, lane < MOE_EXPERTS + MOE_GROUPS)
    lg = jnp.where(is_grp, logits, NEG_INF)
    eg = jnp.where(is_grp, jnp.exp(lg - jnp.max(lg, axis=-1, keepdims=True)), 0.0)
    pg = eg / jnp.sum(eg, axis=-1, keepdims=True)
    p_grp = jnp.max(pg, axis=-1, keepdims=True)
    g_sel = jnp.min(jnp.where(jnp.logical_and(is_grp, pg == p_grp), lane, 2 * LANES),
                    axis=-1, keepdims=True) - MOE_EXPERTS
    in_grp = jnp.logical_and(lane < MOE_EXPERTS, lane // MOE_EPG == g_sel)
    le = jnp.where(in_grp, logits, NEG_INF)
    ee = jnp.where(in_grp, jnp.exp(le - jnp.max(le, axis=-1, keepdims=True)), 0.0)
    pin = ee / jnp.sum(ee, axis=-1, keepdims=True)
    p1 = jnp.max(jnp.where(in_grp, pin, -1.0), axis=-1, keepdims=True)
    i1 = jnp.min(jnp.where(jnp.logical_and(in_grp, pin == p1), lane, 2 * LANES), axis=-1, keepdims=True)
    rest = jnp.logical_and(in_grp, lane != i1)
    p2 = jnp.max(jnp.where(rest, pin, -1.0), axis=-1, keepdims=True)
    i2 = jnp.min(jnp.where(jnp.logical_and(rest, pin == p2), lane, 2 * LANES), axis=-1, keepdims=True)
    tot = p1 + p2
    comb_ref[...] = (jnp.where(lane == i1, p_grp * p1 / tot, 0.0)
                     + jnp.where(lane == i2, p_grp * p2 / tot, 0.0))


def _merge(x2, ya, yb, ma, mb, w_proj_nsa, w_proj_gla, w_out, g_ffn, w_grp, b_grp, w_exp, b_exp):
    T, D = x2.shape
    tm = min(MERGE_TM, T)
    wr = jnp.pad(jnp.concatenate([w_exp, w_grp], axis=1), ((0, 0), (0, LANES - MOE_EXPERTS - MOE_GROUPS)))
    br = jnp.pad(jnp.concatenate([b_exp, b_grp]), (0, LANES - MOE_EXPERTS - MOE_GROUPS)).reshape(1, LANES)
    tok = lambda w: pl.BlockSpec((tm, w), lambda i: (i, 0))
    full = lambda a: pl.BlockSpec(a.shape, lambda i: (0, 0))
    wpa, wpb, wo = w_proj_nsa.astype(BF16), w_proj_gla.astype(BF16), w_out.astype(BF16)
    gf = g_ffn.reshape(1, D)
    sds = jax.ShapeDtypeStruct
    return pl.pallas_call(
        _merge_body,
        grid=(T // tm,),
        in_specs=[tok(D), tok(ya.shape[1]), tok(yb.shape[1]), tok(D), tok(D),
                  full(wpa), full(wpb), full(wo), full(gf), full(wr), full(br)],
        out_specs=(tok(D), tok(D), tok(LANES)),
        out_shape=(sds((T, D), F32), sds((T, D), BF16), sds((T, LANES), F32)),
        compiler_params=pltpu.CompilerParams(
            dimension_semantics=("parallel",), vmem_limit_bytes=VMEM_LIMIT),
        name="merge",
    )(x2, ya, yb, ma, mb, wpa, wpb, wo, gf, wr, br)


def _moe_body(v_ref, comb_ref, h_ref, wg_ref, wu_ref, wd_ref, gfin_ref, o_ref, acc_ref):
    e = pl.program_id(1)

    @pl.when(e == 0)
    def _():
        acc_ref[...] = jnp.zeros_like(acc_ref)

    v = v_ref[...]
    hdn = jax.nn.silu(_dot(v, wg_ref[0])) * _dot(v, wu_ref[0])
    comb = comb_ref[...]
    lane = lax.broadcasted_iota(jnp.int32, comb.shape, 1)
    c = jnp.sum(jnp.where(lane == e, comb, 0.0), axis=-1, keepdims=True)
    acc_ref[...] += _dot((c * hdn).astype(BF16), wd_ref[0])

    @pl.when(e == pl.num_programs(1) - 1)
    def _():
        h = h_ref[...] + acc_ref[...]
        o_ref[...] = h * lax.rsqrt(jnp.mean(h * h, axis=-1, keepdims=True) + EPS) * gfin_ref[...]


def _moe(v, comb, h, w_gate, w_up, w_down, g_final):
    T, D = h.shape
    tm = min(MOE_TM, T)
    E, _, F = w_gate.shape
    wg, wu, wd = w_gate.astype(BF16), w_up.astype(BF16), w_down.astype(BF16)
    tok = lambda w: pl.BlockSpec((tm, w), lambda i, e: (i, 0))
    return pl.pallas_call(
        _moe_body,
        grid=(T // tm, E),
        in_specs=[tok(D), tok(LANES), tok(D),
                  pl.BlockSpec((1, D, F), lambda i, e: (e, 0, 0)),
                  pl.BlockSpec((1, D, F), lambda i, e: (e, 0, 0)),
                  pl.BlockSpec((1, F, D), lambda i, e: (e, 0, 0)),
                  pl.BlockSpec((1, D), lambda i, e: (0, 0))],
        out_specs=tok(D),
        out_shape=jax.ShapeDtypeStruct((T, D), F32),
        scratch_shapes=[pltpu.VMEM((tm, D), F32)],
        compiler_params=pltpu.CompilerParams(
            dimension_semantics=("parallel", "arbitrary"), vmem_limit_bytes=VMEM_LIMIT),
        name="moe",
    )(v, comb, h, wg, wu, wd, g_final.reshape(1, D))


def _rope_tables(positions):
    half = NSA_DH // 2
    inv = 1.0 / (ROPE_THETA ** (jnp.arange(half, dtype=F32) / half))
    ang = positions.astype(F32)[..., None] * inv
    cos, sin = jnp.cos(ang), jnp.sin(ang)
    cs = jnp.concatenate([cos, cos, cos, cos], axis=-1)
    sn = jnp.concatenate([-sin, sin, -sin, sin], axis=-1)
    return cs, sn


def _layer(h, positions, g_mix, w_in, cmp_pos_k, cmp_w1_k, cmp_b1_k, cmp_w2_k, cmp_b2_k,
           cmp_pos_v, cmp_w1_v, cmp_b1_v, cmp_w2_v, cmp_b2_v, gla_w_a2, gla_b_a, gla_norm_g,
           w_proj_nsa, w_proj_gla, w_out, g_ffn, w_grp, b_grp, w_exp, b_exp, w_gate, w_up, w_down, g_out):
    B, S, D = h.shape
    cs, sn = _rope_tables(positions)
    n_chunks = S // CMP_STRIDE
    cmp_end = jnp.minimum(CMP_STRIDE * jnp.arange(n_chunks) + CMP_BLOCK - 1, S - 1)
    cs_c, sn_c = _rope_tables(jnp.take(positions, cmp_end, axis=1))
    (q, kvc, ks, vs, kw, vw, gq, gk, gv, gvt, gr, ma, mb, ng, ga) = _in_proj(h, g_mix, w_in, cs, sn)
    kc, vc = _compress(kvc, cs_c, sn_c, cmp_pos_k, cmp_w1_k, cmp_b1_k, cmp_w2_k, cmp_b2_k,
                       cmp_pos_v, cmp_w1_v, cmp_b1_v, cmp_w2_v, cmp_b2_v)
    ya = _nsa_attention(q, kc, vc, ks, vs, kw, vw, ng)
    yb = _gla(gq, gk, gv, gvt, ga, gr, gla_w_a2, gla_b_a, gla_norm_g)
    T = B * S
    h1, v, comb = _merge(h.reshape(T, D), ya.reshape(T, -1), yb.reshape(T, -1), ma.reshape(T, D), mb.reshape(T, D),
                         w_proj_nsa, w_proj_gla, w_out, g_ffn, w_grp, b_grp, w_exp, b_exp)
    return _moe(v, comb, h1, w_gate, w_up, w_down, g_out).reshape(B, S, D)


def kernel(x, positions, g_mix, w_in, cmp_pos_k, cmp_w1_k, cmp_b1_k, cmp_w2_k, cmp_b2_k, cmp_pos_v, cmp_w1_v,
           cmp_b1_v, cmp_w2_v, cmp_b2_v, gla_w_a2, gla_b_a, gla_norm_g, w_proj_nsa, w_proj_gla, w_out, g_ffn,
           w_grp, b_grp, w_exp, b_exp, w_gate, w_up, w_down, g_final):
    depth = g_mix.shape[0]
    assert depth == 1, "the final norm is fused into the single layer's expert kernel"
    return _layer(x, positions, g_mix[0], w_in[0], cmp_pos_k[0], cmp_w1_k[0], cmp_b1_k[0], cmp_w2_k[0],
                  cmp_b2_k[0], cmp_pos_v[0], cmp_w1_v[0], cmp_b1_v[0], cmp_w2_v[0], cmp_b2_v[0],
                  gla_w_a2[0], gla_b_a[0], gla_norm_g[0], w_proj_nsa[0], w_proj_gla[0], w_out[0],
                  g_ffn[0], w_grp[0], b_grp[0], w_exp[0], b_exp[0], w_gate[0], w_up[0], w_down[0], g_final)
```

```python
import functools

import numpy as np
import jax
import jax.numpy as jnp
from jax import lax
from jax.experimental import pallas as pl
from jax.experimental.pallas import tpu as pltpu

F32 = jnp.float32
BF16 = jnp.bfloat16

NSA_HEADS = 8
NSA_KV_GROUPS = 2
NSA_HPG = NSA_HEADS // NSA_KV_GROUPS
NSA_DH = 64
CMP_BLOCK = 32
CMP_STRIDE = 16
CMP_HIDDEN = 256
SLC_BLOCK = 64
SLC_TOPK = 16
WINDOW = 512
GLA_HEADS = 4
GLA_DK = 64
GLA_DV = 128
GLA_RANK = 16
GLA_TAU = 16.0
MOE_GROUPS = 4
MOE_EPG = 8
MOE_EXPERTS = MOE_GROUPS * MOE_EPG
MOE_DFF = 512
ROPE_THETA = 10000.0
EPS = 1e-6
NEG_INF = -1e30
FORCED_SCORE = 1e4

LANES = 128
VMEM_LIMIT = 56 * 1024 * 1024

IN_TM = 512
ATT_TQ = 128
ATT_TK = 512
GLA_C = 128
GLA_SUB = 16
MERGE_TM = 512
MOE_TM = 1024


def _dot(a, b):
    return jnp.dot(a, b, preferred_element_type=F32)


def _dot_nt(a, b):
    return lax.dot_general(a, b, (((1,), (1,)), ((), ())), preferred_element_type=F32)


def _split3(x):
    x1 = x.astype(BF16)
    r1 = x - x1.astype(F32)
    x2 = r1.astype(BF16)
    r2 = r1 - x2.astype(F32)
    x3 = r2.astype(BF16)
    return x1, x2, x3


def _dot_exact_lhs(a_bf16, x):
    x1, x2, x3 = _split3(x)
    return _dot(a_bf16, x1) + _dot(a_bf16, x2) + _dot(a_bf16, x3)


def _dot_f32(a, b):
    a1, a2, a3 = _split3(a)
    b1, b2, b3 = _split3(b)
    return (_dot(a1, b1) + (_dot(a1, b2) + _dot(a2, b1))
            + (_dot(a1, b3) + _dot(a2, b2) + _dot(a3, b1)))


def _rope_lanes(z, cs, sn):
    w = z.shape[-1]
    lane = lax.broadcasted_iota(jnp.int32, z.shape, 1)
    first_half = (lane % NSA_DH) < (NSA_DH // 2)
    rot = jnp.where(first_half, pltpu.roll(z, w - NSA_DH // 2, 1), pltpu.roll(z, NSA_DH // 2, 1))
    reps = w // LANES
    if reps > 1:
        cs = jnp.concatenate([cs] * reps, axis=1)
        sn = jnp.concatenate([sn] * reps, axis=1)
    return z * cs + rot * sn


_SEC = {}
_off = 0
for _name, _w in (("q", 512), ("kvc", 256), ("ks", 128), ("vs", 128), ("kw", 128), ("vw", 128),
                  ("gq", 256), ("gk", 256), ("gv", 512), ("gr", 512), ("ma", 1024), ("mb", 1024),
                  ("ng", 128), ("ga", 128)):
    _SEC[_name] = (_off, _off + _w)
    _off += _w
IN_NW = _off


def _in_proj_body(x_ref, g_ref, w_ref, cs_ref, sn_ref,
                  q_ref, kvc_ref, ks_ref, vs_ref, kw_ref, vw_ref,
                  gq_ref, gk_ref, gv_ref, gvt_ref, gr_ref, ma_ref, mb_ref, ng_ref, ga_ref):
    x = x_ref[0]
    var = jnp.mean(x * x, axis=-1, keepdims=True)
    u = (x * lax.rsqrt(var + EPS) * g_ref[...]).astype(BF16)
    cs = cs_ref[0]
    sn = sn_ref[0]

    def proj(name):
        a, b = _SEC[name]
        return _dot(u, w_ref[:, a:b])

    zq_t = (_rope_lanes(proj("q"), cs, sn) * (NSA_DH ** -0.5)).T
    for h in range(NSA_HEADS):
        q_ref[0, h] = zq_t[h * NSA_DH:(h + 1) * NSA_DH].astype(BF16)
    kvc_ref[0] = proj("kvc")
    zks = _rope_lanes(proj("ks"), cs, sn)
    zkw = _rope_lanes(proj("kw"), cs, sn)
    zvs_t = proj("vs").T
    zvw_t = proj("vw").T
    for g in range(NSA_KV_GROUPS):
        sl = slice(g * NSA_DH, (g + 1) * NSA_DH)
        ks_ref[0, g] = zks[:, sl].astype(BF16)
        kw_ref[0, g] = zkw[:, sl].astype(BF16)
        vs_ref[0, g] = zvs_t[sl].astype(BF16)
        vw_ref[0, g] = zvw_t[sl].astype(BF16)
    zgq = proj("gq") * (GLA_DK ** -0.5)
    zgk = proj("gk")
    for h in range(GLA_HEADS):
        sl = slice(h * GLA_DK, (h + 1) * GLA_DK)
        gq_ref[0, h] = zgq[:, sl]
        gk_ref[0, h] = zgk[:, sl]
    zgv = proj("gv")
    gv_ref[0] = zgv.astype(BF16)
    gvt_ref[0] = zgv.T.astype(BF16)
    gr_ref[0] = proj("gr")
    ma_ref[0] = jax.nn.sigmoid(proj("ma"))
    mb_ref[0] = jax.nn.sigmoid(proj("mb"))
    ng_ref[0] = jax.nn.sigmoid(proj("ng")).T
    ga_ref[0] = proj("ga")


def _in_proj(x, g_mix, w_in, cs, sn):
    B, S, D = x.shape
    tm = min(IN_TM, S)
    splits = np.cumsum((512,) + (128,) * 6 + (24, 256, 256, 512, 16, 512, 1024, 1024))
    (wq, wkc, wvc, wks, wvs, wkw, wvw, wng, wgq, wgk, wgv, wga, wgr, wma, wmb) = jnp.split(
        w_in, splits[:-1].tolist(), axis=1)
    pad = lambda w: jnp.pad(w, ((0, 0), (0, LANES - w.shape[1])))
    w_all = jnp.concatenate([wq, wkc, wvc, wks, wvs, wkw, wvw, wgq, wgk, wgv, wgr, wma, wmb,
                             pad(wng), pad(wga)], axis=1).astype(BF16)
    assert w_all.shape[1] == IN_NW
    grid = (B, S // tm)
    tok = lambda w: pl.BlockSpec((1, tm, w), lambda b, i: (b, i, 0))
    head = lambda n, w: pl.BlockSpec((1, n, tm, w), lambda b, i: (b, 0, i, 0))
    head_t = lambda n, w: pl.BlockSpec((1, n, w, tm), lambda b, i: (b, 0, 0, i))
    sds = jax.ShapeDtypeStruct
    out_shape = (
        sds((B, NSA_HEADS, NSA_DH, S), BF16),
        sds((B, S, 256), F32),
        sds((B, NSA_KV_GROUPS, S, NSA_DH), BF16),
        sds((B, NSA_KV_GROUPS, NSA_DH, S), BF16),
        sds((B, NSA_KV_GROUPS, S, NSA_DH), BF16),
        sds((B, NSA_KV_GROUPS, NSA_DH, S), BF16),
        sds((B, GLA_HEADS, S, GLA_DK), F32),
        sds((B, GLA_HEADS, S, GLA_DK), F32),
        sds((B, S, GLA_HEADS * GLA_DV), BF16),
        sds((B, GLA_HEADS * GLA_DV, S), BF16),
        sds((B, S, GLA_HEADS * GLA_DV), F32),
        sds((B, S, D), F32),
        sds((B, S, D), F32),
        sds((B, LANES, S), F32),
        sds((B, S, LANES), F32),
    )
    out_specs = (
        head_t(NSA_HEADS, NSA_DH), tok(256),
        head(NSA_KV_GROUPS, NSA_DH), head_t(NSA_KV_GROUPS, NSA_DH),
        head(NSA_KV_GROUPS, NSA_DH), head_t(NSA_KV_GROUPS, NSA_DH),
        head(GLA_HEADS, GLA_DK), head(GLA_HEADS, GLA_DK),
        tok(GLA_HEADS * GLA_DV),
        pl.BlockSpec((1, GLA_HEADS * GLA_DV, tm), lambda b, i: (b, 0, i)),
        tok(GLA_HEADS * GLA_DV), tok(D), tok(D),
        pl.BlockSpec((1, LANES, tm), lambda b, i: (b, 0, i)), tok(LANES),
    )
    return pl.pallas_call(
        _in_proj_body,
        grid=grid,
        in_specs=[
            tok(D),
            pl.BlockSpec((1, D), lambda b, i: (0, 0)),
            pl.BlockSpec((D, IN_NW), lambda b, i: (0, 0), pipeline_mode=pl.Buffered(1)),
            tok(LANES), tok(LANES),
        ],
        out_specs=out_specs,
        out_shape=out_shape,
        compiler_params=pltpu.CompilerParams(
            dimension_semantics=("parallel", "parallel"), vmem_limit_bytes=VMEM_LIMIT),
        name="in_proj",
    )(x, g_mix.reshape(1, D), w_all, cs, sn)


def _compress_body(x_ref, p0_ref, p1_ref, w0_ref, w1_ref, b1_ref, w2_ref, b2_ref, cs_ref, sn_ref,
                   kc_ref, vc_ref):
    x = x_ref[0]
    n = x.shape[0]
    y0 = _dot((x + p0_ref[...]).astype(BF16), w0_ref[...])
    y1 = _dot((x + p1_ref[...]).astype(BF16), w1_ref[...])
    h = jax.nn.gelu(y0 + pltpu.roll(y1, n - 1, 0) + b1_ref[...])
    o = _dot(h.astype(BF16), w2_ref[...]) + b2_ref[...]
    k = _rope_lanes(o[:, :LANES], cs_ref[0], sn_ref[0])
    v_t = o[:, LANES:].T
    for g in range(NSA_KV_GROUPS):
        sl = slice(g * NSA_DH, (g + 1) * NSA_DH)
        kc_ref[0, g] = k[:, sl].astype(BF16)
        vc_ref[0, g] = v_t[sl].astype(BF16)


def _compress(kvc, cs_c, sn_c, pos_k, w1_k, b1_k, w2_k, b2_k, pos_v, w1_v, b1_v, w2_v, b2_v):
    B, S, _ = kvc.shape
    n = S // CMP_STRIDE
    ns = 2 * NSA_KV_GROUPS
    x = kvc.reshape(B, n, CMP_STRIDE * ns * NSA_DH)
    eye = jnp.eye(ns, dtype=F32)
    w1s = jnp.stack([w1_k, w1_k, w1_v, w1_v])
    poss = jnp.stack([pos_k, pos_k, pos_v, pos_v])
    wbig, pbig = [], []
    for m in range(CMP_BLOCK // CMP_STRIDE):
        wm = w1s.reshape(ns, CMP_BLOCK, NSA_DH, CMP_HIDDEN)[:, m * CMP_STRIDE:(m + 1) * CMP_STRIDE]
        wb = jnp.einsum('ctdj,ce->tcdej', wm, eye).reshape(CMP_STRIDE * ns * NSA_DH, ns * CMP_HIDDEN)
        wbig.append(wb.astype(BF16))
        pm = poss[:, m * CMP_STRIDE:(m + 1) * CMP_STRIDE]
        pbig.append(pm.transpose(1, 0, 2).reshape(1, CMP_STRIDE * ns * NSA_DH))
    b1 = jnp.concatenate([b1_k, b1_k, b1_v, b1_v]).reshape(1, ns * CMP_HIDDEN)
    w2s = jnp.stack([w2_k, w2_k, w2_v, w2_v])
    w2big = jnp.einsum('cjd,ce->cjed', w2s, eye).reshape(ns * CMP_HIDDEN, ns * NSA_DH).astype(BF16)
    b2 = jnp.concatenate([b2_k, b2_k, b2_v, b2_v]).reshape(1, ns * NSA_DH)
    full = lambda a: pl.BlockSpec(a.shape, lambda b: (0,) * a.ndim)
    sds = jax.ShapeDtypeStruct
    return pl.pallas_call(
        _compress_body,
        grid=(B,),
        in_specs=[pl.BlockSpec((1, n, x.shape[2]), lambda b: (b, 0, 0)),
                  full(pbig[0]), full(pbig[1]), full(wbig[0]), full(wbig[1]), full(b1), full(w2big), full(b2),
                  pl.BlockSpec((1, n, LANES), lambda b: (b, 0, 0)),
                  pl.BlockSpec((1, n, LANES), lambda b: (b, 0, 0))],
        out_specs=(pl.BlockSpec((1, NSA_KV_GROUPS, n, NSA_DH), lambda b: (b, 0, 0, 0)),
                   pl.BlockSpec((1, NSA_KV_GROUPS, NSA_DH, n), lambda b: (b, 0, 0, 0))),
        out_shape=(sds((B, NSA_KV_GROUPS, n, NSA_DH), BF16), sds((B, NSA_KV_GROUPS, NSA_DH, n), BF16)),
        compiler_params=pltpu.CompilerParams(
            dimension_semantics=("parallel",), vmem_limit_bytes=VMEM_LIMIT),
        name="compress",
    )(x, pbig[0], pbig[1], wbig[0], wbig[1], b1, w2big, b2, cs_c, sn_c)


def _softmax_step(s, m, l, acc, v_t):
    m_new = jnp.maximum(m, jnp.max(s, axis=0, keepdims=True))
    alpha = jnp.exp(m - m_new)
    p = jnp.exp(s - m_new)
    l = alpha * l + jnp.sum(p, axis=0, keepdims=True)
    acc = alpha * acc + _dot(v_t, p.astype(BF16))
    return m_new, l, acc


def _nsa_body(q_ref, kc_ref, vc_ref, ks_ref, vs_ref, kw_ref, vw_ref, ng_ref, ovt_ref,
              o_ref, bias_ref, *, tq, tk, n_slc):
    gi = pl.program_id(1)
    qi = pl.program_id(2)
    q0 = qi * tq
    n_cmp = kc_ref.shape[2]
    W = NSA_HPG * tq
    q_all = jnp.concatenate([q_ref[0, h] for h in range(NSA_HPG)], axis=1)
    t_lane = jnp.concatenate([q0 + lax.broadcasted_iota(jnp.int32, (1, tq), 1)] * NSA_HPG, axis=1)

    cmp_end = CMP_STRIDE * lax.broadcasted_iota(jnp.int32, (n_cmp, 1), 0) + (CMP_BLOCK - 1)
    ok = cmp_end <= t_lane
    sm = jnp.where(ok, _dot(kc_ref[0, 0], q_all), NEG_INF)
    e = jnp.exp(sm - jnp.max(sm, axis=0, keepdims=True))
    p = jnp.where(ok, e * (1.0 / jnp.sum(e, axis=0, keepdims=True)), 0.0)
    o_cmp = _dot(vc_ref[0, 0], p.astype(BF16))
    p_sum = p[:, 0:tq]
    for h in range(1, NSA_HPG):
        p_sum = p_sum + p[:, h * tq:(h + 1) * tq]

    imp = _dot_exact_lhs(ovt_ref[...], p_sum)
    blk = lax.broadcasted_iota(jnp.int32, (n_slc, tq), 0)
    cur = (q0 + lax.broadcasted_iota(jnp.int32, (n_slc, tq), 1)) // SLC_BLOCK
    forced = (blk == 0) | (blk == cur) | (blk == cur - 1)
    imp = jnp.where(forced, FORCED_SCORE, imp)
    imp = jnp.where(blk <= cur, imp, NEG_INF)
    rank = jnp.zeros((n_slc, tq), F32)
    for i in range(n_slc):
        row = imp[i:i + 1, :]
        rank = rank + jnp.where(row > imp, 1.0, 0.0)
        rank = rank + jnp.where(jnp.logical_and(row == imp, blk > i), 1.0, 0.0)
    bias1 = jnp.where(rank < float(min(SLC_TOPK, n_slc)), 0.0, NEG_INF)
    bias_ref[...] = jnp.concatenate([bias1] * NSA_HPG, axis=1)

    kpos_l = lax.broadcasted_iota(jnp.int32, (tk, 1), 0)
    blocks_per_tile = tk // SLC_BLOCK

    def slc_step(jt, carry):
        k0 = pl.multiple_of(jt * tk, tk)
        bias = jnp.concatenate(
            [jnp.broadcast_to(bias_ref[pl.ds(jt * blocks_per_tile + r, 1), :], (SLC_BLOCK, W))
             for r in range(blocks_per_tile)], axis=0)
        bias = jnp.where(k0 + kpos_l <= t_lane, bias, NEG_INF)
        s = _dot(ks_ref[0, 0, pl.ds(k0, tk), :], q_all) + bias
        return _softmax_step(s, *carry, vs_ref[0, 0, :, pl.ds(k0, tk)])

    init = (jnp.full((1, W), NEG_INF, F32), jnp.zeros((1, W), F32), jnp.zeros((NSA_DH, W), F32))
    _, l_s, acc_s = lax.fori_loop(0, (q0 + tq + tk - 1) // tk, slc_step, init)

    wk = WINDOW + tq
    w0 = pl.multiple_of(jnp.maximum(q0 - WINDOW, 0), tq)
    kpos = w0 + lax.broadcasted_iota(jnp.int32, (wk, 1), 0)
    okw = jnp.logical_and(kpos <= t_lane, kpos > t_lane - WINDOW)
    sw = jnp.where(okw, _dot(kw_ref[0, 0, pl.ds(w0, wk), :], q_all), NEG_INF)
    ew = jnp.exp(sw - jnp.max(sw, axis=0, keepdims=True))
    l_w = jnp.sum(ew, axis=0, keepdims=True)
    acc_w = _dot(vw_ref[0, 0, :, pl.ds(w0, wk)], ew.astype(BF16))

    def gate(j):
        return jnp.concatenate([ng_ref[0, pl.ds((gi * NSA_HPG + h) * 3 + j, 1), :]
                                for h in range(NSA_HPG)], axis=1)

    out_t = gate(0) * o_cmp + (gate(1) * (1.0 / l_s)) * acc_s + (gate(2) * (1.0 / l_w)) * acc_w
    out_t = jnp.concatenate([out_t[:, h * tq:(h + 1) * tq] for h in range(NSA_HPG)], axis=0)
    o_ref[0] = out_t.T.astype(o_ref.dtype)


def _nsa_attention(q, kc, vc, ks, vs, kw, vw, ng):
    B, H, dh, S = q.shape
    tq = min(ATT_TQ, S)
    tk = min(ATT_TK, S)
    assert S % tk == 0 and tk % tq == 0 and S >= WINDOW + tq
    n_cmp = kc.shape[2]
    n_slc = S // SLC_BLOCK
    c0 = CMP_STRIDE * np.arange(n_cmp)[None, :]
    s0 = SLC_BLOCK * np.arange(n_slc)[:, None]
    ov = np.clip(np.minimum(c0 + CMP_BLOCK, s0 + SLC_BLOCK) - np.maximum(c0, s0), 0, None) / CMP_BLOCK
    ovt = jnp.asarray(ov, BF16)
    grid = (B, NSA_KV_GROUPS, S // tq)
    k_spec = lambda n: pl.BlockSpec((1, 1, n, dh), lambda b, g, i: (b, g, 0, 0))
    vt_spec = lambda n: pl.BlockSpec((1, 1, dh, n), lambda b, g, i: (b, g, 0, 0))
    return pl.pallas_call(
        functools.partial(_nsa_body, tq=tq, tk=tk, n_slc=n_slc),
        grid=grid,
        in_specs=[
            pl.BlockSpec((1, NSA_HPG, dh, tq), lambda b, g, i: (b, g, 0, i)),
            k_spec(n_cmp), vt_spec(n_cmp), k_spec(S), vt_spec(S), k_spec(S), vt_spec(S),
            pl.BlockSpec((1, LANES, tq), lambda b, g, i: (b, 0, i)),
            pl.BlockSpec(ovt.shape, lambda b, g, i: (0, 0)),
        ],
        out_specs=pl.BlockSpec((1, tq, NSA_HPG * dh), lambda b, g, i: (b, i, g)),
        out_shape=jax.ShapeDtypeStruct((B, S, H * dh), BF16),
        scratch_shapes=[pltpu.VMEM((n_slc, NSA_HPG * tq), F32)],
        compiler_params=pltpu.CompilerParams(
            dimension_semantics=("parallel", "parallel", "arbitrary"), vmem_limit_bytes=VMEM_LIMIT),
        name="nsa_attn",
    )(q, kc, vc, ks, vs, kw, vw, ng, ovt)


def _gla_body(q_ref, k_ref, v_ref, vt_ref, ga_ref, r_ref, wa_ref, ba_ref, ng_ref, tri_ref,
              o_ref, state_ref, *, n_chunks):
    C, SUB = GLA_C, GLA_SUB
    state_ref[...] = jnp.zeros_like(state_ref)
    wa = wa_ref[0]
    ba = ba_ref[0]
    tri = tri_ref[...]
    t_loc = lax.broadcasted_iota(jnp.int32, (C, 1), 0)

    def chunk(c, _):
        c0 = pl.multiple_of(c * C, C)
        q = q_ref[0, 0, pl.ds(c0, C), :]
        k = k_ref[0, 0, pl.ds(c0, C), :]
        v = v_ref[0, pl.ds(c0, C), :]
        vt = vt_ref[0, :, pl.ds(c0, C)]
        la = jax.nn.log_sigmoid(_dot_f32(ga_ref[0, pl.ds(c0, C), :], wa) + ba) / GLA_TAU
        b = _dot_exact_lhs(tri, la)
        b_last = b[C - 1:C, :]
        st = state_ref[...]
        o = _dot_nt((q * jnp.exp(b)).astype(BF16), st.astype(BF16))
        o_sub = [o[i * SUB:(i + 1) * SUB] for i in range(C // SUB)]
        for i in range(1, C // SUB):
            r0 = i * SUB
            b_ref0 = b[r0:r0 + 1, :]
            qt = q[r0:r0 + SUB] * jnp.exp(b[r0:r0 + SUB] - b_ref0)
            kt = k[:r0] * jnp.exp(b_ref0 - b[:r0])
            a = _dot_nt(qt.astype(BF16), kt.astype(BF16))
            lt = lax.broadcasted_iota(jnp.int32, (SUB, r0), 0)
            ls = lax.broadcasted_iota(jnp.int32, (SUB, r0), 1)
            a = jnp.where(lt + (r0 - SUB) >= ls, a, 0.0)
            o_sub[i] = o_sub[i] + _dot(a.astype(BF16), v[:r0])
        o = jnp.concatenate(o_sub, axis=0)
        vf = v.astype(F32)
        for d in range(SUB):
            if d == 0:
                kd, bd, vd = k, b, vf
            else:
                kd, bd, vd = pltpu.roll(k, d, 0), pltpu.roll(b, d, 0), pltpu.roll(vf, d, 0)
            valid = t_loc >= d
            w = jnp.exp(jnp.where(valid, b - bd, 0.0))
            a_d = jnp.sum(jnp.where(valid, q * kd * w, 0.0), axis=-1, keepdims=True)
            o = o + a_d * vd
        k_dec = (k * jnp.exp(b_last - b)).astype(BF16)
        state_ref[...] = st * jnp.exp(b_last) + _dot(vt, k_dec)
        o = o * lax.rsqrt(jnp.mean(o * o, axis=-1, keepdims=True) + EPS) * ng_ref[0]
        o_ref[0, pl.ds(c0, C), :] = (o * jax.nn.silu(r_ref[0, pl.ds(c0, C), :])).astype(o_ref.dtype)
        return 0

    lax.fori_loop(0, n_chunks, chunk, 0)


def _gla(gq, gk, gv, gvt, ga, gr, w_a2, b_a, norm_g):
    B, H, S, dk = gq.shape
    dv = GLA_DV
    wa = jnp.pad(w_a2, ((0, LANES - GLA_RANK), (0, 0))).reshape(LANES, H, dk).transpose(1, 0, 2)
    ba = b_a.reshape(H, 1, dk)
    ng = norm_g.reshape(H, 1, dv)
    tri = jnp.asarray(np.tril(np.ones((GLA_C, GLA_C))), BF16)
    qk_spec = pl.BlockSpec((1, 1, S, dk), lambda b, h: (b, h, 0, 0))
    tok_spec = lambda w: pl.BlockSpec((1, S, w), lambda b, h: (b, 0, h))
    return pl.pallas_call(
        functools.partial(_gla_body, n_chunks=S // GLA_C),
        grid=(B, H),
        in_specs=[qk_spec, qk_spec, tok_spec(dv),
                  pl.BlockSpec((1, dv, S), lambda b, h: (b, h, 0)),
                  pl.BlockSpec((1, S, LANES), lambda b, h: (b, 0, 0)),
                  tok_spec(dv),
                  pl.BlockSpec((1, LANES, dk), lambda b, h: (h, 0, 0)),
                  pl.BlockSpec((1, 1, dk), lambda b, h: (h, 0, 0)),
                  pl.BlockSpec((1, 1, dv), lambda b, h: (h, 0, 0)),
                  pl.BlockSpec(tri.shape, lambda b, h: (0, 0))],
        out_specs=tok_spec(dv),
        out_shape=jax.ShapeDtypeStruct((B, S, H * dv), BF16),
        scratch_shapes=[pltpu.VMEM((dv, dk), F32)],
        compiler_params=pltpu.CompilerParams(
            dimension_semantics=("parallel", "parallel"), vmem_limit_bytes=VMEM_LIMIT),
        name="gla",
    )(gq, gk, gv, gvt, ga, gr, wa, ba, ng, tri)


def _merge_body(x_ref, ya_ref, yb_ref, ma_ref, mb_ref, wpa_ref, wpb_ref, wo_ref, gf_ref, wr_ref, br_ref,
                h_ref, v_ref, comb_ref):
    y_a = _dot(ya_ref[...], wpa_ref[...])
    y_b = _dot(yb_ref[...], wpb_ref[...])
    mixed = ma_ref[...] * y_a + mb_ref[...] * y_b
    h = x_ref[...] + _dot(mixed.astype(BF16), wo_ref[...])
    h_ref[...] = h
    v = h * lax.rsqrt(jnp.mean(h * h, axis=-1, keepdims=True) + EPS) * gf_ref[...]
    v_ref[...] = v.astype(BF16)
    logits = _dot_f32(v, wr_ref[...]) + br_ref[...]
    lane = lax.broadcasted_iota(jnp.int32, logits.shape, 1)
    is_grp = jnp.logical_and(lane >= MOE_EXPERTS, lane < MOE_EXPERTS + MOE_GROUPS)
    lg = jnp.where(is_grp, logits, NEG_INF)
    eg = jnp.where(is_grp, jnp.exp(lg - jnp.max(lg, axis=-1, keepdims=True)), 0.0)
    pg = eg / jnp.sum(eg, axis=-1, keepdims=True)
    p_grp = jnp.max(pg, axis=-1, keepdims=True)
    g_sel = jnp.min(jnp.where(jnp.logical_and(is_grp, pg == p_grp), lane, 2 * LANES),
                    axis=-1, keepdims=True) - MOE_EXPERTS
    in_grp = jnp.logical_and(lane < MOE_EXPERTS, lane // MOE_EPG == g_sel)
    le = jnp.where(in_grp, logits, NEG_INF)
    ee = jnp.where(in_grp, jnp.exp(le - jnp.max(le, axis=-1, keepdims=True)), 0.0)
    pin = ee / jnp.sum(ee, axis=-1, keepdims=True)
    p1 = jnp.max(jnp.where(in_grp, pin, -1.0), axis=-1, keepdims=True)
    i1 = jnp.min(jnp.where(jnp.logical_and(in_grp, pin == p1), lane, 2 * LANES), axis=-1, keepdims=True)
    rest = jnp.logical_and(in_grp, lane != i1)
    p2 = jnp.max(jnp.where(rest, pin, -1.0), axis=-1, keepdims=True)
    i2 = jnp.min(jnp.where(jnp.logical_and(rest, pin == p2), lane, 2 * LANES), axis=-1, keepdims=True)
    tot = p1 + p2
    comb_ref[...] = (jnp.where(lane == i1, p_grp * p1 / tot, 0.0)
                     + jnp.where(lane == i2, p_grp * p2 / tot, 0.0))


def _merge(x2, ya, yb, ma, mb, w_proj_nsa, w_proj_gla, w_out, g_ffn, w_grp, b_grp, w_exp, b_exp):
    T, D = x2.shape
    tm = min(MERGE_TM, T)
    wr = jnp.pad(jnp.concatenate([w_exp, w_grp], axis=1), ((0, 0), (0, LANES - MOE_EXPERTS - MOE_GROUPS)))
    br = jnp.pad(jnp.concatenate([b_exp, b_grp]), (0, LANES - MOE_EXPERTS - MOE_GROUPS)).reshape(1, LANES)
    tok = lambda w: pl.BlockSpec((tm, w), lambda i: (i, 0))
    full = lambda a: pl.BlockSpec(a.shape, lambda i: (0, 0))
    wpa, wpb, wo = w_proj_nsa.astype(BF16), w_proj_gla.astype(BF16), w_out.astype(BF16)
    gf = g_ffn.reshape(1, D)
    sds = jax.ShapeDtypeStruct
    return pl.pallas_call(
        _merge_body,
        grid=(T // tm,),
        in_specs=[tok(D), tok(ya.shape[1]), tok(yb.shape[1]), tok(D), tok(D),
                  full(wpa), full(wpb), full(wo), full(gf), full(wr), full(br)],
        out_specs=(tok(D), tok(D), tok(LANES)),
        out_shape=(sds((T, D), F32), sds((T, D), BF16), sds((T, LANES), F32)),
        compiler_params=pltpu.CompilerParams(
            dimension_semantics=("parallel",), vmem_limit_bytes=VMEM_LIMIT),
        name="merge",
    )(x2, ya, yb, ma, mb, wpa, wpb, wo, gf, wr, br)


def _moe_body(v_ref, comb_ref, h_ref, wg_ref, wu_ref, wd_ref, gfin_ref, o_ref, acc_ref):
    e = pl.program_id(1)

    @pl.when(e == 0)
    def _():
        acc_ref[...] = jnp.zeros_like(acc_ref)

    v = v_ref[...]
    hdn = jax.nn.silu(_dot(v, wg_ref[0])) * _dot(v, wu_ref[0])
    comb = comb_ref[...]
    lane = lax.broadcasted_iota(jnp.int32, comb.shape, 1)
    c = jnp.sum(jnp.where(lane == e, comb, 0.0), axis=-1, keepdims=True)
    acc_ref[...] += _dot((c * hdn).astype(BF16), wd_ref[0])

    @pl.when(e == pl.num_programs(1) - 1)
    def _():
        h = h_ref[...] + acc_ref[...]
        o_ref[...] = h * lax.rsqrt(jnp.mean(h * h, axis=-1, keepdims=True) + EPS) * gfin_ref[...]


def _moe(v, comb, h, w_gate, w_up, w_down, g_final):
    T, D = h.shape
    tm = min(MOE_TM, T)
    E, _, F = w_gate.shape
    wg, wu, wd = w_gate.astype(BF16), w_up.astype(BF16), w_down.astype(BF16)
    tok = lambda w: pl.BlockSpec((tm, w), lambda i, e: (i, 0))
    return pl.pallas_call(
        _moe_body,
        grid=(T // tm, E),
        in_specs=[tok(D), tok(LANES), tok(D),
                  pl.BlockSpec((1, D, F), lambda i, e: (e, 0, 0)),
                  pl.BlockSpec((1, D, F), lambda i, e: (e, 0, 0)),
                  pl.BlockSpec((1, F, D), lambda i, e: (e, 0, 0)),
                  pl.BlockSpec((1, D), lambda i, e: (0, 0))],
        out_specs=tok(D),
        out_shape=jax.ShapeDtypeStruct((T, D), F32),
        scratch_shapes=[pltpu.VMEM((tm, D), F32)],
        compiler_params=pltpu.CompilerParams(
            dimension_semantics=("parallel", "arbitrary"), vmem_limit_bytes=VMEM_LIMIT),
        name="moe",
    )(v, comb, h, wg, wu, wd, g_final.reshape(1, D))


def _rope_tables(positions):
    half = NSA_DH // 2
    inv = 1.0 / (ROPE_THETA ** (jnp.arange(half, dtype=F32) / half))
    ang = positions.astype(F32)[..., None] * inv
    cos, sin = jnp.cos(ang), jnp.sin(ang)
    cs = jnp.concatenate([cos, cos, cos, cos], axis=-1)
    sn = jnp.concatenate([-sin, sin, -sin, sin], axis=-1)
    return cs, sn


def _layer(h, positions, g_mix, w_in, cmp_pos_k, cmp_w1_k, cmp_b1_k, cmp_w2_k, cmp_b2_k,
           cmp_pos_v, cmp_w1_v, cmp_b1_v, cmp_w2_v, cmp_b2_v, gla_w_a2, gla_b_a, gla_norm_g,
           w_proj_nsa, w_proj_gla, w_out, g_ffn, w_grp, b_grp, w_exp, b_exp, w_gate, w_up, w_down, g_out):
    B, S, D = h.shape
    cs, sn = _rope_tables(positions)
    n_chunks = S // CMP_STRIDE
    cmp_end = jnp.minimum(CMP_STRIDE * jnp.arange(n_chunks) + CMP_BLOCK - 1, S - 1)
    cs_c, sn_c = _rope_tables(jnp.take(positions, cmp_end, axis=1))
    (q, kvc, ks, vs, kw, vw, gq, gk, gv, gvt, gr, ma, mb, ng, ga) = _in_proj(h, g_mix, w_in, cs, sn)
    kc, vc = _compress(kvc, cs_c, sn_c, cmp_pos_k, cmp_w1_k, cmp_b1_k, cmp_w2_k, cmp_b2_k,
                       cmp_pos_v, cmp_w1_v, cmp_b1_v, cmp_w2_v, cmp_b2_v)
    ya = _nsa_attention(q, kc, vc, ks, vs, kw, vw, ng)
    yb = _gla(gq, gk, gv, gvt, ga, gr, gla_w_a2, gla_b_a, gla_norm_g)
    T = B * S
    h1, v, comb = _merge(h.reshape(T, D), ya.reshape(T, -1), yb.reshape(T, -1), ma.reshape(T, D), mb.reshape(T, D),
                         w_proj_nsa, w_proj_gla, w_out, g_ffn, w_grp, b_grp, w_exp, b_exp)
    return _moe(v, comb, h1, w_gate, w_up, w_down, g_out).reshape(B, S, D)


def kernel(x, positions, g_mix, w_in, cmp_pos_k, cmp_w1_k, cmp_b1_k, cmp_w2_k, cmp_b2_k, cmp_pos_v, cmp_w1_v,
           cmp_b1_v, cmp_w2_v, cmp_b2_v, gla_w_a2, gla_b_a, gla_norm_g, w_proj_nsa, w_proj_gla, w_out, g_ffn,
           w_grp, b_grp, w_exp, b_exp, w_gate, w_up, w_down, g_final):
    depth = g_mix.shape[0]
    assert depth == 1, "the final norm is fused into the single layer's expert kernel"
    return _layer(x, positions, g_mix[0], w_in[0], cmp_pos_k[0], cmp_w1_k[0], cmp_b1_k[0], cmp_w2_k[0],
                  cmp_b2_k[0], cmp_pos_v[0], cmp_w1_v[0], cmp_b1_v[0], cmp_w2_v[0], cmp_b2_v[0],
                  gla_w_a2[0], gla_b_a[0], gla_norm_g[0], w_proj_nsa[0], w_proj_gla[0], w_out[0],
                  g_ffn[0], w_grp[0], b_grp[0], w_exp[0], b_exp[0], w_gate[0], w_up[0], w_down[0], g_final)
```

```python
import functools

import numpy as np
import jax
import jax.numpy as jnp
from jax import lax
from jax.experimental import pallas as pl
from jax.experimental.pallas import tpu as pltpu

F32 = jnp.float32
BF16 = jnp.bfloat16

NSA_HEADS = 8
NSA_KV_GROUPS = 2
NSA_HPG = NSA_HEADS // NSA_KV_GROUPS
NSA_DH = 64
CMP_BLOCK = 32
CMP_STRIDE = 16
CMP_HIDDEN = 256
SLC_BLOCK = 64
SLC_TOPK = 16
WINDOW = 512
GLA_HEADS = 4
GLA_DK = 64
GLA_DV = 128
GLA_RANK = 16
GLA_TAU = 16.0
MOE_GROUPS = 4
MOE_EPG = 8
MOE_EXPERTS = MOE_GROUPS * MOE_EPG
MOE_DFF = 512
ROPE_THETA = 10000.0
EPS = 1e-6
NEG_INF = -1e30
FORCED_SCORE = 1e4

LANES = 128
VMEM_LIMIT = 56 * 1024 * 1024

IN_TM = 512
ATT_TQ = 128
ATT_TK = 512
GLA_C = 128
GLA_SUB = 16
MERGE_TM = 512
MOE_TW = 1024
MOE_RT = 256
MOE_ET = 128
MOE_ALIGN = 16


def _dot(a, b):
    return jnp.dot(a, b, preferred_element_type=F32)


def _dot_nt(a, b):
    return lax.dot_general(a, b, (((1,), (1,)), ((), ())), preferred_element_type=F32)


def _split3(x):
    x1 = x.astype(BF16)
    r1 = x - x1.astype(F32)
    x2 = r1.astype(BF16)
    r2 = r1 - x2.astype(F32)
    x3 = r2.astype(BF16)
    return x1, x2, x3


def _dot_exact_lhs(a_bf16, x):
    x1, x2, x3 = _split3(x)
    return _dot(a_bf16, x1) + _dot(a_bf16, x2) + _dot(a_bf16, x3)


def _dot_f32(a, b):
    a1, a2, a3 = _split3(a)
    b1, b2, b3 = _split3(b)
    return (_dot(a1, b1) + (_dot(a1, b2) + _dot(a2, b1))
            + (_dot(a1, b3) + _dot(a2, b2) + _dot(a3, b1)))


def _rope_lanes(z, cs, sn):
    w = z.shape[-1]
    lane = lax.broadcasted_iota(jnp.int32, z.shape, 1)
    first_half = (lane % NSA_DH) < (NSA_DH // 2)
    rot = jnp.where(first_half, pltpu.roll(z, w - NSA_DH // 2, 1), pltpu.roll(z, NSA_DH // 2, 1))
    reps = w // LANES
    if reps > 1:
        cs = jnp.concatenate([cs] * reps, axis=1)
        sn = jnp.concatenate([sn] * reps, axis=1)
    return z * cs + rot * sn


_SEC = {}
_off = 0
for _name, _w in (("q", 512), ("kvc", 256), ("ks", 128), ("vs", 128), ("kw", 128), ("vw", 128),
                  ("gq", 256), ("gk", 256), ("gv", 512), ("gr", 512), ("ma", 1024), ("mb", 1024),
                  ("ng", 128), ("ga", 128)):
    _SEC[_name] = (_off, _off + _w)
    _off += _w
IN_NW = _off


def _in_proj_body(x_ref, g_ref, w_ref, cs_ref, sn_ref,
                  q_ref, kvc_ref, ks_ref, vs_ref, kw_ref, vw_ref,
                  gq_ref, gk_ref, gv_ref, gvt_ref, gr_ref, ma_ref, mb_ref, ng_ref, ga_ref):
    x = x_ref[0]
    var = jnp.mean(x * x, axis=-1, keepdims=True)
    u = (x * lax.rsqrt(var + EPS) * g_ref[...]).astype(BF16)
    cs = cs_ref[0]
    sn = sn_ref[0]

    def proj(name):
        a, b = _SEC[name]
        return _dot(u, w_ref[:, a:b])

    zq_t = (_rope_lanes(proj("q"), cs, sn) * (NSA_DH ** -0.5)).T
    for h in range(NSA_HEADS):
        q_ref[0, h] = zq_t[h * NSA_DH:(h + 1) * NSA_DH].astype(BF16)
    kvc_ref[0] = proj("kvc")
    zks = _rope_lanes(proj("ks"), cs, sn)
    zkw = _rope_lanes(proj("kw"), cs, sn)
    zvs_t = proj("vs").T
    zvw_t = proj("vw").T
    for g in range(NSA_KV_GROUPS):
        sl = slice(g * NSA_DH, (g + 1) * NSA_DH)
        ks_ref[0, g] = zks[:, sl].astype(BF16)
        kw_ref[0, g] = zkw[:, sl].astype(BF16)
        vs_ref[0, g] = zvs_t[sl].astype(BF16)
        vw_ref[0, g] = zvw_t[sl].astype(BF16)
    zgq = proj("gq") * (GLA_DK ** -0.5)
    zgk = proj("gk")
    for h in range(GLA_HEADS):
        sl = slice(h * GLA_DK, (h + 1) * GLA_DK)
        gq_ref[0, h] = zgq[:, sl]
        gk_ref[0, h] = zgk[:, sl]
    zgv = proj("gv")
    gv_ref[0] = zgv.astype(BF16)
    gvt_ref[0] = zgv.T.astype(BF16)
    gr_ref[0] = proj("gr")
    ma_ref[0] = jax.nn.sigmoid(proj("ma"))
    mb_ref[0] = jax.nn.sigmoid(proj("mb"))
    ng_ref[0] = jax.nn.sigmoid(proj("ng")).T
    ga_ref[0] = proj("ga")


def _in_proj(x, g_mix, w_in, cs, sn):
    B, S, D = x.shape
    tm = min(IN_TM, S)
    splits = np.cumsum((512,) + (128,) * 6 + (24, 256, 256, 512, 16, 512, 1024, 1024))
    (wq, wkc, wvc, wks, wvs, wkw, wvw, wng, wgq, wgk, wgv, wga, wgr, wma, wmb) = jnp.split(
        w_in, splits[:-1].tolist(), axis=1)
    pad = lambda w: jnp.pad(w, ((0, 0), (0, LANES - w.shape[1])))
    w_all = jnp.concatenate([wq, wkc, wvc, wks, wvs, wkw, wvw, wgq, wgk, wgv, wgr, wma, wmb,
                             pad(wng), pad(wga)], axis=1).astype(BF16)
    assert w_all.shape[1] == IN_NW
    grid = (B, S // tm)
    tok = lambda w: pl.BlockSpec((1, tm, w), lambda b, i: (b, i, 0))
    head = lambda n, w: pl.BlockSpec((1, n, tm, w), lambda b, i: (b, 0, i, 0))
    head_t = lambda n, w: pl.BlockSpec((1, n, w, tm), lambda b, i: (b, 0, 0, i))
    sds = jax.ShapeDtypeStruct
    out_shape = (
        sds((B, NSA_HEADS, NSA_DH, S), BF16),
        sds((B, S, 256), F32),
        sds((B, NSA_KV_GROUPS, S, NSA_DH), BF16),
        sds((B, NSA_KV_GROUPS, NSA_DH, S), BF16),
        sds((B, NSA_KV_GROUPS, S, NSA_DH), BF16),
        sds((B, NSA_KV_GROUPS, NSA_DH, S), BF16),
        sds((B, GLA_HEADS, S, GLA_DK), F32),
        sds((B, GLA_HEADS, S, GLA_DK), F32),
        sds((B, S, GLA_HEADS * GLA_DV), BF16),
        sds((B, GLA_HEADS * GLA_DV, S), BF16),
        sds((B, S, GLA_HEADS * GLA_DV), F32),
        sds((B, S, D), F32),
        sds((B, S, D), F32),
        sds((B, LANES, S), F32),
        sds((B, S, LANES), F32),
    )
    out_specs = (
        head_t(NSA_HEADS, NSA_DH), tok(256),
        head(NSA_KV_GROUPS, NSA_DH), head_t(NSA_KV_GROUPS, NSA_DH),
        head(NSA_KV_GROUPS, NSA_DH), head_t(NSA_KV_GROUPS, NSA_DH),
        head(GLA_HEADS, GLA_DK), head(GLA_HEADS, GLA_DK),
        tok(GLA_HEADS * GLA_DV),
        pl.BlockSpec((1, GLA_HEADS * GLA_DV, tm), lambda b, i: (b, 0, i)),
        tok(GLA_HEADS * GLA_DV), tok(D), tok(D),
        pl.BlockSpec((1, LANES, tm), lambda b, i: (b, 0, i)), tok(LANES),
    )
    return pl.pallas_call(
        _in_proj_body,
        grid=grid,
        in_specs=[
            tok(D),
            pl.BlockSpec((1, D), lambda b, i: (0, 0)),
            pl.BlockSpec((D, IN_NW), lambda b, i: (0, 0), pipeline_mode=pl.Buffered(1)),
            tok(LANES), tok(LANES),
        ],
        out_specs=out_specs,
        out_shape=out_shape,
        compiler_params=pltpu.CompilerParams(
            dimension_semantics=("parallel", "parallel"), vmem_limit_bytes=VMEM_LIMIT),
        name="in_proj",
    )(x, g_mix.reshape(1, D), w_all, cs, sn)


def _compress_body(x_ref, p0_ref, p1_ref, w0_ref, w1_ref, b1_ref, w2_ref, b2_ref, cs_ref, sn_ref,
                   kc_ref, vc_ref):
    x = x_ref[0]
    n = x.shape[0]
    y0 = _dot((x + p0_ref[...]).astype(BF16), w0_ref[...])
    y1 = _dot((x + p1_ref[...]).astype(BF16), w1_ref[...])
    h = jax.nn.gelu(y0 + pltpu.roll(y1, n - 1, 0) + b1_ref[...])
    o = _dot(h.astype(BF16), w2_ref[...]) + b2_ref[...]
    k = _rope_lanes(o[:, :LANES], cs_ref[0], sn_ref[0])
    v_t = o[:, LANES:].T
    for g in range(NSA_KV_GROUPS):
        sl = slice(g * NSA_DH, (g + 1) * NSA_DH)
        kc_ref[0, g] = k[:, sl].astype(BF16)
        vc_ref[0, g] = v_t[sl].astype(BF16)


def _compress(kvc, cs_c, sn_c, pos_k, w1_k, b1_k, w2_k, b2_k, pos_v, w1_v, b1_v, w2_v, b2_v):
    B, S, _ = kvc.shape
    n = S // CMP_STRIDE
    ns = 2 * NSA_KV_GROUPS
    x = kvc.reshape(B, n, CMP_STRIDE * ns * NSA_DH)
    eye = jnp.eye(ns, dtype=F32)
    w1s = jnp.stack([w1_k, w1_k, w1_v, w1_v])
    poss = jnp.stack([pos_k, pos_k, pos_v, pos_v])
    wbig, pbig = [], []
    for m in range(CMP_BLOCK // CMP_STRIDE):
        wm = w1s.reshape(ns, CMP_BLOCK, NSA_DH, CMP_HIDDEN)[:, m * CMP_STRIDE:(m + 1) * CMP_STRIDE]
        wb = jnp.einsum('ctdj,ce->tcdej', wm, eye).reshape(CMP_STRIDE * ns * NSA_DH, ns * CMP_HIDDEN)
        wbig.append(wb.astype(BF16))
        pm = poss[:, m * CMP_STRIDE:(m + 1) * CMP_STRIDE]
        pbig.append(pm.transpose(1, 0, 2).reshape(1, CMP_STRIDE * ns * NSA_DH))
    b1 = jnp.concatenate([b1_k, b1_k, b1_v, b1_v]).reshape(1, ns * CMP_HIDDEN)
    w2s = jnp.stack([w2_k, w2_k, w2_v, w2_v])
    w2big = jnp.einsum('cjd,ce->cjed', w2s, eye).reshape(ns * CMP_HIDDEN, ns * NSA_DH).astype(BF16)
    b2 = jnp.concatenate([b2_k, b2_k, b2_v, b2_v]).reshape(1, ns * NSA_DH)
    full = lambda a: pl.BlockSpec(a.shape, lambda b: (0,) * a.ndim)
    sds = jax.ShapeDtypeStruct
    return pl.pallas_call(
        _compress_body,
        grid=(B,),
        in_specs=[pl.BlockSpec((1, n, x.shape[2]), lambda b: (b, 0, 0)),
                  full(pbig[0]), full(pbig[1]), full(wbig[0]), full(wbig[1]), full(b1), full(w2big), full(b2),
                  pl.BlockSpec((1, n, LANES), lambda b: (b, 0, 0)),
                  pl.BlockSpec((1, n, LANES), lambda b: (b, 0, 0))],
        out_specs=(pl.BlockSpec((1, NSA_KV_GROUPS, n, NSA_DH), lambda b: (b, 0, 0, 0)),
                   pl.BlockSpec((1, NSA_KV_GROUPS, NSA_DH, n), lambda b: (b, 0, 0, 0))),
        out_shape=(sds((B, NSA_KV_GROUPS, n, NSA_DH), BF16), sds((B, NSA_KV_GROUPS, NSA_DH, n), BF16)),
        compiler_params=pltpu.CompilerParams(
            dimension_semantics=("parallel",), vmem_limit_bytes=VMEM_LIMIT),
        name="compress",
    )(x, pbig[0], pbig[1], wbig[0], wbig[1], b1, w2big, b2, cs_c, sn_c)


def _softmax_step(s, m, l, acc, v_t):
    m_new = jnp.maximum(m, jnp.max(s, axis=0, keepdims=True))
    alpha = jnp.exp(m - m_new)
    p = jnp.exp(s - m_new)
    l = alpha * l + jnp.sum(p, axis=0, keepdims=True)
    acc = alpha * acc + _dot(v_t, p.astype(BF16))
    return m_new, l, acc


def _nsa_body(q_ref, kc_ref, vc_ref, ks_ref, vs_ref, kw_ref, vw_ref, ng_ref, ovt_ref,
              o_ref, bias_ref, *, tq, tk, n_slc):
    gi = pl.program_id(1)
    qi = pl.program_id(2)
    q0 = qi * tq
    n_cmp = kc_ref.shape[2]
    W = NSA_HPG * tq
    q_all = jnp.concatenate([q_ref[0, h] for h in range(NSA_HPG)], axis=1)
    t_lane = jnp.concatenate([q0 + lax.broadcasted_iota(jnp.int32, (1, tq), 1)] * NSA_HPG, axis=1)

    cmp_end = CMP_STRIDE * lax.broadcasted_iota(jnp.int32, (n_cmp, 1), 0) + (CMP_BLOCK - 1)
    ok = cmp_end <= t_lane
    sm = jnp.where(ok, _dot(kc_ref[0, 0], q_all), NEG_INF)
    e = jnp.exp(sm - jnp.max(sm, axis=0, keepdims=True))
    p = jnp.where(ok, e * (1.0 / jnp.sum(e, axis=0, keepdims=True)), 0.0)
    o_cmp = _dot(vc_ref[0, 0], p.astype(BF16))
    p_sum = p[:, 0:tq]
    for h in range(1, NSA_HPG):
        p_sum = p_sum + p[:, h * tq:(h + 1) * tq]

    imp = _dot_exact_lhs(ovt_ref[...], p_sum)
    blk = lax.broadcasted_iota(jnp.int32, (n_slc, tq), 0)
    cur = (q0 + lax.broadcasted_iota(jnp.int32, (n_slc, tq), 1)) // SLC_BLOCK
    forced = (blk == 0) | (blk == cur) | (blk == cur - 1)
    imp = jnp.where(forced, FORCED_SCORE, imp)
    imp = jnp.where(blk <= cur, imp, NEG_INF)
    rank = jnp.zeros((n_slc, tq), F32)
    for i in range(n_slc):
        row = imp[i:i + 1, :]
        rank = rank + jnp.where(row > imp, 1.0, 0.0)
        rank = rank + jnp.where(jnp.logical_and(row == imp, blk > i), 1.0, 0.0)
    bias1 = jnp.where(rank < float(min(SLC_TOPK, n_slc)), 0.0, NEG_INF)
    bias_ref[...] = jnp.concatenate([bias1] * NSA_HPG, axis=1)

    kpos_l = lax.broadcasted_iota(jnp.int32, (tk, 1), 0)
    blocks_per_tile = tk // SLC_BLOCK

    def slc_step(jt, carry):
        k0 = pl.multiple_of(jt * tk, tk)
        bias = jnp.concatenate(
            [jnp.broadcast_to(bias_ref[pl.ds(jt * blocks_per_tile + r, 1), :], (SLC_BLOCK, W))
             for r in range(blocks_per_tile)], axis=0)
        bias = jnp.where(k0 + kpos_l <= t_lane, bias, NEG_INF)
        s = _dot(ks_ref[0, 0, pl.ds(k0, tk), :], q_all) + bias
        return _softmax_step(s, *carry, vs_ref[0, 0, :, pl.ds(k0, tk)])

    init = (jnp.full((1, W), NEG_INF, F32), jnp.zeros((1, W), F32), jnp.zeros((NSA_DH, W), F32))
    _, l_s, acc_s = lax.fori_loop(0, (q0 + tq + tk - 1) // tk, slc_step, init)

    wk = WINDOW + tq
    w0 = pl.multiple_of(jnp.maximum(q0 - WINDOW, 0), tq)
    kpos = w0 + lax.broadcasted_iota(jnp.int32, (wk, 1), 0)
    okw = jnp.logical_and(kpos <= t_lane, kpos > t_lane - WINDOW)
    sw = jnp.where(okw, _dot(kw_ref[0, 0, pl.ds(w0, wk), :], q_all), NEG_INF)
    ew = jnp.exp(sw - jnp.max(sw, axis=0, keepdims=True))
    l_w = jnp.sum(ew, axis=0, keepdims=True)
    acc_w = _dot(vw_ref[0, 0, :, pl.ds(w0, wk)], ew.astype(BF16))

    def gate(j):
        return jnp.concatenate([ng_ref[0, pl.ds((gi * NSA_HPG + h) * 3 + j, 1), :]
                                for h in range(NSA_HPG)], axis=1)

    out_t = gate(0) * o_cmp + (gate(1) * (1.0 / l_s)) * acc_s + (gate(2) * (1.0 / l_w)) * acc_w
    out_t = jnp.concatenate([out_t[:, h * tq:(h + 1) * tq] for h in range(NSA_HPG)], axis=0)
    o_ref[0] = out_t.T.astype(o_ref.dtype)


def _nsa_attention(q, kc, vc, ks, vs, kw, vw, ng):
    B, H, dh, S = q.shape
    tq = min(ATT_TQ, S)
    tk = min(ATT_TK, S)
    assert S % tk == 0 and tk % tq == 0 and S >= WINDOW + tq
    n_cmp = kc.shape[2]
    n_slc = S // SLC_BLOCK
    c0 = CMP_STRIDE * np.arange(n_cmp)[None, :]
    s0 = SLC_BLOCK * np.arange(n_slc)[:, None]
    ov = np.clip(np.minimum(c0 + CMP_BLOCK, s0 + SLC_BLOCK) - np.maximum(c0, s0), 0, None) / CMP_BLOCK
    ovt = jnp.asarray(ov, BF16)
    grid = (B, NSA_KV_GROUPS, S // tq)
    k_spec = lambda n: pl.BlockSpec((1, 1, n, dh), lambda b, g, i: (b, g, 0, 0))
    vt_spec = lambda n: pl.BlockSpec((1, 1, dh, n), lambda b, g, i: (b, g, 0, 0))
    return pl.pallas_call(
        functools.partial(_nsa_body, tq=tq, tk=tk, n_slc=n_slc),
        grid=grid,
        in_specs=[
            pl.BlockSpec((1, NSA_HPG, dh, tq), lambda b, g, i: (b, g, 0, i)),
            k_spec(n_cmp), vt_spec(n_cmp), k_spec(S), vt_spec(S), k_spec(S), vt_spec(S),
            pl.BlockSpec((1, LANES, tq), lambda b, g, i: (b, 0, i)),
            pl.BlockSpec(ovt.shape, lambda b, g, i: (0, 0)),
        ],
        out_specs=pl.BlockSpec((1, tq, NSA_HPG * dh), lambda b, g, i: (b, i, g)),
        out_shape=jax.ShapeDtypeStruct((B, S, H * dh), BF16),
        scratch_shapes=[pltpu.VMEM((n_slc, NSA_HPG * tq), F32)],
        compiler_params=pltpu.CompilerParams(
            dimension_semantics=("parallel", "parallel", "arbitrary"), vmem_limit_bytes=VMEM_LIMIT),
        name="nsa_attn",
    )(q, kc, vc, ks, vs, kw, vw, ng, ovt)


def _gla_body(q_ref, k_ref, v_ref, vt_ref, ga_ref, r_ref, wa_ref, ba_ref, ng_ref, tri_ref,
              o_ref, state_ref, *, n_chunks):
    C, SUB = GLA_C, GLA_SUB
    state_ref[...] = jnp.zeros_like(state_ref)
    wa = wa_ref[0]
    ba = ba_ref[0]
    tri = tri_ref[...]
    t_loc = lax.broadcasted_iota(jnp.int32, (C, 1), 0)

    def chunk(c, _):
        c0 = pl.multiple_of(c * C, C)
        q = q_ref[0, 0, pl.ds(c0, C), :]
        k = k_ref[0, 0, pl.ds(c0, C), :]
        v = v_ref[0, pl.ds(c0, C), :]
        vt = vt_ref[0, :, pl.ds(c0, C)]
        la = jax.nn.log_sigmoid(_dot_f32(ga_ref[0, pl.ds(c0, C), :], wa) + ba) / GLA_TAU
        b = _dot_exact_lhs(tri, la)
        b_last = b[C - 1:C, :]
        st = state_ref[...]
        o = _dot_nt((q * jnp.exp(b)).astype(BF16), st.astype(BF16))
        o_sub = [o[i * SUB:(i + 1) * SUB] for i in range(C // SUB)]
        for i in range(1, C // SUB):
            r0 = i * SUB
            b_ref0 = b[r0:r0 + 1, :]
            qt = q[r0:r0 + SUB] * jnp.exp(b[r0:r0 + SUB] - b_ref0)
            kt = k[:r0] * jnp.exp(b_ref0 - b[:r0])
            a = _dot_nt(qt.astype(BF16), kt.astype(BF16))
            lt = lax.broadcasted_iota(jnp.int32, (SUB, r0), 0)
            ls = lax.broadcasted_iota(jnp.int32, (SUB, r0), 1)
            a = jnp.where(lt + (r0 - SUB) >= ls, a, 0.0)
            o_sub[i] = o_sub[i] + _dot(a.astype(BF16), v[:r0])
        o = jnp.concatenate(o_sub, axis=0)
        vf = v.astype(F32)
        for d in range(SUB):
            if d == 0:
                kd, bd, vd = k, b, vf
            else:
                kd, bd, vd = pltpu.roll(k, d, 0), pltpu.roll(b, d, 0), pltpu.roll(vf, d, 0)
            valid = t_loc >= d
            w = jnp.exp(jnp.where(valid, b - bd, 0.0))
            a_d = jnp.sum(jnp.where(valid, q * kd * w, 0.0), axis=-1, keepdims=True)
            o = o + a_d * vd
        k_dec = (k * jnp.exp(b_last - b)).astype(BF16)
        state_ref[...] = st * jnp.exp(b_last) + _dot(vt, k_dec)
        o = o * lax.rsqrt(jnp.mean(o * o, axis=-1, keepdims=True) + EPS) * ng_ref[0]
        o_ref[0, pl.ds(c0, C), :] = (o * jax.nn.silu(r_ref[0, pl.ds(c0, C), :])).astype(o_ref.dtype)
        return 0

    lax.fori_loop(0, n_chunks, chunk, 0)


def _gla(gq, gk, gv, gvt, ga, gr, w_a2, b_a, norm_g):
    B, H, S, dk = gq.shape
    dv = GLA_DV
    wa = jnp.pad(w_a2, ((0, LANES - GLA_RANK), (0, 0))).reshape(LANES, H, dk).transpose(1, 0, 2)
    ba = b_a.reshape(H, 1, dk)
    ng = norm_g.reshape(H, 1, dv)
    tri = jnp.asarray(np.tril(np.ones((GLA_C, GLA_C))), BF16)
    qk_spec = pl.BlockSpec((1, 1, S, dk), lambda b, h: (b, h, 0, 0))
    tok_spec = lambda w: pl.BlockSpec((1, S, w), lambda b, h: (b, 0, h))
    return pl.pallas_call(
        functools.partial(_gla_body, n_chunks=S // GLA_C),
        grid=(B, H),
        in_specs=[qk_spec, qk_spec, tok_spec(dv),
                  pl.BlockSpec((1, dv, S), lambda b, h: (b, h, 0)),
                  pl.BlockSpec((1, S, LANES), lambda b, h: (b, 0, 0)),
                  tok_spec(dv),
                  pl.BlockSpec((1, LANES, dk), lambda b, h: (h, 0, 0)),
                  pl.BlockSpec((1, 1, dk), lambda b, h: (h, 0, 0)),
                  pl.BlockSpec((1, 1, dv), lambda b, h: (h, 0, 0)),
                  pl.BlockSpec(tri.shape, lambda b, h: (0, 0))],
        out_specs=tok_spec(dv),
        out_shape=jax.ShapeDtypeStruct((B, S, H * dv), BF16),
        scratch_shapes=[pltpu.VMEM((dv, dk), F32)],
        compiler_params=pltpu.CompilerParams(
            dimension_semantics=("parallel", "parallel"), vmem_limit_bytes=VMEM_LIMIT),
        name="gla",
    )(gq, gk, gv, gvt, ga, gr, wa, ba, ng, tri)


def _merge_body(x_ref, ya_ref, yb_ref, ma_ref, mb_ref, wpa_ref, wpb_ref, wo_ref, gf_ref, wr_ref, br_ref,
                h_ref, v_ref, comb_ref):
    y_a = _dot(ya_ref[...], wpa_ref[...])
    y_b = _dot(yb_ref[...], wpb_ref[...])
    mixed = ma_ref[...] * y_a + mb_ref[...] * y_b
    h = x_ref[...] + _dot(mixed.astype(BF16), wo_ref[...])
    h_ref[...] = h
    v = h * lax.rsqrt(jnp.mean(h * h, axis=-1, keepdims=True) + EPS) * gf_ref[...]
    v_ref[...] = v.astype(BF16)
    logits = _dot_f32(v, wr_ref[...]) + br_ref[...]
    lane = lax.broadcasted_iota(jnp.int32, logits.shape, 1)
    is_grp = jnp.logical_and(lane >= MOE_EXPERTS, lane < MOE_EXPERTS + MOE_GROUPS)
    lg = jnp.where(is_grp, logits, NEG_INF)
    eg = jnp.where(is_grp, jnp.exp(lg - jnp.max(lg, axis=-1, keepdims=True)), 0.0)
    pg = eg / jnp.sum(eg, axis=-1, keepdims=True)
    p_grp = jnp.max(pg, axis=-1, keepdims=True)
    g_sel = jnp.min(jnp.where(jnp.logical_and(is_grp, pg == p_grp), lane, 2 * LANES),
                    axis=-1, keepdims=True) - MOE_EXPERTS
    in_grp = jnp.logical_and(lane < MOE_EXPERTS, lane // MOE_EPG == g_sel)
    le = jnp.where(in_grp, logits, NEG_INF)
    ee = jnp.where(in_grp, jnp.exp(le - jnp.max(le, axis=-1, keepdims=True)), 0.0)
    pin = ee / jnp.sum(ee, axis=-1, keepdims=True)
    p1 = jnp.max(jnp.where(in_grp, pin, -1.0), axis=-1, keepdims=True)
    i1 = jnp.min(jnp.where(jnp.logical_and(in_grp, pin == p1), lane, 2 * LANES), axis=-1, keepdims=True)
    rest = jnp.logical_and(in_grp, lane != i1)
    p2 = jnp.max(jnp.where(rest, pin, -1.0), axis=-1, keepdims=True)
    i2 = jnp.min(jnp.where(jnp.logical_and(rest, pin == p2), lane, 2 * LANES), axis=-1, keepdims=True)
    tot = p1 + p2
    comb_ref[...] = (jnp.where(lane == i1, p_grp * p1 / tot, 0.0)
                     + jnp.where(lane == i2, p_grp * p2 / tot, 0.0))


def _merge(x2, ya, yb, ma, mb, w_proj_nsa, w_proj_gla, w_out, g_ffn, w_grp, b_grp, w_exp, b_exp):
    T, D = x2.shape
    tm = min(MERGE_TM, T)
    wr = jnp.pad(jnp.concatenate([w_exp, w_grp], axis=1), ((0, 0), (0, LANES - MOE_EXPERTS - MOE_GROUPS)))
    br = jnp.pad(jnp.concatenate([b_exp, b_grp]), (0, LANES - MOE_EXPERTS - MOE_GROUPS)).reshape(1, LANES)
    tok = lambda w: pl.BlockSpec((tm, w), lambda i: (i, 0))
    full = lambda a: pl.BlockSpec(a.shape, lambda i: (0, 0))
    wpa, wpb, wo = w_proj_nsa.astype(BF16), w_proj_gla.astype(BF16), w_out.astype(BF16)
    gf = g_ffn.reshape(1, D)
    sds = jax.ShapeDtypeStruct
    return pl.pallas_call(
        _merge_body,
        grid=(T // tm,),
        in_specs=[tok(D), tok(ya.shape[1]), tok(yb.shape[1]), tok(D), tok(D),
                  full(wpa), full(wpb), full(wo), full(gf), full(wr), full(br)],
        out_specs=(tok(D), tok(D), tok(LANES)),
        out_shape=(sds((T, D), F32), sds((T, D), BF16), sds((T, LANES), F32)),
        compiler_params=pltpu.CompilerParams(
            dimension_semantics=("parallel",), vmem_limit_bytes=VMEM_LIMIT),
        name="merge",
    )(x2, ya, yb, ma, mb, wpa, wpb, wo, gf, wr, br)


def _moe_rows(tw):
    rows = 2 * tw + MOE_EXPERTS * (MOE_ALIGN - 1) + MOE_RT
    return -(-rows // MOE_RT) * MOE_RT


def _moe_body(v_ref, comb_ref, h_ref, wg_ref, wu_ref, wd_ref, gfin_ref, tri_ref, o_ref,
              xs_ref, z_ref, cw_ref, meta_ref, pos_ref, *, tw, rmax):
    e = pl.program_id(1)
    lane = lax.broadcasted_iota(jnp.int32, (1, LANES), 1)
    no_row = -1.0

    @pl.when(e == 0)
    def _route():
        comb = comb_ref[...]
        assigned = comb > 0.0
        a = jnp.where(assigned, 1.0, 0.0)
        tri = tri_ref[...]
        run = jnp.zeros((1, LANES), F32)
        ranks = []
        for b in range(tw // MOE_RT):
            ab = a[b * MOE_RT:(b + 1) * MOE_RT]
            ranks.append(_dot(tri, ab.astype(BF16)) + run)
            run = run + jnp.sum(ab, axis=0, keepdims=True)
        rank = jnp.concatenate(ranks, axis=0)
        cnt_pad = jnp.floor((run + (MOE_ALIGN - 1)) * (1.0 / MOE_ALIGN)) * MOE_ALIGN
        incl = jnp.broadcast_to(cnt_pad, (8, LANES))
        lane8 = lax.broadcasted_iota(jnp.int32, (8, LANES), 1)
        shift = 1
        while shift < MOE_EXPERTS:
            incl = incl + jnp.where(lane8 >= shift, pltpu.roll(incl, shift, 1), 0.0)
            shift *= 2
        offs = incl[0:1] - cnt_pad
        meta_ref[0:1, :] = offs
        meta_ref[1:2, :] = run
        row_of = offs + rank
        pos_a = jnp.min(jnp.where(assigned, row_of, 1e9), axis=-1, keepdims=True)
        pos_b = jnp.max(jnp.where(assigned, row_of, no_row), axis=-1, keepdims=True)
        pos_a = jnp.where(pos_a > 1e8, no_row, pos_a)
        pos_b = jnp.where(pos_b == pos_a, no_row, pos_b)
        lane_t = lax.broadcasted_iota(jnp.int32, (tw, LANES), 1)
        pos_ref[...] = jnp.where(lane_t == 0, pos_a, jnp.where(lane_t == 1, pos_b, no_row))
        pos_t = pos_ref[...].T
        pa, pb = pos_t[0:1], pos_t[1:2]
        c_split = jnp.concatenate(_split3(comb), axis=1)
        v = v_ref[...]
        for rt in range(rmax // MOE_RT):
            r = (rt * MOE_RT + lax.broadcasted_iota(jnp.int32, (MOE_RT, 1), 0)).astype(F32)
            p = jnp.where(r == pa, 1.0, jnp.where(r == pb, 1.0, 0.0)).astype(BF16)
            xs_ref[rt * MOE_RT:(rt + 1) * MOE_RT, :] = _dot(p, v).astype(BF16)
            cw = _dot(p, c_split)
            cw_ref[rt * MOE_RT:(rt + 1) * MOE_RT, :] = (cw[:, :LANES] + cw[:, LANES:2 * LANES]) + cw[:, 2 * LANES:]
        z_ref[...] = jnp.zeros_like(z_ref)

    pick = lambda row: jnp.sum(jnp.where(lane == e, meta_ref[row:row + 1, :], 0.0)).astype(jnp.int32)
    off_e, cnt_e = pick(0), pick(1)

    def row_tile(i, _):
        r0 = pl.multiple_of(off_e + i * MOE_ET, MOE_ALIGN)
        x = xs_ref[pl.ds(r0, MOE_ET), :]
        c = jnp.sum(jnp.where(lane == e, cw_ref[pl.ds(r0, MOE_ET), :], 0.0), axis=-1, keepdims=True)
        hdn = jax.nn.silu(_dot(x, wg_ref[0])) * _dot(x, wu_ref[0])
        z_ref[pl.ds(r0, MOE_ET), :] = _dot((c * hdn).astype(BF16), wd_ref[0]).astype(BF16)
        return 0

    lax.fori_loop(0, (cnt_e + MOE_ET - 1) // MOE_ET, row_tile, 0)

    @pl.when(e == pl.num_programs(1) - 1)
    def _combine():
        r = lax.broadcasted_iota(jnp.int32, (1, rmax), 1).astype(F32)
        z = z_ref[...]
        for tt in range(tw // MOE_RT):
            rows = slice(tt * MOE_RT, (tt + 1) * MOE_RT)
            pa, pb = pos_ref[rows, 0:1], pos_ref[rows, 1:2]
            q = jnp.where(r == pa, 1.0, jnp.where(r == pb, 1.0, 0.0)).astype(BF16)
            h = h_ref[rows, :] + _dot(q, z)
            o_ref[rows, :] = h * lax.rsqrt(jnp.mean(h * h, axis=-1, keepdims=True) + EPS) * gfin_ref[...]


def _moe(v, comb, h, w_gate, w_up, w_down, g_final):
    T, D = h.shape
    tw = min(MOE_TW, T)
    rmax = _moe_rows(tw)
    E, _, F = w_gate.shape
    wg, wu, wd = w_gate.astype(BF16), w_up.astype(BF16), w_down.astype(BF16)
    tri = jnp.asarray(np.tril(np.ones((MOE_RT, MOE_RT)), -1), BF16)
    tok = lambda w: pl.BlockSpec((tw, w), lambda i, e: (i, 0))
    return pl.pallas_call(
        functools.partial(_moe_body, tw=tw, rmax=rmax),
        grid=(T // tw, E),
        in_specs=[tok(D), tok(LANES), tok(D),
                  pl.BlockSpec((1, D, F), lambda i, e: (e, 0, 0)),
                  pl.BlockSpec((1, D, F), lambda i, e: (e, 0, 0)),
                  pl.BlockSpec((1, F, D), lambda i, e: (e, 0, 0)),
                  pl.BlockSpec((1, D), lambda i, e: (0, 0)),
                  pl.BlockSpec(tri.shape, lambda i, e: (0, 0))],
        out_specs=tok(D),
        out_shape=jax.ShapeDtypeStruct((T, D), F32),
        scratch_shapes=[pltpu.VMEM((rmax, D), BF16),
                        pltpu.VMEM((rmax, D), BF16),
                        pltpu.VMEM((rmax, LANES), F32),
                        pltpu.VMEM((8, LANES), F32),
                        pltpu.VMEM((tw, LANES), F32)],
        compiler_params=pltpu.CompilerParams(
            dimension_semantics=("parallel", "arbitrary"), vmem_limit_bytes=VMEM_LIMIT),
        name="moe",
    )(v, comb, h, wg, wu, wd, g_final.reshape(1, D), tri)


def _rope_tables(positions):
    half = NSA_DH // 2
    inv = 1.0 / (ROPE_THETA ** (jnp.arange(half, dtype=F32) / half))
    ang = positions.astype(F32)[..., None] * inv
    cos, sin = jnp.cos(ang), jnp.sin(ang)
    cs = jnp.concatenate([cos, cos, cos, cos], axis=-1)
    sn = jnp.concatenate([-sin, sin, -sin, sin], axis=-1)
    return cs, sn


def _layer(h, positions, g_mix, w_in, cmp_pos_k, cmp_w1_k, cmp_b1_k, cmp_w2_k, cmp_b2_k,
           cmp_pos_v, cmp_w1_v, cmp_b1_v, cmp_w2_v, cmp_b2_v, gla_w_a2, gla_b_a, gla_norm_g,
           w_proj_nsa, w_proj_gla, w_out, g_ffn, w_grp, b_grp, w_exp, b_exp, w_gate, w_up, w_down, g_out):
    B, S, D = h.shape
    cs, sn = _rope_tables(positions)
    n_chunks = S // CMP_STRIDE
    cmp_end = jnp.minimum(CMP_STRIDE * jnp.arange(n_chunks) + CMP_BLOCK - 1, S - 1)
    cs_c, sn_c = _rope_tables(jnp.take(positions, cmp_end, axis=1))
    (q, kvc, ks, vs, kw, vw, gq, gk, gv, gvt, gr, ma, mb, ng, ga) = _in_proj(h, g_mix, w_in, cs, sn)
    kc, vc = _compress(kvc, cs_c, sn_c, cmp_pos_k, cmp_w1_k, cmp_b1_k, cmp_w2_k, cmp_b2_k,
                       cmp_pos_v, cmp_w1_v, cmp_b1_v, cmp_w2_v, cmp_b2_v)
    ya = _nsa_attention(q, kc, vc, ks, vs, kw, vw, ng)
    yb = _gla(gq, gk, gv, gvt, ga, gr, gla_w_a2, gla_b_a, gla_norm_g)
    T = B * S
    h1, v, comb = _merge(h.reshape(T, D), ya.reshape(T, -1), yb.reshape(T, -1), ma.reshape(T, D), mb.reshape(T, D),
                         w_proj_nsa, w_proj_gla, w_out, g_ffn, w_grp, b_grp, w_exp, b_exp)
    return _moe(v, comb, h1, w_gate, w_up, w_down, g_out).reshape(B, S, D)


def kernel(x, positions, g_mix, w_in, cmp_pos_k, cmp_w1_k, cmp_b1_k, cmp_w2_k, cmp_b2_k, cmp_pos_v, cmp_w1_v,
           cmp_b1_v, cmp_w2_v, cmp_b2_v, gla_w_a2, gla_b_a, gla_norm_g, w_proj_nsa, w_proj_gla, w_out, g_ffn,
           w_grp, b_grp, w_exp, b_exp, w_gate, w_up, w_down, g_final):
    depth = g_mix.shape[0]
    assert depth == 1, "the final norm is fused into the single layer's expert kernel"
    return _layer(x, positions, g_mix[0], w_in[0], cmp_pos_k[0], cmp_w1_k[0], cmp_b1_k[0], cmp_w2_k[0],
                  cmp_b2_k[0], cmp_pos_v[0], cmp_w1_v[0], cmp_b1_v[0], cmp_w2_v[0], cmp_b2_v[0],
                  gla_w_a2[0], gla_b_a[0], gla_norm_g[0], w_proj_nsa[0], w_proj_gla[0], w_out[0],
                  g_ffn[0], w_grp[0], b_grp[0], w_exp[0], b_exp[0], w_gate[0], w_up[0], w_down[0], g_final)
```

```python
import functools

import numpy as np
import jax
import jax.numpy as jnp
from jax import lax
from jax.experimental import pallas as pl
from jax.experimental.pallas import tpu as pltpu

F32 = jnp.float32
BF16 = jnp.bfloat16

NSA_HEADS = 8
NSA_KV_GROUPS = 2
NSA_HPG = NSA_HEADS // NSA_KV_GROUPS
NSA_DH = 64
CMP_BLOCK = 32
CMP_STRIDE = 16
CMP_HIDDEN = 256
SLC_BLOCK = 64
SLC_TOPK = 16
WINDOW = 512
GLA_HEADS = 4
GLA_DK = 64
GLA_DV = 128
GLA_RANK = 16
GLA_TAU = 16.0
MOE_GROUPS = 4
MOE_EPG = 8
MOE_EXPERTS = MOE_GROUPS * MOE_EPG
MOE_DFF = 512
ROPE_THETA = 10000.0
EPS = 1e-6
NEG_INF = -1e30
FORCED_SCORE = 1e4

LANES = 128
VMEM_LIMIT = 56 * 1024 * 1024

IN_TM = 512
ATT_TQ = 128
ATT_TK = 512
GLA_C = 128
GLA_SUB = 16
GLA_HP = 2
GLA_UNROLL_DECAY = 4
GLA_UNROLL_PLAIN = 4
GLA_PLAIN_DECAY = 60.0
MERGE_TM = 512
MOE_TW = 1024
MOE_RT = 256
MOE_ET = 128
MOE_ALIGN = 16


def _dot(a, b):
    return jnp.dot(a, b, preferred_element_type=F32)


def _dot_nt(a, b):
    return lax.dot_general(a, b, (((1,), (1,)), ((), ())), preferred_element_type=F32)


def _split3(x):
    x1 = x.astype(BF16)
    r1 = x - x1.astype(F32)
    x2 = r1.astype(BF16)
    r2 = r1 - x2.astype(F32)
    x3 = r2.astype(BF16)
    return x1, x2, x3


def _dot_exact_lhs(a_bf16, x):
    x1, x2, x3 = _split3(x)
    return _dot(a_bf16, x1) + _dot(a_bf16, x2) + _dot(a_bf16, x3)


def _dot_f32(a, b):
    a1, a2, a3 = _split3(a)
    b1, b2, b3 = _split3(b)
    return (_dot(a1, b1) + (_dot(a1, b2) + _dot(a2, b1))
            + (_dot(a1, b3) + _dot(a2, b2) + _dot(a3, b1)))


def _rope_lanes(z, cs, sn):
    w = z.shape[-1]
    lane = lax.broadcasted_iota(jnp.int32, z.shape, 1)
    first_half = (lane % NSA_DH) < (NSA_DH // 2)
    rot = jnp.where(first_half, pltpu.roll(z, w - NSA_DH // 2, 1), pltpu.roll(z, NSA_DH // 2, 1))
    reps = w // LANES
    if reps > 1:
        cs = jnp.concatenate([cs] * reps, axis=1)
        sn = jnp.concatenate([sn] * reps, axis=1)
    return z * cs + rot * sn


_SEC = {}
_off = 0
for _name, _w in (("q", 512), ("kvc", 256), ("ks", 128), ("vs", 128), ("kw", 128), ("vw", 128),
                  ("gq", 256), ("gk", 256), ("gv", 512), ("gr", 512), ("ma", 1024), ("mb", 1024),
                  ("ng", 128), ("ga", 128)):
    _SEC[_name] = (_off, _off + _w)
    _off += _w
IN_NW = _off


def _in_proj_body(x_ref, g_ref, w_ref, cs_ref, sn_ref,
                  q_ref, kvc_ref, ks_ref, vs_ref, kw_ref, vw_ref,
                  gq_ref, gk_ref, gv_ref, gvt_ref, gr_ref, ma_ref, mb_ref, ng_ref, ga_ref):
    x = x_ref[0]
    var = jnp.mean(x * x, axis=-1, keepdims=True)
    u = (x * lax.rsqrt(var + EPS) * g_ref[...]).astype(BF16)
    cs = cs_ref[0]
    sn = sn_ref[0]

    def proj(name):
        a, b = _SEC[name]
        return _dot(u, w_ref[:, a:b])

    zq_t = (_rope_lanes(proj("q"), cs, sn) * (NSA_DH ** -0.5)).T
    for h in range(NSA_HEADS):
        q_ref[0, h] = zq_t[h * NSA_DH:(h + 1) * NSA_DH].astype(BF16)
    kvc_ref[0] = proj("kvc")
    zks = _rope_lanes(proj("ks"), cs, sn)
    zkw = _rope_lanes(proj("kw"), cs, sn)
    zvs_t = proj("vs").T
    zvw_t = proj("vw").T
    for g in range(NSA_KV_GROUPS):
        sl = slice(g * NSA_DH, (g + 1) * NSA_DH)
        ks_ref[0, g] = zks[:, sl].astype(BF16)
        kw_ref[0, g] = zkw[:, sl].astype(BF16)
        vs_ref[0, g] = zvs_t[sl].astype(BF16)
        vw_ref[0, g] = zvw_t[sl].astype(BF16)
    gq_ref[0] = proj("gq") * (GLA_DK ** -0.5)
    gk_ref[0] = proj("gk")
    zgv = proj("gv")
    gv_ref[0] = zgv.astype(BF16)
    gvt_ref[0] = zgv.T.astype(BF16)
    gr_ref[0] = proj("gr")
    ma_ref[0] = jax.nn.sigmoid(proj("ma")).astype(ma_ref.dtype)
    mb_ref[0] = jax.nn.sigmoid(proj("mb")).astype(mb_ref.dtype)
    ng_ref[0] = jax.nn.sigmoid(proj("ng")).T
    ga_ref[0] = proj("ga")


def _in_proj(x, g_mix, w_in, cs, sn):
    B, S, D = x.shape
    tm = min(IN_TM, S)
    splits = np.cumsum((512,) + (128,) * 6 + (24, 256, 256, 512, 16, 512, 1024, 1024))
    (wq, wkc, wvc, wks, wvs, wkw, wvw, wng, wgq, wgk, wgv, wga, wgr, wma, wmb) = jnp.split(
        w_in, splits[:-1].tolist(), axis=1)
    pad = lambda w: jnp.pad(w, ((0, 0), (0, LANES - w.shape[1])))
    w_all = jnp.concatenate([wq, wkc, wvc, wks, wvs, wkw, wvw, wgq, wgk, wgv, wgr, wma, wmb,
                             pad(wng), pad(wga)], axis=1).astype(BF16)
    assert w_all.shape[1] == IN_NW
    grid = (B, S // tm)
    tok = lambda w: pl.BlockSpec((1, tm, w), lambda b, i: (b, i, 0))
    head = lambda n, w: pl.BlockSpec((1, n, tm, w), lambda b, i: (b, 0, i, 0))
    head_t = lambda n, w: pl.BlockSpec((1, n, w, tm), lambda b, i: (b, 0, 0, i))
    sds = jax.ShapeDtypeStruct
    out_shape = (
        sds((B, NSA_HEADS, NSA_DH, S), BF16),
        sds((B, S, 256), F32),
        sds((B, NSA_KV_GROUPS, S, NSA_DH), BF16),
        sds((B, NSA_KV_GROUPS, NSA_DH, S), BF16),
        sds((B, NSA_KV_GROUPS, S, NSA_DH), BF16),
        sds((B, NSA_KV_GROUPS, NSA_DH, S), BF16),
        sds((B, S, GLA_HEADS * GLA_DK), F32),
        sds((B, S, GLA_HEADS * GLA_DK), F32),
        sds((B, S, GLA_HEADS * GLA_DV), BF16),
        sds((B, GLA_HEADS * GLA_DV, S), BF16),
        sds((B, S, GLA_HEADS * GLA_DV), F32),
        sds((B, S, D), BF16),
        sds((B, S, D), BF16),
        sds((B, LANES, S), F32),
        sds((B, S, LANES), F32),
    )
    out_specs = (
        head_t(NSA_HEADS, NSA_DH), tok(256),
        head(NSA_KV_GROUPS, NSA_DH), head_t(NSA_KV_GROUPS, NSA_DH),
        head(NSA_KV_GROUPS, NSA_DH), head_t(NSA_KV_GROUPS, NSA_DH),
        tok(GLA_HEADS * GLA_DK), tok(GLA_HEADS * GLA_DK),
        tok(GLA_HEADS * GLA_DV),
        pl.BlockSpec((1, GLA_HEADS * GLA_DV, tm), lambda b, i: (b, 0, i)),
        tok(GLA_HEADS * GLA_DV), tok(D), tok(D),
        pl.BlockSpec((1, LANES, tm), lambda b, i: (b, 0, i)), tok(LANES),
    )
    return pl.pallas_call(
        _in_proj_body,
        grid=grid,
        in_specs=[
            tok(D),
            pl.BlockSpec((1, D), lambda b, i: (0, 0)),
            pl.BlockSpec((D, IN_NW), lambda b, i: (0, 0), pipeline_mode=pl.Buffered(1)),
            tok(LANES), tok(LANES),
        ],
        out_specs=out_specs,
        out_shape=out_shape,
        compiler_params=pltpu.CompilerParams(
            dimension_semantics=("parallel", "parallel"), vmem_limit_bytes=VMEM_LIMIT),
        name="in_proj",
    )(x, g_mix.reshape(1, D), w_all, cs, sn)


def _compress_body(x_ref, p0_ref, p1_ref, w0_ref, w1_ref, b1_ref, w2_ref, b2_ref, cs_ref, sn_ref,
                   kc_ref, vc_ref):
    x = x_ref[0]
    n = x.shape[0]
    y0 = _dot((x + p0_ref[...]).astype(BF16), w0_ref[...])
    y1 = _dot((x + p1_ref[...]).astype(BF16), w1_ref[...])
    h = jax.nn.gelu(y0 + pltpu.roll(y1, n - 1, 0) + b1_ref[...])
    o = _dot(h.astype(BF16), w2_ref[...]) + b2_ref[...]
    k = _rope_lanes(o[:, :LANES], cs_ref[0], sn_ref[0])
    v_t = o[:, LANES:].T
    for g in range(NSA_KV_GROUPS):
        sl = slice(g * NSA_DH, (g + 1) * NSA_DH)
        kc_ref[0, g] = k[:, sl].astype(BF16)
        vc_ref[0, g] = v_t[sl].astype(BF16)


def _compress(kvc, cs_c, sn_c, pos_k, w1_k, b1_k, w2_k, b2_k, pos_v, w1_v, b1_v, w2_v, b2_v):
    B, S, _ = kvc.shape
    n = S // CMP_STRIDE
    ns = 2 * NSA_KV_GROUPS
    x = kvc.reshape(B, n, CMP_STRIDE * ns * NSA_DH)
    eye = jnp.eye(ns, dtype=F32)
    w1s = jnp.stack([w1_k, w1_k, w1_v, w1_v])
    poss = jnp.stack([pos_k, pos_k, pos_v, pos_v])
    wbig, pbig = [], []
    for m in range(CMP_BLOCK // CMP_STRIDE):
        wm = w1s.reshape(ns, CMP_BLOCK, NSA_DH, CMP_HIDDEN)[:, m * CMP_STRIDE:(m + 1) * CMP_STRIDE]
        wb = jnp.einsum('ctdj,ce->tcdej', wm, eye).reshape(CMP_STRIDE * ns * NSA_DH, ns * CMP_HIDDEN)
        wbig.append(wb.astype(BF16))
        pm = poss[:, m * CMP_STRIDE:(m + 1) * CMP_STRIDE]
        pbig.append(pm.transpose(1, 0, 2).reshape(1, CMP_STRIDE * ns * NSA_DH))
    b1 = jnp.concatenate([b1_k, b1_k, b1_v, b1_v]).reshape(1, ns * CMP_HIDDEN)
    w2s = jnp.stack([w2_k, w2_k, w2_v, w2_v])
    w2big = jnp.einsum('cjd,ce->cjed', w2s, eye).reshape(ns * CMP_HIDDEN, ns * NSA_DH).astype(BF16)
    b2 = jnp.concatenate([b2_k, b2_k, b2_v, b2_v]).reshape(1, ns * NSA_DH)
    full = lambda a: pl.BlockSpec(a.shape, lambda b: (0,) * a.ndim)
    sds = jax.ShapeDtypeStruct
    return pl.pallas_call(
        _compress_body,
        grid=(B,),
        in_specs=[pl.BlockSpec((1, n, x.shape[2]), lambda b: (b, 0, 0)),
                  full(pbig[0]), full(pbig[1]), full(wbig[0]), full(wbig[1]), full(b1), full(w2big), full(b2),
                  pl.BlockSpec((1, n, LANES), lambda b: (b, 0, 0)),
                  pl.BlockSpec((1, n, LANES), lambda b: (b, 0, 0))],
        out_specs=(pl.BlockSpec((1, NSA_KV_GROUPS, n, NSA_DH), lambda b: (b, 0, 0, 0)),
                   pl.BlockSpec((1, NSA_KV_GROUPS, NSA_DH, n), lambda b: (b, 0, 0, 0))),
        out_shape=(sds((B, NSA_KV_GROUPS, n, NSA_DH), BF16), sds((B, NSA_KV_GROUPS, NSA_DH, n), BF16)),
        compiler_params=pltpu.CompilerParams(
            dimension_semantics=("parallel",), vmem_limit_bytes=VMEM_LIMIT),
        name="compress",
    )(x, pbig[0], pbig[1], wbig[0], wbig[1], b1, w2big, b2, cs_c, sn_c)


def _softmax_step(s, m, l, acc, v_t):
    m_new = jnp.maximum(m, jnp.max(s, axis=0, keepdims=True))
    alpha = jnp.exp(m - m_new)
    p = jnp.exp(s - m_new)
    l = alpha * l + jnp.sum(p, axis=0, keepdims=True)
    acc = alpha * acc + _dot(v_t, p.astype(BF16))
    return m_new, l, acc


def _nsa_body(q_ref, kc_ref, vc_ref, ks_ref, vs_ref, kw_ref, vw_ref, ng_ref, ovt_ref,
              o_ref, bias_ref, *, tq, tk, n_slc):
    gi = pl.program_id(1)
    qi = pl.program_id(2)
    q0 = qi * tq
    n_cmp = kc_ref.shape[2]
    W = NSA_HPG * tq
    q_all = jnp.concatenate([q_ref[0, h] for h in range(NSA_HPG)], axis=1)
    t_lane = jnp.concatenate([q0 + lax.broadcasted_iota(jnp.int32, (1, tq), 1)] * NSA_HPG, axis=1)

    cmp_end = CMP_STRIDE * lax.broadcasted_iota(jnp.int32, (n_cmp, 1), 0) + (CMP_BLOCK - 1)
    ok = cmp_end <= t_lane
    sm = jnp.where(ok, _dot(kc_ref[0, 0], q_all), NEG_INF)
    e = jnp.exp(sm - jnp.max(sm, axis=0, keepdims=True))
    p = jnp.where(ok, e * (1.0 / jnp.sum(e, axis=0, keepdims=True)), 0.0)
    o_cmp = _dot(vc_ref[0, 0], p.astype(BF16))
    p_sum = p[:, 0:tq]
    for h in range(1, NSA_HPG):
        p_sum = p_sum + p[:, h * tq:(h + 1) * tq]

    imp = _dot_exact_lhs(ovt_ref[...], p_sum)
    blk = lax.broadcasted_iota(jnp.int32, (n_slc, tq), 0)
    cur = (q0 + lax.broadcasted_iota(jnp.int32, (n_slc, tq), 1)) // SLC_BLOCK
    forced = (blk == 0) | (blk == cur) | (blk == cur - 1)
    imp = jnp.where(forced, FORCED_SCORE, imp)
    imp = jnp.where(blk <= cur, imp, NEG_INF)
    rank = jnp.zeros((n_slc, tq), F32)
    for i in range(n_slc):
        row = imp[i:i + 1, :]
        rank = rank + jnp.where(row > imp, 1.0, 0.0)
        rank = rank + jnp.where(jnp.logical_and(row == imp, blk > i), 1.0, 0.0)
    bias1 = jnp.where(rank < float(min(SLC_TOPK, n_slc)), 0.0, NEG_INF)
    bias_ref[...] = jnp.concatenate([bias1] * NSA_HPG, axis=1)

    kpos_l = lax.broadcasted_iota(jnp.int32, (tk, 1), 0)
    blocks_per_tile = tk // SLC_BLOCK

    def slc_step(jt, carry):
        k0 = pl.multiple_of(jt * tk, tk)
        bias = jnp.concatenate(
            [jnp.broadcast_to(bias_ref[pl.ds(jt * blocks_per_tile + r, 1), :], (SLC_BLOCK, W))
             for r in range(blocks_per_tile)], axis=0)
        bias = jnp.where(k0 + kpos_l <= t_lane, bias, NEG_INF)
        s = _dot(ks_ref[0, 0, pl.ds(k0, tk), :], q_all) + bias
        return _softmax_step(s, *carry, vs_ref[0, 0, :, pl.ds(k0, tk)])

    init = (jnp.full((1, W), NEG_INF, F32), jnp.zeros((1, W), F32), jnp.zeros((NSA_DH, W), F32))
    _, l_s, acc_s = lax.fori_loop(0, (q0 + tq + tk - 1) // tk, slc_step, init)

    wk = WINDOW + tq
    w0 = pl.multiple_of(jnp.maximum(q0 - WINDOW, 0), tq)
    kpos = w0 + lax.broadcasted_iota(jnp.int32, (wk, 1), 0)
    okw = jnp.logical_and(kpos <= t_lane, kpos > t_lane - WINDOW)
    sw = jnp.where(okw, _dot(kw_ref[0, 0, pl.ds(w0, wk), :], q_all), NEG_INF)
    ew = jnp.exp(sw - jnp.max(sw, axis=0, keepdims=True))
    l_w = jnp.sum(ew, axis=0, keepdims=True)
    acc_w = _dot(vw_ref[0, 0, :, pl.ds(w0, wk)], ew.astype(BF16))

    def gate(j):
        return jnp.concatenate([ng_ref[0, pl.ds((gi * NSA_HPG + h) * 3 + j, 1), :]
                                for h in range(NSA_HPG)], axis=1)

    out_t = gate(0) * o_cmp + (gate(1) * (1.0 / l_s)) * acc_s + (gate(2) * (1.0 / l_w)) * acc_w
    out_t = jnp.concatenate([out_t[:, h * tq:(h + 1) * tq] for h in range(NSA_HPG)], axis=0)
    o_ref[0] = out_t.T.astype(o_ref.dtype)


def _nsa_attention(q, kc, vc, ks, vs, kw, vw, ng):
    B, H, dh, S = q.shape
    tq = min(ATT_TQ, S)
    tk = min(ATT_TK, S)
    assert S % tk == 0 and tk % tq == 0 and S >= WINDOW + tq
    n_cmp = kc.shape[2]
    n_slc = S // SLC_BLOCK
    c0 = CMP_STRIDE * np.arange(n_cmp)[None, :]
    s0 = SLC_BLOCK * np.arange(n_slc)[:, None]
    ov = np.clip(np.minimum(c0 + CMP_BLOCK, s0 + SLC_BLOCK) - np.maximum(c0, s0), 0, None) / CMP_BLOCK
    ovt = jnp.asarray(ov, BF16)
    grid = (B, NSA_KV_GROUPS, S // tq)
    k_spec = lambda n: pl.BlockSpec((1, 1, n, dh), lambda b, g, i: (b, g, 0, 0))
    vt_spec = lambda n: pl.BlockSpec((1, 1, dh, n), lambda b, g, i: (b, g, 0, 0))
    return pl.pallas_call(
        functools.partial(_nsa_body, tq=tq, tk=tk, n_slc=n_slc),
        grid=grid,
        in_specs=[
            pl.BlockSpec((1, NSA_HPG, dh, tq), lambda b, g, i: (b, g, 0, i)),
            k_spec(n_cmp), vt_spec(n_cmp), k_spec(S), vt_spec(S), k_spec(S), vt_spec(S),
            pl.BlockSpec((1, LANES, tq), lambda b, g, i: (b, 0, i)),
            pl.BlockSpec(ovt.shape, lambda b, g, i: (0, 0)),
        ],
        out_specs=pl.BlockSpec((1, tq, NSA_HPG * dh), lambda b, g, i: (b, i, g)),
        out_shape=jax.ShapeDtypeStruct((B, S, H * dh), BF16),
        scratch_shapes=[pltpu.VMEM((n_slc, NSA_HPG * tq), F32)],
        compiler_params=pltpu.CompilerParams(
            dimension_semantics=("parallel", "parallel", "arbitrary"), vmem_limit_bytes=VMEM_LIMIT),
        name="nsa_attn",
    )(q, kc, vc, ks, vs, kw, vw, ng, ovt)


def _dot_3pass(a, b):
    a1 = a.astype(BF16)
    a2 = (a - a1.astype(F32)).astype(BF16)
    b1 = b.astype(BF16)
    b2 = (b - b1.astype(F32)).astype(BF16)
    return _dot(a1, b1) + (_dot(a1, b2) + _dot(a2, b1))


def _gla_pair_body(q_ref, k_ref, v_ref, vt_ref, ga_ref, r_ref, wa_ref, ba_ref, ng_ref, tri_ref,
                   o_ref, state_ref, b_ref, kp_ref, bp_ref, vp_ref, plain_ref, *, n_chunks):
    C, SUB, dk, dv, hp = GLA_C, GLA_SUB, GLA_DK, GLA_DV, GLA_HP
    W = hp * dk
    lane = lax.broadcasted_iota(jnp.int32, (1, W), 1)
    of_head = [lane // dk == hh for hh in range(hp)]
    tri = tri_ref[...]
    t_loc = lax.broadcasted_iota(jnp.int32, (C, 1), 0)
    state_ref[...] = jnp.zeros_like(state_ref)
    kp_ref[0:SUB, :] = jnp.zeros((SUB, W), F32)
    bp_ref[0:SUB, :] = jnp.zeros((SUB, W), F32)
    vp_ref[:, 0:SUB, :] = jnp.zeros((hp, SUB, dv), F32)

    def decay_chunk(c, all_plain):
        c0 = pl.multiple_of(c * C, C)
        la = jax.nn.log_sigmoid(_dot_3pass(ga_ref[0, pl.ds(c0, C), :], wa_ref[0]) + ba_ref[0]) / GLA_TAU
        b = _dot_exact_lhs(tri, la)
        b_ref[pl.ds(c0, C), :] = b
        plain = (jnp.min(b[C - 1:C, :]) > -GLA_PLAIN_DECAY).astype(jnp.int32)
        plain_ref[c] = plain
        return jnp.minimum(all_plain, plain)

    all_plain = lax.fori_loop(0, n_chunks, decay_chunk, jnp.int32(1), unroll=GLA_UNROLL_DECAY)

    def chunk(c, check_decay):
        c0 = pl.multiple_of(c * C, C)
        b = b_ref[pl.ds(c0, C), :]
        q = q_ref[0, pl.ds(c0, C), :]
        k = k_ref[0, pl.ds(c0, C), :]
        v = [v_ref[0, pl.ds(c0, C), hh * dv:(hh + 1) * dv] for hh in range(hp)]
        vt = [vt_ref[0, hh * dv:(hh + 1) * dv, pl.ds(c0, C)] for hh in range(hp)]
        b_last = b[C - 1:C, :]
        st = state_ref[...]
        st_b = st.astype(BF16)
        qg = q * jnp.exp(b)
        qg_h = [jnp.where(of_head[hh], qg, 0.0).astype(BF16) for hh in range(hp)]
        o_inter = [_dot_nt(qg_h[hh], st_b) for hh in range(hp)]

        def intra_plain():
            ke = (k * jnp.exp(-b)).astype(BF16)
            row = lax.broadcasted_iota(jnp.int32, (C, C), 0)
            col = lax.broadcasted_iota(jnp.int32, (C, C), 1)
            return tuple(_dot(jnp.where(row >= col, _dot_nt(qg_h[hh], ke), 0.0).astype(BF16), v[hh])
                         for hh in range(hp))

        def intra_strong_decay():
            far = [[jnp.zeros((SUB, dv), F32)] for _ in range(hp)]
            for i in range(1, C // SUB):
                r0 = i * SUB
                b_first = b[r0:r0 + 1, :]
                qt = q[r0:r0 + SUB] * jnp.exp(b[r0:r0 + SUB] - b_first)
                kt = (k[:r0] * jnp.exp(b_first - b[:r0])).astype(BF16)
                lt = lax.broadcasted_iota(jnp.int32, (SUB, r0), 0)
                ls = lax.broadcasted_iota(jnp.int32, (SUB, r0), 1)
                for hh in range(hp):
                    a = _dot_nt(jnp.where(of_head[hh], qt, 0.0).astype(BF16), kt)
                    a = jnp.where(lt + (r0 - SUB) >= ls, a, 0.0)
                    far[hh].append(_dot(a.astype(BF16), v[hh][:r0]))
            acc = [jnp.concatenate(far[hh], axis=0) for hh in range(hp)]
            kp_ref[SUB:SUB + C, :] = k
            bp_ref[SUB:SUB + C, :] = b
            for hh in range(hp):
                vp_ref[hh, SUB:SUB + C, :] = v[hh].astype(F32)
            for d in range(SUB):
                kd = kp_ref[SUB - d:SUB - d + C, :]
                bd = bp_ref[SUB - d:SUB - d + C, :]
                valid = t_loc >= d
                w = jnp.exp(jnp.where(valid, b - bd, 0.0))
                x = jnp.where(valid, q * kd * w, 0.0)
                for hh in range(hp):
                    a_d = jnp.sum(jnp.where(of_head[hh], x, 0.0), axis=-1, keepdims=True)
                    acc[hh] = acc[hh] + a_d * vp_ref[hh, SUB - d:SUB - d + C, :]
            return tuple(acc)

        if check_decay:
            o_intra = lax.cond(plain_ref[c] > 0, intra_plain, intra_strong_decay)
        else:
            o_intra = intra_plain()
        k_dec = (k * jnp.exp(b_last - b)).astype(BF16)
        upd = _dot(vt[hp - 1], k_dec)
        for hh in range(hp - 2, -1, -1):
            upd = jnp.where(of_head[hh], _dot(vt[hh], k_dec), upd)
        state_ref[...] = st * jnp.exp(b_last) + upd
        for hh in range(hp):
            o = o_inter[hh] + o_intra[hh]
            o = o * lax.rsqrt(jnp.mean(o * o, axis=-1, keepdims=True) + EPS) * ng_ref[hh]
            gate = jax.nn.silu(r_ref[0, pl.ds(c0, C), hh * dv:(hh + 1) * dv])
            o_ref[0, pl.ds(c0, C), hh * dv:(hh + 1) * dv] = (o * gate).astype(o_ref.dtype)
        return 0

    @pl.when(all_plain > 0)
    def _():
        lax.fori_loop(0, n_chunks, lambda c, _: chunk(c, False), 0, unroll=GLA_UNROLL_PLAIN)

    @pl.when(all_plain <= 0)
    def _():
        lax.fori_loop(0, n_chunks, lambda c, _: chunk(c, True), 0)


def _gla_pairs(gq, gk, gv, gvt, ga, gr, w_a2, b_a, norm_g):
    B, S, _ = gq.shape
    H, dk, dv, hp = GLA_HEADS, GLA_DK, GLA_DV, GLA_HP
    W = hp * dk
    assert W == LANES and H % hp == 0
    wa = jnp.pad(w_a2, ((0, LANES - GLA_RANK), (0, 0))).reshape(LANES, H // hp, W).transpose(1, 0, 2)
    ba = b_a.reshape(H // hp, 1, W)
    ng = norm_g.reshape(H, 1, dv)
    tri = jnp.asarray(np.tril(np.ones((GLA_C, GLA_C))), BF16)
    n_chunks = S // GLA_C
    tok_spec = lambda w: pl.BlockSpec((1, S, hp * w), lambda b, h: (b, 0, h))
    return pl.pallas_call(
        functools.partial(_gla_pair_body, n_chunks=n_chunks),
        grid=(B, H // hp),
        in_specs=[tok_spec(dk), tok_spec(dk), tok_spec(dv),
                  pl.BlockSpec((1, hp * dv, S), lambda b, h: (b, h, 0)),
                  pl.BlockSpec((1, S, LANES), lambda b, h: (b, 0, 0)),
                  tok_spec(dv),
                  pl.BlockSpec((1, LANES, W), lambda b, h: (h, 0, 0)),
                  pl.BlockSpec((1, 1, W), lambda b, h: (h, 0, 0)),
                  pl.BlockSpec((hp, 1, dv), lambda b, h: (h, 0, 0)),
                  pl.BlockSpec(tri.shape, lambda b, h: (0, 0))],
        out_specs=tok_spec(dv),
        out_shape=jax.ShapeDtypeStruct((B, S, H * dv), BF16),
        scratch_shapes=[pltpu.VMEM((dv, W), F32),
                        pltpu.VMEM((S, W), F32),
                        pltpu.VMEM((GLA_SUB + GLA_C, W), F32),
                        pltpu.VMEM((GLA_SUB + GLA_C, W), F32),
                        pltpu.VMEM((hp, GLA_SUB + GLA_C, dv), F32),
                        pltpu.SMEM((n_chunks,), jnp.int32)],
        compiler_params=pltpu.CompilerParams(
            dimension_semantics=("parallel", "parallel"), vmem_limit_bytes=VMEM_LIMIT),
        name="gla",
    )(gq, gk, gv, gvt, ga, gr, wa, ba, ng, tri)


def _merge_body(x_ref, ya_ref, yb_ref, ma_ref, mb_ref, wpa_ref, wpb_ref, wo_ref, gf_ref, wr_ref, br_ref,
                h_ref, v_ref, comb_ref):
    y_a = _dot(ya_ref[...], wpa_ref[...])
    y_b = _dot(yb_ref[...], wpb_ref[...])
    mixed = ma_ref[...] * y_a + mb_ref[...] * y_b
    h = x_ref[...] + _dot(mixed.astype(BF16), wo_ref[...])
    h_ref[...] = h
    v = h * lax.rsqrt(jnp.mean(h * h, axis=-1, keepdims=True) + EPS) * gf_ref[...]
    v_ref[...] = v.astype(BF16)
    logits = _dot_f32(v, wr_ref[...]) + br_ref[...]
    lane = lax.broadcasted_iota(jnp.int32, logits.shape, 1)
    is_grp = jnp.logical_and(lane >= MOE_EXPERTS, lane < MOE_EXPERTS + MOE_GROUPS)
    lg = jnp.where(is_grp, logits, NEG_INF)
    eg = jnp.where(is_grp, jnp.exp(lg - jnp.max(lg, axis=-1, keepdims=True)), 0.0)
    pg = eg / jnp.sum(eg, axis=-1, keepdims=True)
    p_grp = jnp.max(pg, axis=-1, keepdims=True)
    g_sel = jnp.min(jnp.where(jnp.logical_and(is_grp, pg == p_grp), lane, 2 * LANES),
                    axis=-1, keepdims=True) - MOE_EXPERTS
    in_grp = jnp.logical_and(lane < MOE_EXPERTS, lane // MOE_EPG == g_sel)
    le = jnp.where(in_grp, logits, NEG_INF)
    ee = jnp.where(in_grp, jnp.exp(le - jnp.max(le, axis=-1, keepdims=True)), 0.0)
    pin = ee / jnp.sum(ee, axis=-1, keepdims=True)
    p1 = jnp.max(jnp.where(in_grp, pin, -1.0), axis=-1, keepdims=True)
    i1 = jnp.min(jnp.where(jnp.logical_and(in_grp, pin == p1), lane, 2 * LANES), axis=-1, keepdims=True)
    rest = jnp.logical_and(in_grp, lane != i1)
    p2 = jnp.max(jnp.where(rest, pin, -1.0), axis=-1, keepdims=True)
    i2 = jnp.min(jnp.where(jnp.logical_and(rest, pin == p2), lane, 2 * LANES), axis=-1, keepdims=True)
    tot = p1 + p2
    comb_ref[...] = (jnp.where(lane == i1, p_grp * p1 / tot, 0.0)
                     + jnp.where(lane == i2, p_grp * p2 / tot, 0.0))


def _merge(x2, ya, yb, ma, mb, w_proj_nsa, w_proj_gla, w_out, g_ffn, w_grp, b_grp, w_exp, b_exp):
    T, D = x2.shape
    tm = min(MERGE_TM, T)
    wr = jnp.pad(jnp.concatenate([w_exp, w_grp], axis=1), ((0, 0), (0, LANES - MOE_EXPERTS - MOE_GROUPS)))
    br = jnp.pad(jnp.concatenate([b_exp, b_grp]), (0, LANES - MOE_EXPERTS - MOE_GROUPS)).reshape(1, LANES)
    tok = lambda w: pl.BlockSpec((tm, w), lambda i: (i, 0))
    full = lambda a: pl.BlockSpec(a.shape, lambda i: (0, 0))
    wpa, wpb, wo = w_proj_nsa.astype(BF16), w_proj_gla.astype(BF16), w_out.astype(BF16)
    gf = g_ffn.reshape(1, D)
    sds = jax.ShapeDtypeStruct
    return pl.pallas_call(
        _merge_body,
        grid=(T // tm,),
        in_specs=[tok(D), tok(ya.shape[1]), tok(yb.shape[1]), tok(D), tok(D),
                  full(wpa), full(wpb), full(wo), full(gf), full(wr), full(br)],
        out_specs=(tok(D), tok(D), tok(LANES)),
        out_shape=(sds((T, D), F32), sds((T, D), BF16), sds((T, LANES), F32)),
        compiler_params=pltpu.CompilerParams(
            dimension_semantics=("parallel",), vmem_limit_bytes=VMEM_LIMIT),
        name="merge",
    )(x2, ya, yb, ma, mb, wpa, wpb, wo, gf, wr, br)


def _moe_rows(tw):
    rows = 2 * tw + MOE_EXPERTS * (MOE_ALIGN - 1) + MOE_RT
    return -(-rows // MOE_RT) * MOE_RT


def _moe_body(v_ref, comb_ref, h_ref, wg_ref, wu_ref, wd_ref, gfin_ref, tri_ref, o_ref,
              xs_ref, z_ref, cw_ref, meta_ref, pos_ref, *, tw, rmax):
    e = pl.program_id(1)
    lane = lax.broadcasted_iota(jnp.int32, (1, LANES), 1)
    no_row = -1.0

    @pl.when(e == 0)
    def _route():
        comb = comb_ref[...]
        assigned = comb > 0.0
        a = jnp.where(assigned, 1.0, 0.0)
        tri = tri_ref[...]
        run = jnp.zeros((1, LANES), F32)
        ranks = []
        for b in range(tw // MOE_RT):
            ab = a[b * MOE_RT:(b + 1) * MOE_RT]
            ranks.append(_dot(tri, ab.astype(BF16)) + run)
            run = run + jnp.sum(ab, axis=0, keepdims=True)
        rank = jnp.concatenate(ranks, axis=0)
        cnt_pad = jnp.floor((run + (MOE_ALIGN - 1)) * (1.0 / MOE_ALIGN)) * MOE_ALIGN
        incl = jnp.broadcast_to(cnt_pad, (8, LANES))
        lane8 = lax.broadcasted_iota(jnp.int32, (8, LANES), 1)
        shift = 1
        while shift < MOE_EXPERTS:
            incl = incl + jnp.where(lane8 >= shift, pltpu.roll(incl, shift, 1), 0.0)
            shift *= 2
        offs = incl[0:1] - cnt_pad
        meta_ref[0:1, :] = offs
        meta_ref[1:2, :] = run
        row_of = offs + rank
        pos_a = jnp.min(jnp.where(assigned, row_of, 1e9), axis=-1, keepdims=True)
        pos_b = jnp.max(jnp.where(assigned, row_of, no_row), axis=-1, keepdims=True)
        pos_a = jnp.where(pos_a > 1e8, no_row, pos_a)
        pos_b = jnp.where(pos_b == pos_a, no_row, pos_b)
        lane_t = lax.broadcasted_iota(jnp.int32, (tw, LANES), 1)
        pos_ref[...] = jnp.where(lane_t == 0, pos_a, jnp.where(lane_t == 1, pos_b, no_row))
        pos_t = pos_ref[...].T
        pa, pb = pos_t[0:1], pos_t[1:2]
        c_split = jnp.concatenate(_split3(comb), axis=1)
        v = v_ref[...]
        for rt in range(rmax // MOE_RT):
            r = (rt * MOE_RT + lax.broadcasted_iota(jnp.int32, (MOE_RT, 1), 0)).astype(F32)
            p = jnp.where(r == pa, 1.0, jnp.where(r == pb, 1.0, 0.0)).astype(BF16)
            xs_ref[rt * MOE_RT:(rt + 1) * MOE_RT, :] = _dot(p, v).astype(BF16)
            cw = _dot(p, c_split)
            cw_ref[rt * MOE_RT:(rt + 1) * MOE_RT, :] = (cw[:, :LANES] + cw[:, LANES:2 * LANES]) + cw[:, 2 * LANES:]
        z_ref[...] = jnp.zeros_like(z_ref)

    pick = lambda row: jnp.sum(jnp.where(lane == e, meta_ref[row:row + 1, :], 0.0)).astype(jnp.int32)
    off_e, cnt_e = pick(0), pick(1)

    def row_tile(i, _):
        r0 = pl.multiple_of(off_e + i * MOE_ET, MOE_ALIGN)
        x = xs_ref[pl.ds(r0, MOE_ET), :]
        c = jnp.sum(jnp.where(lane == e, cw_ref[pl.ds(r0, MOE_ET), :], 0.0), axis=-1, keepdims=True)
        hdn = jax.nn.silu(_dot(x, wg_ref[0])) * _dot(x, wu_ref[0])
        z_ref[pl.ds(r0, MOE_ET), :] = _dot((c * hdn).astype(BF16), wd_ref[0]).astype(BF16)
        return 0

    lax.fori_loop(0, (cnt_e + MOE_ET - 1) // MOE_ET, row_tile, 0)

    @pl.when(e == pl.num_programs(1) - 1)
    def _combine():
        r = lax.broadcasted_iota(jnp.int32, (1, rmax), 1).astype(F32)
        z = z_ref[...]
        for tt in range(tw // MOE_RT):
            rows = slice(tt * MOE_RT, (tt + 1) * MOE_RT)
            pa, pb = pos_ref[rows, 0:1], pos_ref[rows, 1:2]
            q = jnp.where(r == pa, 1.0, jnp.where(r == pb, 1.0, 0.0)).astype(BF16)
            h = h_ref[rows, :] + _dot(q, z)
            o_ref[rows, :] = h * lax.rsqrt(jnp.mean(h * h, axis=-1, keepdims=True) + EPS) * gfin_ref[...]


def _moe(v, comb, h, w_gate, w_up, w_down, g_final):
    T, D = h.shape
    tw = min(MOE_TW, T)
    rmax = _moe_rows(tw)
    E, _, F = w_gate.shape
    wg, wu, wd = w_gate.astype(BF16), w_up.astype(BF16), w_down.astype(BF16)
    tri = jnp.asarray(np.tril(np.ones((MOE_RT, MOE_RT)), -1), BF16)
    tok = lambda w: pl.BlockSpec((tw, w), lambda i, e: (i, 0))
    return pl.pallas_call(
        functools.partial(_moe_body, tw=tw, rmax=rmax),
        grid=(T // tw, E),
        in_specs=[tok(D), tok(LANES), tok(D),
                  pl.BlockSpec((1, D, F), lambda i, e: (e, 0, 0)),
                  pl.BlockSpec((1, D, F), lambda i, e: (e, 0, 0)),
                  pl.BlockSpec((1, F, D), lambda i, e: (e, 0, 0)),
                  pl.BlockSpec((1, D), lambda i, e: (0, 0)),
                  pl.BlockSpec(tri.shape, lambda i, e: (0, 0))],
        out_specs=tok(D),
        out_shape=jax.ShapeDtypeStruct((T, D), F32),
        scratch_shapes=[pltpu.VMEM((rmax, D), BF16),
                        pltpu.VMEM((rmax, D), BF16),
                        pltpu.VMEM((rmax, LANES), F32),
                        pltpu.VMEM((8, LANES), F32),
                        pltpu.VMEM((tw, LANES), F32)],
        compiler_params=pltpu.CompilerParams(
            dimension_semantics=("parallel", "arbitrary"), vmem_limit_bytes=VMEM_LIMIT),
        name="moe",
    )(v, comb, h, wg, wu, wd, g_final.reshape(1, D), tri)


def _rope_tables(positions):
    half = NSA_DH // 2
    inv = 1.0 / (ROPE_THETA ** (jnp.arange(half, dtype=F32) / half))
    ang = positions.astype(F32)[..., None] * inv
    cos, sin = jnp.cos(ang), jnp.sin(ang)
    cs = jnp.concatenate([cos, cos, cos, cos], axis=-1)
    sn = jnp.concatenate([-sin, sin, -sin, sin], axis=-1)
    return cs, sn


def _layer(h, positions, g_mix, w_in, cmp_pos_k, cmp_w1_k, cmp_b1_k, cmp_w2_k, cmp_b2_k,
           cmp_pos_v, cmp_w1_v, cmp_b1_v, cmp_w2_v, cmp_b2_v, gla_w_a2, gla_b_a, gla_norm_g,
           w_proj_nsa, w_proj_gla, w_out, g_ffn, w_grp, b_grp, w_exp, b_exp, w_gate, w_up, w_down, g_out):
    B, S, D = h.shape
    cs, sn = _rope_tables(positions)
    n_chunks = S // CMP_STRIDE
    cmp_end = jnp.minimum(CMP_STRIDE * jnp.arange(n_chunks) + CMP_BLOCK - 1, S - 1)
    cs_c, sn_c = _rope_tables(jnp.take(positions, cmp_end, axis=1))
    (q, kvc, ks, vs, kw, vw, gq, gk, gv, gvt, gr, ma, mb, ng, ga) = _in_proj(h, g_mix, w_in, cs, sn)
    kc, vc = _compress(kvc, cs_c, sn_c, cmp_pos_k, cmp_w1_k, cmp_b1_k, cmp_w2_k, cmp_b2_k,
                       cmp_pos_v, cmp_w1_v, cmp_b1_v, cmp_w2_v, cmp_b2_v)
    ya = _nsa_attention(q, kc, vc, ks, vs, kw, vw, ng)
    yb = _gla_pairs(gq, gk, gv, gvt, ga, gr, gla_w_a2, gla_b_a, gla_norm_g)
    T = B * S
    h1, v, comb = _merge(h.reshape(T, D), ya.reshape(T, -1), yb.reshape(T, -1), ma.reshape(T, D), mb.reshape(T, D),
                         w_proj_nsa, w_proj_gla, w_out, g_ffn, w_grp, b_grp, w_exp, b_exp)
    return _moe(v, comb, h1, w_gate, w_up, w_down, g_out).reshape(B, S, D)


def kernel(x, positions, g_mix, w_in, cmp_pos_k, cmp_w1_k, cmp_b1_k, cmp_w2_k, cmp_b2_k, cmp_pos_v, cmp_w1_v,
           cmp_b1_v, cmp_w2_v, cmp_b2_v, gla_w_a2, gla_b_a, gla_norm_g, w_proj_nsa, w_proj_gla, w_out, g_ffn,
           w_grp, b_grp, w_exp, b_exp, w_gate, w_up, w_down, g_final):
    depth = g_mix.shape[0]
    assert depth == 1, "the final norm is fused into the single layer's expert kernel"
    return _layer(x, positions, g_mix[0], w_in[0], cmp_pos_k[0], cmp_w1_k[0], cmp_b1_k[0], cmp_w2_k[0],
                  cmp_b2_k[0], cmp_pos_v[0], cmp_w1_v[0], cmp_b1_v[0], cmp_w2_v[0], cmp_b2_v[0],
                  gla_w_a2[0], gla_b_a[0], gla_norm_g[0], w_proj_nsa[0], w_proj_gla[0], w_out[0],
                  g_ffn[0], w_grp[0], b_grp[0], w_exp[0], b_exp[0], w_gate[0], w_up[0], w_down[0], g_final)
```

```python
import functools

import numpy as np
import jax
import jax.numpy as jnp
from jax import lax
from jax.experimental import pallas as pl
from jax.experimental.pallas import tpu as pltpu

F32 = jnp.float32
BF16 = jnp.bfloat16

NSA_HEADS = 8
NSA_KV_GROUPS = 2
NSA_HPG = NSA_HEADS // NSA_KV_GROUPS
NSA_DH = 64
CMP_BLOCK = 32
CMP_STRIDE = 16
CMP_HIDDEN = 256
SLC_BLOCK = 64
SLC_TOPK = 16
WINDOW = 512
GLA_HEADS = 4
GLA_DK = 64
GLA_DV = 128
GLA_RANK = 16
GLA_TAU = 16.0
MOE_GROUPS = 4
MOE_EPG = 8
MOE_EXPERTS = MOE_GROUPS * MOE_EPG
MOE_DFF = 512
ROPE_THETA = 10000.0
EPS = 1e-6
NEG_INF = -1e30
FORCED_SCORE = 1e4
LOG2E = 1.4426950408889634

LANES = 128
VMEM_LIMIT = 56 * 1024 * 1024

IN_TM = 512
ATT_TQ = 128
ATT_TK = 512
GLA_C = 128
GLA_SUB = 16
GLA_HP = 2
GLA_UNROLL_DECAY = 4
GLA_UNROLL_PLAIN = 4
GLA_PLAIN_DECAY = 60.0
MERGE_TM = 512
MOE_TW = 1024
MOE_WINDOWS = 2
FINAL_TM = 1024
MOE_RT = 256
MOE_ET = 128
MOE_ALIGN = 16


def _dot(a, b):
    return jnp.dot(a, b, preferred_element_type=F32)


def _dot_nt(a, b):
    return lax.dot_general(a, b, (((1,), (1,)), ((), ())), preferred_element_type=F32)


def _split3(x):
    x1 = x.astype(BF16)
    r1 = x - x1.astype(F32)
    x2 = r1.astype(BF16)
    r2 = r1 - x2.astype(F32)
    x3 = r2.astype(BF16)
    return x1, x2, x3


def _dot_exact_lhs(a_bf16, x):
    x1, x2, x3 = _split3(x)
    return _dot(a_bf16, x1) + _dot(a_bf16, x2) + _dot(a_bf16, x3)


def _dot_f32(a, b):
    a1, a2, a3 = _split3(a)
    b1, b2, b3 = _split3(b)
    return (_dot(a1, b1) + (_dot(a1, b2) + _dot(a2, b1))
            + (_dot(a1, b3) + _dot(a2, b2) + _dot(a3, b1)))


def _rope_lanes(z, cs, sn):
    w = z.shape[-1]
    lane = lax.broadcasted_iota(jnp.int32, z.shape, 1)
    first_half = (lane % NSA_DH) < (NSA_DH // 2)
    rot = jnp.where(first_half, pltpu.roll(z, w - NSA_DH // 2, 1), pltpu.roll(z, NSA_DH // 2, 1))
    reps = w // LANES
    if reps > 1:
        cs = jnp.concatenate([cs] * reps, axis=1)
        sn = jnp.concatenate([sn] * reps, axis=1)
    return z * cs + rot * sn


_SEC = {}
_off = 0
for _name, _w in (("q", 512), ("kvc", 256), ("ks", 128), ("vs", 128), ("kw", 128), ("vw", 128),
                  ("gq", 256), ("gk", 256), ("gv", 512), ("gr", 512), ("ma", 1024), ("mb", 1024),
                  ("ng", 128), ("ga", 128)):
    _SEC[_name] = (_off, _off + _w)
    _off += _w
IN_NW = _off


def _in_proj_body(x_ref, g_ref, w_ref, cs_ref, sn_ref,
                  q_ref, kvc_ref, ks_ref, vs_ref, kw_ref, vw_ref,
                  gq_ref, gk_ref, gv_ref, gvt_ref, gr_ref, ma_ref, mb_ref, ng_ref, ga_ref):
    x = x_ref[0]
    var = jnp.mean(x * x, axis=-1, keepdims=True)
    u = (x * lax.rsqrt(var + EPS) * g_ref[...]).astype(BF16)
    cs = cs_ref[0]
    sn = sn_ref[0]

    def proj(name):
        a, b = _SEC[name]
        return _dot(u, w_ref[:, a:b])

    zq_t = (_rope_lanes(proj("q"), cs, sn) * (NSA_DH ** -0.5 * LOG2E)).T
    for h in range(NSA_HEADS):
        q_ref[0, h] = zq_t[h * NSA_DH:(h + 1) * NSA_DH].astype(BF16)
    kvc_ref[0] = proj("kvc")
    zks = _rope_lanes(proj("ks"), cs, sn)
    zkw = _rope_lanes(proj("kw"), cs, sn)
    zvs_t = proj("vs").T
    zvw_t = proj("vw").T
    tm = zks.shape[0]
    lane = lax.broadcasted_iota(jnp.int32, (tm, LANES), 1)
    blk = (pl.program_id(1) * tm + lax.broadcasted_iota(jnp.int32, (tm, LANES), 0)) // SLC_BLOCK
    onehot = jnp.where(lane - NSA_DH == blk, 1.0, 0.0)
    for g in range(NSA_KV_GROUPS):
        sl = slice(g * NSA_DH, (g + 1) * NSA_DH)
        k_front = zks if g == 0 else pltpu.roll(zks, LANES - g * NSA_DH, 1)
        ks_ref[0, g] = jnp.where(lane < NSA_DH, k_front, onehot).astype(BF16)
        kw_ref[0, g] = zkw[:, sl].astype(BF16)
        vs_ref[0, g] = zvs_t[sl].astype(BF16)
        vw_ref[0, g] = zvw_t[sl].astype(BF16)
    gq_ref[0] = proj("gq") * (GLA_DK ** -0.5)
    gk_ref[0] = proj("gk")
    zgv = proj("gv")
    gv_ref[0] = zgv.astype(BF16)
    gvt_ref[0] = zgv.T.astype(BF16)
    gr_ref[0] = proj("gr")
    ma_ref[0] = jax.nn.sigmoid(proj("ma")).astype(ma_ref.dtype)
    mb_ref[0] = jax.nn.sigmoid(proj("mb")).astype(mb_ref.dtype)
    ng_ref[0] = jax.nn.sigmoid(proj("ng")).T
    ga_ref[0] = proj("ga")


def _in_proj(x, g_mix, w_in, cs, sn):
    B, S, D = x.shape
    tm = min(IN_TM, S)
    splits = np.cumsum((512,) + (128,) * 6 + (24, 256, 256, 512, 16, 512, 1024, 1024))
    (wq, wkc, wvc, wks, wvs, wkw, wvw, wng, wgq, wgk, wgv, wga, wgr, wma, wmb) = jnp.split(
        w_in, splits[:-1].tolist(), axis=1)
    pad = lambda w: jnp.pad(w, ((0, 0), (0, LANES - w.shape[1])))
    w_all = jnp.concatenate([wq, wkc, wvc, wks, wvs, wkw, wvw, wgq, wgk, wgv, wgr, wma, wmb,
                             pad(wng), pad(wga)], axis=1).astype(BF16)
    assert w_all.shape[1] == IN_NW
    grid = (B, S // tm)
    tok = lambda w: pl.BlockSpec((1, tm, w), lambda b, i: (b, i, 0))
    head = lambda n, w: pl.BlockSpec((1, n, tm, w), lambda b, i: (b, 0, i, 0))
    head_t = lambda n, w: pl.BlockSpec((1, n, w, tm), lambda b, i: (b, 0, 0, i))
    sds = jax.ShapeDtypeStruct
    out_shape = (
        sds((B, NSA_HEADS, NSA_DH, S), BF16),
        sds((B, S, 256), F32),
        sds((B, NSA_KV_GROUPS, S, LANES), BF16),
        sds((B, NSA_KV_GROUPS, NSA_DH, S), BF16),
        sds((B, NSA_KV_GROUPS, S, NSA_DH), BF16),
        sds((B, NSA_KV_GROUPS, NSA_DH, S), BF16),
        sds((B, S, GLA_HEADS * GLA_DK), F32),
        sds((B, S, GLA_HEADS * GLA_DK), F32),
        sds((B, S, GLA_HEADS * GLA_DV), BF16),
        sds((B, GLA_HEADS * GLA_DV, S), BF16),
        sds((B, S, GLA_HEADS * GLA_DV), F32),
        sds((B, S, D), BF16),
        sds((B, S, D), BF16),
        sds((B, LANES, S), F32),
        sds((B, S, LANES), F32),
    )
    out_specs = (
        head_t(NSA_HEADS, NSA_DH), tok(256),
        head(NSA_KV_GROUPS, LANES), head_t(NSA_KV_GROUPS, NSA_DH),
        head(NSA_KV_GROUPS, NSA_DH), head_t(NSA_KV_GROUPS, NSA_DH),
        tok(GLA_HEADS * GLA_DK), tok(GLA_HEADS * GLA_DK),
        tok(GLA_HEADS * GLA_DV),
        pl.BlockSpec((1, GLA_HEADS * GLA_DV, tm), lambda b, i: (b, 0, i)),
        tok(GLA_HEADS * GLA_DV), tok(D), tok(D),
        pl.BlockSpec((1, LANES, tm), lambda b, i: (b, 0, i)), tok(LANES),
    )
    return pl.pallas_call(
        _in_proj_body,
        grid=grid,
        in_specs=[
            tok(D),
            pl.BlockSpec((1, D), lambda b, i: (0, 0)),
            pl.BlockSpec((D, IN_NW), lambda b, i: (0, 0), pipeline_mode=pl.Buffered(1)),
            tok(LANES), tok(LANES),
        ],
        out_specs=out_specs,
        out_shape=out_shape,
        compiler_params=pltpu.CompilerParams(
            dimension_semantics=("parallel", "parallel"), vmem_limit_bytes=VMEM_LIMIT),
        name="in_proj",
    )(x, g_mix.reshape(1, D), w_all, cs, sn)


def _compress_body(x_ref, p0_ref, p1_ref, w0_ref, w1_ref, b1_ref, w2_ref, b2_ref, cs_ref, sn_ref,
                   kc_ref, vc_ref):
    x = x_ref[0]
    n = x.shape[0]
    y0 = _dot((x + p0_ref[...]).astype(BF16), w0_ref[...])
    y1 = _dot((x + p1_ref[...]).astype(BF16), w1_ref[...])
    h = jax.nn.gelu(y0 + pltpu.roll(y1, n - 1, 0) + b1_ref[...])
    o = _dot(h.astype(BF16), w2_ref[...]) + b2_ref[...]
    k = _rope_lanes(o[:, :LANES], cs_ref[0], sn_ref[0])
    v_t = o[:, LANES:].T
    for g in range(NSA_KV_GROUPS):
        sl = slice(g * NSA_DH, (g + 1) * NSA_DH)
        kc_ref[0, g] = k[:, sl].astype(BF16)
        vc_ref[0, g] = v_t[sl].astype(BF16)


def _compress(kvc, cs_c, sn_c, pos_k, w1_k, b1_k, w2_k, b2_k, pos_v, w1_v, b1_v, w2_v, b2_v):
    B, S, _ = kvc.shape
    n = S // CMP_STRIDE
    ns = 2 * NSA_KV_GROUPS
    x = kvc.reshape(B, n, CMP_STRIDE * ns * NSA_DH)
    eye = jnp.eye(ns, dtype=F32)
    w1s = jnp.stack([w1_k, w1_k, w1_v, w1_v])
    poss = jnp.stack([pos_k, pos_k, pos_v, pos_v])
    wbig, pbig = [], []
    for m in range(CMP_BLOCK // CMP_STRIDE):
        wm = w1s.reshape(ns, CMP_BLOCK, NSA_DH, CMP_HIDDEN)[:, m * CMP_STRIDE:(m + 1) * CMP_STRIDE]
        wb = jnp.einsum('ctdj,ce->tcdej', wm, eye).reshape(CMP_STRIDE * ns * NSA_DH, ns * CMP_HIDDEN)
        wbig.append(wb.astype(BF16))
        pm = poss[:, m * CMP_STRIDE:(m + 1) * CMP_STRIDE]
        pbig.append(pm.transpose(1, 0, 2).reshape(1, CMP_STRIDE * ns * NSA_DH))
    b1 = jnp.concatenate([b1_k, b1_k, b1_v, b1_v]).reshape(1, ns * CMP_HIDDEN)
    w2s = jnp.stack([w2_k, w2_k, w2_v, w2_v])
    w2big = jnp.einsum('cjd,ce->cjed', w2s, eye).reshape(ns * CMP_HIDDEN, ns * NSA_DH).astype(BF16)
    b2 = jnp.concatenate([b2_k, b2_k, b2_v, b2_v]).reshape(1, ns * NSA_DH)
    full = lambda a: pl.BlockSpec(a.shape, lambda b: (0,) * a.ndim)
    sds = jax.ShapeDtypeStruct
    return pl.pallas_call(
        _compress_body,
        grid=(B,),
        in_specs=[pl.BlockSpec((1, n, x.shape[2]), lambda b: (b, 0, 0)),
                  full(pbig[0]), full(pbig[1]), full(wbig[0]), full(wbig[1]), full(b1), full(w2big), full(b2),
                  pl.BlockSpec((1, n, LANES), lambda b: (b, 0, 0)),
                  pl.BlockSpec((1, n, LANES), lambda b: (b, 0, 0))],
        out_specs=(pl.BlockSpec((1, NSA_KV_GROUPS, n, NSA_DH), lambda b: (b, 0, 0, 0)),
                   pl.BlockSpec((1, NSA_KV_GROUPS, NSA_DH, n), lambda b: (b, 0, 0, 0))),
        out_shape=(sds((B, NSA_KV_GROUPS, n, NSA_DH), BF16), sds((B, NSA_KV_GROUPS, NSA_DH, n), BF16)),
        compiler_params=pltpu.CompilerParams(
            dimension_semantics=("parallel",), vmem_limit_bytes=VMEM_LIMIT),
        name="compress",
    )(x, pbig[0], pbig[1], wbig[0], wbig[1], b1, w2big, b2, cs_c, sn_c)


NSA_ONES = 16


def _with_ones(v_t):
    return jnp.concatenate([v_t, jnp.ones((NSA_ONES, v_t.shape[1]), v_t.dtype)], axis=0)


def _softmax_step(s, m, acc, v_t):
    m_new = jnp.maximum(m, jnp.max(s, axis=0, keepdims=True))
    p = jnp.exp2(s - m_new).astype(BF16)
    acc = jnp.exp2(m - m_new) * acc + _dot(_with_ones(v_t), p)
    return m_new, acc


def _nsa_body(q_ref, kc_ref, vc_ref, ks_ref, vs_ref, kw_ref, vw_ref, ng_ref, ovt_ref,
              o_ref, sa_ref, sb_ref, sw_ref, *, tq, tk, n_slc):
    gi = pl.program_id(1)
    qi = pl.program_id(2)
    q0 = qi * tq
    n_cmp = kc_ref.shape[2]
    W = NSA_HPG * tq
    per_head = lambda x: jnp.concatenate([x] * NSA_HPG, axis=1)
    q_all = jnp.concatenate([q_ref[0, h] for h in range(NSA_HPG)], axis=1)
    t_lane = q0 + lax.broadcasted_iota(jnp.int32, (1, tq), 1)

    cmp_end = CMP_STRIDE * lax.broadcasted_iota(jnp.int32, (n_cmp, 1), 0) + (CMP_BLOCK - 1)
    ok = cmp_end <= t_lane
    sm = _dot(kc_ref[0, 0], q_all) + per_head(jnp.where(ok, 0.0, NEG_INF))
    e = jnp.exp2(sm - jnp.max(sm, axis=0, keepdims=True))
    p = e * (1.0 / jnp.sum(e, axis=0, keepdims=True)) * per_head(jnp.where(ok, 1.0, 0.0))
    o_cmp = _dot(vc_ref[0, 0], p.astype(BF16))
    p_sum = p[:, 0:tq]
    for h in range(1, NSA_HPG):
        p_sum = p_sum + p[:, h * tq:(h + 1) * tq]

    p_hi = p_sum.astype(BF16)
    p_lo = (p_sum - p_hi.astype(F32)).astype(BF16)
    imp = _dot(ovt_ref[...], p_hi) + _dot(ovt_ref[...], p_lo)
    blk = lax.broadcasted_iota(jnp.int32, (n_slc, tq), 0)
    cur = (q0 + lax.broadcasted_iota(jnp.int32, (n_slc, tq), 1)) // SLC_BLOCK
    forced = (blk == 0) | (blk == cur) | (blk == cur - 1)
    imp = jnp.where(forced, FORCED_SCORE, imp)
    imp = jnp.where(blk <= cur, imp, NEG_INF)
    wk = WINDOW + tq
    w0 = pl.multiple_of(jnp.maximum(q0 - WINDOW, 0), tq)
    kpos = w0 + lax.broadcasted_iota(jnp.int32, (wk, 1), 0)
    okw = jnp.logical_and(kpos <= t_lane, kpos > t_lane - WINDOW)
    sw_ref[...] = _dot(kw_ref[0, 0, pl.ds(w0, wk), :], q_all) + per_head(jnp.where(okw, 0.0, NEG_INF))

    SUBL = 8
    groups = [imp[r:r + SUBL] for r in range(0, n_slc, SUBL)]
    ranks = [jnp.zeros((SUBL, tq), F32) for _ in groups]
    for i in range(n_slc):
        row = imp[i:i + 1, :]
        for gidx, grp in enumerate(groups):
            r = gidx * SUBL
            ge = jnp.where(row >= grp, 1.0, 0.0)
            gt = jnp.where(row > grp, 1.0, 0.0)
            if r > i:
                inc = ge
            elif r + SUBL - 1 <= i:
                inc = gt
            else:
                inc = jnp.where(blk[r:r + SUBL] > i, ge, gt)
            ranks[gidx] = ranks[gidx] + inc
    rank = jnp.concatenate(ranks, axis=0)
    bias = jnp.where(rank < float(min(SLC_TOPK, n_slc)), 0.0, NEG_INF)
    if n_slc < NSA_DH:
        bias = jnp.concatenate([bias, jnp.zeros((NSA_DH - n_slc, tq), F32)], axis=0)
    q_aug = jnp.concatenate([q_all, per_head(bias).astype(BF16)], axis=0)

    kpos_l = lax.broadcasted_iota(jnp.int32, (tk, 1), 0)

    n_tiles = q0 // tk + 1
    last_tile = ks_ref.shape[2] // tk - 1

    def tile_start(jt):
        return pl.multiple_of(jnp.minimum(jt, last_tile) * tk, tk)

    def scores_into(s_ref, jt):
        causal = per_head(jnp.where(jt * tk + kpos_l <= t_lane, 0.0, NEG_INF))
        s_ref[...] = _dot(ks_ref[0, 0, pl.ds(tile_start(jt), tk), :], q_aug) + causal

    def consume(s_ref, jt, m, acc):
        return _softmax_step(s_ref[...], m, acc, vs_ref[0, 0, :, pl.ds(tile_start(jt), tk)])

    def slc_pair(i, carry):
        m, acc = carry
        scores_into(sb_ref, 2 * i + 1)
        m, acc = consume(sa_ref, 2 * i, m, acc)
        scores_into(sa_ref, 2 * i + 2)
        return consume(sb_ref, 2 * i + 1, m, acc)

    init = (jnp.full((1, W), NEG_INF, F32), jnp.zeros((NSA_DH + NSA_ONES, W), F32))
    scores_into(sa_ref, 0)

    sw = sw_ref[...]
    ew = jnp.exp2(sw - jnp.max(sw, axis=0, keepdims=True)).astype(BF16)
    acc_w = _dot(_with_ones(vw_ref[0, 0, :, pl.ds(w0, wk)]), ew)
    l_w = acc_w[NSA_DH:NSA_DH + 1]
    acc_w = acc_w[:NSA_DH]

    _, acc_s = lax.fori_loop(0, (n_tiles + 1) // 2, slc_pair, init)
    l_s = acc_s[NSA_DH:NSA_DH + 1]
    acc_s = acc_s[:NSA_DH]

    def gate(j):
        return jnp.concatenate([ng_ref[0, pl.ds((gi * NSA_HPG + h) * 3 + j, 1), :]
                                for h in range(NSA_HPG)], axis=1)

    out_t = gate(0) * o_cmp + (gate(1) * (1.0 / l_s)) * acc_s + (gate(2) * (1.0 / l_w)) * acc_w
    out_t = jnp.concatenate([out_t[:, h * tq:(h + 1) * tq] for h in range(NSA_HPG)], axis=0)
    o_ref[0] = out_t.T.astype(o_ref.dtype)


def _nsa_attention(q, kc, vc, ks, vs, kw, vw, ng):
    B, H, dh, S = q.shape
    tq = min(ATT_TQ, S)
    tk = min(ATT_TK, S)
    assert S % tk == 0 and tk % tq == 0 and S >= WINDOW + tq and S // SLC_BLOCK <= dh
    n_cmp = kc.shape[2]
    n_slc = S // SLC_BLOCK
    c0 = CMP_STRIDE * np.arange(n_cmp)[None, :]
    s0 = SLC_BLOCK * np.arange(n_slc)[:, None]
    ov = np.clip(np.minimum(c0 + CMP_BLOCK, s0 + SLC_BLOCK) - np.maximum(c0, s0), 0, None) / CMP_BLOCK
    ovt = jnp.asarray(ov, BF16)
    grid = (B, NSA_KV_GROUPS, S // tq)
    k_spec = lambda n: pl.BlockSpec((1, 1, n, dh), lambda b, g, i: (b, g, 0, 0))
    vt_spec = lambda n: pl.BlockSpec((1, 1, dh, n), lambda b, g, i: (b, g, 0, 0))
    return pl.pallas_call(
        functools.partial(_nsa_body, tq=tq, tk=tk, n_slc=n_slc),
        grid=grid,
        in_specs=[
            pl.BlockSpec((1, NSA_HPG, dh, tq), lambda b, g, i: (b, g, 0, i)),
            k_spec(n_cmp), vt_spec(n_cmp),
            pl.BlockSpec((1, 1, S, 2 * dh), lambda b, g, i: (b, g, 0, 0)), vt_spec(S),
            k_spec(S), vt_spec(S),
            pl.BlockSpec((1, LANES, tq), lambda b, g, i: (b, 0, i)),
            pl.BlockSpec(ovt.shape, lambda b, g, i: (0, 0)),
        ],
        out_specs=pl.BlockSpec((1, tq, NSA_HPG * dh), lambda b, g, i: (b, i, g)),
        out_shape=jax.ShapeDtypeStruct((B, S, H * dh), BF16),
        scratch_shapes=[pltpu.VMEM((tk, NSA_HPG * tq), F32),
                        pltpu.VMEM((tk, NSA_HPG * tq), F32),
                        pltpu.VMEM((WINDOW + tq, NSA_HPG * tq), F32)],
        compiler_params=pltpu.CompilerParams(
            dimension_semantics=("parallel", "parallel", "arbitrary"), vmem_limit_bytes=VMEM_LIMIT),
        name="nsa_attn",
    )(q, kc, vc, ks, vs, kw, vw, ng, ovt)


def _dot_3pass(a, b):
    a1 = a.astype(BF16)
    a2 = (a - a1.astype(F32)).astype(BF16)
    b1 = b.astype(BF16)
    b2 = (b - b1.astype(F32)).astype(BF16)
    return _dot(a1, b1) + (_dot(a1, b2) + _dot(a2, b1))


def _gla_pair_body(q_ref, k_ref, v_ref, vt_ref, ga_ref, r_ref, wa_ref, ba_ref, ng_ref, tri_ref,
                   o_ref, state_ref, b_ref, kp_ref, bp_ref, vp_ref, plain_ref, *, n_chunks):
    C, SUB, dk, dv, hp = GLA_C, GLA_SUB, GLA_DK, GLA_DV, GLA_HP
    W = hp * dk
    lane = lax.broadcasted_iota(jnp.int32, (1, W), 1)
    of_head = [lane // dk == hh for hh in range(hp)]
    tri = tri_ref[...]
    t_loc = lax.broadcasted_iota(jnp.int32, (C, 1), 0)
    state_ref[...] = jnp.zeros_like(state_ref)
    kp_ref[0:SUB, :] = jnp.zeros((SUB, W), F32)
    bp_ref[0:SUB, :] = jnp.zeros((SUB, W), F32)
    vp_ref[:, 0:SUB, :] = jnp.zeros((hp, SUB, dv), F32)

    def decay_chunk(c, all_plain):
        c0 = pl.multiple_of(c * C, C)
        la = jax.nn.log_sigmoid(_dot_3pass(ga_ref[0, pl.ds(c0, C), :], wa_ref[0]) + ba_ref[0]) / GLA_TAU
        b = _dot_exact_lhs(tri, la)
        b_ref[pl.ds(c0, C), :] = b
        plain = (jnp.min(b[C - 1:C, :]) > -GLA_PLAIN_DECAY).astype(jnp.int32)
        plain_ref[c] = plain
        return jnp.minimum(all_plain, plain)

    all_plain = lax.fori_loop(0, n_chunks, decay_chunk, jnp.int32(1), unroll=GLA_UNROLL_DECAY)

    def chunk(c, check_decay):
        c0 = pl.multiple_of(c * C, C)
        b = b_ref[pl.ds(c0, C), :]
        q = q_ref[0, pl.ds(c0, C), :]
        k = k_ref[0, pl.ds(c0, C), :]
        v = [v_ref[0, pl.ds(c0, C), hh * dv:(hh + 1) * dv] for hh in range(hp)]
        vt = [vt_ref[0, hh * dv:(hh + 1) * dv, pl.ds(c0, C)] for hh in range(hp)]
        b_last = b[C - 1:C, :]
        st = state_ref[...]
        st_b = st.astype(BF16)
        qg = q * jnp.exp(b)
        qg_h = [jnp.where(of_head[hh], qg, 0.0).astype(BF16) for hh in range(hp)]
        o_inter = [_dot_nt(qg_h[hh], st_b) for hh in range(hp)]

        def intra_plain():
            ke = (k * jnp.exp(-b)).astype(BF16)
            row = lax.broadcasted_iota(jnp.int32, (C, C), 0)
            col = lax.broadcasted_iota(jnp.int32, (C, C), 1)
            return tuple(_dot(jnp.where(row >= col, _dot_nt(qg_h[hh], ke), 0.0).astype(BF16), v[hh])
                         for hh in range(hp))

        def intra_strong_decay():
            far = [[jnp.zeros((SUB, dv), F32)] for _ in range(hp)]
            for i in range(1, C // SUB):
                r0 = i * SUB
                b_first = b[r0:r0 + 1, :]
                qt = q[r0:r0 + SUB] * jnp.exp(b[r0:r0 + SUB] - b_first)
                kt = (k[:r0] * jnp.exp(b_first - b[:r0])).astype(BF16)
                lt = lax.broadcasted_iota(jnp.int32, (SUB, r0), 0)
                ls = lax.broadcasted_iota(jnp.int32, (SUB, r0), 1)
                for hh in range(hp):
                    a = _dot_nt(jnp.where(of_head[hh], qt, 0.0).astype(BF16), kt)
                    a = jnp.where(lt + (r0 - SUB) >= ls, a, 0.0)
                    far[hh].append(_dot(a.astype(BF16), v[hh][:r0]))
            acc = [jnp.concatenate(far[hh], axis=0) for hh in range(hp)]
            kp_ref[SUB:SUB + C, :] = k
            bp_ref[SUB:SUB + C, :] = b
            for hh in range(hp):
                vp_ref[hh, SUB:SUB + C, :] = v[hh].astype(F32)
            for d in range(SUB):
                kd = kp_ref[SUB - d:SUB - d + C, :]
                bd = bp_ref[SUB - d:SUB - d + C, :]
                valid = t_loc >= d
                w = jnp.exp(jnp.where(valid, b - bd, 0.0))
                x = jnp.where(valid, q * kd * w, 0.0)
                for hh in range(hp):
                    a_d = jnp.sum(jnp.where(of_head[hh], x, 0.0), axis=-1, keepdims=True)
                    acc[hh] = acc[hh] + a_d * vp_ref[hh, SUB - d:SUB - d + C, :]
            return tuple(acc)

        if check_decay:
            o_intra = lax.cond(plain_ref[c] > 0, intra_plain, intra_strong_decay)
        else:
            o_intra = intra_plain()
        k_dec = (k * jnp.exp(b_last - b)).astype(BF16)
        upd = _dot(vt[hp - 1], k_dec)
        for hh in range(hp - 2, -1, -1):
            upd = jnp.where(of_head[hh], _dot(vt[hh], k_dec), upd)
        state_ref[...] = st * jnp.exp(b_last) + upd
        for hh in range(hp):
            o = o_inter[hh] + o_intra[hh]
            o = o * lax.rsqrt(jnp.mean(o * o, axis=-1, keepdims=True) + EPS) * ng_ref[hh]
            gate = jax.nn.silu(r_ref[0, pl.ds(c0, C), hh * dv:(hh + 1) * dv])
            o_ref[0, pl.ds(c0, C), hh * dv:(hh + 1) * dv] = (o * gate).astype(o_ref.dtype)
        return 0

    @pl.when(all_plain > 0)
    def _():
        lax.fori_loop(0, n_chunks, lambda c, _: chunk(c, False), 0, unroll=GLA_UNROLL_PLAIN)

    @pl.when(all_plain <= 0)
    def _():
        lax.fori_loop(0, n_chunks, lambda c, _: chunk(c, True), 0)


def _gla_pairs(gq, gk, gv, gvt, ga, gr, w_a2, b_a, norm_g):
    B, S, _ = gq.shape
    H, dk, dv, hp = GLA_HEADS, GLA_DK, GLA_DV, GLA_HP
    W = hp * dk
    assert W == LANES and H % hp == 0
    wa = jnp.pad(w_a2, ((0, LANES - GLA_RANK), (0, 0))).reshape(LANES, H // hp, W).transpose(1, 0, 2)
    ba = b_a.reshape(H // hp, 1, W)
    ng = norm_g.reshape(H, 1, dv)
    tri = jnp.asarray(np.tril(np.ones((GLA_C, GLA_C))), BF16)
    n_chunks = S // GLA_C
    tok_spec = lambda w: pl.BlockSpec((1, S, hp * w), lambda b, h: (b, 0, h))
    return pl.pallas_call(
        functools.partial(_gla_pair_body, n_chunks=n_chunks),
        grid=(B, H // hp),
        in_specs=[tok_spec(dk), tok_spec(dk), tok_spec(dv),
                  pl.BlockSpec((1, hp * dv, S), lambda b, h: (b, h, 0)),
                  pl.BlockSpec((1, S, LANES), lambda b, h: (b, 0, 0)),
                  tok_spec(dv),
                  pl.BlockSpec((1, LANES, W), lambda b, h: (h, 0, 0)),
                  pl.BlockSpec((1, 1, W), lambda b, h: (h, 0, 0)),
                  pl.BlockSpec((hp, 1, dv), lambda b, h: (h, 0, 0)),
                  pl.BlockSpec(tri.shape, lambda b, h: (0, 0))],
        out_specs=tok_spec(dv),
        out_shape=jax.ShapeDtypeStruct((B, S, H * dv), BF16),
        scratch_shapes=[pltpu.VMEM((dv, W), F32),
                        pltpu.VMEM((S, W), F32),
                        pltpu.VMEM((GLA_SUB + GLA_C, W), F32),
                        pltpu.VMEM((GLA_SUB + GLA_C, W), F32),
                        pltpu.VMEM((hp, GLA_SUB + GLA_C, dv), F32),
                        pltpu.SMEM((n_chunks,), jnp.int32)],
        compiler_params=pltpu.CompilerParams(
            dimension_semantics=("parallel", "parallel"), vmem_limit_bytes=VMEM_LIMIT),
        name="gla",
    )(gq, gk, gv, gvt, ga, gr, wa, ba, ng, tri)


def _merge_body(x_ref, ya_ref, yb_ref, ma_ref, mb_ref, wpa_ref, wpb_ref, wo_ref, gf_ref, wr_ref, br_ref,
                h_ref, v_ref, comb_ref):
    y_a = _dot(ya_ref[...], wpa_ref[...])
    y_b = _dot(yb_ref[...], wpb_ref[...])
    mixed = ma_ref[...] * y_a + mb_ref[...] * y_b
    h = x_ref[...] + _dot(mixed.astype(BF16), wo_ref[...])
    h_ref[...] = h
    v = h * lax.rsqrt(jnp.mean(h * h, axis=-1, keepdims=True) + EPS) * gf_ref[...]
    v_ref[...] = v.astype(BF16)
    logits = _dot_f32(v, wr_ref[...]) + br_ref[...]
    lane = lax.broadcasted_iota(jnp.int32, logits.shape, 1)
    is_grp = jnp.logical_and(lane >= MOE_EXPERTS, lane < MOE_EXPERTS + MOE_GROUPS)
    lg = jnp.where(is_grp, logits, NEG_INF)
    eg = jnp.where(is_grp, jnp.exp(lg - jnp.max(lg, axis=-1, keepdims=True)), 0.0)
    pg = eg / jnp.sum(eg, axis=-1, keepdims=True)
    p_grp = jnp.max(pg, axis=-1, keepdims=True)
    g_sel = jnp.min(jnp.where(jnp.logical_and(is_grp, pg == p_grp), lane, 2 * LANES),
                    axis=-1, keepdims=True) - MOE_EXPERTS
    in_grp = jnp.logical_and(lane < MOE_EXPERTS, lane // MOE_EPG == g_sel)
    le = jnp.where(in_grp, logits, NEG_INF)
    ee = jnp.where(in_grp, jnp.exp(le - jnp.max(le, axis=-1, keepdims=True)), 0.0)
    pin = ee / jnp.sum(ee, axis=-1, keepdims=True)
    p1 = jnp.max(jnp.where(in_grp, pin, -1.0), axis=-1, keepdims=True)
    i1 = jnp.min(jnp.where(jnp.logical_and(in_grp, pin == p1), lane, 2 * LANES), axis=-1, keepdims=True)
    rest = jnp.logical_and(in_grp, lane != i1)
    p2 = jnp.max(jnp.where(rest, pin, -1.0), axis=-1, keepdims=True)
    i2 = jnp.min(jnp.where(jnp.logical_and(rest, pin == p2), lane, 2 * LANES), axis=-1, keepdims=True)
    tot = p1 + p2
    comb_ref[...] = (jnp.where(lane == i1, p_grp * p1 / tot, 0.0)
                     + jnp.where(lane == i2, p_grp * p2 / tot, 0.0))


def _merge(x2, ya, yb, ma, mb, w_proj_nsa, w_proj_gla, w_out, g_ffn, w_grp, b_grp, w_exp, b_exp):
    T, D = x2.shape
    tm = min(MERGE_TM, T)
    wr = jnp.pad(jnp.concatenate([w_exp, w_grp], axis=1), ((0, 0), (0, LANES - MOE_EXPERTS - MOE_GROUPS)))
    br = jnp.pad(jnp.concatenate([b_exp, b_grp]), (0, LANES - MOE_EXPERTS - MOE_GROUPS)).reshape(1, LANES)
    tok = lambda w: pl.BlockSpec((tm, w), lambda i: (i, 0))
    full = lambda a: pl.BlockSpec(a.shape, lambda i: (0, 0))
    wpa, wpb, wo = w_proj_nsa.astype(BF16), w_proj_gla.astype(BF16), w_out.astype(BF16)
    gf = g_ffn.reshape(1, D)
    sds = jax.ShapeDtypeStruct
    return pl.pallas_call(
        _merge_body,
        grid=(T // tm,),
        in_specs=[tok(D), tok(ya.shape[1]), tok(yb.shape[1]), tok(D), tok(D),
                  full(wpa), full(wpb), full(wo), full(gf), full(wr), full(br)],
        out_specs=(tok(D), tok(D), tok(LANES)),
        out_shape=(sds((T, D), F32), sds((T, D), BF16), sds((T, LANES), F32)),
        compiler_params=pltpu.CompilerParams(
            dimension_semantics=("parallel",), vmem_limit_bytes=VMEM_LIMIT),
        name="merge",
    )(x2, ya, yb, ma, mb, wpa, wpb, wo, gf, wr, br)


def _moe_rows(tw):
    rows = 2 * tw + MOE_EXPERTS * (MOE_ALIGN - 1) + MOE_RT
    return -(-rows // MOE_RT) * MOE_RT


def _moe_window(e, v_ref, comb_ref, wg_ref, wu_ref, wd_ref, tri_ref, y_ref,
                xs_ref, z_ref, cw_ref, meta_ref, pos_ref, tw, rmax):
    lane = lax.broadcasted_iota(jnp.int32, (1, LANES), 1)
    no_row = -1.0

    @pl.when(e == 0)
    def _route():
        comb = comb_ref[...]
        assigned = comb > 0.0
        a = jnp.where(assigned, 1.0, 0.0)
        tri = tri_ref[...]
        run = jnp.zeros((1, LANES), F32)
        ranks = []
        for b in range(tw // MOE_RT):
            ab = a[b * MOE_RT:(b + 1) * MOE_RT]
            ranks.append(_dot(tri, ab.astype(BF16)) + run)
            run = run + jnp.sum(ab, axis=0, keepdims=True)
        rank = jnp.concatenate(ranks, axis=0)
        cnt_pad = jnp.floor((run + (MOE_ALIGN - 1)) * (1.0 / MOE_ALIGN)) * MOE_ALIGN
        incl = jnp.broadcast_to(cnt_pad, (8, LANES))
        lane8 = lax.broadcasted_iota(jnp.int32, (8, LANES), 1)
        shift = 1
        while shift < MOE_EXPERTS:
            incl = incl + jnp.where(lane8 >= shift, pltpu.roll(incl, shift, 1), 0.0)
            shift *= 2
        offs = incl[0:1] - cnt_pad
        meta_ref[0:1, :] = offs
        meta_ref[1:2, :] = run
        row_of = offs + rank
        pos_a = jnp.min(jnp.where(assigned, row_of, 1e9), axis=-1, keepdims=True)
        pos_b = jnp.max(jnp.where(assigned, row_of, no_row), axis=-1, keepdims=True)
        pos_a = jnp.where(pos_a > 1e8, no_row, pos_a)
        pos_b = jnp.where(pos_b == pos_a, no_row, pos_b)
        lane_t = lax.broadcasted_iota(jnp.int32, (tw, LANES), 1)
        pos_ref[...] = jnp.where(lane_t == 0, pos_a, jnp.where(lane_t == 1, pos_b, no_row))
        pos_t = pos_ref[...].T
        pa, pb = pos_t[0:1], pos_t[1:2]
        c_split = jnp.concatenate(_split3(comb), axis=1)
        v = v_ref[...]
        for rt in range(rmax // MOE_RT):
            r = (rt * MOE_RT + lax.broadcasted_iota(jnp.int32, (MOE_RT, 1), 0)).astype(F32)
            p = jnp.where(r == pa, 1.0, jnp.where(r == pb, 1.0, 0.0)).astype(BF16)
            xs_ref[rt * MOE_RT:(rt + 1) * MOE_RT, :] = _dot(p, v).astype(BF16)
            cw = _dot(p, c_split)
            cw_ref[rt * MOE_RT:(rt + 1) * MOE_RT, :] = (cw[:, :LANES] + cw[:, LANES:2 * LANES]) + cw[:, 2 * LANES:]
        z_ref[...] = jnp.zeros_like(z_ref)

    pick = lambda row: jnp.sum(jnp.where(lane == e, meta_ref[row:row + 1, :], 0.0)).astype(jnp.int32)
    off_e, cnt_e = pick(0), pick(1)

    def row_tile(i, _):
        r0 = pl.multiple_of(off_e + i * MOE_ET, MOE_ALIGN)
        x = xs_ref[pl.ds(r0, MOE_ET), :]
        c = jnp.sum(jnp.where(lane == e, cw_ref[pl.ds(r0, MOE_ET), :], 0.0), axis=-1, keepdims=True)
        hdn = jax.nn.silu(_dot(x, wg_ref[0])) * _dot(x, wu_ref[0])
        z_ref[pl.ds(r0, MOE_ET), :] = _dot((c * hdn).astype(BF16), wd_ref[0]).astype(BF16)
        return 0

    lax.fori_loop(0, (cnt_e + MOE_ET - 1) // MOE_ET, row_tile, 0)

    @pl.when(e == pl.num_programs(1) - 1)
    def _combine():
        r = lax.broadcasted_iota(jnp.int32, (1, rmax), 1).astype(F32)
        z = z_ref[...]
        for tt in range(tw // MOE_RT):
            rows = slice(tt * MOE_RT, (tt + 1) * MOE_RT)
            pa, pb = pos_ref[rows, 0:1], pos_ref[rows, 1:2]
            q = jnp.where(r == pa, 1.0, jnp.where(r == pb, 1.0, 0.0)).astype(BF16)
            y_ref[rows, :] = _dot(q, z).astype(y_ref.dtype)


def _moe_body(v_ref, comb_ref, wg_ref, wu_ref, wd_ref, tri_ref, y_ref,
              xs_ref, z_ref, cw_ref, meta_ref, pos_ref, *, tw, rmax, n_win):
    e = pl.program_id(1)
    for w in range(n_win):
        rows = pl.ds(w * tw, tw)
        _moe_window(e, v_ref.at[rows], comb_ref.at[rows], wg_ref, wu_ref, wd_ref, tri_ref, y_ref.at[rows],
                    xs_ref.at[w], z_ref.at[w], cw_ref.at[w], meta_ref.at[w], pos_ref.at[w], tw, rmax)


def _moe(v, comb, w_gate, w_up, w_down):
    T, D = v.shape
    tw = min(MOE_TW, T)
    n_win = min(MOE_WINDOWS, T // tw)
    rmax = _moe_rows(tw)
    E, _, F = w_gate.shape
    wg, wu, wd = w_gate.astype(BF16), w_up.astype(BF16), w_down.astype(BF16)
    tri = jnp.asarray(np.tril(np.ones((MOE_RT, MOE_RT)), -1), BF16)
    tok = lambda w: pl.BlockSpec((n_win * tw, w), lambda i, e: (i, 0))
    tok_in = lambda w: pl.BlockSpec((n_win * tw, w), lambda i, e: (i, 0), pipeline_mode=pl.Buffered(1))
    return pl.pallas_call(
        functools.partial(_moe_body, tw=tw, rmax=rmax, n_win=n_win),
        grid=(T // (n_win * tw), E),
        in_specs=[tok_in(D), tok_in(LANES),
                  pl.BlockSpec((1, D, F), lambda i, e: (e, 0, 0)),
                  pl.BlockSpec((1, D, F), lambda i, e: (e, 0, 0)),
                  pl.BlockSpec((1, F, D), lambda i, e: (e, 0, 0)),
                  pl.BlockSpec(tri.shape, lambda i, e: (0, 0))],
        out_specs=tok(D),
        out_shape=jax.ShapeDtypeStruct((T, D), BF16),
        scratch_shapes=[pltpu.VMEM((n_win, rmax, D), BF16),
                        pltpu.VMEM((n_win, rmax, D), BF16),
                        pltpu.VMEM((n_win, rmax, LANES), F32),
                        pltpu.VMEM((n_win, 8, LANES), F32),
                        pltpu.VMEM((n_win, tw, LANES), F32)],
        compiler_params=pltpu.CompilerParams(
            dimension_semantics=("parallel", "arbitrary"), vmem_limit_bytes=VMEM_LIMIT),
        name="moe",
    )(v, comb, wg, wu, wd, tri)


def _final_body(h_ref, y_ref, g_ref, o_ref):
    h = h_ref[...] + y_ref[...].astype(F32)
    o_ref[...] = h * lax.rsqrt(jnp.mean(h * h, axis=-1, keepdims=True) + EPS) * g_ref[...]


def _final_norm(h, y, g_final):
    T, D = h.shape
    tm = min(FINAL_TM, T)
    tok = pl.BlockSpec((tm, D), lambda i: (i, 0))
    return pl.pallas_call(
        _final_body,
        grid=(T // tm,),
        in_specs=[tok, tok, pl.BlockSpec((1, D), lambda i: (0, 0))],
        out_specs=tok,
        out_shape=jax.ShapeDtypeStruct((T, D), F32),
        compiler_params=pltpu.CompilerParams(
            dimension_semantics=("parallel",), vmem_limit_bytes=VMEM_LIMIT),
        name="final_norm",
    )(h, y, g_final.reshape(1, D))


def _rope_tables(positions):
    half = NSA_DH // 2
    inv = 1.0 / (ROPE_THETA ** (jnp.arange(half, dtype=F32) / half))
    ang = positions.astype(F32)[..., None] * inv
    cos, sin = jnp.cos(ang), jnp.sin(ang)
    cs = jnp.concatenate([cos, cos, cos, cos], axis=-1)
    sn = jnp.concatenate([-sin, sin, -sin, sin], axis=-1)
    return cs, sn


def _layer(h, positions, g_mix, w_in, cmp_pos_k, cmp_w1_k, cmp_b1_k, cmp_w2_k, cmp_b2_k,
           cmp_pos_v, cmp_w1_v, cmp_b1_v, cmp_w2_v, cmp_b2_v, gla_w_a2, gla_b_a, gla_norm_g,
           w_proj_nsa, w_proj_gla, w_out, g_ffn, w_grp, b_grp, w_exp, b_exp, w_gate, w_up, w_down, g_out):
    B, S, D = h.shape
    cs, sn = _rope_tables(positions)
    n_chunks = S // CMP_STRIDE
    cmp_end = jnp.minimum(CMP_STRIDE * jnp.arange(n_chunks) + CMP_BLOCK - 1, S - 1)
    cs_c, sn_c = _rope_tables(jnp.take(positions, cmp_end, axis=1))
    (q, kvc, ks, vs, kw, vw, gq, gk, gv, gvt, gr, ma, mb, ng, ga) = _in_proj(h, g_mix, w_in, cs, sn)
    kc, vc = _compress(kvc, cs_c, sn_c, cmp_pos_k, cmp_w1_k, cmp_b1_k, cmp_w2_k, cmp_b2_k,
                       cmp_pos_v, cmp_w1_v, cmp_b1_v, cmp_w2_v, cmp_b2_v)
    ya = _nsa_attention(q, kc, vc, ks, vs, kw, vw, ng)
    yb = _gla_pairs(gq, gk, gv, gvt, ga, gr, gla_w_a2, gla_b_a, gla_norm_g)
    T = B * S
    h1, v, comb = _merge(h.reshape(T, D), ya.reshape(T, -1), yb.reshape(T, -1), ma.reshape(T, D), mb.reshape(T, D),
                         w_proj_nsa, w_proj_gla, w_out, g_ffn, w_grp, b_grp, w_exp, b_exp)
    y = _moe(v, comb, w_gate, w_up, w_down)
    return _final_norm(h1, y, g_out).reshape(B, S, D)


def kernel(x, positions, g_mix, w_in, cmp_pos_k, cmp_w1_k, cmp_b1_k, cmp_w2_k, cmp_b2_k, cmp_pos_v, cmp_w1_v,
           cmp_b1_v, cmp_w2_v, cmp_b2_v, gla_w_a2, gla_b_a, gla_norm_g, w_proj_nsa, w_proj_gla, w_out, g_ffn,
           w_grp, b_grp, w_exp, b_exp, w_gate, w_up, w_down, g_final):
    depth = g_mix.shape[0]
    assert depth == 1, "the final norm closes the single layer"
    return _layer(x, positions, g_mix[0], w_in[0], cmp_pos_k[0], cmp_w1_k[0], cmp_b1_k[0], cmp_w2_k[0],
                  cmp_b2_k[0], cmp_pos_v[0], cmp_w1_v[0], cmp_b1_v[0], cmp_w2_v[0], cmp_b2_v[0],
                  gla_w_a2[0], gla_b_a[0], gla_norm_g[0], w_proj_nsa[0], w_proj_gla[0], w_out[0],
                  g_ffn[0], w_grp[0], b_grp[0], w_exp[0], b_exp[0], w_gate[0], w_up[0], w_down[0], g_final)
```

```python
import functools

import numpy as np
import jax
import jax.numpy as jnp
from jax import lax
from jax.experimental import pallas as pl
from jax.experimental.pallas import tpu as pltpu

F32 = jnp.float32
BF16 = jnp.bfloat16

NSA_HEADS = 8
NSA_KV_GROUPS = 2
NSA_HPG = NSA_HEADS // NSA_KV_GROUPS
NSA_DH = 64
CMP_BLOCK = 32
CMP_STRIDE = 16
CMP_HIDDEN = 256
SLC_BLOCK = 64
SLC_TOPK = 16
WINDOW = 512
GLA_HEADS = 4
GLA_DK = 64
GLA_DV = 128
GLA_RANK = 16
GLA_TAU = 16.0
MOE_GROUPS = 4
MOE_EPG = 8
MOE_EXPERTS = MOE_GROUPS * MOE_EPG
MOE_DFF = 512
ROPE_THETA = 10000.0
EPS = 1e-6
NEG_INF = -1e30
FORCED_SCORE = 1e4
LOG2E = 1.4426950408889634

LANES = 128
VMEM_LIMIT = 56 * 1024 * 1024

IN_TM = 512
ATT_TQ = 128
ATT_TK = 512
GLA_C = 128
GLA_SUB = 16
GLA_HP = 2
GLA_UNROLL_DECAY = 4
GLA_UNROLL_PLAIN = 4
GLA_PLAIN_DECAY = 60.0
MERGE_TM = 512
MOE_TW = 1024
MOE_WINDOWS = 2
FINAL_TM = 1024
MOE_RT = 256
MOE_ET = 128
MOE_ALIGN = 16


def _dot(a, b):
    return jnp.dot(a, b, preferred_element_type=F32)


def _dot_nt(a, b):
    return lax.dot_general(a, b, (((1,), (1,)), ((), ())), preferred_element_type=F32)


def _split3(x):
    x1 = x.astype(BF16)
    r1 = x - x1.astype(F32)
    x2 = r1.astype(BF16)
    r2 = r1 - x2.astype(F32)
    x3 = r2.astype(BF16)
    return x1, x2, x3


def _dot_exact_lhs(a_bf16, x):
    x1, x2, x3 = _split3(x)
    return _dot(a_bf16, x1) + _dot(a_bf16, x2) + _dot(a_bf16, x3)


def _dot_f32(a, b):
    a1, a2, a3 = _split3(a)
    b1, b2, b3 = _split3(b)
    return (_dot(a1, b1) + (_dot(a1, b2) + _dot(a2, b1))
            + (_dot(a1, b3) + _dot(a2, b2) + _dot(a3, b1)))


def _rope_lanes(z, cs, sn):
    w = z.shape[-1]
    lane = lax.broadcasted_iota(jnp.int32, z.shape, 1)
    first_half = (lane % NSA_DH) < (NSA_DH // 2)
    rot = jnp.where(first_half, pltpu.roll(z, w - NSA_DH // 2, 1), pltpu.roll(z, NSA_DH // 2, 1))
    reps = w // LANES
    if reps > 1:
        cs = jnp.concatenate([cs] * reps, axis=1)
        sn = jnp.concatenate([sn] * reps, axis=1)
    return z * cs + rot * sn


_SEC = {}
_off = 0
for _name, _w in (("q", 512), ("kvc", 256), ("ks", 128), ("vs", 128), ("kw", 128), ("vw", 128),
                  ("gq", 256), ("gk", 256), ("gv", 512), ("gr", 512), ("ma", 1024), ("mb", 1024),
                  ("ng", 128), ("ga", 128)):
    _SEC[_name] = (_off, _off + _w)
    _off += _w
IN_NW = _off


def _in_proj_body(x_ref, g_ref, w_ref, cs_ref, sn_ref,
                  q_ref, kvc_ref, ks_ref, vs_ref, kw_ref, vw_ref,
                  gq_ref, gk_ref, gv_ref, gvt_ref, gr_ref, ma_ref, mb_ref, ng_ref, ga_ref):
    x = x_ref[0]
    var = jnp.mean(x * x, axis=-1, keepdims=True)
    u = (x * lax.rsqrt(var + EPS) * g_ref[...]).astype(BF16)
    cs = cs_ref[0]
    sn = sn_ref[0]

    def proj(name):
        a, b = _SEC[name]
        return _dot(u, w_ref[:, a:b])

    zq_t = (_rope_lanes(proj("q"), cs, sn) * (NSA_DH ** -0.5 * LOG2E)).T
    for h in range(NSA_HEADS):
        q_ref[0, h] = zq_t[h * NSA_DH:(h + 1) * NSA_DH].astype(BF16)
    kvc_ref[0] = proj("kvc")
    zks = _rope_lanes(proj("ks"), cs, sn)
    zkw = _rope_lanes(proj("kw"), cs, sn)
    zvs_t = proj("vs").T
    zvw_t = proj("vw").T
    tm = zks.shape[0]
    lane = lax.broadcasted_iota(jnp.int32, (tm, LANES), 1)
    blk = (pl.program_id(1) * tm + lax.broadcasted_iota(jnp.int32, (tm, LANES), 0)) // SLC_BLOCK
    onehot = jnp.where(lane - NSA_DH == blk, 1.0, 0.0)
    for g in range(NSA_KV_GROUPS):
        sl = slice(g * NSA_DH, (g + 1) * NSA_DH)
        k_front = zks if g == 0 else pltpu.roll(zks, LANES - g * NSA_DH, 1)
        ks_ref[0, g] = jnp.where(lane < NSA_DH, k_front, onehot).astype(BF16)
        kw_ref[0, g] = zkw[:, sl].astype(BF16)
        vs_ref[0, g] = zvs_t[sl].astype(BF16)
        vw_ref[0, g] = zvw_t[sl].astype(BF16)
    gq_ref[0] = proj("gq") * (GLA_DK ** -0.5)
    gk_ref[0] = proj("gk")
    zgv = proj("gv")
    gv_ref[0] = zgv.astype(BF16)
    gvt_ref[0] = zgv.T.astype(BF16)
    gr_ref[0] = proj("gr")
    ma_ref[0] = jax.nn.sigmoid(proj("ma")).astype(ma_ref.dtype)
    mb_ref[0] = jax.nn.sigmoid(proj("mb")).astype(mb_ref.dtype)
    ng_ref[0] = jax.nn.sigmoid(proj("ng")).T
    ga_ref[0] = proj("ga")


def _in_proj(x, g_mix, w_in, cs, sn):
    B, S, D = x.shape
    tm = min(IN_TM, S)
    splits = np.cumsum((512,) + (128,) * 6 + (24, 256, 256, 512, 16, 512, 1024, 1024))
    (wq, wkc, wvc, wks, wvs, wkw, wvw, wng, wgq, wgk, wgv, wga, wgr, wma, wmb) = jnp.split(
        w_in, splits[:-1].tolist(), axis=1)
    pad = lambda w: jnp.pad(w, ((0, 0), (0, LANES - w.shape[1])))
    w_all = jnp.concatenate([wq, wkc, wvc, wks, wvs, wkw, wvw, wgq, wgk, wgv, wgr, wma, wmb,
                             pad(wng), pad(wga)], axis=1).astype(BF16)
    assert w_all.shape[1] == IN_NW
    grid = (B, S // tm)
    tok = lambda w: pl.BlockSpec((1, tm, w), lambda b, i: (b, i, 0))
    head = lambda n, w: pl.BlockSpec((1, n, tm, w), lambda b, i: (b, 0, i, 0))
    head_t = lambda n, w: pl.BlockSpec((1, n, w, tm), lambda b, i: (b, 0, 0, i))
    sds = jax.ShapeDtypeStruct
    out_shape = (
        sds((B, NSA_HEADS, NSA_DH, S), BF16),
        sds((B, S, 256), F32),
        sds((B, NSA_KV_GROUPS, S, LANES), BF16),
        sds((B, NSA_KV_GROUPS, NSA_DH, S), BF16),
        sds((B, NSA_KV_GROUPS, S, NSA_DH), BF16),
        sds((B, NSA_KV_GROUPS, NSA_DH, S), BF16),
        sds((B, S, GLA_HEADS * GLA_DK), F32),
        sds((B, S, GLA_HEADS * GLA_DK), F32),
        sds((B, S, GLA_HEADS * GLA_DV), BF16),
        sds((B, GLA_HEADS * GLA_DV, S), BF16),
        sds((B, S, GLA_HEADS * GLA_DV), F32),
        sds((B, S, D), BF16),
        sds((B, S, D), BF16),
        sds((B, LANES, S), F32),
        sds((B, S, LANES), F32),
    )
    out_specs = (
        head_t(NSA_HEADS, NSA_DH), tok(256),
        head(NSA_KV_GROUPS, LANES), head_t(NSA_KV_GROUPS, NSA_DH),
        head(NSA_KV_GROUPS, NSA_DH), head_t(NSA_KV_GROUPS, NSA_DH),
        tok(GLA_HEADS * GLA_DK), tok(GLA_HEADS * GLA_DK),
        tok(GLA_HEADS * GLA_DV),
        pl.BlockSpec((1, GLA_HEADS * GLA_DV, tm), lambda b, i: (b, 0, i)),
        tok(GLA_HEADS * GLA_DV), tok(D), tok(D),
        pl.BlockSpec((1, LANES, tm), lambda b, i: (b, 0, i)), tok(LANES),
    )
    return pl.pallas_call(
        _in_proj_body,
        grid=grid,
        in_specs=[
            tok(D),
            pl.BlockSpec((1, D), lambda b, i: (0, 0)),
            pl.BlockSpec((D, IN_NW), lambda b, i: (0, 0), pipeline_mode=pl.Buffered(1)),
            tok(LANES), tok(LANES),
        ],
        out_specs=out_specs,
        out_shape=out_shape,
        compiler_params=pltpu.CompilerParams(
            dimension_semantics=("parallel", "parallel"), vmem_limit_bytes=VMEM_LIMIT),
        name="in_proj",
    )(x, g_mix.reshape(1, D), w_all, cs, sn)


def _compress_body(x_ref, p0_ref, p1_ref, w0_ref, w1_ref, b1_ref, w2_ref, b2_ref, cs_ref, sn_ref,
                   kc_ref, vc_ref, xk_ref, xv_ref):
    n = x_ref.shape[1] // CMP_STRIDE
    half_refs = (xk_ref, xv_ref)
    for s, half in enumerate(half_refs):
        half[...] = x_ref[0, :, s * LANES:(s + 1) * LANES]
    hid = w0_ref.shape[1] // len(half_refs)
    y0 = [None, None]
    y1 = [None, None]
    for t in range(CMP_STRIDE):
        for s, half in enumerate(half_refs):
            xt = half[pl.ds(t, n, stride=CMP_STRIDE), :]
            rows = slice((2 * t + s) * LANES, (2 * t + s + 1) * LANES)
            cols = slice(s * hid, (s + 1) * hid)
            lanes = slice(s * LANES, (s + 1) * LANES)
            d0 = _dot((xt + p0_ref[t:t + 1, lanes]).astype(BF16), w0_ref[rows, cols])
            d1 = _dot((xt + p1_ref[t:t + 1, lanes]).astype(BF16), w1_ref[rows, cols])
            y0[s] = d0 if y0[s] is None else y0[s] + d0
            y1[s] = d1 if y1[s] is None else y1[s] + d1
    y0 = jnp.concatenate(y0, axis=1)
    y1 = jnp.concatenate(y1, axis=1)
    h = jax.nn.gelu(y0 + pltpu.roll(y1, n - 1, 0) + b1_ref[...])
    o = _dot(h.astype(BF16), w2_ref[...]) + b2_ref[...]
    k = _rope_lanes(o[:, :LANES], cs_ref[0], sn_ref[0])
    v_t = o[:, LANES:].T
    for g in range(NSA_KV_GROUPS):
        sl = slice(g * NSA_DH, (g + 1) * NSA_DH)
        kc_ref[0, g] = k[:, sl].astype(BF16)
        vc_ref[0, g] = v_t[sl].astype(BF16)


def _compress(kvc, cs_c, sn_c, pos_k, w1_k, b1_k, w2_k, b2_k, pos_v, w1_v, b1_v, w2_v, b2_v):
    B, S, _ = kvc.shape
    n = S // CMP_STRIDE
    ns = 2 * NSA_KV_GROUPS
    eye = jnp.eye(ns, dtype=F32)
    w1s = jnp.stack([w1_k, w1_k, w1_v, w1_v])
    poss = jnp.stack([pos_k, pos_k, pos_v, pos_v])
    wbig, pbig = [], []
    for m in range(CMP_BLOCK // CMP_STRIDE):
        wm = w1s.reshape(ns, CMP_BLOCK, NSA_DH, CMP_HIDDEN)[:, m * CMP_STRIDE:(m + 1) * CMP_STRIDE]
        wb = jnp.einsum('ctdj,ce->tcdej', wm, eye).reshape(CMP_STRIDE * ns * NSA_DH, ns * CMP_HIDDEN)
        wbig.append(wb.astype(BF16))
        pm = poss[:, m * CMP_STRIDE:(m + 1) * CMP_STRIDE]
        pbig.append(pm.transpose(1, 0, 2).reshape(CMP_STRIDE, ns * NSA_DH))
    b1 = jnp.concatenate([b1_k, b1_k, b1_v, b1_v]).reshape(1, ns * CMP_HIDDEN)
    w2s = jnp.stack([w2_k, w2_k, w2_v, w2_v])
    w2big = jnp.einsum('cjd,ce->cjed', w2s, eye).reshape(ns * CMP_HIDDEN, ns * NSA_DH).astype(BF16)
    b2 = jnp.concatenate([b2_k, b2_k, b2_v, b2_v]).reshape(1, ns * NSA_DH)
    full = lambda a: pl.BlockSpec(a.shape, lambda b: (0,) * a.ndim)
    sds = jax.ShapeDtypeStruct
    return pl.pallas_call(
        _compress_body,
        grid=(B,),
        in_specs=[pl.BlockSpec((1, S, ns * NSA_DH), lambda b: (b, 0, 0)),
                  full(pbig[0]), full(pbig[1]), full(wbig[0]), full(wbig[1]), full(b1), full(w2big), full(b2),
                  pl.BlockSpec((1, n, LANES), lambda b: (b, 0, 0)),
                  pl.BlockSpec((1, n, LANES), lambda b: (b, 0, 0))],
        out_specs=(pl.BlockSpec((1, NSA_KV_GROUPS, n, NSA_DH), lambda b: (b, 0, 0, 0)),
                   pl.BlockSpec((1, NSA_KV_GROUPS, NSA_DH, n), lambda b: (b, 0, 0, 0))),
        out_shape=(sds((B, NSA_KV_GROUPS, n, NSA_DH), BF16), sds((B, NSA_KV_GROUPS, NSA_DH, n), BF16)),
        scratch_shapes=[pltpu.VMEM((S, LANES), F32), pltpu.VMEM((S, LANES), F32)],
        compiler_params=pltpu.CompilerParams(
            dimension_semantics=("parallel",), vmem_limit_bytes=VMEM_LIMIT),
        name="compress",
    )(kvc, pbig[0], pbig[1], wbig[0], wbig[1], b1, w2big, b2, cs_c, sn_c)


NSA_ONES = 16


def _with_ones(v_t):
    return jnp.concatenate([v_t, jnp.ones((NSA_ONES, v_t.shape[1]), v_t.dtype)], axis=0)


def _softmax_step(s, m, acc, v_t):
    m_new = jnp.maximum(m, jnp.max(s, axis=0, keepdims=True))
    p = jnp.exp2(s - m_new).astype(BF16)
    acc = jnp.exp2(m - m_new) * acc + _dot(_with_ones(v_t), p)
    return m_new, acc


def _nsa_body(q_ref, kc_ref, vc_ref, ks_ref, vs_ref, kw_ref, vw_ref, ng_ref, ovt_ref,
              o_ref, sa_ref, sb_ref, sw_ref, *, tq, tk, n_slc):
    gi = pl.program_id(1)
    qi = pl.program_id(2)
    q0 = qi * tq
    n_cmp = kc_ref.shape[2]
    W = NSA_HPG * tq
    per_head = lambda x: jnp.concatenate([x] * NSA_HPG, axis=1)
    q_all = jnp.concatenate([q_ref[0, h] for h in range(NSA_HPG)], axis=1)
    t_lane = q0 + lax.broadcasted_iota(jnp.int32, (1, tq), 1)

    cmp_end = CMP_STRIDE * lax.broadcasted_iota(jnp.int32, (n_cmp, 1), 0) + (CMP_BLOCK - 1)
    ok = cmp_end <= t_lane
    sm = _dot(kc_ref[0, 0], q_all) + per_head(jnp.where(ok, 0.0, NEG_INF))
    e = jnp.exp2(sm - jnp.max(sm, axis=0, keepdims=True))
    p = e * (1.0 / jnp.sum(e, axis=0, keepdims=True)) * per_head(jnp.where(ok, 1.0, 0.0))
    o_cmp = _dot(vc_ref[0, 0], p.astype(BF16))
    p_sum = p[:, 0:tq]
    for h in range(1, NSA_HPG):
        p_sum = p_sum + p[:, h * tq:(h + 1) * tq]

    p_hi = p_sum.astype(BF16)
    p_lo = (p_sum - p_hi.astype(F32)).astype(BF16)
    imp = _dot(ovt_ref[...], p_hi) + _dot(ovt_ref[...], p_lo)
    blk = lax.broadcasted_iota(jnp.int32, (n_slc, tq), 0)
    cur = (q0 + lax.broadcasted_iota(jnp.int32, (n_slc, tq), 1)) // SLC_BLOCK
    forced = (blk == 0) | (blk == cur) | (blk == cur - 1)
    imp = jnp.where(forced, FORCED_SCORE, imp)
    imp = jnp.where(blk <= cur, imp, NEG_INF)
    wk = WINDOW + tq
    w0 = pl.multiple_of(jnp.maximum(q0 - WINDOW, 0), tq)
    kpos = w0 + lax.broadcasted_iota(jnp.int32, (wk, 1), 0)
    okw = jnp.logical_and(kpos <= t_lane, kpos > t_lane - WINDOW)
    sw_ref[...] = _dot(kw_ref[0, 0, pl.ds(w0, wk), :], q_all) + per_head(jnp.where(okw, 0.0, NEG_INF))

    SUBL = 8
    groups = [imp[r:r + SUBL] for r in range(0, n_slc, SUBL)]
    ranks = [jnp.zeros((SUBL, tq), F32) for _ in groups]
    for i in range(n_slc):
        row = imp[i:i + 1, :]
        for gidx, grp in enumerate(groups):
            r = gidx * SUBL
            ge = jnp.where(row >= grp, 1.0, 0.0)
            gt = jnp.where(row > grp, 1.0, 0.0)
            if r > i:
                inc = ge
            elif r + SUBL - 1 <= i:
                inc = gt
            else:
                inc = jnp.where(blk[r:r + SUBL] > i, ge, gt)
            ranks[gidx] = ranks[gidx] + inc
    rank = jnp.concatenate(ranks, axis=0)
    bias = jnp.where(rank < float(min(SLC_TOPK, n_slc)), 0.0, NEG_INF)
    if n_slc < NSA_DH:
        bias = jnp.concatenate([bias, jnp.zeros((NSA_DH - n_slc, tq), F32)], axis=0)
    q_aug = jnp.concatenate([q_all, per_head(bias).astype(BF16)], axis=0)

    kpos_l = lax.broadcasted_iota(jnp.int32, (tk, 1), 0)

    n_tiles = q0 // tk + 1
    last_tile = ks_ref.shape[2] // tk - 1

    def tile_start(jt):
        return pl.multiple_of(jnp.minimum(jt, last_tile) * tk, tk)

    def scores_into(s_ref, jt):
        causal = per_head(jnp.where(jt * tk + kpos_l <= t_lane, 0.0, NEG_INF))
        s_ref[...] = _dot(ks_ref[0, 0, pl.ds(tile_start(jt), tk), :], q_aug) + causal

    def consume(s_ref, jt, m, acc):
        return _softmax_step(s_ref[...], m, acc, vs_ref[0, 0, :, pl.ds(tile_start(jt), tk)])

    def slc_pair(i, carry):
        m, acc = carry
        scores_into(sb_ref, 2 * i + 1)
        m, acc = consume(sa_ref, 2 * i, m, acc)
        scores_into(sa_ref, 2 * i + 2)
        return consume(sb_ref, 2 * i + 1, m, acc)

    init = (jnp.full((1, W), NEG_INF, F32), jnp.zeros((NSA_DH + NSA_ONES, W), F32))
    scores_into(sa_ref, 0)

    sw = sw_ref[...]
    ew = jnp.exp2(sw - jnp.max(sw, axis=0, keepdims=True)).astype(BF16)
    acc_w = _dot(_with_ones(vw_ref[0, 0, :, pl.ds(w0, wk)]), ew)
    l_w = acc_w[NSA_DH:NSA_DH + 1]
    acc_w = acc_w[:NSA_DH]

    _, acc_s = lax.fori_loop(0, (n_tiles + 1) // 2, slc_pair, init)
    l_s = acc_s[NSA_DH:NSA_DH + 1]
    acc_s = acc_s[:NSA_DH]

    def gate(j):
        return jnp.concatenate([ng_ref[0, pl.ds((gi * NSA_HPG + h) * 3 + j, 1), :]
                                for h in range(NSA_HPG)], axis=1)

    out_t = gate(0) * o_cmp + (gate(1) * (1.0 / l_s)) * acc_s + (gate(2) * (1.0 / l_w)) * acc_w
    out_t = jnp.concatenate([out_t[:, h * tq:(h + 1) * tq] for h in range(NSA_HPG)], axis=0)
    o_ref[0] = out_t.T.astype(o_ref.dtype)


def _nsa_attention(q, kc, vc, ks, vs, kw, vw, ng):
    B, H, dh, S = q.shape
    tq = min(ATT_TQ, S)
    tk = min(ATT_TK, S)
    assert S % tk == 0 and tk % tq == 0 and S >= WINDOW + tq and S // SLC_BLOCK <= dh
    n_cmp = kc.shape[2]
    n_slc = S // SLC_BLOCK
    c0 = CMP_STRIDE * np.arange(n_cmp)[None, :]
    s0 = SLC_BLOCK * np.arange(n_slc)[:, None]
    ov = np.clip(np.minimum(c0 + CMP_BLOCK, s0 + SLC_BLOCK) - np.maximum(c0, s0), 0, None) / CMP_BLOCK
    ovt = jnp.asarray(ov, BF16)
    grid = (B, NSA_KV_GROUPS, S // tq)
    k_spec = lambda n: pl.BlockSpec((1, 1, n, dh), lambda b, g, i: (b, g, 0, 0))
    vt_spec = lambda n: pl.BlockSpec((1, 1, dh, n), lambda b, g, i: (b, g, 0, 0))
    return pl.pallas_call(
        functools.partial(_nsa_body, tq=tq, tk=tk, n_slc=n_slc),
        grid=grid,
        in_specs=[
            pl.BlockSpec((1, NSA_HPG, dh, tq), lambda b, g, i: (b, g, 0, i)),
            k_spec(n_cmp), vt_spec(n_cmp),
            pl.BlockSpec((1, 1, S, 2 * dh), lambda b, g, i: (b, g, 0, 0)), vt_spec(S),
            k_spec(S), vt_spec(S),
            pl.BlockSpec((1, LANES, tq), lambda b, g, i: (b, 0, i)),
            pl.BlockSpec(ovt.shape, lambda b, g, i: (0, 0)),
        ],
        out_specs=pl.BlockSpec((1, tq, NSA_HPG * dh), lambda b, g, i: (b, i, g)),
        out_shape=jax.ShapeDtypeStruct((B, S, H * dh), BF16),
        scratch_shapes=[pltpu.VMEM((tk, NSA_HPG * tq), F32),
                        pltpu.VMEM((tk, NSA_HPG * tq), F32),
                        pltpu.VMEM((WINDOW + tq, NSA_HPG * tq), F32)],
        compiler_params=pltpu.CompilerParams(
            dimension_semantics=("parallel", "parallel", "arbitrary"), vmem_limit_bytes=VMEM_LIMIT),
        name="nsa_attn",
    )(q, kc, vc, ks, vs, kw, vw, ng, ovt)


def _dot_3pass(a, b):
    a1 = a.astype(BF16)
    a2 = (a - a1.astype(F32)).astype(BF16)
    b1 = b.astype(BF16)
    b2 = (b - b1.astype(F32)).astype(BF16)
    return _dot(a1, b1) + (_dot(a1, b2) + _dot(a2, b1))


def _gla_pair_body(q_ref, k_ref, v_ref, vt_ref, ga_ref, r_ref, wa_ref, ba_ref, ng_ref, tri_ref,
                   o_ref, state_ref, b_ref, kp_ref, bp_ref, vp_ref, plain_ref, *, n_chunks):
    C, SUB, dk, dv, hp = GLA_C, GLA_SUB, GLA_DK, GLA_DV, GLA_HP
    W = hp * dk
    lane = lax.broadcasted_iota(jnp.int32, (1, W), 1)
    of_head = [lane // dk == hh for hh in range(hp)]
    tri = tri_ref[...]
    t_loc = lax.broadcasted_iota(jnp.int32, (C, 1), 0)
    state_ref[...] = jnp.zeros_like(state_ref)
    kp_ref[0:SUB, :] = jnp.zeros((SUB, W), F32)
    bp_ref[0:SUB, :] = jnp.zeros((SUB, W), F32)
    vp_ref[:, 0:SUB, :] = jnp.zeros((hp, SUB, dv), F32)

    def decay_chunk(c, all_plain):
        c0 = pl.multiple_of(c * C, C)
        la = jax.nn.log_sigmoid(_dot_3pass(ga_ref[0, pl.ds(c0, C), :], wa_ref[0]) + ba_ref[0]) / GLA_TAU
        b = _dot_exact_lhs(tri, la)
        b_ref[pl.ds(c0, C), :] = b
        plain = (jnp.min(b[C - 1:C, :]) > -GLA_PLAIN_DECAY).astype(jnp.int32)
        plain_ref[c] = plain
        return jnp.minimum(all_plain, plain)

    all_plain = lax.fori_loop(0, n_chunks, decay_chunk, jnp.int32(1), unroll=GLA_UNROLL_DECAY)

    def chunk(c, check_decay):
        c0 = pl.multiple_of(c * C, C)
        b = b_ref[pl.ds(c0, C), :]
        q = q_ref[0, pl.ds(c0, C), :]
        k = k_ref[0, pl.ds(c0, C), :]
        v = [v_ref[0, pl.ds(c0, C), hh * dv:(hh + 1) * dv] for hh in range(hp)]
        vt = [vt_ref[0, hh * dv:(hh + 1) * dv, pl.ds(c0, C)] for hh in range(hp)]
        b_last = b[C - 1:C, :]
        st = state_ref[...]
        st_b = st.astype(BF16)
        qg = q * jnp.exp(b)
        qg_h = [jnp.where(of_head[hh], qg, 0.0).astype(BF16) for hh in range(hp)]
        o_inter = [_dot_nt(qg_h[hh], st_b) for hh in range(hp)]

        def intra_plain():
            ke = (k * jnp.exp(-b)).astype(BF16)
            row = lax.broadcasted_iota(jnp.int32, (C, C), 0)
            col = lax.broadcasted_iota(jnp.int32, (C, C), 1)
            return tuple(_dot(jnp.where(row >= col, _dot_nt(qg_h[hh], ke), 0.0).astype(BF16), v[hh])
                         for hh in range(hp))

        def intra_strong_decay():
            far = [[jnp.zeros((SUB, dv), F32)] for _ in range(hp)]
            for i in range(1, C // SUB):
                r0 = i * SUB
                b_first = b[r0:r0 + 1, :]
                qt = q[r0:r0 + SUB] * jnp.exp(b[r0:r0 + SUB] - b_first)
                kt = (k[:r0] * jnp.exp(b_first - b[:r0])).astype(BF16)
                lt = lax.broadcasted_iota(jnp.int32, (SUB, r0), 0)
                ls = lax.broadcasted_iota(jnp.int32, (SUB, r0), 1)
                for hh in range(hp):
                    a = _dot_nt(jnp.where(of_head[hh], qt, 0.0).astype(BF16), kt)
                    a = jnp.where(lt + (r0 - SUB) >= ls, a, 0.0)
                    far[hh].append(_dot(a.astype(BF16), v[hh][:r0]))
            acc = [jnp.concatenate(far[hh], axis=0) for hh in range(hp)]
            kp_ref[SUB:SUB + C, :] = k
            bp_ref[SUB:SUB + C, :] = b
            for hh in range(hp):
                vp_ref[hh, SUB:SUB + C, :] = v[hh].astype(F32)
            for d in range(SUB):
                kd = kp_ref[SUB - d:SUB - d + C, :]
                bd = bp_ref[SUB - d:SUB - d + C, :]
                valid = t_loc >= d
                w = jnp.exp(jnp.where(valid, b - bd, 0.0))
                x = jnp.where(valid, q * kd * w, 0.0)
                for hh in range(hp):
                    a_d = jnp.sum(jnp.where(of_head[hh], x, 0.0), axis=-1, keepdims=True)
                    acc[hh] = acc[hh] + a_d * vp_ref[hh, SUB - d:SUB - d + C, :]
            return tuple(acc)

        if check_decay:
            o_intra = lax.cond(plain_ref[c] > 0, intra_plain, intra_strong_decay)
        else:
            o_intra = intra_plain()
        k_dec = (k * jnp.exp(b_last - b)).astype(BF16)
        upd = _dot(vt[hp - 1], k_dec)
        for hh in range(hp - 2, -1, -1):
            upd = jnp.where(of_head[hh], _dot(vt[hh], k_dec), upd)
        state_ref[...] = st * jnp.exp(b_last) + upd
        for hh in range(hp):
            o = o_inter[hh] + o_intra[hh]
            o = o * lax.rsqrt(jnp.mean(o * o, axis=-1, keepdims=True) + EPS) * ng_ref[hh]
            gate = jax.nn.silu(r_ref[0, pl.ds(c0, C), hh * dv:(hh + 1) * dv])
            o_ref[0, pl.ds(c0, C), hh * dv:(hh + 1) * dv] = (o * gate).astype(o_ref.dtype)
        return 0

    @pl.when(all_plain > 0)
    def _():
        lax.fori_loop(0, n_chunks, lambda c, _: chunk(c, False), 0, unroll=GLA_UNROLL_PLAIN)

    @pl.when(all_plain <= 0)
    def _():
        lax.fori_loop(0, n_chunks, lambda c, _: chunk(c, True), 0)


def _gla_pairs(gq, gk, gv, gvt, ga, gr, w_a2, b_a, norm_g):
    B, S, _ = gq.shape
    H, dk, dv, hp = GLA_HEADS, GLA_DK, GLA_DV, GLA_HP
    W = hp * dk
    assert W == LANES and H % hp == 0
    wa = jnp.pad(w_a2, ((0, LANES - GLA_RANK), (0, 0))).reshape(LANES, H // hp, W).transpose(1, 0, 2)
    ba = b_a.reshape(H // hp, 1, W)
    ng = norm_g.reshape(H, 1, dv)
    tri = jnp.asarray(np.tril(np.ones((GLA_C, GLA_C))), BF16)
    n_chunks = S // GLA_C
    tok_spec = lambda w: pl.BlockSpec((1, S, hp * w), lambda b, h: (b, 0, h))
    return pl.pallas_call(
        functools.partial(_gla_pair_body, n_chunks=n_chunks),
        grid=(B, H // hp),
        in_specs=[tok_spec(dk), tok_spec(dk), tok_spec(dv),
                  pl.BlockSpec((1, hp * dv, S), lambda b, h: (b, h, 0)),
                  pl.BlockSpec((1, S, LANES), lambda b, h: (b, 0, 0)),
                  tok_spec(dv),
                  pl.BlockSpec((1, LANES, W), lambda b, h: (h, 0, 0)),
                  pl.BlockSpec((1, 1, W), lambda b, h: (h, 0, 0)),
                  pl.BlockSpec((hp, 1, dv), lambda b, h: (h, 0, 0)),
                  pl.BlockSpec(tri.shape, lambda b, h: (0, 0))],
        out_specs=tok_spec(dv),
        out_shape=jax.ShapeDtypeStruct((B, S, H * dv), BF16),
        scratch_shapes=[pltpu.VMEM((dv, W), F32),
                        pltpu.VMEM((S, W), F32),
                        pltpu.VMEM((GLA_SUB + GLA_C, W), F32),
                        pltpu.VMEM((GLA_SUB + GLA_C, W), F32),
                        pltpu.VMEM((hp, GLA_SUB + GLA_C, dv), F32),
                        pltpu.SMEM((n_chunks,), jnp.int32)],
        compiler_params=pltpu.CompilerParams(
            dimension_semantics=("parallel", "parallel"), vmem_limit_bytes=VMEM_LIMIT),
        name="gla",
    )(gq, gk, gv, gvt, ga, gr, wa, ba, ng, tri)


def _merge_body(x_ref, ya_ref, yb_ref, ma_ref, mb_ref, wpa_ref, wpb_ref, wo_ref, gf_ref, wr_ref, br_ref,
                h_ref, v_ref, comb_ref):
    y_a = _dot(ya_ref[...], wpa_ref[...])
    y_b = _dot(yb_ref[...], wpb_ref[...])
    mixed = ma_ref[...] * y_a + mb_ref[...] * y_b
    h = x_ref[...] + _dot(mixed.astype(BF16), wo_ref[...])
    h_ref[...] = h
    v = h * lax.rsqrt(jnp.mean(h * h, axis=-1, keepdims=True) + EPS) * gf_ref[...]
    v_ref[...] = v.astype(BF16)
    wr = wr_ref[...]
    w_hi = wr.astype(BF16)
    w_lo = (wr - w_hi.astype(F32)).astype(BF16)
    v_hi = v.astype(BF16)
    v_lo = (v - v_hi.astype(F32)).astype(BF16)
    both = _dot(v_hi, jnp.concatenate([w_hi, w_lo], axis=1))
    logits = (both[:, :LANES] + both[:, LANES:]) + _dot(v_lo, w_hi) + br_ref[...]
    lane = lax.broadcasted_iota(jnp.int32, logits.shape, 1)
    is_grp = jnp.logical_and(lane >= MOE_EXPERTS, lane < MOE_EXPERTS + MOE_GROUPS)
    lg = jnp.where(is_grp, logits, NEG_INF)
    eg = jnp.where(is_grp, jnp.exp(lg - jnp.max(lg, axis=-1, keepdims=True)), 0.0)
    pg = eg / jnp.sum(eg, axis=-1, keepdims=True)
    p_grp = jnp.max(pg, axis=-1, keepdims=True)
    g_sel = jnp.min(jnp.where(jnp.logical_and(is_grp, pg == p_grp), lane, 2 * LANES),
                    axis=-1, keepdims=True) - MOE_EXPERTS
    in_grp = jnp.logical_and(lane < MOE_EXPERTS, lane // MOE_EPG == g_sel)
    le = jnp.where(in_grp, logits, NEG_INF)
    ee = jnp.where(in_grp, jnp.exp(le - jnp.max(le, axis=-1, keepdims=True)), 0.0)
    pin = ee / jnp.sum(ee, axis=-1, keepdims=True)
    p1 = jnp.max(jnp.where(in_grp, pin, -1.0), axis=-1, keepdims=True)
    i1 = jnp.min(jnp.where(jnp.logical_and(in_grp, pin == p1), lane, 2 * LANES), axis=-1, keepdims=True)
    rest = jnp.logical_and(in_grp, lane != i1)
    p2 = jnp.max(jnp.where(rest, pin, -1.0), axis=-1, keepdims=True)
    i2 = jnp.min(jnp.where(jnp.logical_and(rest, pin == p2), lane, 2 * LANES), axis=-1, keepdims=True)
    tot = p1 + p2
    comb_ref[...] = (jnp.where(lane == i1, p_grp * p1 / tot, 0.0)
                     + jnp.where(lane == i2, p_grp * p2 / tot, 0.0))


def _merge(x2, ya, yb, ma, mb, w_proj_nsa, w_proj_gla, w_out, g_ffn, w_grp, b_grp, w_exp, b_exp):
    T, D = x2.shape
    tm = min(MERGE_TM, T)
    wr = jnp.pad(jnp.concatenate([w_exp, w_grp], axis=1), ((0, 0), (0, LANES - MOE_EXPERTS - MOE_GROUPS)))
    br = jnp.pad(jnp.concatenate([b_exp, b_grp]), (0, LANES - MOE_EXPERTS - MOE_GROUPS)).reshape(1, LANES)
    tok = lambda w: pl.BlockSpec((tm, w), lambda i: (i, 0))
    full = lambda a: pl.BlockSpec(a.shape, lambda i: (0, 0))
    wpa, wpb, wo = w_proj_nsa.astype(BF16), w_proj_gla.astype(BF16), w_out.astype(BF16)
    gf = g_ffn.reshape(1, D)
    sds = jax.ShapeDtypeStruct
    return pl.pallas_call(
        _merge_body,
        grid=(T // tm,),
        in_specs=[tok(D), tok(ya.shape[1]), tok(yb.shape[1]), tok(D), tok(D),
                  full(wpa), full(wpb), full(wo), full(gf), full(wr), full(br)],
        out_specs=(tok(D), tok(D), tok(LANES)),
        out_shape=(sds((T, D), F32), sds((T, D), BF16), sds((T, LANES), F32)),
        compiler_params=pltpu.CompilerParams(
            dimension_semantics=("parallel",), vmem_limit_bytes=VMEM_LIMIT),
        name="merge",
    )(x2, ya, yb, ma, mb, wpa, wpb, wo, gf, wr, br)


def _moe_rows(tw):
    rows = 2 * tw + MOE_EXPERTS * (MOE_ALIGN - 1) + MOE_RT
    return -(-rows // MOE_RT) * MOE_RT


def _moe_window(e, v_ref, comb_ref, wg_ref, wu_ref, wd_ref, tri_ref, y_ref,
                xs_ref, z_ref, meta_ref, pos_ref, tw, rmax):
    lane = lax.broadcasted_iota(jnp.int32, (1, LANES), 1)
    no_row = -1.0

    @pl.when(e == 0)
    def _route():
        comb = comb_ref[...]
        assigned = comb > 0.0
        a = jnp.where(assigned, 1.0, 0.0)
        tri = tri_ref[...]
        run = jnp.zeros((1, LANES), F32)
        ranks = []
        for b in range(tw // MOE_RT):
            ab = a[b * MOE_RT:(b + 1) * MOE_RT]
            ranks.append(_dot(tri, ab.astype(BF16)) + run)
            run = run + jnp.sum(ab, axis=0, keepdims=True)
        rank = jnp.concatenate(ranks, axis=0)
        cnt_pad = jnp.floor((run + (MOE_ALIGN - 1)) * (1.0 / MOE_ALIGN)) * MOE_ALIGN
        incl = jnp.broadcast_to(cnt_pad, (8, LANES))
        lane8 = lax.broadcasted_iota(jnp.int32, (8, LANES), 1)
        shift = 1
        while shift < MOE_EXPERTS:
            incl = incl + jnp.where(lane8 >= shift, pltpu.roll(incl, shift, 1), 0.0)
            shift *= 2
        offs = incl[0:1] - cnt_pad
        meta_ref[0:1, :] = offs
        meta_ref[1:2, :] = run
        row_of = offs + rank
        pos_a = jnp.min(jnp.where(assigned, row_of, 1e9), axis=-1, keepdims=True)
        pos_b = jnp.max(jnp.where(assigned, row_of, no_row), axis=-1, keepdims=True)
        pos_a = jnp.where(pos_a > 1e8, no_row, pos_a)
        pos_b = jnp.where(pos_b == pos_a, no_row, pos_b)
        w_a = jnp.sum(jnp.where(jnp.logical_and(assigned, row_of == pos_a), comb, 0.0), axis=-1, keepdims=True)
        w_b = jnp.sum(jnp.where(jnp.logical_and(assigned, row_of == pos_b), comb, 0.0), axis=-1, keepdims=True)
        lane_t = lax.broadcasted_iota(jnp.int32, (tw, LANES), 1)
        pos_ref[...] = jnp.where(lane_t == 0, pos_a, jnp.where(lane_t == 1, pos_b, jnp.where(
            lane_t == 2, w_a, jnp.where(lane_t == 3, w_b, no_row))))
        pos_t = pos_ref[...].T
        pa, pb = pos_t[0:1], pos_t[1:2]
        v = v_ref[...]
        for rt in range(rmax // MOE_RT):
            r = (rt * MOE_RT + lax.broadcasted_iota(jnp.int32, (MOE_RT, 1), 0)).astype(F32)
            p = jnp.where(r == pa, 1.0, jnp.where(r == pb, 1.0, 0.0)).astype(BF16)
            xs_ref[rt * MOE_RT:(rt + 1) * MOE_RT, :] = _dot(p, v).astype(BF16)
        z_ref[...] = jnp.zeros_like(z_ref)

    pick = lambda row: jnp.sum(jnp.where(lane == e, meta_ref[row:row + 1, :], 0.0)).astype(jnp.int32)
    off_e, cnt_e = pick(0), pick(1)

    def row_tile(i, _):
        r0 = pl.multiple_of(off_e + i * MOE_ET, MOE_ALIGN)
        x = xs_ref[pl.ds(r0, MOE_ET), :]
        hdn = jax.nn.silu(_dot(x, wg_ref[0])) * _dot(x, wu_ref[0])
        z_ref[pl.ds(r0, MOE_ET), :] = _dot(hdn.astype(BF16), wd_ref[0]).astype(BF16)
        return 0

    lax.fori_loop(0, (cnt_e + MOE_ET - 1) // MOE_ET, row_tile, 0)

    @pl.when(e == pl.num_programs(1) - 1)
    def _combine():
        r = lax.broadcasted_iota(jnp.int32, (1, rmax), 1).astype(F32)
        z = z_ref[...]
        for tt in range(tw // MOE_RT):
            rows = slice(tt * MOE_RT, (tt + 1) * MOE_RT)
            pa, pb = pos_ref[rows, 0:1], pos_ref[rows, 1:2]
            w_a, w_b = pos_ref[rows, 2:3], pos_ref[rows, 3:4]
            q = jnp.where(r == pa, w_a, jnp.where(r == pb, w_b, 0.0)).astype(BF16)
            y_ref[rows, :] = _dot(q, z).astype(y_ref.dtype)


def _moe_body(v_ref, comb_ref, wg_ref, wu_ref, wd_ref, tri_ref, y_ref,
              xs_ref, z_ref, meta_ref, pos_ref, *, tw, rmax, n_win):
    e = pl.program_id(1)
    for w in range(n_win):
        rows = pl.ds(w * tw, tw)
        _moe_window(e, v_ref.at[rows], comb_ref.at[rows], wg_ref, wu_ref, wd_ref, tri_ref, y_ref.at[rows],
                    xs_ref.at[w], z_ref.at[w], meta_ref.at[w], pos_ref.at[w], tw, rmax)


def _moe(v, comb, w_gate, w_up, w_down):
    T, D = v.shape
    tw = min(MOE_TW, T)
    n_win = min(MOE_WINDOWS, T // tw)
    rmax = _moe_rows(tw)
    E, _, F = w_gate.shape
    wg, wu, wd = w_gate.astype(BF16), w_up.astype(BF16), w_down.astype(BF16)
    tri = jnp.asarray(np.tril(np.ones((MOE_RT, MOE_RT)), -1), BF16)
    tok = lambda w: pl.BlockSpec((n_win * tw, w), lambda i, e: (i, 0))
    tok_in = lambda w: pl.BlockSpec((n_win * tw, w), lambda i, e: (i, 0), pipeline_mode=pl.Buffered(1))
    return pl.pallas_call(
        functools.partial(_moe_body, tw=tw, rmax=rmax, n_win=n_win),
        grid=(T // (n_win * tw), E),
        in_specs=[tok_in(D), tok_in(LANES),
                  pl.BlockSpec((1, D, F), lambda i, e: (e, 0, 0)),
                  pl.BlockSpec((1, D, F), lambda i, e: (e, 0, 0)),
                  pl.BlockSpec((1, F, D), lambda i, e: (e, 0, 0)),
                  pl.BlockSpec(tri.shape, lambda i, e: (0, 0))],
        out_specs=tok(D),
        out_shape=jax.ShapeDtypeStruct((T, D), BF16),
        scratch_shapes=[pltpu.VMEM((n_win, rmax, D), BF16),
                        pltpu.VMEM((n_win, rmax, D), BF16),
                        pltpu.VMEM((n_win, 8, LANES), F32),
                        pltpu.VMEM((n_win, tw, LANES), F32)],
        compiler_params=pltpu.CompilerParams(
            dimension_semantics=("parallel", "arbitrary"), vmem_limit_bytes=VMEM_LIMIT),
        name="moe",
    )(v, comb, wg, wu, wd, tri)


def _final_body(h_ref, y_ref, g_ref, o_ref):
    h = h_ref[...] + y_ref[...].astype(F32)
    o_ref[...] = h * lax.rsqrt(jnp.mean(h * h, axis=-1, keepdims=True) + EPS) * g_ref[...]


def _final_norm(h, y, g_final):
    T, D = h.shape
    tm = min(FINAL_TM, T)
    tok = pl.BlockSpec((tm, D), lambda i: (i, 0))
    return pl.pallas_call(
        _final_body,
        grid=(T // tm,),
        in_specs=[tok, tok, pl.BlockSpec((1, D), lambda i: (0, 0))],
        out_specs=tok,
        out_shape=jax.ShapeDtypeStruct((T, D), F32),
        compiler_params=pltpu.CompilerParams(
            dimension_semantics=("parallel",), vmem_limit_bytes=VMEM_LIMIT),
        name="final_norm",
    )(h, y, g_final.reshape(1, D))


def _rope_tables(positions):
    half = NSA_DH // 2
    inv = 1.0 / (ROPE_THETA ** (jnp.arange(half, dtype=F32) / half))
    ang = positions.astype(F32)[..., None] * inv
    cos, sin = jnp.cos(ang), jnp.sin(ang)
    cs = jnp.concatenate([cos, cos, cos, cos], axis=-1)
    sn = jnp.concatenate([-sin, sin, -sin, sin], axis=-1)
    return cs, sn


def _layer(h, positions, g_mix, w_in, cmp_pos_k, cmp_w1_k, cmp_b1_k, cmp_w2_k, cmp_b2_k,
           cmp_pos_v, cmp_w1_v, cmp_b1_v, cmp_w2_v, cmp_b2_v, gla_w_a2, gla_b_a, gla_norm_g,
           w_proj_nsa, w_proj_gla, w_out, g_ffn, w_grp, b_grp, w_exp, b_exp, w_gate, w_up, w_down, g_out):
    B, S, D = h.shape
    cs, sn = _rope_tables(positions)
    n_chunks = S // CMP_STRIDE
    cmp_end = jnp.minimum(CMP_STRIDE * jnp.arange(n_chunks) + CMP_BLOCK - 1, S - 1)
    cs_c, sn_c = _rope_tables(jnp.take(positions, cmp_end, axis=1))
    (q, kvc, ks, vs, kw, vw, gq, gk, gv, gvt, gr, ma, mb, ng, ga) = _in_proj(h, g_mix, w_in, cs, sn)
    kc, vc = _compress(kvc, cs_c, sn_c, cmp_pos_k, cmp_w1_k, cmp_b1_k, cmp_w2_k, cmp_b2_k,
                       cmp_pos_v, cmp_w1_v, cmp_b1_v, cmp_w2_v, cmp_b2_v)
    ya = _nsa_attention(q, kc, vc, ks, vs, kw, vw, ng)
    yb = _gla_pairs(gq, gk, gv, gvt, ga, gr, gla_w_a2, gla_b_a, gla_norm_g)
    T = B * S
    h1, v, comb = _merge(h.reshape(T, D), ya.reshape(T, -1), yb.reshape(T, -1), ma.reshape(T, D), mb.reshape(T, D),
                         w_proj_nsa, w_proj_gla, w_out, g_ffn, w_grp, b_grp, w_exp, b_exp)
    y = _moe(v, comb, w_gate, w_up, w_down)
    return _final_norm(h1, y, g_out).reshape(B, S, D)


def kernel(x, positions, g_mix, w_in, cmp_pos_k, cmp_w1_k, cmp_b1_k, cmp_w2_k, cmp_b2_k, cmp_pos_v, cmp_w1_v,
           cmp_b1_v, cmp_w2_v, cmp_b2_v, gla_w_a2, gla_b_a, gla_norm_g, w_proj_nsa, w_proj_gla, w_out, g_ffn,
           w_grp, b_grp, w_exp, b_exp, w_gate, w_up, w_down, g_final):
    depth = g_mix.shape[0]
    assert depth == 1, "the final norm closes the single layer"
    return _layer(x, positions, g_mix[0], w_in[0], cmp_pos_k[0], cmp_w1_k[0], cmp_b1_k[0], cmp_w2_k[0],
                  cmp_b2_k[0], cmp_pos_v[0], cmp_w1_v[0], cmp_b1_v[0], cmp_w2_v[0], cmp_b2_v[0],
                  gla_w_a2[0], gla_b_a[0], gla_norm_g[0], w_proj_nsa[0], w_proj_gla[0], w_out[0],
                  g_ffn[0], w_grp[0], b_grp[0], w_exp[0], b_exp[0], w_gate[0], w_up[0], w_down[0], g_final)
```

```python
import functools

import numpy as np
import jax
import jax.numpy as jnp
from jax import lax
from jax.experimental import pallas as pl
from jax.experimental.pallas import tpu as pltpu

F32 = jnp.float32
BF16 = jnp.bfloat16

NSA_HEADS = 8
NSA_KV_GROUPS = 2
NSA_HPG = NSA_HEADS // NSA_KV_GROUPS
NSA_DH = 64
CMP_BLOCK = 32
CMP_STRIDE = 16
CMP_HIDDEN = 256
SLC_BLOCK = 64
SLC_TOPK = 16
WINDOW = 512
GLA_HEADS = 4
GLA_DK = 64
GLA_DV = 128
GLA_RANK = 16
GLA_TAU = 16.0
MOE_GROUPS = 4
MOE_EPG = 8
MOE_EXPERTS = MOE_GROUPS * MOE_EPG
MOE_DFF = 512
ROPE_THETA = 10000.0
EPS = 1e-6
NEG_INF = -1e30
FORCED_SCORE = 1e4
LOG2E = 1.4426950408889634

LANES = 128
VMEM_LIMIT = 56 * 1024 * 1024

IN_TM = 512
ATT_TQ = 128
ATT_TK = 512
GLA_C = 128
GLA_SUB = 16
GLA_HP = 2
GLA_UNROLL_DECAY = 4
GLA_UNROLL_PLAIN = 4
GLA_PLAIN_DECAY = 60.0
MERGE_TM = 512
MOE_TW = 1024
MOE_WINDOWS = 2
FINAL_TM = 1024
MOE_RT = 256
MOE_ET = 128
MOE_ALIGN = 16


def _dot(a, b):
    return jnp.dot(a, b, preferred_element_type=F32)


def _dot_nt(a, b):
    return lax.dot_general(a, b, (((1,), (1,)), ((), ())), preferred_element_type=F32)


def _split3(x):
    x1 = x.astype(BF16)
    r1 = x - x1.astype(F32)
    x2 = r1.astype(BF16)
    r2 = r1 - x2.astype(F32)
    x3 = r2.astype(BF16)
    return x1, x2, x3


def _dot_exact_lhs(a_bf16, x):
    x1, x2, x3 = _split3(x)
    return _dot(a_bf16, x1) + _dot(a_bf16, x2) + _dot(a_bf16, x3)


def _dot_f32(a, b):
    a1, a2, a3 = _split3(a)
    b1, b2, b3 = _split3(b)
    return (_dot(a1, b1) + (_dot(a1, b2) + _dot(a2, b1))
            + (_dot(a1, b3) + _dot(a2, b2) + _dot(a3, b1)))


def _rope_lanes(z, cs, sn):
    w = z.shape[-1]
    lane = lax.broadcasted_iota(jnp.int32, z.shape, 1)
    first_half = (lane % NSA_DH) < (NSA_DH // 2)
    rot = jnp.where(first_half, pltpu.roll(z, w - NSA_DH // 2, 1), pltpu.roll(z, NSA_DH // 2, 1))
    reps = w // LANES
    if reps > 1:
        cs = jnp.concatenate([cs] * reps, axis=1)
        sn = jnp.concatenate([sn] * reps, axis=1)
    return z * cs + rot * sn


_SEC = {}
_off = 0
for _name, _w in (("q", 512), ("kvc", 256), ("ks", 128), ("vs", 128), ("kw", 128), ("vw", 128),
                  ("gq", 256), ("gk", 256), ("gv", 512), ("gr", 512), ("ma", 1024), ("mb", 1024),
                  ("ng", 128), ("ga", 128)):
    _SEC[_name] = (_off, _off + _w)
    _off += _w
IN_NW = _off


def _in_proj_body(x_ref, g_ref, w_ref, pos_ref, inv_ref,
                  q_ref, kvc_ref, ks_ref, vs_ref, kw_ref, vw_ref,
                  gq_ref, gk_ref, gv_ref, gvt_ref, gr_ref, ma_ref, mb_ref, ng_ref, ga_ref):
    x = x_ref[0]
    var = jnp.mean(x * x, axis=-1, keepdims=True)
    u = (x * lax.rsqrt(var + EPS) * g_ref[...]).astype(BF16)
    ang = pos_ref[0].astype(F32) * inv_ref[...]
    lane_r = lax.broadcasted_iota(jnp.int32, ang.shape, 1)
    cs = jnp.cos(ang)
    sn = jnp.where((lane_r % NSA_DH) < (NSA_DH // 2), -1.0, 1.0) * jnp.sin(ang)

    def proj(name):
        a, b = _SEC[name]
        return _dot(u, w_ref[:, a:b])

    zq_t = (_rope_lanes(proj("q"), cs, sn) * (NSA_DH ** -0.5 * LOG2E)).T
    for h in range(NSA_HEADS):
        q_ref[0, h] = zq_t[h * NSA_DH:(h + 1) * NSA_DH].astype(BF16)
    kvc_ref[0] = proj("kvc")
    zks = _rope_lanes(proj("ks"), cs, sn)
    zkw = _rope_lanes(proj("kw"), cs, sn)
    zvs_t = proj("vs").T
    zvw_t = proj("vw").T
    tm = zks.shape[0]
    lane = lax.broadcasted_iota(jnp.int32, (tm, LANES), 1)
    blk = (pl.program_id(1) * tm + lax.broadcasted_iota(jnp.int32, (tm, LANES), 0)) // SLC_BLOCK
    onehot = jnp.where(lane - NSA_DH == blk, 1.0, 0.0)
    for g in range(NSA_KV_GROUPS):
        sl = slice(g * NSA_DH, (g + 1) * NSA_DH)
        k_front = zks if g == 0 else pltpu.roll(zks, LANES - g * NSA_DH, 1)
        ks_ref[0, g] = jnp.where(lane < NSA_DH, k_front, onehot).astype(BF16)
        kw_ref[0, g] = zkw[:, sl].astype(BF16)
        vs_ref[0, g] = zvs_t[sl].astype(BF16)
        vw_ref[0, g] = zvw_t[sl].astype(BF16)
    gq_ref[0] = proj("gq") * (GLA_DK ** -0.5)
    gk_ref[0] = proj("gk")
    zgv = proj("gv")
    gv_ref[0] = zgv.astype(BF16)
    gvt_ref[0] = zgv.T.astype(BF16)
    gr_ref[0] = proj("gr")
    ma_ref[0] = jax.nn.sigmoid(proj("ma")).astype(ma_ref.dtype)
    mb_ref[0] = jax.nn.sigmoid(proj("mb")).astype(mb_ref.dtype)
    ng_ref[0] = jax.nn.sigmoid(proj("ng")).T
    ga_ref[0] = proj("ga")


def _in_proj(x, g_mix, w_in, positions):
    B, S, D = x.shape
    inv = jnp.tile(_rope_inv_freq(), LANES // (NSA_DH // 2)).reshape(1, LANES)
    tm = min(IN_TM, S)
    splits = np.cumsum((512,) + (128,) * 6 + (24, 256, 256, 512, 16, 512, 1024, 1024))
    (wq, wkc, wvc, wks, wvs, wkw, wvw, wng, wgq, wgk, wgv, wga, wgr, wma, wmb) = jnp.split(
        w_in, splits[:-1].tolist(), axis=1)
    pad = lambda w: jnp.pad(w, ((0, 0), (0, LANES - w.shape[1])))
    w_all = jnp.concatenate([wq, wkc, wvc, wks, wvs, wkw, wvw, wgq, wgk, wgv, wgr, wma, wmb,
                             pad(wng), pad(wga)], axis=1).astype(BF16)
    assert w_all.shape[1] == IN_NW
    grid = (B, S // tm)
    tok = lambda w: pl.BlockSpec((1, tm, w), lambda b, i: (b, i, 0))
    head = lambda n, w: pl.BlockSpec((1, n, tm, w), lambda b, i: (b, 0, i, 0))
    head_t = lambda n, w: pl.BlockSpec((1, n, w, tm), lambda b, i: (b, 0, 0, i))
    sds = jax.ShapeDtypeStruct
    out_shape = (
        sds((B, NSA_HEADS, NSA_DH, S), BF16),
        sds((B, S, 256), F32),
        sds((B, NSA_KV_GROUPS, S, LANES), BF16),
        sds((B, NSA_KV_GROUPS, NSA_DH, S), BF16),
        sds((B, NSA_KV_GROUPS, S, NSA_DH), BF16),
        sds((B, NSA_KV_GROUPS, NSA_DH, S), BF16),
        sds((B, S, GLA_HEADS * GLA_DK), F32),
        sds((B, S, GLA_HEADS * GLA_DK), F32),
        sds((B, S, GLA_HEADS * GLA_DV), BF16),
        sds((B, GLA_HEADS * GLA_DV, S), BF16),
        sds((B, S, GLA_HEADS * GLA_DV), F32),
        sds((B, S, D), BF16),
        sds((B, S, D), BF16),
        sds((B, LANES, S), F32),
        sds((B, S, LANES), F32),
    )
    out_specs = (
        head_t(NSA_HEADS, NSA_DH), tok(256),
        head(NSA_KV_GROUPS, LANES), head_t(NSA_KV_GROUPS, NSA_DH),
        head(NSA_KV_GROUPS, NSA_DH), head_t(NSA_KV_GROUPS, NSA_DH),
        tok(GLA_HEADS * GLA_DK), tok(GLA_HEADS * GLA_DK),
        tok(GLA_HEADS * GLA_DV),
        pl.BlockSpec((1, GLA_HEADS * GLA_DV, tm), lambda b, i: (b, 0, i)),
        tok(GLA_HEADS * GLA_DV), tok(D), tok(D),
        pl.BlockSpec((1, LANES, tm), lambda b, i: (b, 0, i)), tok(LANES),
    )
    return pl.pallas_call(
        _in_proj_body,
        grid=grid,
        in_specs=[
            tok(D),
            pl.BlockSpec((1, D), lambda b, i: (0, 0)),
            pl.BlockSpec((D, IN_NW), lambda b, i: (0, 0), pipeline_mode=pl.Buffered(1)),
            tok(1),
            pl.BlockSpec((1, LANES), lambda b, i: (0, 0)),
        ],
        out_specs=out_specs,
        out_shape=out_shape,
        compiler_params=pltpu.CompilerParams(
            dimension_semantics=("parallel", "parallel"), vmem_limit_bytes=VMEM_LIMIT),
        name="in_proj",
    )(x, g_mix.reshape(1, D), w_all, positions.reshape(B, S, 1), inv)


def _compress_body(x_ref, p0_ref, p1_ref, w0_ref, w1_ref, b1_ref, w2_ref, b2_ref, cs_ref, sn_ref,
                   kc_ref, vc_ref, xk_ref, xv_ref):
    n = x_ref.shape[1] // CMP_STRIDE
    half_refs = (xk_ref, xv_ref)
    for s, half in enumerate(half_refs):
        half[...] = x_ref[0, :, s * LANES:(s + 1) * LANES]
    hid = w0_ref.shape[1] // len(half_refs)
    y0 = [None, None]
    y1 = [None, None]
    for t in range(CMP_STRIDE):
        for s, half in enumerate(half_refs):
            xt = half[pl.ds(t, n, stride=CMP_STRIDE), :]
            rows = slice((2 * t + s) * LANES, (2 * t + s + 1) * LANES)
            cols = slice(s * hid, (s + 1) * hid)
            lanes = slice(s * LANES, (s + 1) * LANES)
            d0 = _dot((xt + p0_ref[t:t + 1, lanes]).astype(BF16), w0_ref[rows, cols])
            d1 = _dot((xt + p1_ref[t:t + 1, lanes]).astype(BF16), w1_ref[rows, cols])
            y0[s] = d0 if y0[s] is None else y0[s] + d0
            y1[s] = d1 if y1[s] is None else y1[s] + d1
    y0 = jnp.concatenate(y0, axis=1)
    y1 = jnp.concatenate(y1, axis=1)
    h = jax.nn.gelu(y0 + pltpu.roll(y1, n - 1, 0) + b1_ref[...])
    o = _dot(h.astype(BF16), w2_ref[...]) + b2_ref[...]
    k = _rope_lanes(o[:, :LANES], cs_ref[0], sn_ref[0])
    v_t = o[:, LANES:].T
    for g in range(NSA_KV_GROUPS):
        sl = slice(g * NSA_DH, (g + 1) * NSA_DH)
        kc_ref[0, g] = k[:, sl].astype(BF16)
        vc_ref[0, g] = v_t[sl].astype(BF16)


def _compress(kvc, cs_c, sn_c, pos_k, w1_k, b1_k, w2_k, b2_k, pos_v, w1_v, b1_v, w2_v, b2_v):
    B, S, _ = kvc.shape
    n = S // CMP_STRIDE
    ns = 2 * NSA_KV_GROUPS
    eye = jnp.eye(ns, dtype=F32)
    w1s = jnp.stack([w1_k, w1_k, w1_v, w1_v])
    poss = jnp.stack([pos_k, pos_k, pos_v, pos_v])
    wbig, pbig = [], []
    for m in range(CMP_BLOCK // CMP_STRIDE):
        wm = w1s.reshape(ns, CMP_BLOCK, NSA_DH, CMP_HIDDEN)[:, m * CMP_STRIDE:(m + 1) * CMP_STRIDE]
        wb = jnp.einsum('ctdj,ce->tcdej', wm, eye).reshape(CMP_STRIDE * ns * NSA_DH, ns * CMP_HIDDEN)
        wbig.append(wb.astype(BF16))
        pm = poss[:, m * CMP_STRIDE:(m + 1) * CMP_STRIDE]
        pbig.append(pm.transpose(1, 0, 2).reshape(CMP_STRIDE, ns * NSA_DH))
    b1 = jnp.concatenate([b1_k, b1_k, b1_v, b1_v]).reshape(1, ns * CMP_HIDDEN)
    w2s = jnp.stack([w2_k, w2_k, w2_v, w2_v])
    w2big = jnp.einsum('cjd,ce->cjed', w2s, eye).reshape(ns * CMP_HIDDEN, ns * NSA_DH).astype(BF16)
    b2 = jnp.concatenate([b2_k, b2_k, b2_v, b2_v]).reshape(1, ns * NSA_DH)
    full = lambda a: pl.BlockSpec(a.shape, lambda b: (0,) * a.ndim)
    sds = jax.ShapeDtypeStruct
    return pl.pallas_call(
        _compress_body,
        grid=(B,),
        in_specs=[pl.BlockSpec((1, S, ns * NSA_DH), lambda b: (b, 0, 0)),
                  full(pbig[0]), full(pbig[1]), full(wbig[0]), full(wbig[1]), full(b1), full(w2big), full(b2),
                  pl.BlockSpec((1, n, LANES), lambda b: (b, 0, 0)),
                  pl.BlockSpec((1, n, LANES), lambda b: (b, 0, 0))],
        out_specs=(pl.BlockSpec((1, NSA_KV_GROUPS, n, NSA_DH), lambda b: (b, 0, 0, 0)),
                   pl.BlockSpec((1, NSA_KV_GROUPS, NSA_DH, n), lambda b: (b, 0, 0, 0))),
        out_shape=(sds((B, NSA_KV_GROUPS, n, NSA_DH), BF16), sds((B, NSA_KV_GROUPS, NSA_DH, n), BF16)),
        scratch_shapes=[pltpu.VMEM((S, LANES), F32), pltpu.VMEM((S, LANES), F32)],
        compiler_params=pltpu.CompilerParams(
            dimension_semantics=("parallel",), vmem_limit_bytes=VMEM_LIMIT),
        name="compress",
    )(kvc, pbig[0], pbig[1], wbig[0], wbig[1], b1, w2big, b2, cs_c, sn_c)


NSA_ONES = 16


def _with_ones(v_t):
    return jnp.concatenate([v_t, jnp.ones((NSA_ONES, v_t.shape[1]), v_t.dtype)], axis=0)


def _softmax_step(s, m, acc, v_t):
    m_new = jnp.maximum(m, jnp.max(s, axis=0, keepdims=True))
    p = jnp.exp2(s - m_new).astype(BF16)
    acc = jnp.exp2(m - m_new) * acc + _dot(_with_ones(v_t), p)
    return m_new, acc


def _nsa_body(q_ref, kc_ref, vc_ref, ks_ref, vs_ref, kw_ref, vw_ref, ng_ref, ovt_ref,
              o_ref, sa_ref, sb_ref, sw_ref, *, tq, tk, n_slc):
    gi = pl.program_id(1)
    qi = pl.program_id(2)
    q0 = qi * tq
    n_cmp = kc_ref.shape[2]
    W = NSA_HPG * tq
    per_head = lambda x: jnp.concatenate([x] * NSA_HPG, axis=1)
    q_all = jnp.concatenate([q_ref[0, h] for h in range(NSA_HPG)], axis=1)
    t_lane = q0 + lax.broadcasted_iota(jnp.int32, (1, tq), 1)

    cmp_end = CMP_STRIDE * lax.broadcasted_iota(jnp.int32, (n_cmp, 1), 0) + (CMP_BLOCK - 1)
    ok = cmp_end <= t_lane
    sm = _dot(kc_ref[0, 0], q_all) + per_head(jnp.where(ok, 0.0, NEG_INF))
    wk = WINDOW + tq
    w0 = pl.multiple_of(jnp.maximum(q0 - WINDOW, 0), tq)
    kpos = w0 + lax.broadcasted_iota(jnp.int32, (wk, 1), 0)
    okw = jnp.logical_and(kpos <= t_lane, kpos > t_lane - WINDOW)
    sw_ref[...] = _dot(kw_ref[0, 0, pl.ds(w0, wk), :], q_all) + per_head(jnp.where(okw, 0.0, NEG_INF))
    e = jnp.exp2(sm - jnp.max(sm, axis=0, keepdims=True))
    p = e * (1.0 / jnp.sum(e, axis=0, keepdims=True)) * per_head(jnp.where(ok, 1.0, 0.0))
    o_cmp = _dot(vc_ref[0, 0], p.astype(BF16))
    p_sum = p[:, 0:tq]
    for h in range(1, NSA_HPG):
        p_sum = p_sum + p[:, h * tq:(h + 1) * tq]

    p_hi = p_sum.astype(BF16)
    p_lo = (p_sum - p_hi.astype(F32)).astype(BF16)
    imp = _dot(ovt_ref[...], p_hi) + _dot(ovt_ref[...], p_lo)
    blk = lax.broadcasted_iota(jnp.int32, (n_slc, tq), 0)
    cur = (q0 + lax.broadcasted_iota(jnp.int32, (n_slc, tq), 1)) // SLC_BLOCK
    forced = (blk == 0) | (blk == cur) | (blk == cur - 1)
    imp = jnp.where(forced, FORCED_SCORE, imp)
    imp = jnp.where(blk <= cur, imp, NEG_INF)
    SUBL = 8

    def block_bias(n):
        groups = [imp[r:r + SUBL] for r in range(0, n, SUBL)]
        ranks = [jnp.zeros((SUBL, tq), F32) for _ in groups]
        for i in range(n):
            row = imp[i:i + 1, :]
            for gidx, grp in enumerate(groups):
                r = gidx * SUBL
                ge = jnp.where(row >= grp, 1.0, 0.0)
                gt = jnp.where(row > grp, 1.0, 0.0)
                if r > i:
                    inc = ge
                elif r + SUBL - 1 <= i:
                    inc = gt
                else:
                    inc = jnp.where(blk[r:r + SUBL] > i, ge, gt)
                ranks[gidx] = ranks[gidx] + inc
        bias = jnp.where(jnp.concatenate(ranks, axis=0) < float(min(SLC_TOPK, n_slc)), 0.0, NEG_INF)
        return jnp.concatenate([bias, jnp.zeros((NSA_DH - n, tq), F32)], axis=0) if n < NSA_DH else bias

    quarter = n_slc // 4
    visible = (q0 + tq - 1) // SLC_BLOCK + 1
    bias = lax.cond(
        visible <= 2 * quarter,
        lambda: lax.cond(visible <= quarter, lambda: block_bias(quarter), lambda: block_bias(2 * quarter)),
        lambda: lax.cond(visible <= 3 * quarter, lambda: block_bias(3 * quarter), lambda: block_bias(n_slc)))
    q_aug = jnp.concatenate([q_all, per_head(bias).astype(BF16)], axis=0)

    kpos_l = lax.broadcasted_iota(jnp.int32, (tk, 1), 0)

    n_tiles = q0 // tk + 1
    last_tile = ks_ref.shape[2] // tk - 1

    def tile_start(jt):
        return pl.multiple_of(jnp.minimum(jt, last_tile) * tk, tk)

    def scores_into(s_ref, jt):
        causal = per_head(jnp.where(jt * tk + kpos_l <= t_lane, 0.0, NEG_INF))
        s_ref[...] = _dot(ks_ref[0, 0, pl.ds(tile_start(jt), tk), :], q_aug) + causal

    def consume(s_ref, jt, m, acc):
        return _softmax_step(s_ref[...], m, acc, vs_ref[0, 0, :, pl.ds(tile_start(jt), tk)])

    def slc_pair(i, carry):
        m, acc = carry
        scores_into(sb_ref, 2 * i + 1)
        m, acc = consume(sa_ref, 2 * i, m, acc)
        scores_into(sa_ref, 2 * i + 2)
        return consume(sb_ref, 2 * i + 1, m, acc)

    init = (jnp.full((1, W), NEG_INF, F32), jnp.zeros((NSA_DH + NSA_ONES, W), F32))
    scores_into(sa_ref, 0)

    sw = sw_ref[...]
    ew = jnp.exp2(sw - jnp.max(sw, axis=0, keepdims=True)).astype(BF16)
    acc_w = _dot(_with_ones(vw_ref[0, 0, :, pl.ds(w0, wk)]), ew)
    l_w = acc_w[NSA_DH:NSA_DH + 1]
    acc_w = acc_w[:NSA_DH]

    _, acc_s = lax.fori_loop(0, (n_tiles + 1) // 2, slc_pair, init)
    l_s = acc_s[NSA_DH:NSA_DH + 1]
    acc_s = acc_s[:NSA_DH]

    def gate(j):
        return jnp.concatenate([ng_ref[0, pl.ds((gi * NSA_HPG + h) * 3 + j, 1), :]
                                for h in range(NSA_HPG)], axis=1)

    out_t = gate(0) * o_cmp + (gate(1) * (1.0 / l_s)) * acc_s + (gate(2) * (1.0 / l_w)) * acc_w
    out_t = jnp.concatenate([out_t[:, h * tq:(h + 1) * tq] for h in range(NSA_HPG)], axis=0)
    o_ref[0] = out_t.T.astype(o_ref.dtype)


def _nsa_attention(q, kc, vc, ks, vs, kw, vw, ng):
    B, H, dh, S = q.shape
    tq = min(ATT_TQ, S)
    tk = min(ATT_TK, S)
    assert S % tk == 0 and tk % tq == 0 and S >= WINDOW + tq and S // SLC_BLOCK <= dh
    n_cmp = kc.shape[2]
    n_slc = S // SLC_BLOCK
    c0 = CMP_STRIDE * np.arange(n_cmp)[None, :]
    s0 = SLC_BLOCK * np.arange(n_slc)[:, None]
    ov = np.clip(np.minimum(c0 + CMP_BLOCK, s0 + SLC_BLOCK) - np.maximum(c0, s0), 0, None) / CMP_BLOCK
    ovt = jnp.asarray(ov, BF16)
    grid = (B, NSA_KV_GROUPS, S // tq)
    k_spec = lambda n: pl.BlockSpec((1, 1, n, dh), lambda b, g, i: (b, g, 0, 0))
    vt_spec = lambda n: pl.BlockSpec((1, 1, dh, n), lambda b, g, i: (b, g, 0, 0))
    return pl.pallas_call(
        functools.partial(_nsa_body, tq=tq, tk=tk, n_slc=n_slc),
        grid=grid,
        in_specs=[
            pl.BlockSpec((1, NSA_HPG, dh, tq), lambda b, g, i: (b, g, 0, i)),
            k_spec(n_cmp), vt_spec(n_cmp),
            pl.BlockSpec((1, 1, S, 2 * dh), lambda b, g, i: (b, g, 0, 0)), vt_spec(S),
            k_spec(S), vt_spec(S),
            pl.BlockSpec((1, LANES, tq), lambda b, g, i: (b, 0, i)),
            pl.BlockSpec(ovt.shape, lambda b, g, i: (0, 0)),
        ],
        out_specs=pl.BlockSpec((1, tq, NSA_HPG * dh), lambda b, g, i: (b, i, g)),
        out_shape=jax.ShapeDtypeStruct((B, S, H * dh), BF16),
        scratch_shapes=[pltpu.VMEM((tk, NSA_HPG * tq), F32),
                        pltpu.VMEM((tk, NSA_HPG * tq), F32),
                        pltpu.VMEM((WINDOW + tq, NSA_HPG * tq), F32)],
        compiler_params=pltpu.CompilerParams(
            dimension_semantics=("parallel", "parallel", "arbitrary"), vmem_limit_bytes=VMEM_LIMIT),
        name="nsa_attn",
    )(q, kc, vc, ks, vs, kw, vw, ng, ovt)


def _dot_3pass(a, b):
    a1 = a.astype(BF16)
    a2 = (a - a1.astype(F32)).astype(BF16)
    b1 = b.astype(BF16)
    b2 = (b - b1.astype(F32)).astype(BF16)
    return _dot(a1, b1) + (_dot(a1, b2) + _dot(a2, b1))


def _gla_pair_body(q_ref, k_ref, v_ref, vt_ref, ga_ref, r_ref, wa_ref, ba_ref, ng_ref, tri_ref,
                   o_ref, state_ref, b_ref, kp_ref, bp_ref, vp_ref, plain_ref, *, n_chunks):
    C, SUB, dk, dv, hp = GLA_C, GLA_SUB, GLA_DK, GLA_DV, GLA_HP
    W = hp * dk
    lane = lax.broadcasted_iota(jnp.int32, (1, W), 1)
    of_head = [lane // dk == hh for hh in range(hp)]
    tri = tri_ref[...]
    t_loc = lax.broadcasted_iota(jnp.int32, (C, 1), 0)
    state_ref[...] = jnp.zeros_like(state_ref)
    kp_ref[0:SUB, :] = jnp.zeros((SUB, W), F32)
    bp_ref[0:SUB, :] = jnp.zeros((SUB, W), F32)
    vp_ref[:, 0:SUB, :] = jnp.zeros((hp, SUB, dv), F32)

    def decay_chunk(c, all_plain):
        c0 = pl.multiple_of(c * C, C)
        la = jax.nn.log_sigmoid(_dot_3pass(ga_ref[0, pl.ds(c0, C), :], wa_ref[0]) + ba_ref[0]) / GLA_TAU
        b = _dot_exact_lhs(tri, la)
        b_ref[pl.ds(c0, C), :] = b
        plain = (jnp.min(b[C - 1:C, :]) > -GLA_PLAIN_DECAY).astype(jnp.int32)
        plain_ref[c] = plain
        return jnp.minimum(all_plain, plain)

    all_plain = lax.fori_loop(0, n_chunks, decay_chunk, jnp.int32(1), unroll=GLA_UNROLL_DECAY)

    def chunk(c, check_decay):
        c0 = pl.multiple_of(c * C, C)
        b = b_ref[pl.ds(c0, C), :]
        q = q_ref[0, pl.ds(c0, C), :]
        k = k_ref[0, pl.ds(c0, C), :]
        v = [v_ref[0, pl.ds(c0, C), hh * dv:(hh + 1) * dv] for hh in range(hp)]
        vt = [vt_ref[0, hh * dv:(hh + 1) * dv, pl.ds(c0, C)] for hh in range(hp)]
        b_last = b[C - 1:C, :]
        st = state_ref[...]
        st_b = st.astype(BF16)
        qg = q * jnp.exp(b)
        qg_h = [jnp.where(of_head[hh], qg, 0.0).astype(BF16) for hh in range(hp)]
        o_inter = [_dot_nt(qg_h[hh], st_b) for hh in range(hp)]

        def intra_plain():
            ke = (k * jnp.exp(-b)).astype(BF16)
            row = lax.broadcasted_iota(jnp.int32, (C, C), 0)
            col = lax.broadcasted_iota(jnp.int32, (C, C), 1)
            return tuple(_dot(jnp.where(row >= col, _dot_nt(qg_h[hh], ke), 0.0).astype(BF16), v[hh])
                         for hh in range(hp))

        def intra_strong_decay():
            far = [[jnp.zeros((SUB, dv), F32)] for _ in range(hp)]
            for i in range(1, C // SUB):
                r0 = i * SUB
                b_first = b[r0:r0 + 1, :]
                qt = q[r0:r0 + SUB] * jnp.exp(b[r0:r0 + SUB] - b_first)
                kt = (k[:r0] * jnp.exp(b_first - b[:r0])).astype(BF16)
                lt = lax.broadcasted_iota(jnp.int32, (SUB, r0), 0)
                ls = lax.broadcasted_iota(jnp.int32, (SUB, r0), 1)
                for hh in range(hp):
                    a = _dot_nt(jnp.where(of_head[hh], qt, 0.0).astype(BF16), kt)
                    a = jnp.where(lt + (r0 - SUB) >= ls, a, 0.0)
                    far[hh].append(_dot(a.astype(BF16), v[hh][:r0]))
            acc = [jnp.concatenate(far[hh], axis=0) for hh in range(hp)]
            kp_ref[SUB:SUB + C, :] = k
            bp_ref[SUB:SUB + C, :] = b
            for hh in range(hp):
                vp_ref[hh, SUB:SUB + C, :] = v[hh].astype(F32)
            for d in range(SUB):
                kd = kp_ref[SUB - d:SUB - d + C, :]
                bd = bp_ref[SUB - d:SUB - d + C, :]
                valid = t_loc >= d
                w = jnp.exp(jnp.where(valid, b - bd, 0.0))
                x = jnp.where(valid, q * kd * w, 0.0)
                for hh in range(hp):
                    a_d = jnp.sum(jnp.where(of_head[hh], x, 0.0), axis=-1, keepdims=True)
                    acc[hh] = acc[hh] + a_d * vp_ref[hh, SUB - d:SUB - d + C, :]
            return tuple(acc)

        if check_decay:
            o_intra = lax.cond(plain_ref[c] > 0, intra_plain, intra_strong_decay)
        else:
            o_intra = intra_plain()
        k_dec = (k * jnp.exp(b_last - b)).astype(BF16)
        upd = _dot(vt[hp - 1], k_dec)
        for hh in range(hp - 2, -1, -1):
            upd = jnp.where(of_head[hh], _dot(vt[hh], k_dec), upd)
        state_ref[...] = st * jnp.exp(b_last) + upd
        for hh in range(hp):
            o = o_inter[hh] + o_intra[hh]
            o = o * lax.rsqrt(jnp.mean(o * o, axis=-1, keepdims=True) + EPS) * ng_ref[hh]
            gate = jax.nn.silu(r_ref[0, pl.ds(c0, C), hh * dv:(hh + 1) * dv])
            o_ref[0, pl.ds(c0, C), hh * dv:(hh + 1) * dv] = (o * gate).astype(o_ref.dtype)
        return 0

    @pl.when(all_plain > 0)
    def _():
        lax.fori_loop(0, n_chunks, lambda c, _: chunk(c, False), 0, unroll=GLA_UNROLL_PLAIN)

    @pl.when(all_plain <= 0)
    def _():
        lax.fori_loop(0, n_chunks, lambda c, _: chunk(c, True), 0)


def _gla_pairs(gq, gk, gv, gvt, ga, gr, w_a2, b_a, norm_g):
    B, S, _ = gq.shape
    H, dk, dv, hp = GLA_HEADS, GLA_DK, GLA_DV, GLA_HP
    W = hp * dk
    assert W == LANES and H % hp == 0
    wa = jnp.pad(w_a2, ((0, LANES - GLA_RANK), (0, 0))).reshape(LANES, H // hp, W).transpose(1, 0, 2)
    ba = b_a.reshape(H // hp, 1, W)
    ng = norm_g.reshape(H, 1, dv)
    tri = jnp.asarray(np.tril(np.ones((GLA_C, GLA_C))), BF16)
    n_chunks = S // GLA_C
    tok_spec = lambda w: pl.BlockSpec((1, S, hp * w), lambda b, h: (b, 0, h))
    return pl.pallas_call(
        functools.partial(_gla_pair_body, n_chunks=n_chunks),
        grid=(B, H // hp),
        in_specs=[tok_spec(dk), tok_spec(dk), tok_spec(dv),
                  pl.BlockSpec((1, hp * dv, S), lambda b, h: (b, h, 0)),
                  pl.BlockSpec((1, S, LANES), lambda b, h: (b, 0, 0)),
                  tok_spec(dv),
                  pl.BlockSpec((1, LANES, W), lambda b, h: (h, 0, 0)),
                  pl.BlockSpec((1, 1, W), lambda b, h: (h, 0, 0)),
                  pl.BlockSpec((hp, 1, dv), lambda b, h: (h, 0, 0)),
                  pl.BlockSpec(tri.shape, lambda b, h: (0, 0))],
        out_specs=tok_spec(dv),
        out_shape=jax.ShapeDtypeStruct((B, S, H * dv), BF16),
        scratch_shapes=[pltpu.VMEM((dv, W), F32),
                        pltpu.VMEM((S, W), F32),
                        pltpu.VMEM((GLA_SUB + GLA_C, W), F32),
                        pltpu.VMEM((GLA_SUB + GLA_C, W), F32),
                        pltpu.VMEM((hp, GLA_SUB + GLA_C, dv), F32),
                        pltpu.SMEM((n_chunks,), jnp.int32)],
        compiler_params=pltpu.CompilerParams(
            dimension_semantics=("parallel", "parallel"), vmem_limit_bytes=VMEM_LIMIT),
        name="gla",
    )(gq, gk, gv, gvt, ga, gr, wa, ba, ng, tri)


def _merge_body(x_ref, ya_ref, yb_ref, ma_ref, mb_ref, wpa_ref, wpb_ref, wo_ref, gf_ref, wr_ref, br_ref,
                h_ref, v_ref, comb_ref):
    y_a = _dot(ya_ref[...], wpa_ref[...])
    y_b = _dot(yb_ref[...], wpb_ref[...])
    mixed = ma_ref[...] * y_a + mb_ref[...] * y_b
    h = x_ref[...] + _dot(mixed.astype(BF16), wo_ref[...])
    h_ref[...] = h
    v = h * lax.rsqrt(jnp.mean(h * h, axis=-1, keepdims=True) + EPS) * gf_ref[...]
    v_ref[...] = v.astype(BF16)
    wr = wr_ref[...]
    w_hi = wr.astype(BF16)
    w_lo = (wr - w_hi.astype(F32)).astype(BF16)
    v_hi = v.astype(BF16)
    v_lo = (v - v_hi.astype(F32)).astype(BF16)
    both = _dot(v_hi, jnp.concatenate([w_hi, w_lo], axis=1))
    logits = (both[:, :LANES] + both[:, LANES:]) + _dot(v_lo, w_hi) + br_ref[...]
    lane = lax.broadcasted_iota(jnp.int32, logits.shape, 1)
    is_grp = jnp.logical_and(lane >= MOE_EXPERTS, lane < MOE_EXPERTS + MOE_GROUPS)
    lg = jnp.where(is_grp, logits, NEG_INF)
    eg = jnp.where(is_grp, jnp.exp(lg - jnp.max(lg, axis=-1, keepdims=True)), 0.0)
    pg = eg / jnp.sum(eg, axis=-1, keepdims=True)
    p_grp = jnp.max(pg, axis=-1, keepdims=True)
    g_sel = jnp.min(jnp.where(jnp.logical_and(is_grp, pg == p_grp), lane, 2 * LANES),
                    axis=-1, keepdims=True) - MOE_EXPERTS
    in_grp = jnp.logical_and(lane < MOE_EXPERTS, lane // MOE_EPG == g_sel)
    le = jnp.where(in_grp, logits, NEG_INF)
    ee = jnp.where(in_grp, jnp.exp(le - jnp.max(le, axis=-1, keepdims=True)), 0.0)
    pin = ee / jnp.sum(ee, axis=-1, keepdims=True)
    p1 = jnp.max(jnp.where(in_grp, pin, -1.0), axis=-1, keepdims=True)
    i1 = jnp.min(jnp.where(jnp.logical_and(in_grp, pin == p1), lane, 2 * LANES), axis=-1, keepdims=True)
    rest = jnp.logical_and(in_grp, lane != i1)
    p2 = jnp.max(jnp.where(rest, pin, -1.0), axis=-1, keepdims=True)
    i2 = jnp.min(jnp.where(jnp.logical_and(rest, pin == p2), lane, 2 * LANES), axis=-1, keepdims=True)
    tot = p1 + p2
    comb_ref[...] = (jnp.where(lane == i1, p_grp * p1 / tot, 0.0)
                     + jnp.where(lane == i2, p_grp * p2 / tot, 0.0))


def _merge(x2, ya, yb, ma, mb, w_proj_nsa, w_proj_gla, w_out, g_ffn, w_grp, b_grp, w_exp, b_exp):
    T, D = x2.shape
    tm = min(MERGE_TM, T)
    wr = jnp.pad(jnp.concatenate([w_exp, w_grp], axis=1), ((0, 0), (0, LANES - MOE_EXPERTS - MOE_GROUPS)))
    br = jnp.pad(jnp.concatenate([b_exp, b_grp]), (0, LANES - MOE_EXPERTS - MOE_GROUPS)).reshape(1, LANES)
    tok = lambda w: pl.BlockSpec((tm, w), lambda i: (i, 0))
    full = lambda a: pl.BlockSpec(a.shape, lambda i: (0, 0))
    wpa, wpb, wo = w_proj_nsa.astype(BF16), w_proj_gla.astype(BF16), w_out.astype(BF16)
    gf = g_ffn.reshape(1, D)
    sds = jax.ShapeDtypeStruct
    return pl.pallas_call(
        _merge_body,
        grid=(T // tm,),
        in_specs=[tok(D), tok(ya.shape[1]), tok(yb.shape[1]), tok(D), tok(D),
                  full(wpa), full(wpb), full(wo), full(gf), full(wr), full(br)],
        out_specs=(tok(D), tok(D), tok(LANES)),
        out_shape=(sds((T, D), F32), sds((T, D), BF16), sds((T, LANES), F32)),
        compiler_params=pltpu.CompilerParams(
            dimension_semantics=("parallel",), vmem_limit_bytes=VMEM_LIMIT),
        name="merge",
    )(x2, ya, yb, ma, mb, wpa, wpb, wo, gf, wr, br)


def _moe_rows(tw):
    rows = 2 * tw + MOE_EXPERTS * (MOE_ALIGN - 1) + MOE_RT
    return -(-rows // MOE_RT) * MOE_RT


def _moe_window(e, phase, v_ref, comb_ref, tri_ref, y_ref, xs_ref, z_ref, meta_ref, pos_ref, tw, rmax):
    no_row = -1.0

    @pl.when(e == 0 if phase == "route" else False)
    def _route():
        comb = comb_ref[...]
        assigned = comb > 0.0
        a = jnp.where(assigned, 1.0, 0.0)
        tri = tri_ref[...]
        run = jnp.zeros((1, LANES), F32)
        ranks = []
        for b in range(tw // MOE_RT):
            ab = a[b * MOE_RT:(b + 1) * MOE_RT]
            ranks.append(_dot(tri, ab.astype(BF16)) + run)
            run = run + jnp.sum(ab, axis=0, keepdims=True)
        rank = jnp.concatenate(ranks, axis=0)
        cnt_pad = jnp.floor((run + (MOE_ALIGN - 1)) * (1.0 / MOE_ALIGN)) * MOE_ALIGN
        incl = jnp.broadcast_to(cnt_pad, (8, LANES))
        lane8 = lax.broadcasted_iota(jnp.int32, (8, LANES), 1)
        shift = 1
        while shift < MOE_EXPERTS:
            incl = incl + jnp.where(lane8 >= shift, pltpu.roll(incl, shift, 1), 0.0)
            shift *= 2
        offs = incl[0:1] - cnt_pad
        meta_ref[0:1, :] = offs
        meta_ref[1:2, :] = run
        row_of = offs + rank
        pos_a = jnp.min(jnp.where(assigned, row_of, 1e9), axis=-1, keepdims=True)
        pos_b = jnp.max(jnp.where(assigned, row_of, no_row), axis=-1, keepdims=True)
        pos_a = jnp.where(pos_a > 1e8, no_row, pos_a)
        pos_b = jnp.where(pos_b == pos_a, no_row, pos_b)
        w_a = jnp.sum(jnp.where(jnp.logical_and(assigned, row_of == pos_a), comb, 0.0), axis=-1, keepdims=True)
        w_b = jnp.sum(jnp.where(jnp.logical_and(assigned, row_of == pos_b), comb, 0.0), axis=-1, keepdims=True)
        lane_t = lax.broadcasted_iota(jnp.int32, (tw, LANES), 1)
        pos_ref[...] = jnp.where(lane_t == 0, pos_a, jnp.where(lane_t == 1, pos_b, jnp.where(
            lane_t == 2, w_a, jnp.where(lane_t == 3, w_b, no_row))))
        pos_t = pos_ref[...].T
        pa, pb = pos_t[0:1], pos_t[1:2]
        v = v_ref[...]
        for rt in range(rmax // MOE_RT):
            r = (rt * MOE_RT + lax.broadcasted_iota(jnp.int32, (MOE_RT, 1), 0)).astype(F32)
            p = jnp.where(r == pa, 1.0, jnp.where(r == pb, 1.0, 0.0)).astype(BF16)
            xs_ref[rt * MOE_RT:(rt + 1) * MOE_RT, :] = _dot(p, v).astype(BF16)
        z_ref[...] = jnp.zeros_like(z_ref)

    @pl.when(e == pl.num_programs(1) - 1 if phase == "combine" else False)
    def _combine():
        r = lax.broadcasted_iota(jnp.int32, (1, rmax), 1).astype(F32)
        z = z_ref[...]
        for tt in range(tw // MOE_RT):
            rows = slice(tt * MOE_RT, (tt + 1) * MOE_RT)
            pa, pb = pos_ref[rows, 0:1], pos_ref[rows, 1:2]
            w_a, w_b = pos_ref[rows, 2:3], pos_ref[rows, 3:4]
            q = jnp.where(r == pa, w_a, jnp.where(r == pb, w_b, 0.0)).astype(BF16)
            y_ref[rows, :] = _dot(q, z).astype(y_ref.dtype)


def _moe_body(v_ref, comb_ref, wg_ref, wu_ref, wd_ref, tri_ref, y_ref,
              xs_ref, z_ref, meta_ref, pos_ref, *, tw, rmax, n_win):
    e = pl.program_id(1)
    lane = lax.broadcasted_iota(jnp.int32, (1, LANES), 1)

    def window_phase(phase):
        for w in range(n_win):
            rows = pl.ds(w * tw, tw)
            _moe_window(e, phase, v_ref.at[rows], comb_ref.at[rows], tri_ref, y_ref.at[rows],
                        xs_ref.at[w], z_ref.at[w], meta_ref.at[w], pos_ref.at[w], tw, rmax)

    window_phase("route")

    pick = lambda w, row: jnp.sum(jnp.where(lane == e, meta_ref[w, row:row + 1, :], 0.0)).astype(jnp.int32)
    offs = [pick(w, 0) for w in range(n_win)]
    tiles = [(pick(w, 1) + MOE_ET - 1) // MOE_ET for w in range(n_win)]

    def expert(x):
        hdn = jax.nn.silu(_dot(x, wg_ref[0])) * _dot(x, wu_ref[0])
        return _dot(hdn.astype(BF16), wd_ref[0]).astype(BF16)

    def joint_tile(i, _):
        r0 = [pl.multiple_of(offs[w] + i * MOE_ET, MOE_ALIGN) for w in range(n_win)]
        z = expert(jnp.concatenate([xs_ref[w, pl.ds(r0[w], MOE_ET), :] for w in range(n_win)], axis=0))
        for w in range(n_win):
            z_ref[w, pl.ds(r0[w], MOE_ET), :] = z[w * MOE_ET:(w + 1) * MOE_ET]
        return 0

    n_joint = functools.reduce(jnp.minimum, tiles)
    lax.fori_loop(0, n_joint, joint_tile, 0)
    for w in range(n_win):
        def own_tile(i, _, w=w):
            r0 = pl.multiple_of(offs[w] + i * MOE_ET, MOE_ALIGN)
            z_ref[w, pl.ds(r0, MOE_ET), :] = expert(xs_ref[w, pl.ds(r0, MOE_ET), :])
            return 0

        lax.fori_loop(n_joint, tiles[w], own_tile, 0)

    window_phase("combine")


def _moe(v, comb, w_gate, w_up, w_down):
    T, D = v.shape
    tw = min(MOE_TW, T)
    n_win = min(MOE_WINDOWS, T // tw)
    rmax = _moe_rows(tw)
    E, _, F = w_gate.shape
    wg, wu, wd = w_gate.astype(BF16), w_up.astype(BF16), w_down.astype(BF16)
    tri = jnp.asarray(np.tril(np.ones((MOE_RT, MOE_RT)), -1), BF16)
    tok = lambda w: pl.BlockSpec((n_win * tw, w), lambda i, e: (i, 0))
    tok_in = lambda w: pl.BlockSpec((n_win * tw, w), lambda i, e: (i, 0), pipeline_mode=pl.Buffered(1))
    return pl.pallas_call(
        functools.partial(_moe_body, tw=tw, rmax=rmax, n_win=n_win),
        grid=(T // (n_win * tw), E),
        in_specs=[tok_in(D), tok_in(LANES),
                  pl.BlockSpec((1, D, F), lambda i, e: (e, 0, 0)),
                  pl.BlockSpec((1, D, F), lambda i, e: (e, 0, 0)),
                  pl.BlockSpec((1, F, D), lambda i, e: (e, 0, 0)),
                  pl.BlockSpec(tri.shape, lambda i, e: (0, 0))],
        out_specs=tok(D),
        out_shape=jax.ShapeDtypeStruct((T, D), BF16),
        scratch_shapes=[pltpu.VMEM((n_win, rmax, D), BF16),
                        pltpu.VMEM((n_win, rmax, D), BF16),
                        pltpu.VMEM((n_win, 8, LANES), F32),
                        pltpu.VMEM((n_win, tw, LANES), F32)],
        compiler_params=pltpu.CompilerParams(
            dimension_semantics=("parallel", "arbitrary"), vmem_limit_bytes=VMEM_LIMIT),
        name="moe",
    )(v, comb, wg, wu, wd, tri)


def _final_body(h_ref, y_ref, g_ref, o_ref):
    h = h_ref[...] + y_ref[...].astype(F32)
    o_ref[...] = h * lax.rsqrt(jnp.mean(h * h, axis=-1, keepdims=True) + EPS) * g_ref[...]


def _final_norm(h, y, g_final):
    T, D = h.shape
    tm = min(FINAL_TM, T)
    tok = pl.BlockSpec((tm, D), lambda i: (i, 0))
    return pl.pallas_call(
        _final_body,
        grid=(T // tm,),
        in_specs=[tok, tok, pl.BlockSpec((1, D), lambda i: (0, 0))],
        out_specs=tok,
        out_shape=jax.ShapeDtypeStruct((T, D), F32),
        compiler_params=pltpu.CompilerParams(
            dimension_semantics=("parallel",), vmem_limit_bytes=VMEM_LIMIT),
        name="final_norm",
    )(h, y, g_final.reshape(1, D))


def _rope_inv_freq():
    half = NSA_DH // 2
    return 1.0 / (ROPE_THETA ** (jnp.arange(half, dtype=F32) / half))


def _rope_tables(positions):
    ang = positions.astype(F32)[..., None] * _rope_inv_freq()
    cos, sin = jnp.cos(ang), jnp.sin(ang)
    cs = jnp.concatenate([cos, cos, cos, cos], axis=-1)
    sn = jnp.concatenate([-sin, sin, -sin, sin], axis=-1)
    return cs, sn


def _layer(h, positions, g_mix, w_in, cmp_pos_k, cmp_w1_k, cmp_b1_k, cmp_w2_k, cmp_b2_k,
           cmp_pos_v, cmp_w1_v, cmp_b1_v, cmp_w2_v, cmp_b2_v, gla_w_a2, gla_b_a, gla_norm_g,
           w_proj_nsa, w_proj_gla, w_out, g_ffn, w_grp, b_grp, w_exp, b_exp, w_gate, w_up, w_down, g_out):
    B, S, D = h.shape
    n_chunks = S // CMP_STRIDE
    cmp_end = jnp.minimum(CMP_STRIDE * jnp.arange(n_chunks) + CMP_BLOCK - 1, S - 1)
    cs_c, sn_c = _rope_tables(jnp.take(positions, cmp_end, axis=1))
    (q, kvc, ks, vs, kw, vw, gq, gk, gv, gvt, gr, ma, mb, ng, ga) = _in_proj(h, g_mix, w_in, positions)
    kc, vc = _compress(kvc, cs_c, sn_c, cmp_pos_k, cmp_w1_k, cmp_b1_k, cmp_w2_k, cmp_b2_k,
                       cmp_pos_v, cmp_w1_v, cmp_b1_v, cmp_w2_v, cmp_b2_v)
    ya = _nsa_attention(q, kc, vc, ks, vs, kw, vw, ng)
    yb = _gla_pairs(gq, gk, gv, gvt, ga, gr, gla_w_a2, gla_b_a, gla_norm_g)
    T = B * S
    h1, v, comb = _merge(h.reshape(T, D), ya.reshape(T, -1), yb.reshape(T, -1), ma.reshape(T, D), mb.reshape(T, D),
                         w_proj_nsa, w_proj_gla, w_out, g_ffn, w_grp, b_grp, w_exp, b_exp)
    y = _moe(v, comb, w_gate, w_up, w_down)
    return _final_norm(h1, y, g_out).reshape(B, S, D)


def kernel(x, positions, g_mix, w_in, cmp_pos_k, cmp_w1_k, cmp_b1_k, cmp_w2_k, cmp_b2_k, cmp_pos_v, cmp_w1_v,
           cmp_b1_v, cmp_w2_v, cmp_b2_v, gla_w_a2, gla_b_a, gla_norm_g, w_proj_nsa, w_proj_gla, w_out, g_ffn,
           w_grp, b_grp, w_exp, b_exp, w_gate, w_up, w_down, g_final):
    depth = g_mix.shape[0]
    assert depth == 1, "the final norm closes the single layer"
    return _layer(x, positions, g_mix[0], w_in[0], cmp_pos_k[0], cmp_w1_k[0], cmp_b1_k[0], cmp_w2_k[0],
                  cmp_b2_k[0], cmp_pos_v[0], cmp_w1_v[0], cmp_b1_v[0], cmp_w2_v[0], cmp_b2_v[0],
                  gla_w_a2[0], gla_b_a[0], gla_norm_g[0], w_proj_nsa[0], w_proj_gla[0], w_out[0],
                  g_ffn[0], w_grp[0], b_grp[0], w_exp[0], b_exp[0], w_gate[0], w_up[0], w_down[0], g_final)
```

```python
import functools

import numpy as np
import jax
import jax.numpy as jnp
from jax import lax
from jax.experimental import pallas as pl
from jax.experimental.pallas import tpu as pltpu

F32 = jnp.float32
BF16 = jnp.bfloat16

NSA_HEADS = 8
NSA_KV_GROUPS = 2
NSA_HPG = NSA_HEADS // NSA_KV_GROUPS
NSA_DH = 64
CMP_BLOCK = 32
CMP_STRIDE = 16
CMP_HIDDEN = 256
SLC_BLOCK = 64
SLC_TOPK = 16
WINDOW = 512
GLA_HEADS = 4
GLA_DK = 64
GLA_DV = 128
GLA_RANK = 16
GLA_TAU = 16.0
MOE_GROUPS = 4
MOE_EPG = 8
MOE_EXPERTS = MOE_GROUPS * MOE_EPG
MOE_DFF = 512
ROPE_THETA = 10000.0
EPS = 1e-6
NEG_INF = -1e30
FORCED_SCORE = 1e4
LOG2E = 1.4426950408889634

LANES = 128
VMEM_LIMIT = 56 * 1024 * 1024

IN_TM = 512
ATT_TQ = 128
ATT_TK = 512
GLA_C = 128
GLA_SUB = 16
GLA_HP = 2
GLA_UNROLL_DECAY = 4
GLA_UNROLL_PLAIN = 4
GLA_PLAIN_DECAY = 60.0
MERGE_TM = 512
MOE_TW = 1024
MOE_WINDOWS = 2
FINAL_TM = 1024
MOE_RT = 256
MOE_ET = 128
MOE_ALIGN = 16


def _dot(a, b):
    return jnp.dot(a, b, preferred_element_type=F32)


def _dot_nt(a, b):
    return lax.dot_general(a, b, (((1,), (1,)), ((), ())), preferred_element_type=F32)


def _split3(x):
    x1 = x.astype(BF16)
    r1 = x - x1.astype(F32)
    x2 = r1.astype(BF16)
    r2 = r1 - x2.astype(F32)
    x3 = r2.astype(BF16)
    return x1, x2, x3


def _dot_exact_lhs(a_bf16, x):
    x1, x2, x3 = _split3(x)
    return _dot(a_bf16, x1) + _dot(a_bf16, x2) + _dot(a_bf16, x3)


def _dot_f32(a, b):
    a1, a2, a3 = _split3(a)
    b1, b2, b3 = _split3(b)
    return (_dot(a1, b1) + (_dot(a1, b2) + _dot(a2, b1))
            + (_dot(a1, b3) + _dot(a2, b2) + _dot(a3, b1)))


def _rope_lanes(z, cs, sn):
    w = z.shape[-1]
    lane = lax.broadcasted_iota(jnp.int32, z.shape, 1)
    first_half = (lane % NSA_DH) < (NSA_DH // 2)
    rot = jnp.where(first_half, pltpu.roll(z, w - NSA_DH // 2, 1), pltpu.roll(z, NSA_DH // 2, 1))
    reps = w // LANES
    if reps > 1:
        cs = jnp.concatenate([cs] * reps, axis=1)
        sn = jnp.concatenate([sn] * reps, axis=1)
    return z * cs + rot * sn


_SEC = {}
_off = 0
for _name, _w in (("q", 512), ("kvc", 256), ("ks", 128), ("vs", 128), ("kw", 128), ("vw", 128),
                  ("gq", 256), ("gk", 256), ("gv", 512), ("gr", 512), ("ma", 1024), ("mb", 1024),
                  ("ng", 128), ("ga", 128)):
    _SEC[_name] = (_off, _off + _w)
    _off += _w
IN_NW = _off


def _in_proj_body(x_ref, g_ref, w_ref, pos_ref, inv_ref,
                  q_ref, kvc_ref, ks_ref, vs_ref, kw_ref, vw_ref,
                  gq_ref, gk_ref, gv_ref, gvt_ref, gr_ref, ma_ref, mb_ref, ng_ref, ga_ref):
    x = x_ref[0]
    var = jnp.mean(x * x, axis=-1, keepdims=True)
    u = (x * lax.rsqrt(var + EPS) * g_ref[...]).astype(BF16)
    ang = pos_ref[0].astype(F32) * inv_ref[...]
    lane_r = lax.broadcasted_iota(jnp.int32, ang.shape, 1)
    cs = jnp.cos(ang)
    sn = jnp.where((lane_r % NSA_DH) < (NSA_DH // 2), -1.0, 1.0) * jnp.sin(ang)

    def proj(name):
        a, b = _SEC[name]
        return _dot(u, w_ref[:, a:b])

    zq_t = (_rope_lanes(proj("q"), cs, sn) * (NSA_DH ** -0.5 * LOG2E)).T
    for h in range(NSA_HEADS):
        q_ref[0, h] = zq_t[h * NSA_DH:(h + 1) * NSA_DH].astype(BF16)
    kvc_ref[0] = proj("kvc")
    zks = _rope_lanes(proj("ks"), cs, sn)
    zkw = _rope_lanes(proj("kw"), cs, sn)
    zvs_t = proj("vs").T
    zvw_t = proj("vw").T
    tm = zks.shape[0]
    lane = lax.broadcasted_iota(jnp.int32, (tm, LANES), 1)
    blk = (pl.program_id(1) * tm + lax.broadcasted_iota(jnp.int32, (tm, LANES), 0)) // SLC_BLOCK
    onehot = jnp.where(lane - NSA_DH == blk, 1.0, 0.0)
    for g in range(NSA_KV_GROUPS):
        sl = slice(g * NSA_DH, (g + 1) * NSA_DH)
        k_front = zks if g == 0 else pltpu.roll(zks, LANES - g * NSA_DH, 1)
        ks_ref[0, g] = jnp.where(lane < NSA_DH, k_front, onehot).astype(BF16)
        kw_ref[0, g] = zkw[:, sl].astype(BF16)
        vs_ref[0, g] = zvs_t[sl].astype(BF16)
        vw_ref[0, g] = zvw_t[sl].astype(BF16)
    gq_ref[0] = proj("gq") * (GLA_DK ** -0.5)
    gk_ref[0] = proj("gk")
    zgv = proj("gv")
    gv_ref[0] = zgv.astype(BF16)
    gvt_ref[0] = zgv.T.astype(BF16)
    gr_ref[0] = proj("gr")
    ma_ref[0] = jax.nn.sigmoid(proj("ma")).astype(ma_ref.dtype)
    mb_ref[0] = jax.nn.sigmoid(proj("mb")).astype(mb_ref.dtype)
    ng_ref[0] = jax.nn.sigmoid(proj("ng")).T
    ga_ref[0] = proj("ga")


def _in_proj(x, g_mix, w_in, positions):
    B, S, D = x.shape
    inv = jnp.tile(_rope_inv_freq(), LANES // (NSA_DH // 2)).reshape(1, LANES)
    tm = min(IN_TM, S)
    splits = np.cumsum((512,) + (128,) * 6 + (24, 256, 256, 512, 16, 512, 1024, 1024))
    (wq, wkc, wvc, wks, wvs, wkw, wvw, wng, wgq, wgk, wgv, wga, wgr, wma, wmb) = jnp.split(
        w_in, splits[:-1].tolist(), axis=1)
    pad = lambda w: jnp.pad(w, ((0, 0), (0, LANES - w.shape[1])))
    w_all = jnp.concatenate([wq, wkc, wvc, wks, wvs, wkw, wvw, wgq, wgk, wgv, wgr, wma, wmb,
                             pad(wng), pad(wga)], axis=1).astype(BF16)
    assert w_all.shape[1] == IN_NW
    grid = (B, S // tm)
    tok = lambda w: pl.BlockSpec((1, tm, w), lambda b, i: (b, i, 0))
    head = lambda n, w: pl.BlockSpec((1, n, tm, w), lambda b, i: (b, 0, i, 0))
    head_t = lambda n, w: pl.BlockSpec((1, n, w, tm), lambda b, i: (b, 0, 0, i))
    sds = jax.ShapeDtypeStruct
    out_shape = (
        sds((B, NSA_HEADS, NSA_DH, S), BF16),
        sds((B, S, 256), F32),
        sds((B, NSA_KV_GROUPS, S, LANES), BF16),
        sds((B, NSA_KV_GROUPS, NSA_DH, S), BF16),
        sds((B, NSA_KV_GROUPS, S, NSA_DH), BF16),
        sds((B, NSA_KV_GROUPS, NSA_DH, S), BF16),
        sds((B, S, GLA_HEADS * GLA_DK), F32),
        sds((B, S, GLA_HEADS * GLA_DK), F32),
        sds((B, S, GLA_HEADS * GLA_DV), BF16),
        sds((B, GLA_HEADS * GLA_DV, S), BF16),
        sds((B, S, GLA_HEADS * GLA_DV), F32),
        sds((B, S, D), BF16),
        sds((B, S, D), BF16),
        sds((B, LANES, S), F32),
        sds((B, S, LANES), F32),
    )
    out_specs = (
        head_t(NSA_HEADS, NSA_DH), tok(256),
        head(NSA_KV_GROUPS, LANES), head_t(NSA_KV_GROUPS, NSA_DH),
        head(NSA_KV_GROUPS, NSA_DH), head_t(NSA_KV_GROUPS, NSA_DH),
        tok(GLA_HEADS * GLA_DK), tok(GLA_HEADS * GLA_DK),
        tok(GLA_HEADS * GLA_DV),
        pl.BlockSpec((1, GLA_HEADS * GLA_DV, tm), lambda b, i: (b, 0, i)),
        tok(GLA_HEADS * GLA_DV), tok(D), tok(D),
        pl.BlockSpec((1, LANES, tm), lambda b, i: (b, 0, i)), tok(LANES),
    )
    return pl.pallas_call(
        _in_proj_body,
        grid=grid,
        in_specs=[
            tok(D),
            pl.BlockSpec((1, D), lambda b, i: (0, 0)),
            pl.BlockSpec((D, IN_NW), lambda b, i: (0, 0), pipeline_mode=pl.Buffered(1)),
            tok(1),
            pl.BlockSpec((1, LANES), lambda b, i: (0, 0)),
        ],
        out_specs=out_specs,
        out_shape=out_shape,
        compiler_params=pltpu.CompilerParams(
            dimension_semantics=("parallel", "parallel"), vmem_limit_bytes=VMEM_LIMIT),
        name="in_proj",
    )(x, g_mix.reshape(1, D), w_all, positions.reshape(B, S, 1), inv)


def _compress_body(x_ref, p0_ref, p1_ref, w0_ref, w1_ref, b1_ref, w2_ref, b2_ref, cs_ref, sn_ref,
                   kc_ref, vc_ref, xk_ref, xv_ref):
    n = x_ref.shape[1] // CMP_STRIDE
    half_refs = (xk_ref, xv_ref)
    for s, half in enumerate(half_refs):
        half[...] = x_ref[0, :, s * LANES:(s + 1) * LANES]
    hid = w0_ref.shape[1] // len(half_refs)
    y0 = [None, None]
    y1 = [None, None]
    for t in range(CMP_STRIDE):
        for s, half in enumerate(half_refs):
            xt = half[pl.ds(t, n, stride=CMP_STRIDE), :]
            rows = slice((2 * t + s) * LANES, (2 * t + s + 1) * LANES)
            cols = slice(s * hid, (s + 1) * hid)
            lanes = slice(s * LANES, (s + 1) * LANES)
            d0 = _dot((xt + p0_ref[t:t + 1, lanes]).astype(BF16), w0_ref[rows, cols])
            d1 = _dot((xt + p1_ref[t:t + 1, lanes]).astype(BF16), w1_ref[rows, cols])
            y0[s] = d0 if y0[s] is None else y0[s] + d0
            y1[s] = d1 if y1[s] is None else y1[s] + d1
    y0 = jnp.concatenate(y0, axis=1)
    y1 = jnp.concatenate(y1, axis=1)
    h = jax.nn.gelu(y0 + pltpu.roll(y1, n - 1, 0) + b1_ref[...])
    o = _dot(h.astype(BF16), w2_ref[...]) + b2_ref[...]
    k = _rope_lanes(o[:, :LANES], cs_ref[0], sn_ref[0])
    v_t = o[:, LANES:].T
    for g in range(NSA_KV_GROUPS):
        sl = slice(g * NSA_DH, (g + 1) * NSA_DH)
        kc_ref[0, g] = k[:, sl].astype(BF16)
        vc_ref[0, g] = v_t[sl].astype(BF16)


def _compress(kvc, cs_c, sn_c, pos_k, w1_k, b1_k, w2_k, b2_k, pos_v, w1_v, b1_v, w2_v, b2_v):
    B, S, _ = kvc.shape
    n = S // CMP_STRIDE
    ns = 2 * NSA_KV_GROUPS
    eye = jnp.eye(ns, dtype=F32)
    w1s = jnp.stack([w1_k, w1_k, w1_v, w1_v])
    poss = jnp.stack([pos_k, pos_k, pos_v, pos_v])
    wbig, pbig = [], []
    for m in range(CMP_BLOCK // CMP_STRIDE):
        wm = w1s.reshape(ns, CMP_BLOCK, NSA_DH, CMP_HIDDEN)[:, m * CMP_STRIDE:(m + 1) * CMP_STRIDE]
        wb = jnp.einsum('ctdj,ce->tcdej', wm, eye).reshape(CMP_STRIDE * ns * NSA_DH, ns * CMP_HIDDEN)
        wbig.append(wb.astype(BF16))
        pm = poss[:, m * CMP_STRIDE:(m + 1) * CMP_STRIDE]
        pbig.append(pm.transpose(1, 0, 2).reshape(CMP_STRIDE, ns * NSA_DH))
    b1 = jnp.concatenate([b1_k, b1_k, b1_v, b1_v]).reshape(1, ns * CMP_HIDDEN)
    w2s = jnp.stack([w2_k, w2_k, w2_v, w2_v])
    w2big = jnp.einsum('cjd,ce->cjed', w2s, eye).reshape(ns * CMP_HIDDEN, ns * NSA_DH).astype(BF16)
    b2 = jnp.concatenate([b2_k, b2_k, b2_v, b2_v]).reshape(1, ns * NSA_DH)
    full = lambda a: pl.BlockSpec(a.shape, lambda b: (0,) * a.ndim)
    sds = jax.ShapeDtypeStruct
    return pl.pallas_call(
        _compress_body,
        grid=(B,),
        in_specs=[pl.BlockSpec((1, S, ns * NSA_DH), lambda b: (b, 0, 0)),
                  full(pbig[0]), full(pbig[1]), full(wbig[0]), full(wbig[1]), full(b1), full(w2big), full(b2),
                  pl.BlockSpec((1, n, LANES), lambda b: (b, 0, 0)),
                  pl.BlockSpec((1, n, LANES), lambda b: (b, 0, 0))],
        out_specs=(pl.BlockSpec((1, NSA_KV_GROUPS, n, NSA_DH), lambda b: (b, 0, 0, 0)),
                   pl.BlockSpec((1, NSA_KV_GROUPS, NSA_DH, n), lambda b: (b, 0, 0, 0))),
        out_shape=(sds((B, NSA_KV_GROUPS, n, NSA_DH), BF16), sds((B, NSA_KV_GROUPS, NSA_DH, n), BF16)),
        scratch_shapes=[pltpu.VMEM((S, LANES), F32), pltpu.VMEM((S, LANES), F32)],
        compiler_params=pltpu.CompilerParams(
            dimension_semantics=("parallel",), vmem_limit_bytes=VMEM_LIMIT),
        name="compress",
    )(kvc, pbig[0], pbig[1], wbig[0], wbig[1], b1, w2big, b2, cs_c, sn_c)


NSA_ONES = 16


def _with_ones(v_t):
    return jnp.concatenate([v_t, jnp.ones((NSA_ONES, v_t.shape[1]), v_t.dtype)], axis=0)


def _softmax_step(s, m, acc, v_t):
    m_new = jnp.maximum(m, jnp.max(s, axis=0, keepdims=True))
    p = jnp.exp2(s - m_new).astype(BF16)
    acc = jnp.exp2(m - m_new) * acc + _dot(_with_ones(v_t), p)
    return m_new, acc


def _nsa_groups_body(q_ref, kc_ref, vc_ref, ks_ref, vs_ref, kw_ref, vw_ref, ng_ref, ovt_ref,
                     o_ref, sa_ref, sb_ref, sw_ref, *, tq, tk, n_slc):
    qi = pl.program_id(1)
    q0 = qi * tq
    n_cmp = kc_ref.shape[2]
    W = NSA_HPG * tq
    G = range(NSA_KV_GROUPS)
    per_head = lambda x: jnp.concatenate([x] * NSA_HPG, axis=1)
    q_all = [jnp.concatenate([q_ref[0, g * NSA_HPG + h] for h in range(NSA_HPG)], axis=1) for g in G]
    t_lane = q0 + lax.broadcasted_iota(jnp.int32, (1, tq), 1)

    cmp_end = CMP_STRIDE * lax.broadcasted_iota(jnp.int32, (n_cmp, 1), 0) + (CMP_BLOCK - 1)
    ok = cmp_end <= t_lane
    cmp_bias = per_head(jnp.where(ok, 0.0, NEG_INF))
    cmp_keep = per_head(jnp.where(ok, 1.0, 0.0))
    sm = [_dot(kc_ref[0, g], q_all[g]) + cmp_bias for g in G]
    wk = WINDOW + tq
    w0 = pl.multiple_of(jnp.maximum(q0 - WINDOW, 0), tq)
    kpos = w0 + lax.broadcasted_iota(jnp.int32, (wk, 1), 0)
    win_bias = per_head(jnp.where(jnp.logical_and(kpos <= t_lane, kpos > t_lane - WINDOW), 0.0, NEG_INF))
    for g in G:
        sw_ref[g] = _dot(kw_ref[0, g, pl.ds(w0, wk), :], q_all[g]) + win_bias
    blk = lax.broadcasted_iota(jnp.int32, (n_slc, tq), 0)
    cur = (q0 + lax.broadcasted_iota(jnp.int32, (n_slc, tq), 1)) // SLC_BLOCK
    forced = (blk == 0) | (blk == cur) | (blk == cur - 1)
    o_cmp, imp = [], []
    for g in G:
        e = jnp.exp2(sm[g] - jnp.max(sm[g], axis=0, keepdims=True))
        p = e * (1.0 / jnp.sum(e, axis=0, keepdims=True)) * cmp_keep
        o_cmp.append(_dot(vc_ref[0, g], p.astype(BF16)))
        p_sum = p[:, 0:tq]
        for h in range(1, NSA_HPG):
            p_sum = p_sum + p[:, h * tq:(h + 1) * tq]
        p_hi = p_sum.astype(BF16)
        p_lo = (p_sum - p_hi.astype(F32)).astype(BF16)
        imp_g = _dot(ovt_ref[...], p_hi) + _dot(ovt_ref[...], p_lo)
        imp.append(jnp.where(blk <= cur, jnp.where(forced, FORCED_SCORE, imp_g), NEG_INF))

    SUBL = 8

    def block_bias(n):
        out = []
        for g in G:
            groups = [imp[g][r:r + SUBL] for r in range(0, n, SUBL)]
            ranks = [jnp.zeros((SUBL, tq), F32) for _ in groups]
            for i in range(n):
                row = imp[g][i:i + 1, :]
                for gidx, grp in enumerate(groups):
                    r = gidx * SUBL
                    ge = jnp.where(row >= grp, 1.0, 0.0)
                    gt = jnp.where(row > grp, 1.0, 0.0)
                    if r > i:
                        inc = ge
                    elif r + SUBL - 1 <= i:
                        inc = gt
                    else:
                        inc = jnp.where(blk[r:r + SUBL] > i, ge, gt)
                    ranks[gidx] = ranks[gidx] + inc
            bias = jnp.where(jnp.concatenate(ranks, axis=0) < float(min(SLC_TOPK, n_slc)), 0.0, NEG_INF)
            out.append(jnp.concatenate([bias, jnp.zeros((NSA_DH - n, tq), F32)], axis=0) if n < NSA_DH else bias)
        return tuple(out)

    quarter = n_slc // 4
    visible = (q0 + tq - 1) // SLC_BLOCK + 1
    bias = lax.cond(
        visible <= 2 * quarter,
        lambda: lax.cond(visible <= quarter, lambda: block_bias(quarter), lambda: block_bias(2 * quarter)),
        lambda: lax.cond(visible <= 3 * quarter, lambda: block_bias(3 * quarter), lambda: block_bias(n_slc)))
    q_aug = [jnp.concatenate([q_all[g], per_head(bias[g]).astype(BF16)], axis=0) for g in G]

    kpos_l = lax.broadcasted_iota(jnp.int32, (tk, 1), 0)
    n_tiles = q0 // tk + 1
    last_tile = ks_ref.shape[2] // tk - 1

    def tile_start(jt):
        return pl.multiple_of(jnp.minimum(jt, last_tile) * tk, tk)

    def scores_into(s_ref, jt):
        causal = per_head(jnp.where(jt * tk + kpos_l <= t_lane, 0.0, NEG_INF))
        for g in G:
            s_ref[g] = _dot(ks_ref[0, g, pl.ds(tile_start(jt), tk), :], q_aug[g]) + causal

    def consume(s_ref, jt, carry):
        return tuple(_softmax_step(s_ref[g], *carry[g], vs_ref[0, g, :, pl.ds(tile_start(jt), tk)]) for g in G)

    def slc_pair(i, carry):
        scores_into(sb_ref, 2 * i + 1)
        carry = consume(sa_ref, 2 * i, carry)
        scores_into(sa_ref, 2 * i + 2)
        return consume(sb_ref, 2 * i + 1, carry)

    init = tuple((jnp.full((1, W), NEG_INF, F32), jnp.zeros((NSA_DH + NSA_ONES, W), F32)) for g in G)
    scores_into(sa_ref, 0)

    acc_w = []
    for g in G:
        sw = sw_ref[g]
        ew = jnp.exp2(sw - jnp.max(sw, axis=0, keepdims=True)).astype(BF16)
        acc_w.append(_dot(_with_ones(vw_ref[0, g, :, pl.ds(w0, wk)]), ew))

    slc = lax.fori_loop(0, (n_tiles + 1) // 2, slc_pair, init)

    for g in G:
        def gate(j):
            return jnp.concatenate([ng_ref[0, (g * NSA_HPG + h) * 3 + j:(g * NSA_HPG + h) * 3 + j + 1, :]
                                    for h in range(NSA_HPG)], axis=1)

        acc_s = slc[g][1]
        l_s, l_w = acc_s[NSA_DH:NSA_DH + 1], acc_w[g][NSA_DH:NSA_DH + 1]
        out_t = (gate(0) * o_cmp[g] + (gate(1) * (1.0 / l_s)) * acc_s[:NSA_DH]
                 + (gate(2) * (1.0 / l_w)) * acc_w[g][:NSA_DH])
        out_t = jnp.concatenate([out_t[:, h * tq:(h + 1) * tq] for h in range(NSA_HPG)], axis=0)
        o_ref[0, :, g * NSA_HPG * NSA_DH:(g + 1) * NSA_HPG * NSA_DH] = out_t.T.astype(o_ref.dtype)


def _nsa_attention(q, kc, vc, ks, vs, kw, vw, ng):
    B, H, dh, S = q.shape
    tq = min(ATT_TQ, S)
    tk = min(ATT_TK, S)
    assert S % tk == 0 and tk % tq == 0 and S >= WINDOW + tq and S // SLC_BLOCK <= dh
    n_cmp = kc.shape[2]
    n_slc = S // SLC_BLOCK
    c0 = CMP_STRIDE * np.arange(n_cmp)[None, :]
    s0 = SLC_BLOCK * np.arange(n_slc)[:, None]
    ov = np.clip(np.minimum(c0 + CMP_BLOCK, s0 + SLC_BLOCK) - np.maximum(c0, s0), 0, None) / CMP_BLOCK
    ovt = jnp.asarray(ov, BF16)
    G = NSA_KV_GROUPS
    grid = (B, S // tq)
    k_spec = lambda n, w: pl.BlockSpec((1, G, n, w), lambda b, i: (b, 0, 0, 0))
    vt_spec = lambda n: pl.BlockSpec((1, G, dh, n), lambda b, i: (b, 0, 0, 0))
    return pl.pallas_call(
        functools.partial(_nsa_groups_body, tq=tq, tk=tk, n_slc=n_slc),
        grid=grid,
        in_specs=[
            pl.BlockSpec((1, H, dh, tq), lambda b, i: (b, 0, 0, i)),
            k_spec(n_cmp, dh), vt_spec(n_cmp),
            k_spec(S, 2 * dh), vt_spec(S),
            k_spec(S, dh), vt_spec(S),
            pl.BlockSpec((1, LANES, tq), lambda b, i: (b, 0, i)),
            pl.BlockSpec(ovt.shape, lambda b, i: (0, 0)),
        ],
        out_specs=pl.BlockSpec((1, tq, H * dh), lambda b, i: (b, i, 0)),
        out_shape=jax.ShapeDtypeStruct((B, S, H * dh), BF16),
        scratch_shapes=[pltpu.VMEM((G, tk, NSA_HPG * tq), F32),
                        pltpu.VMEM((G, tk, NSA_HPG * tq), F32),
                        pltpu.VMEM((G, WINDOW + tq, NSA_HPG * tq), F32)],
        compiler_params=pltpu.CompilerParams(
            dimension_semantics=("parallel", "arbitrary"), vmem_limit_bytes=VMEM_LIMIT),
        name="nsa_attn",
    )(q, kc, vc, ks, vs, kw, vw, ng, ovt)


def _dot_3pass(a, b):
    a1 = a.astype(BF16)
    a2 = (a - a1.astype(F32)).astype(BF16)
    b1 = b.astype(BF16)
    b2 = (b - b1.astype(F32)).astype(BF16)
    return _dot(a1, b1) + (_dot(a1, b2) + _dot(a2, b1))


def _gla_pair_body(q_ref, k_ref, v_ref, vt_ref, ga_ref, r_ref, wa_ref, ba_ref, ng_ref, tri_ref,
                   o_ref, state_ref, b_ref, kp_ref, bp_ref, vp_ref, plain_ref, *, n_chunks):
    C, SUB, dk, dv, hp = GLA_C, GLA_SUB, GLA_DK, GLA_DV, GLA_HP
    W = hp * dk
    lane = lax.broadcasted_iota(jnp.int32, (1, W), 1)
    of_head = [lane // dk == hh for hh in range(hp)]
    tri = tri_ref[...]
    t_loc = lax.broadcasted_iota(jnp.int32, (C, 1), 0)
    state_ref[...] = jnp.zeros_like(state_ref)
    kp_ref[0:SUB, :] = jnp.zeros((SUB, W), F32)
    bp_ref[0:SUB, :] = jnp.zeros((SUB, W), F32)
    vp_ref[:, 0:SUB, :] = jnp.zeros((hp, SUB, dv), F32)

    def decay_chunk(c, all_plain):
        c0 = pl.multiple_of(c * C, C)
        la = jax.nn.log_sigmoid(_dot_3pass(ga_ref[0, pl.ds(c0, C), :], wa_ref[0]) + ba_ref[0]) / GLA_TAU
        b = _dot_exact_lhs(tri, la)
        b_ref[pl.ds(c0, C), :] = b
        plain = (jnp.min(b[C - 1:C, :]) > -GLA_PLAIN_DECAY).astype(jnp.int32)
        plain_ref[c] = plain
        return jnp.minimum(all_plain, plain)

    all_plain = lax.fori_loop(0, n_chunks, decay_chunk, jnp.int32(1), unroll=GLA_UNROLL_DECAY)

    def chunk(c, check_decay):
        c0 = pl.multiple_of(c * C, C)
        b = b_ref[pl.ds(c0, C), :]
        q = q_ref[0, pl.ds(c0, C), :]
        k = k_ref[0, pl.ds(c0, C), :]
        v = [v_ref[0, pl.ds(c0, C), hh * dv:(hh + 1) * dv] for hh in range(hp)]
        vt = [vt_ref[0, hh * dv:(hh + 1) * dv, pl.ds(c0, C)] for hh in range(hp)]
        b_last = b[C - 1:C, :]
        st = state_ref[...]
        st_b = st.astype(BF16)
        qg = q * jnp.exp(b)
        qg_h = [jnp.where(of_head[hh], qg, 0.0).astype(BF16) for hh in range(hp)]
        o_inter = [_dot_nt(qg_h[hh], st_b) for hh in range(hp)]

        def intra_plain():
            ke = (k * jnp.exp(-b)).astype(BF16)
            row = lax.broadcasted_iota(jnp.int32, (C, C), 0)
            col = lax.broadcasted_iota(jnp.int32, (C, C), 1)
            return tuple(_dot(jnp.where(row >= col, _dot_nt(qg_h[hh], ke), 0.0).astype(BF16), v[hh])
                         for hh in range(hp))

        def intra_strong_decay():
            far = [[jnp.zeros((SUB, dv), F32)] for _ in range(hp)]
            for i in range(1, C // SUB):
                r0 = i * SUB
                b_first = b[r0:r0 + 1, :]
                qt = q[r0:r0 + SUB] * jnp.exp(b[r0:r0 + SUB] - b_first)
                kt = (k[:r0] * jnp.exp(b_first - b[:r0])).astype(BF16)
                lt = lax.broadcasted_iota(jnp.int32, (SUB, r0), 0)
                ls = lax.broadcasted_iota(jnp.int32, (SUB, r0), 1)
                for hh in range(hp):
                    a = _dot_nt(jnp.where(of_head[hh], qt, 0.0).astype(BF16), kt)
                    a = jnp.where(lt + (r0 - SUB) >= ls, a, 0.0)
                    far[hh].append(_dot(a.astype(BF16), v[hh][:r0]))
            acc = [jnp.concatenate(far[hh], axis=0) for hh in range(hp)]
            kp_ref[SUB:SUB + C, :] = k
            bp_ref[SUB:SUB + C, :] = b
            for hh in range(hp):
                vp_ref[hh, SUB:SUB + C, :] = v[hh].astype(F32)
            for d in range(SUB):
                kd = kp_ref[SUB - d:SUB - d + C, :]
                bd = bp_ref[SUB - d:SUB - d + C, :]
                valid = t_loc >= d
                w = jnp.exp(jnp.where(valid, b - bd, 0.0))
                x = jnp.where(valid, q * kd * w, 0.0)
                for hh in range(hp):
                    a_d = jnp.sum(jnp.where(of_head[hh], x, 0.0), axis=-1, keepdims=True)
                    acc[hh] = acc[hh] + a_d * vp_ref[hh, SUB - d:SUB - d + C, :]
            return tuple(acc)

        if check_decay:
            o_intra = lax.cond(plain_ref[c] > 0, intra_plain, intra_strong_decay)
        else:
            o_intra = intra_plain()
        k_dec = (k * jnp.exp(b_last - b)).astype(BF16)
        upd = _dot(vt[hp - 1], k_dec)
        for hh in range(hp - 2, -1, -1):
            upd = jnp.where(of_head[hh], _dot(vt[hh], k_dec), upd)
        state_ref[...] = st * jnp.exp(b_last) + upd
        for hh in range(hp):
            o = o_inter[hh] + o_intra[hh]
            o = o * lax.rsqrt(jnp.mean(o * o, axis=-1, keepdims=True) + EPS) * ng_ref[hh]
            gate = jax.nn.silu(r_ref[0, pl.ds(c0, C), hh * dv:(hh + 1) * dv])
            o_ref[0, pl.ds(c0, C), hh * dv:(hh + 1) * dv] = (o * gate).astype(o_ref.dtype)
        return 0

    @pl.when(all_plain > 0)
    def _():
        lax.fori_loop(0, n_chunks, lambda c, _: chunk(c, False), 0, unroll=GLA_UNROLL_PLAIN)

    @pl.when(all_plain <= 0)
    def _():
        lax.fori_loop(0, n_chunks, lambda c, _: chunk(c, True), 0)


def _gla_pairs(gq, gk, gv, gvt, ga, gr, w_a2, b_a, norm_g):
    B, S, _ = gq.shape
    H, dk, dv, hp = GLA_HEADS, GLA_DK, GLA_DV, GLA_HP
    W = hp * dk
    assert W == LANES and H % hp == 0
    wa = jnp.pad(w_a2, ((0, LANES - GLA_RANK), (0, 0))).reshape(LANES, H // hp, W).transpose(1, 0, 2)
    ba = b_a.reshape(H // hp, 1, W)
    ng = norm_g.reshape(H, 1, dv)
    tri = jnp.asarray(np.tril(np.ones((GLA_C, GLA_C))), BF16)
    n_chunks = S // GLA_C
    tok_spec = lambda w: pl.BlockSpec((1, S, hp * w), lambda b, h: (b, 0, h))
    return pl.pallas_call(
        functools.partial(_gla_pair_body, n_chunks=n_chunks),
        grid=(B, H // hp),
        in_specs=[tok_spec(dk), tok_spec(dk), tok_spec(dv),
                  pl.BlockSpec((1, hp * dv, S), lambda b, h: (b, h, 0)),
                  pl.BlockSpec((1, S, LANES), lambda b, h: (b, 0, 0)),
                  tok_spec(dv),
                  pl.BlockSpec((1, LANES, W), lambda b, h: (h, 0, 0)),
                  pl.BlockSpec((1, 1, W), lambda b, h: (h, 0, 0)),
                  pl.BlockSpec((hp, 1, dv), lambda b, h: (h, 0, 0)),
                  pl.BlockSpec(tri.shape, lambda b, h: (0, 0))],
        out_specs=tok_spec(dv),
        out_shape=jax.ShapeDtypeStruct((B, S, H * dv), BF16),
        scratch_shapes=[pltpu.VMEM((dv, W), F32),
                        pltpu.VMEM((S, W), F32),
                        pltpu.VMEM((GLA_SUB + GLA_C, W), F32),
                        pltpu.VMEM((GLA_SUB + GLA_C, W), F32),
                        pltpu.VMEM((hp, GLA_SUB + GLA_C, dv), F32),
                        pltpu.SMEM((n_chunks,), jnp.int32)],
        compiler_params=pltpu.CompilerParams(
            dimension_semantics=("parallel", "parallel"), vmem_limit_bytes=VMEM_LIMIT),
        name="gla",
    )(gq, gk, gv, gvt, ga, gr, wa, ba, ng, tri)


def _merge_body(x_ref, ya_ref, yb_ref, ma_ref, mb_ref, wpa_ref, wpb_ref, wo_ref, gf_ref, wr_ref, br_ref,
                h_ref, v_ref, comb_ref):
    y_a = _dot(ya_ref[...], wpa_ref[...])
    y_b = _dot(yb_ref[...], wpb_ref[...])
    mixed = ma_ref[...] * y_a + mb_ref[...] * y_b
    h = x_ref[...] + _dot(mixed.astype(BF16), wo_ref[...])
    h_ref[...] = h
    v = h * lax.rsqrt(jnp.mean(h * h, axis=-1, keepdims=True) + EPS) * gf_ref[...]
    v_ref[...] = v.astype(BF16)
    wr = wr_ref[...]
    w_hi = wr.astype(BF16)
    w_lo = (wr - w_hi.astype(F32)).astype(BF16)
    v_hi = v.astype(BF16)
    v_lo = (v - v_hi.astype(F32)).astype(BF16)
    both = _dot(v_hi, jnp.concatenate([w_hi, w_lo], axis=1))
    logits = (both[:, :LANES] + both[:, LANES:]) + _dot(v_lo, w_hi) + br_ref[...]
    lane = lax.broadcasted_iota(jnp.int32, logits.shape, 1)
    is_grp = jnp.logical_and(lane >= MOE_EXPERTS, lane < MOE_EXPERTS + MOE_GROUPS)
    lg = jnp.where(is_grp, logits, NEG_INF)
    eg = jnp.where(is_grp, jnp.exp(lg - jnp.max(lg, axis=-1, keepdims=True)), 0.0)
    pg = eg / jnp.sum(eg, axis=-1, keepdims=True)
    p_grp = jnp.max(pg, axis=-1, keepdims=True)
    g_sel = jnp.min(jnp.where(jnp.logical_and(is_grp, pg == p_grp), lane, 2 * LANES),
                    axis=-1, keepdims=True) - MOE_EXPERTS
    in_grp = jnp.logical_and(lane < MOE_EXPERTS, lane // MOE_EPG == g_sel)
    le = jnp.where(in_grp, logits, NEG_INF)
    ee = jnp.where(in_grp, jnp.exp(le - jnp.max(le, axis=-1, keepdims=True)), 0.0)
    pin = ee / jnp.sum(ee, axis=-1, keepdims=True)
    p1 = jnp.max(jnp.where(in_grp, pin, -1.0), axis=-1, keepdims=True)
    i1 = jnp.min(jnp.where(jnp.logical_and(in_grp, pin == p1), lane, 2 * LANES), axis=-1, keepdims=True)
    rest = jnp.logical_and(in_grp, lane != i1)
    p2 = jnp.max(jnp.where(rest, pin, -1.0), axis=-1, keepdims=True)
    i2 = jnp.min(jnp.where(jnp.logical_and(rest, pin == p2), lane, 2 * LANES), axis=-1, keepdims=True)
    tot = p1 + p2
    comb_ref[...] = (jnp.where(lane == i1, p_grp * p1 / tot, 0.0)
                     + jnp.where(lane == i2, p_grp * p2 / tot, 0.0))


def _merge(x2, ya, yb, ma, mb, w_proj_nsa, w_proj_gla, w_out, g_ffn, w_grp, b_grp, w_exp, b_exp):
    T, D = x2.shape
    tm = min(MERGE_TM, T)
    wr = jnp.pad(jnp.concatenate([w_exp, w_grp], axis=1), ((0, 0), (0, LANES - MOE_EXPERTS - MOE_GROUPS)))
    br = jnp.pad(jnp.concatenate([b_exp, b_grp]), (0, LANES - MOE_EXPERTS - MOE_GROUPS)).reshape(1, LANES)
    tok = lambda w: pl.BlockSpec((tm, w), lambda i: (i, 0))
    full = lambda a: pl.BlockSpec(a.shape, lambda i: (0, 0))
    wpa, wpb, wo = w_proj_nsa.astype(BF16), w_proj_gla.astype(BF16), w_out.astype(BF16)
    gf = g_ffn.reshape(1, D)
    sds = jax.ShapeDtypeStruct
    return pl.pallas_call(
        _merge_body,
        grid=(T // tm,),
        in_specs=[tok(D), tok(ya.shape[1]), tok(yb.shape[1]), tok(D), tok(D),
                  full(wpa), full(wpb), full(wo), full(gf), full(wr), full(br)],
        out_specs=(tok(D), tok(D), tok(LANES)),
        out_shape=(sds((T, D), F32), sds((T, D), BF16), sds((T, LANES), F32)),
        compiler_params=pltpu.CompilerParams(
            dimension_semantics=("parallel",), vmem_limit_bytes=VMEM_LIMIT),
        name="merge",
    )(x2, ya, yb, ma, mb, wpa, wpb, wo, gf, wr, br)


def _moe_rows(tw):
    rows = 2 * tw + MOE_EXPERTS * (MOE_ALIGN - 1) + MOE_RT
    return -(-rows // MOE_RT) * MOE_RT


def _moe_window(e, phase, v_ref, comb_ref, tri_ref, y_ref, xs_ref, z_ref, meta_ref, pos_ref, tw, rmax):
    no_row = -1.0

    @pl.when(e == 0 if phase == "route" else False)
    def _route():
        comb = comb_ref[...]
        assigned = comb > 0.0
        a = jnp.where(assigned, 1.0, 0.0)
        tri = tri_ref[...]
        run = jnp.zeros((1, LANES), F32)
        ranks = []
        for b in range(tw // MOE_RT):
            ab = a[b * MOE_RT:(b + 1) * MOE_RT]
            ranks.append(_dot(tri, ab.astype(BF16)) + run)
            run = run + jnp.sum(ab, axis=0, keepdims=True)
        rank = jnp.concatenate(ranks, axis=0)
        cnt_pad = jnp.floor((run + (MOE_ALIGN - 1)) * (1.0 / MOE_ALIGN)) * MOE_ALIGN
        incl = jnp.broadcast_to(cnt_pad, (8, LANES))
        lane8 = lax.broadcasted_iota(jnp.int32, (8, LANES), 1)
        shift = 1
        while shift < MOE_EXPERTS:
            incl = incl + jnp.where(lane8 >= shift, pltpu.roll(incl, shift, 1), 0.0)
            shift *= 2
        offs = incl[0:1] - cnt_pad
        meta_ref[0:1, :] = offs
        meta_ref[1:2, :] = run
        row_of = offs + rank
        pos_a = jnp.min(jnp.where(assigned, row_of, 1e9), axis=-1, keepdims=True)
        pos_b = jnp.max(jnp.where(assigned, row_of, no_row), axis=-1, keepdims=True)
        pos_a = jnp.where(pos_a > 1e8, no_row, pos_a)
        pos_b = jnp.where(pos_b == pos_a, no_row, pos_b)
        w_a = jnp.sum(jnp.where(jnp.logical_and(assigned, row_of == pos_a), comb, 0.0), axis=-1, keepdims=True)
        w_b = jnp.sum(jnp.where(jnp.logical_and(assigned, row_of == pos_b), comb, 0.0), axis=-1, keepdims=True)
        lane_t = lax.broadcasted_iota(jnp.int32, (tw, LANES), 1)
        pos_ref[...] = jnp.where(lane_t == 0, pos_a, jnp.where(lane_t == 1, pos_b, jnp.where(
            lane_t == 2, w_a, jnp.where(lane_t == 3, w_b, no_row))))
        pos_t = pos_ref[...].T
        pa, pb = pos_t[0:1], pos_t[1:2]
        v = v_ref[...]
        for rt in range(rmax // MOE_RT):
            r = (rt * MOE_RT + lax.broadcasted_iota(jnp.int32, (MOE_RT, 1), 0)).astype(F32)
            p = jnp.where(r == pa, 1.0, jnp.where(r == pb, 1.0, 0.0)).astype(BF16)
            xs_ref[rt * MOE_RT:(rt + 1) * MOE_RT, :] = _dot(p, v).astype(BF16)
        z_ref[...] = jnp.zeros_like(z_ref)

    @pl.when(e == pl.num_programs(1) - 1 if phase == "combine" else False)
    def _combine():
        r = lax.broadcasted_iota(jnp.int32, (1, rmax), 1).astype(F32)
        z = z_ref[...]
        for tt in range(tw // MOE_RT):
            rows = slice(tt * MOE_RT, (tt + 1) * MOE_RT)
            pa, pb = pos_ref[rows, 0:1], pos_ref[rows, 1:2]
            w_a, w_b = pos_ref[rows, 2:3], pos_ref[rows, 3:4]
            q = jnp.where(r == pa, w_a, jnp.where(r == pb, w_b, 0.0)).astype(BF16)
            y_ref[rows, :] = _dot(q, z).astype(y_ref.dtype)


def _moe_body(v_ref, comb_ref, wg_ref, wu_ref, wd_ref, tri_ref, y_ref,
              xs_ref, z_ref, meta_ref, pos_ref, *, tw, rmax, n_win):
    e = pl.program_id(1)
    lane = lax.broadcasted_iota(jnp.int32, (1, LANES), 1)

    def window_phase(phase):
        for w in range(n_win):
            rows = pl.ds(w * tw, tw)
            _moe_window(e, phase, v_ref.at[rows], comb_ref.at[rows], tri_ref, y_ref.at[rows],
                        xs_ref.at[w], z_ref.at[w], meta_ref.at[w], pos_ref.at[w], tw, rmax)

    window_phase("route")

    pick = lambda w, row: jnp.sum(jnp.where(lane == e, meta_ref[w, row:row + 1, :], 0.0)).astype(jnp.int32)
    offs = [pick(w, 0) for w in range(n_win)]
    tiles = [(pick(w, 1) + MOE_ET - 1) // MOE_ET for w in range(n_win)]

    def expert(x):
        hdn = jax.nn.silu(_dot(x, wg_ref[0])) * _dot(x, wu_ref[0])
        return _dot(hdn.astype(BF16), wd_ref[0]).astype(BF16)

    def joint_tile(i, _):
        r0 = [pl.multiple_of(offs[w] + i * MOE_ET, MOE_ALIGN) for w in range(n_win)]
        z = expert(jnp.concatenate([xs_ref[w, pl.ds(r0[w], MOE_ET), :] for w in range(n_win)], axis=0))
        for w in range(n_win):
            z_ref[w, pl.ds(r0[w], MOE_ET), :] = z[w * MOE_ET:(w + 1) * MOE_ET]
        return 0

    n_joint = functools.reduce(jnp.minimum, tiles)
    lax.fori_loop(0, n_joint, joint_tile, 0)
    for w in range(n_win):
        def own_tile(i, _, w=w):
            r0 = pl.multiple_of(offs[w] + i * MOE_ET, MOE_ALIGN)
            z_ref[w, pl.ds(r0, MOE_ET), :] = expert(xs_ref[w, pl.ds(r0, MOE_ET), :])
            return 0

        lax.fori_loop(n_joint, tiles[w], own_tile, 0)

    window_phase("combine")


def _moe(v, comb, w_gate, w_up, w_down):
    T, D = v.shape
    tw = min(MOE_TW, T)
    n_win = min(MOE_WINDOWS, T // tw)
    rmax = _moe_rows(tw)
    E, _, F = w_gate.shape
    wg, wu, wd = w_gate.astype(BF16), w_up.astype(BF16), w_down.astype(BF16)
    tri = jnp.asarray(np.tril(np.ones((MOE_RT, MOE_RT)), -1), BF16)
    tok = lambda w: pl.BlockSpec((n_win * tw, w), lambda i, e: (i, 0))
    tok_in = lambda w: pl.BlockSpec((n_win * tw, w), lambda i, e: (i, 0), pipeline_mode=pl.Buffered(1))
    return pl.pallas_call(
        functools.partial(_moe_body, tw=tw, rmax=rmax, n_win=n_win),
        grid=(T // (n_win * tw), E),
        in_specs=[tok_in(D), tok_in(LANES),
                  pl.BlockSpec((1, D, F), lambda i, e: (e, 0, 0)),
                  pl.BlockSpec((1, D, F), lambda i, e: (e, 0, 0)),
                  pl.BlockSpec((1, F, D), lambda i, e: (e, 0, 0)),
                  pl.BlockSpec(tri.shape, lambda i, e: (0, 0))],
        out_specs=tok(D),
        out_shape=jax.ShapeDtypeStruct((T, D), BF16),
        scratch_shapes=[pltpu.VMEM((n_win, rmax, D), BF16),
                        pltpu.VMEM((n_win, rmax, D), BF16),
                        pltpu.VMEM((n_win, 8, LANES), F32),
                        pltpu.VMEM((n_win, tw, LANES), F32)],
        compiler_params=pltpu.CompilerParams(
            dimension_semantics=("parallel", "arbitrary"), vmem_limit_bytes=VMEM_LIMIT),
        name="moe",
    )(v, comb, wg, wu, wd, tri)


def _final_body(h_ref, y_ref, g_ref, o_ref):
    h = h_ref[...] + y_ref[...].astype(F32)
    o_ref[...] = h * lax.rsqrt(jnp.mean(h * h, axis=-1, keepdims=True) + EPS) * g_ref[...]


def _final_norm(h, y, g_final):
    T, D = h.shape
    tm = min(FINAL_TM, T)
    tok = pl.BlockSpec((tm, D), lambda i: (i, 0))
    return pl.pallas_call(
        _final_body,
        grid=(T // tm,),
        in_specs=[tok, tok, pl.BlockSpec((1, D), lambda i: (0, 0))],
        out_specs=tok,
        out_shape=jax.ShapeDtypeStruct((T, D), F32),
        compiler_params=pltpu.CompilerParams(
            dimension_semantics=("parallel",), vmem_limit_bytes=VMEM_LIMIT),
        name="final_norm",
    )(h, y, g_final.reshape(1, D))


def _rope_inv_freq():
    half = NSA_DH // 2
    return 1.0 / (ROPE_THETA ** (jnp.arange(half, dtype=F32) / half))


def _rope_tables(positions):
    ang = positions.astype(F32)[..., None] * _rope_inv_freq()
    cos, sin = jnp.cos(ang), jnp.sin(ang)
    cs = jnp.concatenate([cos, cos, cos, cos], axis=-1)
    sn = jnp.concatenate([-sin, sin, -sin, sin], axis=-1)
    return cs, sn


def _layer(h, positions, g_mix, w_in, cmp_pos_k, cmp_w1_k, cmp_b1_k, cmp_w2_k, cmp_b2_k,
           cmp_pos_v, cmp_w1_v, cmp_b1_v, cmp_w2_v, cmp_b2_v, gla_w_a2, gla_b_a, gla_norm_g,
           w_proj_nsa, w_proj_gla, w_out, g_ffn, w_grp, b_grp, w_exp, b_exp, w_gate, w_up, w_down, g_out):
    B, S, D = h.shape
    n_chunks = S // CMP_STRIDE
    cmp_end = jnp.minimum(CMP_STRIDE * jnp.arange(n_chunks) + CMP_BLOCK - 1, S - 1)
    cs_c, sn_c = _rope_tables(jnp.take(positions, cmp_end, axis=1))
    (q, kvc, ks, vs, kw, vw, gq, gk, gv, gvt, gr, ma, mb, ng, ga) = _in_proj(h, g_mix, w_in, positions)
    kc, vc = _compress(kvc, cs_c, sn_c, cmp_pos_k, cmp_w1_k, cmp_b1_k, cmp_w2_k, cmp_b2_k,
                       cmp_pos_v, cmp_w1_v, cmp_b1_v, cmp_w2_v, cmp_b2_v)
    ya = _nsa_attention(q, kc, vc, ks, vs, kw, vw, ng)
    yb = _gla_pairs(gq, gk, gv, gvt, ga, gr, gla_w_a2, gla_b_a, gla_norm_g)
    T = B * S
    h1, v, comb = _merge(h.reshape(T, D), ya.reshape(T, -1), yb.reshape(T, -1), ma.reshape(T, D), mb.reshape(T, D),
                         w_proj_nsa, w_proj_gla, w_out, g_ffn, w_grp, b_grp, w_exp, b_exp)
    y = _moe(v, comb, w_gate, w_up, w_down)
    return _final_norm(h1, y, g_out).reshape(B, S, D)


def kernel(x, positions, g_mix, w_in, cmp_pos_k, cmp_w1_k, cmp_b1_k, cmp_w2_k, cmp_b2_k, cmp_pos_v, cmp_w1_v,
           cmp_b1_v, cmp_w2_v, cmp_b2_v, gla_w_a2, gla_b_a, gla_norm_g, w_proj_nsa, w_proj_gla, w_out, g_ffn,
           w_grp, b_grp, w_exp, b_exp, w_gate, w_up, w_down, g_final):
    depth = g_mix.shape[0]
    assert depth == 1, "the final norm closes the single layer"
    return _layer(x, positions, g_mix[0], w_in[0], cmp_pos_k[0], cmp_w1_k[0], cmp_b1_k[0], cmp_w2_k[0],
                  cmp_b2_k[0], cmp_pos_v[0], cmp_w1_v[0], cmp_b1_v[0], cmp_w2_v[0], cmp_b2_v[0],
                  gla_w_a2[0], gla_b_a[0], gla_norm_g[0], w_proj_nsa[0], w_proj_gla[0], w_out[0],
                  g_ffn[0], w_grp[0], b_grp[0], w_exp[0], b_exp[0], w_gate[0], w_up[0], w_down[0], g_final)
```

```python
import functools

import numpy as np
import jax
import jax.numpy as jnp
from jax import lax
from jax.experimental import pallas as pl
from jax.experimental.pallas import tpu as pltpu

F32 = jnp.float32
BF16 = jnp.bfloat16

NSA_HEADS = 8
NSA_KV_GROUPS = 2
NSA_HPG = NSA_HEADS // NSA_KV_GROUPS
NSA_DH = 64
CMP_BLOCK = 32
CMP_STRIDE = 16
CMP_HIDDEN = 256
SLC_BLOCK = 64
SLC_TOPK = 16
WINDOW = 512
GLA_HEADS = 4
GLA_DK = 64
GLA_DV = 128
GLA_RANK = 16
GLA_TAU = 16.0
MOE_GROUPS = 4
MOE_EPG = 8
MOE_EXPERTS = MOE_GROUPS * MOE_EPG
MOE_DFF = 512
ROPE_THETA = 10000.0
EPS = 1e-6
NEG_INF = -1e30
FORCED_SCORE = 1e4
LOG2E = 1.4426950408889634

LANES = 128
VMEM_LIMIT = 56 * 1024 * 1024

IN_TM = 512
ATT_TQ = 128
ATT_TK = 512
GLA_C = 128
GLA_SUB = 16
GLA_HP = 2
GLA_UNROLL_DECAY = 4
GLA_UNROLL_PLAIN = 4
GLA_PLAIN_DECAY = 60.0
MERGE_TM = 512
MERGE_PARTS = 1
MOE_TW = 1024
MOE_WINDOWS = 2
FINAL_TM = 1024
MOE_RT = 256
MOE_ET = 128
MOE_ALIGN = 16


def _dot(a, b):
    return jnp.dot(a, b, preferred_element_type=F32)


def _dot_nt(a, b):
    return lax.dot_general(a, b, (((1,), (1,)), ((), ())), preferred_element_type=F32)


def _split3(x):
    x1 = x.astype(BF16)
    r1 = x - x1.astype(F32)
    x2 = r1.astype(BF16)
    r2 = r1 - x2.astype(F32)
    x3 = r2.astype(BF16)
    return x1, x2, x3


def _dot_exact_lhs(a_bf16, x):
    x1, x2, x3 = _split3(x)
    return _dot(a_bf16, x1) + _dot(a_bf16, x2) + _dot(a_bf16, x3)


def _dot_f32(a, b):
    a1, a2, a3 = _split3(a)
    b1, b2, b3 = _split3(b)
    return (_dot(a1, b1) + (_dot(a1, b2) + _dot(a2, b1))
            + (_dot(a1, b3) + _dot(a2, b2) + _dot(a3, b1)))


def _rope_lanes(z, cs, sn):
    w = z.shape[-1]
    lane = lax.broadcasted_iota(jnp.int32, z.shape, 1)
    first_half = (lane % NSA_DH) < (NSA_DH // 2)
    rot = jnp.where(first_half, pltpu.roll(z, w - NSA_DH // 2, 1), pltpu.roll(z, NSA_DH // 2, 1))
    reps = w // LANES
    if reps > 1:
        cs = jnp.concatenate([cs] * reps, axis=1)
        sn = jnp.concatenate([sn] * reps, axis=1)
    return z * cs + rot * sn


_SEC = {}
_off = 0
for _name, _w in (("q", 512), ("kvc", 256), ("ks", 128), ("vs", 128), ("kw", 128), ("vw", 128),
                  ("gq", 256), ("gk", 256), ("gv", 512), ("gr", 512), ("ma", 1024), ("mb", 1024),
                  ("nga", 128)):
    _SEC[_name] = (_off, _off + _w)
    _off += _w
IN_NW = _off


def _in_proj_body(x_ref, g_ref, w_ref, pos_ref, inv_ref, wa_ref, ba_ref, tri_ref,
                  q_ref, kvc_ref, ks_ref, vs_ref, kw_ref, vw_ref,
                  gq_ref, gk_ref, gv_ref, gvt_ref, gr_ref, ma_ref, mb_ref, ng_ref, gb_ref):
    x = x_ref[0]
    var = jnp.mean(x * x, axis=-1, keepdims=True)
    u = (x * lax.rsqrt(var + EPS) * g_ref[...]).astype(BF16)
    ang = pos_ref[0].astype(F32) * inv_ref[...]
    lane_r = lax.broadcasted_iota(jnp.int32, ang.shape, 1)
    cs = jnp.cos(ang)
    sn = jnp.where((lane_r % NSA_DH) < (NSA_DH // 2), -1.0, 1.0) * jnp.sin(ang)

    def proj(name):
        a, b = _SEC[name]
        return _dot(u, w_ref[:, a:b])

    zq_t = (_rope_lanes(proj("q"), cs, sn) * (NSA_DH ** -0.5 * LOG2E)).T
    for h in range(NSA_HEADS):
        q_ref[0, h] = zq_t[h * NSA_DH:(h + 1) * NSA_DH].astype(BF16)
    kvc_ref[0] = proj("kvc")
    zks = _rope_lanes(proj("ks"), cs, sn)
    zkw = _rope_lanes(proj("kw"), cs, sn)
    zvs_t = proj("vs").T
    zvw_t = proj("vw").T
    tm = zks.shape[0]
    lane = lax.broadcasted_iota(jnp.int32, (tm, LANES), 1)
    blk = (pl.program_id(1) * tm + lax.broadcasted_iota(jnp.int32, (tm, LANES), 0)) // SLC_BLOCK
    onehot = jnp.where(lane - NSA_DH == blk, 1.0, 0.0)
    for g in range(NSA_KV_GROUPS):
        sl = slice(g * NSA_DH, (g + 1) * NSA_DH)
        k_front = zks if g == 0 else pltpu.roll(zks, LANES - g * NSA_DH, 1)
        ks_ref[0, g] = jnp.where(lane < NSA_DH, k_front, onehot).astype(BF16)
        kw_ref[0, g] = zkw[:, sl].astype(BF16)
        vs_ref[0, g] = zvs_t[sl].astype(BF16)
        vw_ref[0, g] = zvw_t[sl].astype(BF16)
    gq_ref[0] = proj("gq") * (GLA_DK ** -0.5)
    gk_ref[0] = proj("gk")
    zgv = proj("gv")
    gv_ref[0] = zgv.astype(BF16)
    gvt_ref[0] = zgv.T.astype(BF16)
    gr_ref[0] = proj("gr")
    ma_ref[0] = jax.nn.sigmoid(proj("ma")).astype(ma_ref.dtype)
    mb_ref[0] = jax.nn.sigmoid(proj("mb")).astype(mb_ref.dtype)
    z_nga = proj("nga")
    ng_ref[0] = jax.nn.sigmoid(z_nga).T
    la = jax.nn.log_sigmoid(_dot_3pass(z_nga, wa_ref[...]) + ba_ref[...]) / GLA_TAU
    tri = tri_ref[...]
    gb_ref[0] = jnp.concatenate([_dot_exact_lhs(tri, la[c * GLA_C:(c + 1) * GLA_C])
                                 for c in range(la.shape[0] // GLA_C)], axis=0)


def _in_proj(x, g_mix, w_in, positions, gla_w_a2, gla_b_a):
    B, S, D = x.shape
    inv = jnp.tile(_rope_inv_freq(), LANES // (NSA_DH // 2)).reshape(1, LANES)
    tm = min(IN_TM, S)
    assert tm % GLA_C == 0
    n_gates = 3 * NSA_HEADS
    wa = jnp.pad(gla_w_a2, ((n_gates, LANES - n_gates - GLA_RANK), (0, 0)))
    ba = gla_b_a.reshape(1, -1)
    tri = jnp.asarray(np.tril(np.ones((GLA_C, GLA_C))), BF16)
    splits = np.cumsum((512,) + (128,) * 6 + (24, 256, 256, 512, 16, 512, 1024, 1024))
    (wq, wkc, wvc, wks, wvs, wkw, wvw, wng, wgq, wgk, wgv, wga, wgr, wma, wmb) = jnp.split(
        w_in, splits[:-1].tolist(), axis=1)
    pad = lambda w: jnp.pad(w, ((0, 0), (0, LANES - w.shape[1])))
    w_all = jnp.concatenate([wq, wkc, wvc, wks, wvs, wkw, wvw, wgq, wgk, wgv, wgr, wma, wmb,
                             pad(jnp.concatenate([wng, wga], axis=1))], axis=1).astype(BF16)
    assert w_all.shape[1] == IN_NW
    grid = (B, S // tm)
    tok = lambda w: pl.BlockSpec((1, tm, w), lambda b, i: (b, i, 0))
    head = lambda n, w: pl.BlockSpec((1, n, tm, w), lambda b, i: (b, 0, i, 0))
    head_t = lambda n, w: pl.BlockSpec((1, n, w, tm), lambda b, i: (b, 0, 0, i))
    sds = jax.ShapeDtypeStruct
    out_shape = (
        sds((B, NSA_HEADS, NSA_DH, S), BF16),
        sds((B, S, 256), F32),
        sds((B, NSA_KV_GROUPS, S, LANES), BF16),
        sds((B, NSA_KV_GROUPS, NSA_DH, S), BF16),
        sds((B, NSA_KV_GROUPS, S, NSA_DH), BF16),
        sds((B, NSA_KV_GROUPS, NSA_DH, S), BF16),
        sds((B, S, GLA_HEADS * GLA_DK), F32),
        sds((B, S, GLA_HEADS * GLA_DK), F32),
        sds((B, S, GLA_HEADS * GLA_DV), BF16),
        sds((B, GLA_HEADS * GLA_DV, S), BF16),
        sds((B, S, GLA_HEADS * GLA_DV), F32),
        sds((B, S, D), BF16),
        sds((B, S, D), BF16),
        sds((B, LANES, S), F32),
        sds((B, S, GLA_HEADS * GLA_DK), F32),
    )
    out_specs = (
        head_t(NSA_HEADS, NSA_DH), tok(256),
        head(NSA_KV_GROUPS, LANES), head_t(NSA_KV_GROUPS, NSA_DH),
        head(NSA_KV_GROUPS, NSA_DH), head_t(NSA_KV_GROUPS, NSA_DH),
        tok(GLA_HEADS * GLA_DK), tok(GLA_HEADS * GLA_DK),
        tok(GLA_HEADS * GLA_DV),
        pl.BlockSpec((1, GLA_HEADS * GLA_DV, tm), lambda b, i: (b, 0, i)),
        tok(GLA_HEADS * GLA_DV), tok(D), tok(D),
        pl.BlockSpec((1, LANES, tm), lambda b, i: (b, 0, i)), tok(GLA_HEADS * GLA_DK),
    )
    return pl.pallas_call(
        _in_proj_body,
        grid=grid,
        in_specs=[
            tok(D),
            pl.BlockSpec((1, D), lambda b, i: (0, 0)),
            pl.BlockSpec((D, IN_NW), lambda b, i: (0, 0), pipeline_mode=pl.Buffered(1)),
            tok(1),
            pl.BlockSpec((1, LANES), lambda b, i: (0, 0)),
            pl.BlockSpec(wa.shape, lambda b, i: (0, 0)),
            pl.BlockSpec(ba.shape, lambda b, i: (0, 0)),
            pl.BlockSpec(tri.shape, lambda b, i: (0, 0)),
        ],
        out_specs=out_specs,
        out_shape=out_shape,
        compiler_params=pltpu.CompilerParams(
            dimension_semantics=("parallel", "parallel"), vmem_limit_bytes=VMEM_LIMIT),
        name="in_proj",
    )(x, g_mix.reshape(1, D), w_all, positions.reshape(B, S, 1), inv, wa, ba, tri)


def _compress_body(x_ref, p0_ref, p1_ref, w0_ref, w1_ref, b1_ref, w2_ref, b2_ref, cs_ref, sn_ref,
                   kc_ref, vc_ref, xk_ref, xv_ref):
    n = x_ref.shape[1] // CMP_STRIDE
    half_refs = (xk_ref, xv_ref)
    for s, half in enumerate(half_refs):
        half[...] = x_ref[0, :, s * LANES:(s + 1) * LANES]
    hid = w0_ref.shape[1] // len(half_refs)
    y0 = [None, None]
    y1 = [None, None]
    for t in range(CMP_STRIDE):
        for s, half in enumerate(half_refs):
            xt = half[pl.ds(t, n, stride=CMP_STRIDE), :]
            rows = slice((2 * t + s) * LANES, (2 * t + s + 1) * LANES)
            cols = slice(s * hid, (s + 1) * hid)
            lanes = slice(s * LANES, (s + 1) * LANES)
            d0 = _dot((xt + p0_ref[t:t + 1, lanes]).astype(BF16), w0_ref[rows, cols])
            d1 = _dot((xt + p1_ref[t:t + 1, lanes]).astype(BF16), w1_ref[rows, cols])
            y0[s] = d0 if y0[s] is None else y0[s] + d0
            y1[s] = d1 if y1[s] is None else y1[s] + d1
    y0 = jnp.concatenate(y0, axis=1)
    y1 = jnp.concatenate(y1, axis=1)
    h = jax.nn.gelu(y0 + pltpu.roll(y1, n - 1, 0) + b1_ref[...])
    o = _dot(h.astype(BF16), w2_ref[...]) + b2_ref[...]
    k = _rope_lanes(o[:, :LANES], cs_ref[0], sn_ref[0])
    v_t = o[:, LANES:].T
    for g in range(NSA_KV_GROUPS):
        sl = slice(g * NSA_DH, (g + 1) * NSA_DH)
        kc_ref[0, g] = k[:, sl].astype(BF16)
        vc_ref[0, g] = v_t[sl].astype(BF16)


def _compress(kvc, cs_c, sn_c, pos_k, w1_k, b1_k, w2_k, b2_k, pos_v, w1_v, b1_v, w2_v, b2_v):
    B, S, _ = kvc.shape
    n = S // CMP_STRIDE
    ns = 2 * NSA_KV_GROUPS
    eye = jnp.eye(ns, dtype=F32)
    w1s = jnp.stack([w1_k, w1_k, w1_v, w1_v])
    poss = jnp.stack([pos_k, pos_k, pos_v, pos_v])
    wbig, pbig = [], []
    for m in range(CMP_BLOCK // CMP_STRIDE):
        wm = w1s.reshape(ns, CMP_BLOCK, NSA_DH, CMP_HIDDEN)[:, m * CMP_STRIDE:(m + 1) * CMP_STRIDE]
        wb = jnp.einsum('ctdj,ce->tcdej', wm, eye).reshape(CMP_STRIDE * ns * NSA_DH, ns * CMP_HIDDEN)
        wbig.append(wb.astype(BF16))
        pm = poss[:, m * CMP_STRIDE:(m + 1) * CMP_STRIDE]
        pbig.append(pm.transpose(1, 0, 2).reshape(CMP_STRIDE, ns * NSA_DH))
    b1 = jnp.concatenate([b1_k, b1_k, b1_v, b1_v]).reshape(1, ns * CMP_HIDDEN)
    w2s = jnp.stack([w2_k, w2_k, w2_v, w2_v])
    w2big = jnp.einsum('cjd,ce->cjed', w2s, eye).reshape(ns * CMP_HIDDEN, ns * NSA_DH).astype(BF16)
    b2 = jnp.concatenate([b2_k, b2_k, b2_v, b2_v]).reshape(1, ns * NSA_DH)
    full = lambda a: pl.BlockSpec(a.shape, lambda b: (0,) * a.ndim)
    sds = jax.ShapeDtypeStruct
    return pl.pallas_call(
        _compress_body,
        grid=(B,),
        in_specs=[pl.BlockSpec((1, S, ns * NSA_DH), lambda b: (b, 0, 0)),
                  full(pbig[0]), full(pbig[1]), full(wbig[0]), full(wbig[1]), full(b1), full(w2big), full(b2),
                  pl.BlockSpec((1, n, LANES), lambda b: (b, 0, 0)),
                  pl.BlockSpec((1, n, LANES), lambda b: (b, 0, 0))],
        out_specs=(pl.BlockSpec((1, NSA_KV_GROUPS, n, NSA_DH), lambda b: (b, 0, 0, 0)),
                   pl.BlockSpec((1, NSA_KV_GROUPS, NSA_DH, n), lambda b: (b, 0, 0, 0))),
        out_shape=(sds((B, NSA_KV_GROUPS, n, NSA_DH), BF16), sds((B, NSA_KV_GROUPS, NSA_DH, n), BF16)),
        scratch_shapes=[pltpu.VMEM((S, LANES), F32), pltpu.VMEM((S, LANES), F32)],
        compiler_params=pltpu.CompilerParams(
            dimension_semantics=("parallel",), vmem_limit_bytes=VMEM_LIMIT),
        name="compress",
    )(kvc, pbig[0], pbig[1], wbig[0], wbig[1], b1, w2big, b2, cs_c, sn_c)


NSA_ONES = 16


def _with_ones(v_t):
    return jnp.concatenate([v_t, jnp.ones((NSA_ONES, v_t.shape[1]), v_t.dtype)], axis=0)


def _softmax_step(s, m, acc, v_t):
    m_new = jnp.maximum(m, jnp.max(s, axis=0, keepdims=True))
    p = jnp.exp2(s - m_new).astype(BF16)
    acc = jnp.exp2(m - m_new) * acc + _dot(_with_ones(v_t), p)
    return m_new, acc


def _nsa_groups_body(q_ref, kc_ref, vc_ref, ks_ref, vs_ref, kw_ref, vw_ref, ng_ref, ovt_ref,
                     o_ref, sa_ref, sb_ref, sw_ref, *, tq, tk, n_slc):
    qi = pl.program_id(1)
    q0 = qi * tq
    n_cmp = kc_ref.shape[2]
    W = NSA_HPG * tq
    G = range(NSA_KV_GROUPS)
    per_head = lambda x: jnp.concatenate([x] * NSA_HPG, axis=1)
    q_all = [jnp.concatenate([q_ref[0, g * NSA_HPG + h] for h in range(NSA_HPG)], axis=1) for g in G]
    t_lane = q0 + lax.broadcasted_iota(jnp.int32, (1, tq), 1)

    cmp_end = CMP_STRIDE * lax.broadcasted_iota(jnp.int32, (n_cmp, 1), 0) + (CMP_BLOCK - 1)
    ok = cmp_end <= t_lane
    cmp_bias = per_head(jnp.where(ok, 0.0, NEG_INF))
    cmp_keep = per_head(jnp.where(ok, 1.0, 0.0))
    sm = [_dot(kc_ref[0, g], q_all[g]) + cmp_bias for g in G]
    wk = WINDOW + tq
    w0 = pl.multiple_of(jnp.maximum(q0 - WINDOW, 0), tq)
    kpos = w0 + lax.broadcasted_iota(jnp.int32, (wk, 1), 0)
    win_bias = per_head(jnp.where(jnp.logical_and(kpos <= t_lane, kpos > t_lane - WINDOW), 0.0, NEG_INF))
    for g in G:
        sw_ref[g] = _dot(kw_ref[0, g, pl.ds(w0, wk), :], q_all[g]) + win_bias
    blk = lax.broadcasted_iota(jnp.int32, (n_slc, tq), 0)
    cur = (q0 + lax.broadcasted_iota(jnp.int32, (n_slc, tq), 1)) // SLC_BLOCK
    forced = (blk == 0) | (blk == cur) | (blk == cur - 1)
    o_cmp, imp = [], []
    for g in G:
        e = jnp.exp2(sm[g] - jnp.max(sm[g], axis=0, keepdims=True))
        p = e * (1.0 / jnp.sum(e, axis=0, keepdims=True)) * cmp_keep
        o_cmp.append(_dot(vc_ref[0, g], p.astype(BF16)))
        p_sum = p[:, 0:tq]
        for h in range(1, NSA_HPG):
            p_sum = p_sum + p[:, h * tq:(h + 1) * tq]
        p_hi = p_sum.astype(BF16)
        p_lo = (p_sum - p_hi.astype(F32)).astype(BF16)
        imp_g = _dot(ovt_ref[...], p_hi) + _dot(ovt_ref[...], p_lo)
        imp.append(jnp.where(blk <= cur, jnp.where(forced, FORCED_SCORE, imp_g), NEG_INF))

    SUBL = 8

    def block_bias(n):
        out = []
        for g in G:
            groups = [imp[g][r:r + SUBL] for r in range(0, n, SUBL)]
            ranks = [jnp.zeros((SUBL, tq), F32) for _ in groups]
            for i in range(n):
                row = imp[g][i:i + 1, :]
                for gidx, grp in enumerate(groups):
                    r = gidx * SUBL
                    ge = jnp.where(row >= grp, 1.0, 0.0)
                    gt = jnp.where(row > grp, 1.0, 0.0)
                    if r > i:
                        inc = ge
                    elif r + SUBL - 1 <= i:
                        inc = gt
                    else:
                        inc = jnp.where(blk[r:r + SUBL] > i, ge, gt)
                    ranks[gidx] = ranks[gidx] + inc
            bias = jnp.where(jnp.concatenate(ranks, axis=0) < float(min(SLC_TOPK, n_slc)), 0.0, NEG_INF)
            out.append(jnp.concatenate([bias, jnp.zeros((NSA_DH - n, tq), F32)], axis=0) if n < NSA_DH else bias)
        return tuple(out)

    quarter = n_slc // 4
    visible = (q0 + tq - 1) // SLC_BLOCK + 1
    bias = lax.cond(
        visible <= 2 * quarter,
        lambda: lax.cond(visible <= quarter, lambda: block_bias(quarter), lambda: block_bias(2 * quarter)),
        lambda: lax.cond(visible <= 3 * quarter, lambda: block_bias(3 * quarter), lambda: block_bias(n_slc)))
    q_aug = [jnp.concatenate([q_all[g], per_head(bias[g]).astype(BF16)], axis=0) for g in G]

    kpos_l = lax.broadcasted_iota(jnp.int32, (tk, 1), 0)
    n_tiles = q0 // tk + 1
    last_tile = ks_ref.shape[2] // tk - 1

    def tile_start(jt):
        return pl.multiple_of(jnp.minimum(jt, last_tile) * tk, tk)

    def scores_into(s_ref, jt):
        causal = per_head(jnp.where(jt * tk + kpos_l <= t_lane, 0.0, NEG_INF))
        for g in G:
            s_ref[g] = _dot(ks_ref[0, g, pl.ds(tile_start(jt), tk), :], q_aug[g]) + causal

    def consume(s_ref, jt, carry):
        return tuple(_softmax_step(s_ref[g], *carry[g], vs_ref[0, g, :, pl.ds(tile_start(jt), tk)]) for g in G)

    def slc_pair(i, carry):
        scores_into(sb_ref, 2 * i + 1)
        carry = consume(sa_ref, 2 * i, carry)
        scores_into(sa_ref, 2 * i + 2)
        return consume(sb_ref, 2 * i + 1, carry)

    init = tuple((jnp.full((1, W), NEG_INF, F32), jnp.zeros((NSA_DH + NSA_ONES, W), F32)) for g in G)
    scores_into(sa_ref, 0)

    acc_w = []
    for g in G:
        sw = sw_ref[g]
        ew = jnp.exp2(sw - jnp.max(sw, axis=0, keepdims=True)).astype(BF16)
        acc_w.append(_dot(_with_ones(vw_ref[0, g, :, pl.ds(w0, wk)]), ew))

    slc = lax.fori_loop(0, (n_tiles + 1) // 2, slc_pair, init)

    for g in G:
        def gate(j):
            return jnp.concatenate([ng_ref[0, (g * NSA_HPG + h) * 3 + j:(g * NSA_HPG + h) * 3 + j + 1, :]
                                    for h in range(NSA_HPG)], axis=1)

        acc_s = slc[g][1]
        l_s, l_w = acc_s[NSA_DH:NSA_DH + 1], acc_w[g][NSA_DH:NSA_DH + 1]
        out_t = (gate(0) * o_cmp[g] + (gate(1) * (1.0 / l_s)) * acc_s[:NSA_DH]
                 + (gate(2) * (1.0 / l_w)) * acc_w[g][:NSA_DH])
        out_t = jnp.concatenate([out_t[:, h * tq:(h + 1) * tq] for h in range(NSA_HPG)], axis=0)
        o_ref[0, :, g * NSA_HPG * NSA_DH:(g + 1) * NSA_HPG * NSA_DH] = out_t.T.astype(o_ref.dtype)


def _nsa_attention(q, kc, vc, ks, vs, kw, vw, ng):
    B, H, dh, S = q.shape
    tq = min(ATT_TQ, S)
    tk = min(ATT_TK, S)
    assert S % tk == 0 and tk % tq == 0 and S >= WINDOW + tq and S // SLC_BLOCK <= dh
    n_cmp = kc.shape[2]
    n_slc = S // SLC_BLOCK
    c0 = CMP_STRIDE * np.arange(n_cmp)[None, :]
    s0 = SLC_BLOCK * np.arange(n_slc)[:, None]
    ov = np.clip(np.minimum(c0 + CMP_BLOCK, s0 + SLC_BLOCK) - np.maximum(c0, s0), 0, None) / CMP_BLOCK
    ovt = jnp.asarray(ov, BF16)
    G = NSA_KV_GROUPS
    grid = (B, S // tq)
    k_spec = lambda n, w: pl.BlockSpec((1, G, n, w), lambda b, i: (b, 0, 0, 0))
    vt_spec = lambda n: pl.BlockSpec((1, G, dh, n), lambda b, i: (b, 0, 0, 0))
    return pl.pallas_call(
        functools.partial(_nsa_groups_body, tq=tq, tk=tk, n_slc=n_slc),
        grid=grid,
        in_specs=[
            pl.BlockSpec((1, H, dh, tq), lambda b, i: (b, 0, 0, i)),
            k_spec(n_cmp, dh), vt_spec(n_cmp),
            k_spec(S, 2 * dh), vt_spec(S),
            k_spec(S, dh), vt_spec(S),
            pl.BlockSpec((1, LANES, tq), lambda b, i: (b, 0, i)),
            pl.BlockSpec(ovt.shape, lambda b, i: (0, 0)),
        ],
        out_specs=pl.BlockSpec((1, tq, H * dh), lambda b, i: (b, i, 0)),
        out_shape=jax.ShapeDtypeStruct((B, S, H * dh), BF16),
        scratch_shapes=[pltpu.VMEM((G, tk, NSA_HPG * tq), F32),
                        pltpu.VMEM((G, tk, NSA_HPG * tq), F32),
                        pltpu.VMEM((G, WINDOW + tq, NSA_HPG * tq), F32)],
        compiler_params=pltpu.CompilerParams(
            dimension_semantics=("parallel", "arbitrary"), vmem_limit_bytes=VMEM_LIMIT),
        name="nsa_attn",
    )(q, kc, vc, ks, vs, kw, vw, ng, ovt)


def _dot_3pass(a, b):
    a1 = a.astype(BF16)
    a2 = (a - a1.astype(F32)).astype(BF16)
    b1 = b.astype(BF16)
    b2 = (b - b1.astype(F32)).astype(BF16)
    return _dot(a1, b1) + (_dot(a1, b2) + _dot(a2, b1))


def _gla_pair_body(q_ref, k_ref, v_ref, vt_ref, b_ref, r_ref, ng_ref,
                   o_ref, state_ref, kp_ref, bp_ref, vp_ref, plain_ref, *, n_chunks):
    C, SUB, dk, dv, hp = GLA_C, GLA_SUB, GLA_DK, GLA_DV, GLA_HP
    W = hp * dk
    lane = lax.broadcasted_iota(jnp.int32, (1, W), 1)
    of_head = [lane // dk == hh for hh in range(hp)]
    t_loc = lax.broadcasted_iota(jnp.int32, (C, 1), 0)
    state_ref[...] = jnp.zeros_like(state_ref)
    kp_ref[0:SUB, :] = jnp.zeros((SUB, W), F32)
    bp_ref[0:SUB, :] = jnp.zeros((SUB, W), F32)
    vp_ref[:, 0:SUB, :] = jnp.zeros((hp, SUB, dv), F32)

    def decay_chunk(c, all_plain):
        b_last = b_ref[0, pl.ds(c * C + (C - 1), 1), :]
        plain = (jnp.min(b_last) > -GLA_PLAIN_DECAY).astype(jnp.int32)
        plain_ref[c] = plain
        return jnp.minimum(all_plain, plain)

    all_plain = lax.fori_loop(0, n_chunks, decay_chunk, jnp.int32(1), unroll=GLA_UNROLL_DECAY)

    def chunk(c, check_decay):
        c0 = pl.multiple_of(c * C, C)
        b = b_ref[0, pl.ds(c0, C), :]
        q = q_ref[0, pl.ds(c0, C), :]
        k = k_ref[0, pl.ds(c0, C), :]
        v = [v_ref[0, pl.ds(c0, C), hh * dv:(hh + 1) * dv] for hh in range(hp)]
        vt = [vt_ref[0, hh * dv:(hh + 1) * dv, pl.ds(c0, C)] for hh in range(hp)]
        b_last = b[C - 1:C, :]
        st = state_ref[...]
        st_b = st.astype(BF16)
        qg = q * jnp.exp(b)
        qg_h = [jnp.where(of_head[hh], qg, 0.0).astype(BF16) for hh in range(hp)]
        o_inter = [_dot_nt(qg_h[hh], st_b) for hh in range(hp)]

        def intra_plain():
            ke = (k * jnp.exp(-b)).astype(BF16)
            row = lax.broadcasted_iota(jnp.int32, (C, C), 0)
            col = lax.broadcasted_iota(jnp.int32, (C, C), 1)
            return tuple(_dot(jnp.where(row >= col, _dot_nt(qg_h[hh], ke), 0.0).astype(BF16), v[hh])
                         for hh in range(hp))

        def intra_strong_decay():
            far = [[jnp.zeros((SUB, dv), F32)] for _ in range(hp)]
            for i in range(1, C // SUB):
                r0 = i * SUB
                b_first = b[r0:r0 + 1, :]
                qt = q[r0:r0 + SUB] * jnp.exp(b[r0:r0 + SUB] - b_first)
                kt = (k[:r0] * jnp.exp(b_first - b[:r0])).astype(BF16)
                lt = lax.broadcasted_iota(jnp.int32, (SUB, r0), 0)
                ls = lax.broadcasted_iota(jnp.int32, (SUB, r0), 1)
                for hh in range(hp):
                    a = _dot_nt(jnp.where(of_head[hh], qt, 0.0).astype(BF16), kt)
                    a = jnp.where(lt + (r0 - SUB) >= ls, a, 0.0)
                    far[hh].append(_dot(a.astype(BF16), v[hh][:r0]))
            acc = [jnp.concatenate(far[hh], axis=0) for hh in range(hp)]
            kp_ref[SUB:SUB + C, :] = k
            bp_ref[SUB:SUB + C, :] = b
            for hh in range(hp):
                vp_ref[hh, SUB:SUB + C, :] = v[hh].astype(F32)
            for d in range(SUB):
                kd = kp_ref[SUB - d:SUB - d + C, :]
                bd = bp_ref[SUB - d:SUB - d + C, :]
                valid = t_loc >= d
                w = jnp.exp(jnp.where(valid, b - bd, 0.0))
                x = jnp.where(valid, q * kd * w, 0.0)
                for hh in range(hp):
                    a_d = jnp.sum(jnp.where(of_head[hh], x, 0.0), axis=-1, keepdims=True)
                    acc[hh] = acc[hh] + a_d * vp_ref[hh, SUB - d:SUB - d + C, :]
            return tuple(acc)

        if check_decay:
            o_intra = lax.cond(plain_ref[c] > 0, intra_plain, intra_strong_decay)
        else:
            o_intra = intra_plain()
        k_dec = (k * jnp.exp(b_last - b)).astype(BF16)
        upd = _dot(vt[hp - 1], k_dec)
        for hh in range(hp - 2, -1, -1):
            upd = jnp.where(of_head[hh], _dot(vt[hh], k_dec), upd)
        state_ref[...] = st * jnp.exp(b_last) + upd
        for hh in range(hp):
            o = o_inter[hh] + o_intra[hh]
            o = o * lax.rsqrt(jnp.mean(o * o, axis=-1, keepdims=True) + EPS) * ng_ref[hh]
            gate = jax.nn.silu(r_ref[0, pl.ds(c0, C), hh * dv:(hh + 1) * dv])
            o_ref[0, pl.ds(c0, C), hh * dv:(hh + 1) * dv] = (o * gate).astype(o_ref.dtype)
        return 0

    @pl.when(all_plain > 0)
    def _():
        lax.fori_loop(0, n_chunks, lambda c, _: chunk(c, False), 0, unroll=GLA_UNROLL_PLAIN)

    @pl.when(all_plain <= 0)
    def _():
        lax.fori_loop(0, n_chunks, lambda c, _: chunk(c, True), 0)


def _gla_pairs(gq, gk, gv, gvt, gb, gr, norm_g):
    B, S, _ = gq.shape
    H, dk, dv, hp = GLA_HEADS, GLA_DK, GLA_DV, GLA_HP
    W = hp * dk
    assert W == LANES and H % hp == 0
    ng = norm_g.reshape(H, 1, dv)
    n_chunks = S // GLA_C
    tok_spec = lambda w: pl.BlockSpec((1, S, hp * w), lambda b, h: (b, 0, h))
    return pl.pallas_call(
        functools.partial(_gla_pair_body, n_chunks=n_chunks),
        grid=(B, H // hp),
        in_specs=[tok_spec(dk), tok_spec(dk), tok_spec(dv),
                  pl.BlockSpec((1, hp * dv, S), lambda b, h: (b, h, 0)),
                  tok_spec(dk),
                  tok_spec(dv),
                  pl.BlockSpec((hp, 1, dv), lambda b, h: (h, 0, 0))],
        out_specs=tok_spec(dv),
        out_shape=jax.ShapeDtypeStruct((B, S, H * dv), BF16),
        scratch_shapes=[pltpu.VMEM((dv, W), F32),
                        pltpu.VMEM((GLA_SUB + GLA_C, W), F32),
                        pltpu.VMEM((GLA_SUB + GLA_C, W), F32),
                        pltpu.VMEM((hp, GLA_SUB + GLA_C, dv), F32),
                        pltpu.SMEM((n_chunks,), jnp.int32)],
        compiler_params=pltpu.CompilerParams(
            dimension_semantics=("parallel", "parallel"), vmem_limit_bytes=VMEM_LIMIT),
        name="gla",
    )(gq, gk, gv, gvt, gb, gr, ng)


def _merge_body(x_ref, ya_ref, yb_ref, ma_ref, mb_ref, wpa_ref, wpb_ref, wo_ref, gf_ref, wr_ref, br_ref,
                h_ref, v_ref, comb_ref):
    rows = x_ref.shape[0] // MERGE_PARTS
    part = lambda p: slice(p * rows, (p + 1) * rows)
    wr = wr_ref[...]
    w_hi = wr.astype(BF16)
    w_both = jnp.concatenate([w_hi, (wr - w_hi.astype(F32)).astype(BF16)], axis=1)

    def mix(p):
        sl = part(p)
        y_a = _dot(ya_ref[sl, :], wpa_ref[...])
        y_b = _dot(yb_ref[sl, :], wpb_ref[...])
        mixed = ma_ref[sl, :] * y_a + mb_ref[sl, :] * y_b
        h = x_ref[sl, :] + _dot(mixed.astype(BF16), wo_ref[...])
        h_ref[sl, :] = h
        return h

    def route(p, h):
        sl = part(p)
        v = h * lax.rsqrt(jnp.mean(h * h, axis=-1, keepdims=True) + EPS) * gf_ref[...]
        v_ref[sl, :] = v.astype(BF16)
        v_hi = v.astype(BF16)
        v_lo = (v - v_hi.astype(F32)).astype(BF16)
        both = _dot(v_hi, w_both)
        logits = (both[:, :LANES] + both[:, LANES:]) + _dot(v_lo, w_hi) + br_ref[...]
        comb_ref[sl, :] = _route_weights(logits)

    h_prev = mix(0)
    for p in range(1, MERGE_PARTS):
        h_next = mix(p)
        route(p - 1, h_prev)
        h_prev = h_next
    route(MERGE_PARTS - 1, h_prev)


def _route_weights(logits):
    lane = lax.broadcasted_iota(jnp.int32, logits.shape, 1)
    is_grp = jnp.logical_and(lane >= MOE_EXPERTS, lane < MOE_EXPERTS + MOE_GROUPS)
    lg = jnp.where(is_grp, logits, NEG_INF)
    eg = jnp.where(is_grp, jnp.exp(lg - jnp.max(lg, axis=-1, keepdims=True)), 0.0)
    pg = eg / jnp.sum(eg, axis=-1, keepdims=True)
    p_grp = jnp.max(pg, axis=-1, keepdims=True)
    g_sel = jnp.min(jnp.where(jnp.logical_and(is_grp, pg == p_grp), lane, 2 * LANES),
                    axis=-1, keepdims=True) - MOE_EXPERTS
    in_grp = jnp.logical_and(lane < MOE_EXPERTS, lane // MOE_EPG == g_sel)
    le = jnp.where(in_grp, logits, NEG_INF)
    ee = jnp.where(in_grp, jnp.exp(le - jnp.max(le, axis=-1, keepdims=True)), 0.0)
    pin = ee / jnp.sum(ee, axis=-1, keepdims=True)
    p1 = jnp.max(jnp.where(in_grp, pin, -1.0), axis=-1, keepdims=True)
    i1 = jnp.min(jnp.where(jnp.logical_and(in_grp, pin == p1), lane, 2 * LANES), axis=-1, keepdims=True)
    rest = jnp.logical_and(in_grp, lane != i1)
    p2 = jnp.max(jnp.where(rest, pin, -1.0), axis=-1, keepdims=True)
    i2 = jnp.min(jnp.where(jnp.logical_and(rest, pin == p2), lane, 2 * LANES), axis=-1, keepdims=True)
    tot = p1 + p2
    return jnp.where(lane == i1, p_grp * p1 / tot, 0.0) + jnp.where(lane == i2, p_grp * p2 / tot, 0.0)


def _merge(x2, ya, yb, ma, mb, w_proj_nsa, w_proj_gla, w_out, g_ffn, w_grp, b_grp, w_exp, b_exp):
    T, D = x2.shape
    tm = min(MERGE_TM, T)
    wr = jnp.pad(jnp.concatenate([w_exp, w_grp], axis=1), ((0, 0), (0, LANES - MOE_EXPERTS - MOE_GROUPS)))
    br = jnp.pad(jnp.concatenate([b_exp, b_grp]), (0, LANES - MOE_EXPERTS - MOE_GROUPS)).reshape(1, LANES)
    tok = lambda w: pl.BlockSpec((tm, w), lambda i: (i, 0))
    full = lambda a: pl.BlockSpec(a.shape, lambda i: (0, 0))
    wpa, wpb, wo = w_proj_nsa.astype(BF16), w_proj_gla.astype(BF16), w_out.astype(BF16)
    gf = g_ffn.reshape(1, D)
    sds = jax.ShapeDtypeStruct
    return pl.pallas_call(
        _merge_body,
        grid=(T // tm,),
        in_specs=[tok(D), tok(ya.shape[1]), tok(yb.shape[1]), tok(D), tok(D),
                  full(wpa), full(wpb), full(wo), full(gf), full(wr), full(br)],
        out_specs=(tok(D), tok(D), tok(LANES)),
        out_shape=(sds((T, D), F32), sds((T, D), BF16), sds((T, LANES), F32)),
        compiler_params=pltpu.CompilerParams(
            dimension_semantics=("parallel",), vmem_limit_bytes=VMEM_LIMIT),
        name="merge",
    )(x2, ya, yb, ma, mb, wpa, wpb, wo, gf, wr, br)


def _moe_rows(tw):
    rows = 2 * tw + MOE_EXPERTS * (MOE_ALIGN - 1) + MOE_RT
    return -(-rows // MOE_RT) * MOE_RT


def _moe_window(e, phase, v_ref, comb_ref, tri_ref, y_ref, xs_ref, z_ref, meta_ref, pos_ref, tw, rmax):
    no_row = -1.0
    n_rt = rmax // MOE_RT
    tail_start = (n_rt - 1) * MOE_RT

    @pl.when(e == 0 if phase == "route" else False)
    def _route():
        comb = comb_ref[...]
        assigned = comb > 0.0
        a = jnp.where(assigned, 1.0, 0.0)
        tri = tri_ref[...]
        run = jnp.zeros((1, LANES), F32)
        ranks = []
        for b in range(tw // MOE_RT):
            ab = a[b * MOE_RT:(b + 1) * MOE_RT]
            ranks.append(_dot(tri, ab.astype(BF16)) + run)
            run = run + jnp.sum(ab, axis=0, keepdims=True)
        rank = jnp.concatenate(ranks, axis=0)
        cnt_pad = jnp.floor((run + (MOE_ALIGN - 1)) * (1.0 / MOE_ALIGN)) * MOE_ALIGN
        incl = jnp.broadcast_to(cnt_pad, (8, LANES))
        lane8 = lax.broadcasted_iota(jnp.int32, (8, LANES), 1)
        shift = 1
        while shift < MOE_EXPERTS:
            incl = incl + jnp.where(lane8 >= shift, pltpu.roll(incl, shift, 1), 0.0)
            shift *= 2
        offs = incl[0:1] - cnt_pad
        meta_ref[0:1, :] = offs
        meta_ref[1:2, :] = run
        used = jnp.max(incl[0:1], axis=-1, keepdims=True) + MOE_ET
        meta_ref[2:3, :] = jnp.broadcast_to(used, (1, LANES))
        row_of = offs + rank
        pos_a = jnp.min(jnp.where(assigned, row_of, 1e9), axis=-1, keepdims=True)
        pos_b = jnp.max(jnp.where(assigned, row_of, no_row), axis=-1, keepdims=True)
        pos_a = jnp.where(pos_a > 1e8, no_row, pos_a)
        pos_b = jnp.where(pos_b == pos_a, no_row, pos_b)
        w_a = jnp.sum(jnp.where(jnp.logical_and(assigned, row_of == pos_a), comb, 0.0), axis=-1, keepdims=True)
        w_b = jnp.sum(jnp.where(jnp.logical_and(assigned, row_of == pos_b), comb, 0.0), axis=-1, keepdims=True)
        lane_t = lax.broadcasted_iota(jnp.int32, (tw, LANES), 1)
        pos_ref[...] = jnp.where(lane_t == 0, pos_a, jnp.where(lane_t == 1, pos_b, jnp.where(
            lane_t == 2, w_a, jnp.where(lane_t == 3, w_b, no_row))))
        pos_t = pos_ref[...].T
        pa, pb = pos_t[0:1], pos_t[1:2]
        v = v_ref[...]

        def sort_tile(rt):
            r = (rt * MOE_RT + lax.broadcasted_iota(jnp.int32, (MOE_RT, 1), 0)).astype(F32)
            p = jnp.where(r == pa, 1.0, jnp.where(r == pb, 1.0, 0.0)).astype(BF16)
            xs_ref[rt * MOE_RT:(rt + 1) * MOE_RT, :] = _dot(p, v).astype(BF16)

        for rt in range(n_rt - 1):
            sort_tile(rt)
        pl.when(jnp.max(used) > float(tail_start))(lambda: sort_tile(n_rt - 1))
        z_ref[...] = jnp.zeros_like(z_ref)

    @pl.when(e == pl.num_programs(1) - 1 if phase == "combine" else False)
    def _combine():
        def unsort(n_rows):
            r = lax.broadcasted_iota(jnp.int32, (1, n_rows), 1).astype(F32)
            z = z_ref[0:n_rows, :]
            for tt in range(tw // MOE_RT):
                rows = slice(tt * MOE_RT, (tt + 1) * MOE_RT)
                pa, pb = pos_ref[rows, 0:1], pos_ref[rows, 1:2]
                w_a, w_b = pos_ref[rows, 2:3], pos_ref[rows, 3:4]
                q = jnp.where(r == pa, w_a, jnp.where(r == pb, w_b, 0.0)).astype(BF16)
                y_ref[rows, :] = _dot(q, z).astype(y_ref.dtype)

        long_tail = jnp.max(meta_ref[2:3, :]) > float(tail_start)
        pl.when(long_tail)(lambda: unsort(rmax))
        pl.when(jnp.logical_not(long_tail))(lambda: unsort(tail_start))


def _moe_body(v_ref, comb_ref, wg_ref, wu_ref, wd_ref, tri_ref, y_ref,
              xs_ref, z_ref, meta_ref, pos_ref, *, tw, rmax, n_win):
    e = pl.program_id(1)
    lane = lax.broadcasted_iota(jnp.int32, (1, LANES), 1)

    def window_phase(phase):
        for w in range(n_win):
            rows = pl.ds(w * tw, tw)
            _moe_window(e, phase, v_ref.at[rows], comb_ref.at[rows], tri_ref, y_ref.at[rows],
                        xs_ref.at[w], z_ref.at[w], meta_ref.at[w], pos_ref.at[w], tw, rmax)

    window_phase("route")

    pick = lambda w, row: jnp.sum(jnp.where(lane == e, meta_ref[w, row:row + 1, :], 0.0)).astype(jnp.int32)
    offs = [pick(w, 0) for w in range(n_win)]
    tiles = [(pick(w, 1) + MOE_ET - 1) // MOE_ET for w in range(n_win)]

    def expert(x):
        hdn = jax.nn.silu(_dot(x, wg_ref[0])) * _dot(x, wu_ref[0])
        return _dot(hdn.astype(BF16), wd_ref[0]).astype(BF16)

    def joint_tile(i, _):
        r0 = [pl.multiple_of(offs[w] + i * MOE_ET, MOE_ALIGN) for w in range(n_win)]
        z = expert(jnp.concatenate([xs_ref[w, pl.ds(r0[w], MOE_ET), :] for w in range(n_win)], axis=0))
        for w in range(n_win):
            z_ref[w, pl.ds(r0[w], MOE_ET), :] = z[w * MOE_ET:(w + 1) * MOE_ET]
        return 0

    n_joint = functools.reduce(jnp.minimum, tiles)
    lax.fori_loop(0, n_joint, joint_tile, 0)
    for w in range(n_win):
        def own_tile(i, _, w=w):
            r0 = pl.multiple_of(offs[w] + i * MOE_ET, MOE_ALIGN)
            z_ref[w, pl.ds(r0, MOE_ET), :] = expert(xs_ref[w, pl.ds(r0, MOE_ET), :])
            return 0

        lax.fori_loop(n_joint, tiles[w], own_tile, 0)

    window_phase("combine")


def _moe(v, comb, w_gate, w_up, w_down):
    T, D = v.shape
    tw = min(MOE_TW, T)
    n_win = min(MOE_WINDOWS, T // tw)
    rmax = _moe_rows(tw)
    E, _, F = w_gate.shape
    wg, wu, wd = w_gate.astype(BF16), w_up.astype(BF16), w_down.astype(BF16)
    tri = jnp.asarray(np.tril(np.ones((MOE_RT, MOE_RT)), -1), BF16)
    tok = lambda w: pl.BlockSpec((n_win * tw, w), lambda i, e: (i, 0))
    tok_in = lambda w: pl.BlockSpec((n_win * tw, w), lambda i, e: (i, 0), pipeline_mode=pl.Buffered(1))
    return pl.pallas_call(
        functools.partial(_moe_body, tw=tw, rmax=rmax, n_win=n_win),
        grid=(T // (n_win * tw), E),
        in_specs=[tok_in(D), tok_in(LANES),
                  pl.BlockSpec((1, D, F), lambda i, e: (e, 0, 0)),
                  pl.BlockSpec((1, D, F), lambda i, e: (e, 0, 0)),
                  pl.BlockSpec((1, F, D), lambda i, e: (e, 0, 0)),
                  pl.BlockSpec(tri.shape, lambda i, e: (0, 0))],
        out_specs=tok(D),
        out_shape=jax.ShapeDtypeStruct((T, D), BF16),
        scratch_shapes=[pltpu.VMEM((n_win, rmax, D), BF16),
                        pltpu.VMEM((n_win, rmax, D), BF16),
                        pltpu.VMEM((n_win, 8, LANES), F32),
                        pltpu.VMEM((n_win, tw, LANES), F32)],
        compiler_params=pltpu.CompilerParams(
            dimension_semantics=("parallel", "arbitrary"), vmem_limit_bytes=VMEM_LIMIT),
        name="moe",
    )(v, comb, wg, wu, wd, tri)


def _final_body(h_ref, y_ref, g_ref, o_ref):
    h = h_ref[...] + y_ref[...].astype(F32)
    o_ref[...] = h * lax.rsqrt(jnp.mean(h * h, axis=-1, keepdims=True) + EPS) * g_ref[...]


def _final_norm(h, y, g_final):
    T, D = h.shape
    tm = min(FINAL_TM, T)
    tok = pl.BlockSpec((tm, D), lambda i: (i, 0))
    return pl.pallas_call(
        _final_body,
        grid=(T // tm,),
        in_specs=[tok, tok, pl.BlockSpec((1, D), lambda i: (0, 0))],
        out_specs=tok,
        out_shape=jax.ShapeDtypeStruct((T, D), F32),
        compiler_params=pltpu.CompilerParams(
            dimension_semantics=("parallel",), vmem_limit_bytes=VMEM_LIMIT),
        name="final_norm",
    )(h, y, g_final.reshape(1, D))


def _rope_inv_freq():
    half = NSA_DH // 2
    return 1.0 / (ROPE_THETA ** (jnp.arange(half, dtype=F32) / half))


def _rope_tables(positions):
    ang = positions.astype(F32)[..., None] * _rope_inv_freq()
    cos, sin = jnp.cos(ang), jnp.sin(ang)
    cs = jnp.concatenate([cos, cos, cos, cos], axis=-1)
    sn = jnp.concatenate([-sin, sin, -sin, sin], axis=-1)
    return cs, sn


def _layer(h, positions, g_mix, w_in, cmp_pos_k, cmp_w1_k, cmp_b1_k, cmp_w2_k, cmp_b2_k,
           cmp_pos_v, cmp_w1_v, cmp_b1_v, cmp_w2_v, cmp_b2_v, gla_w_a2, gla_b_a, gla_norm_g,
           w_proj_nsa, w_proj_gla, w_out, g_ffn, w_grp, b_grp, w_exp, b_exp, w_gate, w_up, w_down, g_out):
    B, S, D = h.shape
    n_chunks = S // CMP_STRIDE
    cmp_end = jnp.minimum(CMP_STRIDE * jnp.arange(n_chunks) + CMP_BLOCK - 1, S - 1)
    cs_c, sn_c = _rope_tables(jnp.take(positions, cmp_end, axis=1))
    (q, kvc, ks, vs, kw, vw, gq, gk, gv, gvt, gr, ma, mb, ng, gb) = _in_proj(h, g_mix, w_in, positions,
                                                                             gla_w_a2, gla_b_a)
    kc, vc = _compress(kvc, cs_c, sn_c, cmp_pos_k, cmp_w1_k, cmp_b1_k, cmp_w2_k, cmp_b2_k,
                       cmp_pos_v, cmp_w1_v, cmp_b1_v, cmp_w2_v, cmp_b2_v)
    ya = _nsa_attention(q, kc, vc, ks, vs, kw, vw, ng)
    yb = _gla_pairs(gq, gk, gv, gvt, gb, gr, gla_norm_g)
    T = B * S
    h1, v, comb = _merge(h.reshape(T, D), ya.reshape(T, -1), yb.reshape(T, -1), ma.reshape(T, D), mb.reshape(T, D),
                         w_proj_nsa, w_proj_gla, w_out, g_ffn, w_grp, b_grp, w_exp, b_exp)
    y = _moe(v, comb, w_gate, w_up, w_down)
    return _final_norm(h1, y, g_out).reshape(B, S, D)


def kernel(x, positions, g_mix, w_in, cmp_pos_k, cmp_w1_k, cmp_b1_k, cmp_w2_k, cmp_b2_k, cmp_pos_v, cmp_w1_v,
           cmp_b1_v, cmp_w2_v, cmp_b2_v, gla_w_a2, gla_b_a, gla_norm_g, w_proj_nsa, w_proj_gla, w_out, g_ffn,
           w_grp, b_grp, w_exp, b_exp, w_gate, w_up, w_down, g_final):
    depth = g_mix.shape[0]
    assert depth == 1, "the final norm closes the single layer"
    return _layer(x, positions, g_mix[0], w_in[0], cmp_pos_k[0], cmp_w1_k[0], cmp_b1_k[0], cmp_w2_k[0],
                  cmp_b2_k[0], cmp_pos_v[0], cmp_w1_v[0], cmp_b1_v[0], cmp_w2_v[0], cmp_b2_v[0],
                  gla_w_a2[0], gla_b_a[0], gla_norm_g[0], w_proj_nsa[0], w_proj_gla[0], w_out[0],
                  g_ffn[0], w_grp[0], b_grp[0], w_exp[0], b_exp[0], w_gate[0], w_up[0], w_down[0], g_final)
```

```python
import functools

import numpy as np
import jax
import jax.numpy as jnp
from jax import lax
from jax.experimental import pallas as pl
from jax.experimental.pallas import tpu as pltpu

F32 = jnp.float32
BF16 = jnp.bfloat16

NSA_HEADS = 8
NSA_KV_GROUPS = 2
NSA_HPG = NSA_HEADS // NSA_KV_GROUPS
NSA_DH = 64
CMP_BLOCK = 32
CMP_STRIDE = 16
CMP_HIDDEN = 256
SLC_BLOCK = 64
SLC_TOPK = 16
WINDOW = 512
GLA_HEADS = 4
GLA_DK = 64
GLA_DV = 128
GLA_RANK = 16
GLA_TAU = 16.0
MOE_GROUPS = 4
MOE_EPG = 8
MOE_EXPERTS = MOE_GROUPS * MOE_EPG
MOE_DFF = 512
ROPE_THETA = 10000.0
EPS = 1e-6
NEG_INF = -1e30
FORCED_SCORE = 1e4
LOG2E = 1.4426950408889634

LANES = 128
VMEM_LIMIT = 56 * 1024 * 1024

IN_TM = 512
ATT_TQ = 128
ATT_TK = 256
GLA_C = 128
GLA_SUB = 16
GLA_HP = 2
GLA_UNROLL_DECAY = 4
GLA_UNROLL_PLAIN = 4
GLA_PLAIN_DECAY = 60.0
MERGE_TM = 512
MERGE_PARTS = 1
MOE_TW = 1024
MOE_WINDOWS = 2
FINAL_TM = 1024
MOE_RT = 256
MOE_ET = 128
MOE_ALIGN = 16


def _dot(a, b):
    return jnp.dot(a, b, preferred_element_type=F32)


def _dot_nt(a, b):
    return lax.dot_general(a, b, (((1,), (1,)), ((), ())), preferred_element_type=F32)


def _split3(x):
    x1 = x.astype(BF16)
    r1 = x - x1.astype(F32)
    x2 = r1.astype(BF16)
    r2 = r1 - x2.astype(F32)
    x3 = r2.astype(BF16)
    return x1, x2, x3


def _dot_exact_lhs(a_bf16, x):
    x1, x2, x3 = _split3(x)
    return _dot(a_bf16, x1) + _dot(a_bf16, x2) + _dot(a_bf16, x3)


def _dot_f32(a, b):
    a1, a2, a3 = _split3(a)
    b1, b2, b3 = _split3(b)
    return (_dot(a1, b1) + (_dot(a1, b2) + _dot(a2, b1))
            + (_dot(a1, b3) + _dot(a2, b2) + _dot(a3, b1)))


def _rope_lanes(z, cs, sn):
    w = z.shape[-1]
    lane = lax.broadcasted_iota(jnp.int32, z.shape, 1)
    first_half = (lane % NSA_DH) < (NSA_DH // 2)
    rot = jnp.where(first_half, pltpu.roll(z, w - NSA_DH // 2, 1), pltpu.roll(z, NSA_DH // 2, 1))
    reps = w // LANES
    if reps > 1:
        cs = jnp.concatenate([cs] * reps, axis=1)
        sn = jnp.concatenate([sn] * reps, axis=1)
    return z * cs + rot * sn


_SEC = {}
_off = 0
for _name, _w in (("q", 512), ("kvc", 256), ("ks", 128), ("vs", 128), ("kw", 128), ("vw", 128),
                  ("gq", 256), ("gk", 256), ("gv", 512), ("gr", 512), ("ma", 1024), ("mb", 1024),
                  ("nga", 128)):
    _SEC[_name] = (_off, _off + _w)
    _off += _w
IN_NW = _off


def _in_proj_body(x_ref, g_ref, w_ref, pos_ref, inv_ref, wa_ref, ba_ref, tri_ref,
                  q_ref, kvc_ref, ks_ref, vs_ref, kw_ref, vw_ref,
                  gq_ref, gk_ref, gv_ref, gvt_ref, gr_ref, ma_ref, mb_ref, ng_ref, gb_ref):
    x = x_ref[0]
    var = jnp.mean(x * x, axis=-1, keepdims=True)
    u = (x * lax.rsqrt(var + EPS) * g_ref[...]).astype(BF16)
    ang = pos_ref[0].astype(F32) * inv_ref[...]
    lane_r = lax.broadcasted_iota(jnp.int32, ang.shape, 1)
    cs = jnp.cos(ang)
    sn = jnp.where((lane_r % NSA_DH) < (NSA_DH // 2), -1.0, 1.0) * jnp.sin(ang)

    def proj(name):
        a, b = _SEC[name]
        return _dot(u, w_ref[:, a:b])

    zq_t = (_rope_lanes(proj("q"), cs, sn) * (NSA_DH ** -0.5 * LOG2E)).T
    for h in range(NSA_HEADS):
        q_ref[0, h] = zq_t[h * NSA_DH:(h + 1) * NSA_DH].astype(BF16)
    kvc_ref[0] = proj("kvc")
    zks = _rope_lanes(proj("ks"), cs, sn)
    zkw = _rope_lanes(proj("kw"), cs, sn)
    zvs_t = proj("vs").T
    zvw_t = proj("vw").T
    tm = zks.shape[0]
    lane = lax.broadcasted_iota(jnp.int32, (tm, LANES), 1)
    blk = (pl.program_id(1) * tm + lax.broadcasted_iota(jnp.int32, (tm, LANES), 0)) // SLC_BLOCK
    onehot = jnp.where(lane - NSA_DH == blk, 1.0, 0.0)
    for g in range(NSA_KV_GROUPS):
        sl = slice(g * NSA_DH, (g + 1) * NSA_DH)
        k_front = zks if g == 0 else pltpu.roll(zks, LANES - g * NSA_DH, 1)
        ks_ref[0, g] = jnp.where(lane < NSA_DH, k_front, onehot).astype(BF16)
        kw_ref[0, g] = zkw[:, sl].astype(BF16)
        vs_ref[0, g] = zvs_t[sl].astype(BF16)
        vw_ref[0, g] = zvw_t[sl].astype(BF16)
    gq_ref[0] = proj("gq") * (GLA_DK ** -0.5)
    gk_ref[0] = proj("gk")
    zgv = proj("gv")
    gv_ref[0] = zgv.astype(BF16)
    gvt_ref[0] = zgv.T.astype(BF16)
    gr_ref[0] = proj("gr")
    ma_ref[0] = jax.nn.sigmoid(proj("ma")).astype(ma_ref.dtype)
    mb_ref[0] = jax.nn.sigmoid(proj("mb")).astype(mb_ref.dtype)
    z_nga = proj("nga")
    ng_ref[0] = jax.nn.sigmoid(z_nga).T
    la = jax.nn.log_sigmoid(_dot_3pass(z_nga, wa_ref[...]) + ba_ref[...]) / GLA_TAU
    tri = tri_ref[...]
    gb_ref[0] = jnp.concatenate([_dot_exact_lhs(tri, la[c * GLA_C:(c + 1) * GLA_C])
                                 for c in range(la.shape[0] // GLA_C)], axis=0)


def _in_proj(x, g_mix, w_in, positions, gla_w_a2, gla_b_a):
    B, S, D = x.shape
    inv = jnp.tile(_rope_inv_freq(), LANES // (NSA_DH // 2)).reshape(1, LANES)
    tm = min(IN_TM, S)
    assert tm % GLA_C == 0
    n_gates = 3 * NSA_HEADS
    wa = jnp.pad(gla_w_a2, ((n_gates, LANES - n_gates - GLA_RANK), (0, 0)))
    ba = gla_b_a.reshape(1, -1)
    tri = jnp.asarray(np.tril(np.ones((GLA_C, GLA_C))), BF16)
    splits = np.cumsum((512,) + (128,) * 6 + (24, 256, 256, 512, 16, 512, 1024, 1024))
    (wq, wkc, wvc, wks, wvs, wkw, wvw, wng, wgq, wgk, wgv, wga, wgr, wma, wmb) = jnp.split(
        w_in, splits[:-1].tolist(), axis=1)
    pad = lambda w: jnp.pad(w, ((0, 0), (0, LANES - w.shape[1])))
    w_all = jnp.concatenate([wq, wkc, wvc, wks, wvs, wkw, wvw, wgq, wgk, wgv, wgr, wma, wmb,
                             pad(jnp.concatenate([wng, wga], axis=1))], axis=1).astype(BF16)
    assert w_all.shape[1] == IN_NW
    grid = (B, S // tm)
    tok = lambda w: pl.BlockSpec((1, tm, w), lambda b, i: (b, i, 0))
    head = lambda n, w: pl.BlockSpec((1, n, tm, w), lambda b, i: (b, 0, i, 0))
    head_t = lambda n, w: pl.BlockSpec((1, n, w, tm), lambda b, i: (b, 0, 0, i))
    sds = jax.ShapeDtypeStruct
    out_shape = (
        sds((B, NSA_HEADS, NSA_DH, S), BF16),
        sds((B, S, 256), F32),
        sds((B, NSA_KV_GROUPS, S, LANES), BF16),
        sds((B, NSA_KV_GROUPS, NSA_DH, S), BF16),
        sds((B, NSA_KV_GROUPS, S, NSA_DH), BF16),
        sds((B, NSA_KV_GROUPS, NSA_DH, S), BF16),
        sds((B, S, GLA_HEADS * GLA_DK), F32),
        sds((B, S, GLA_HEADS * GLA_DK), F32),
        sds((B, S, GLA_HEADS * GLA_DV), BF16),
        sds((B, GLA_HEADS * GLA_DV, S), BF16),
        sds((B, S, GLA_HEADS * GLA_DV), F32),
        sds((B, S, D), BF16),
        sds((B, S, D), BF16),
        sds((B, LANES, S), F32),
        sds((B, S, GLA_HEADS * GLA_DK), F32),
    )
    out_specs = (
        head_t(NSA_HEADS, NSA_DH), tok(256),
        head(NSA_KV_GROUPS, LANES), head_t(NSA_KV_GROUPS, NSA_DH),
        head(NSA_KV_GROUPS, NSA_DH), head_t(NSA_KV_GROUPS, NSA_DH),
        tok(GLA_HEADS * GLA_DK), tok(GLA_HEADS * GLA_DK),
        tok(GLA_HEADS * GLA_DV),
        pl.BlockSpec((1, GLA_HEADS * GLA_DV, tm), lambda b, i: (b, 0, i)),
        tok(GLA_HEADS * GLA_DV), tok(D), tok(D),
        pl.BlockSpec((1, LANES, tm), lambda b, i: (b, 0, i)), tok(GLA_HEADS * GLA_DK),
    )
    return pl.pallas_call(
        _in_proj_body,
        grid=grid,
        in_specs=[
            tok(D),
            pl.BlockSpec((1, D), lambda b, i: (0, 0)),
            pl.BlockSpec((D, IN_NW), lambda b, i: (0, 0), pipeline_mode=pl.Buffered(1)),
            tok(1),
            pl.BlockSpec((1, LANES), lambda b, i: (0, 0)),
            pl.BlockSpec(wa.shape, lambda b, i: (0, 0)),
            pl.BlockSpec(ba.shape, lambda b, i: (0, 0)),
            pl.BlockSpec(tri.shape, lambda b, i: (0, 0)),
        ],
        out_specs=out_specs,
        out_shape=out_shape,
        compiler_params=pltpu.CompilerParams(
            dimension_semantics=("parallel", "parallel"), vmem_limit_bytes=VMEM_LIMIT),
        name="in_proj",
    )(x, g_mix.reshape(1, D), w_all, positions.reshape(B, S, 1), inv, wa, ba, tri)


def _compress_body(x_ref, p0_ref, p1_ref, w0_ref, w1_ref, b1_ref, w2_ref, b2_ref, cs_ref, sn_ref,
                   kc_ref, vc_ref, xk_ref, xv_ref):
    n = x_ref.shape[1] // CMP_STRIDE
    half_refs = (xk_ref, xv_ref)
    for s, half in enumerate(half_refs):
        half[...] = x_ref[0, :, s * LANES:(s + 1) * LANES]
    hid = w0_ref.shape[1] // len(half_refs)
    y0 = [None, None]
    y1 = [None, None]
    for t in range(CMP_STRIDE):
        for s, half in enumerate(half_refs):
            xt = half[pl.ds(t, n, stride=CMP_STRIDE), :]
            rows = slice((2 * t + s) * LANES, (2 * t + s + 1) * LANES)
            cols = slice(s * hid, (s + 1) * hid)
            lanes = slice(s * LANES, (s + 1) * LANES)
            d0 = _dot((xt + p0_ref[t:t + 1, lanes]).astype(BF16), w0_ref[rows, cols])
            d1 = _dot((xt + p1_ref[t:t + 1, lanes]).astype(BF16), w1_ref[rows, cols])
            y0[s] = d0 if y0[s] is None else y0[s] + d0
            y1[s] = d1 if y1[s] is None else y1[s] + d1
    y0 = jnp.concatenate(y0, axis=1)
    y1 = jnp.concatenate(y1, axis=1)
    h = jax.nn.gelu(y0 + pltpu.roll(y1, n - 1, 0) + b1_ref[...])
    o = _dot(h.astype(BF16), w2_ref[...]) + b2_ref[...]
    k = _rope_lanes(o[:, :LANES], cs_ref[0], sn_ref[0])
    v_t = o[:, LANES:].T
    for g in range(NSA_KV_GROUPS):
        sl = slice(g * NSA_DH, (g + 1) * NSA_DH)
        kc_ref[0, g] = k[:, sl].astype(BF16)
        vc_ref[0, g] = v_t[sl].astype(BF16)


def _compress(kvc, cs_c, sn_c, pos_k, w1_k, b1_k, w2_k, b2_k, pos_v, w1_v, b1_v, w2_v, b2_v):
    B, S, _ = kvc.shape
    n = S // CMP_STRIDE
    ns = 2 * NSA_KV_GROUPS
    eye = jnp.eye(ns, dtype=F32)
    w1s = jnp.stack([w1_k, w1_k, w1_v, w1_v])
    poss = jnp.stack([pos_k, pos_k, pos_v, pos_v])
    wbig, pbig = [], []
    for m in range(CMP_BLOCK // CMP_STRIDE):
        wm = w1s.reshape(ns, CMP_BLOCK, NSA_DH, CMP_HIDDEN)[:, m * CMP_STRIDE:(m + 1) * CMP_STRIDE]
        wb = jnp.einsum('ctdj,ce->tcdej', wm, eye).reshape(CMP_STRIDE * ns * NSA_DH, ns * CMP_HIDDEN)
        wbig.append(wb.astype(BF16))
        pm = poss[:, m * CMP_STRIDE:(m + 1) * CMP_STRIDE]
        pbig.append(pm.transpose(1, 0, 2).reshape(CMP_STRIDE, ns * NSA_DH))
    b1 = jnp.concatenate([b1_k, b1_k, b1_v, b1_v]).reshape(1, ns * CMP_HIDDEN)
    w2s = jnp.stack([w2_k, w2_k, w2_v, w2_v])
    w2big = jnp.einsum('cjd,ce->cjed', w2s, eye).reshape(ns * CMP_HIDDEN, ns * NSA_DH).astype(BF16)
    b2 = jnp.concatenate([b2_k, b2_k, b2_v, b2_v]).reshape(1, ns * NSA_DH)
    full = lambda a: pl.BlockSpec(a.shape, lambda b: (0,) * a.ndim)
    sds = jax.ShapeDtypeStruct
    return pl.pallas_call(
        _compress_body,
        grid=(B,),
        in_specs=[pl.BlockSpec((1, S, ns * NSA_DH), lambda b: (b, 0, 0)),
                  full(pbig[0]), full(pbig[1]), full(wbig[0]), full(wbig[1]), full(b1), full(w2big), full(b2),
                  pl.BlockSpec((1, n, LANES), lambda b: (b, 0, 0)),
                  pl.BlockSpec((1, n, LANES), lambda b: (b, 0, 0))],
        out_specs=(pl.BlockSpec((1, NSA_KV_GROUPS, n, NSA_DH), lambda b: (b, 0, 0, 0)),
                   pl.BlockSpec((1, NSA_KV_GROUPS, NSA_DH, n), lambda b: (b, 0, 0, 0))),
        out_shape=(sds((B, NSA_KV_GROUPS, n, NSA_DH), BF16), sds((B, NSA_KV_GROUPS, NSA_DH, n), BF16)),
        scratch_shapes=[pltpu.VMEM((S, LANES), F32), pltpu.VMEM((S, LANES), F32)],
        compiler_params=pltpu.CompilerParams(
            dimension_semantics=("parallel",), vmem_limit_bytes=VMEM_LIMIT),
        name="compress",
    )(kvc, pbig[0], pbig[1], wbig[0], wbig[1], b1, w2big, b2, cs_c, sn_c)


NSA_ONES = 16


def _with_ones(v_t):
    return jnp.concatenate([v_t, jnp.ones((NSA_ONES, v_t.shape[1]), v_t.dtype)], axis=0)


def _softmax_step(s, m, acc, v_t):
    m_new = jnp.maximum(m, jnp.max(s, axis=0, keepdims=True))
    p = jnp.exp2(s - m_new).astype(BF16)
    acc = jnp.exp2(m - m_new) * acc + _dot(_with_ones(v_t), p)
    return m_new, acc


def _nsa_groups_body(q_ref, kc_ref, vc_ref, ks_ref, vs_ref, kw_ref, vw_ref, ng_ref, ovt_ref,
                     o_ref, sa_ref, sb_ref, sw_ref, *, tq, tk, n_slc):
    qi = pl.program_id(1)
    q0 = qi * tq
    n_cmp = kc_ref.shape[2]
    W = NSA_HPG * tq
    G = range(NSA_KV_GROUPS)
    per_head = lambda x: jnp.concatenate([x] * NSA_HPG, axis=1)
    q_all = [jnp.concatenate([q_ref[0, g * NSA_HPG + h] for h in range(NSA_HPG)], axis=1) for g in G]
    t_lane = q0 + lax.broadcasted_iota(jnp.int32, (1, tq), 1)

    cmp_end = CMP_STRIDE * lax.broadcasted_iota(jnp.int32, (n_cmp, 1), 0) + (CMP_BLOCK - 1)
    ok = cmp_end <= t_lane
    cmp_bias = per_head(jnp.where(ok, 0.0, NEG_INF))
    cmp_keep = per_head(jnp.where(ok, 1.0, 0.0))
    sm = [_dot(kc_ref[0, g], q_all[g]) + cmp_bias for g in G]
    wk = WINDOW + tq
    w0 = pl.multiple_of(jnp.maximum(q0 - WINDOW, 0), tq)
    kpos = w0 + lax.broadcasted_iota(jnp.int32, (wk, 1), 0)
    win_bias = per_head(jnp.where(jnp.logical_and(kpos <= t_lane, kpos > t_lane - WINDOW), 0.0, NEG_INF))
    for g in G:
        sw_ref[g] = _dot(kw_ref[0, g, pl.ds(w0, wk), :], q_all[g]) + win_bias
    blk = lax.broadcasted_iota(jnp.int32, (n_slc, tq), 0)
    cur = (q0 + lax.broadcasted_iota(jnp.int32, (n_slc, tq), 1)) // SLC_BLOCK
    forced = (blk == 0) | (blk == cur) | (blk == cur - 1)
    o_cmp, imp = [], []
    for g in G:
        e = jnp.exp2(sm[g] - jnp.max(sm[g], axis=0, keepdims=True))
        p = e * (1.0 / jnp.sum(e, axis=0, keepdims=True)) * cmp_keep
        o_cmp.append(_dot(vc_ref[0, g], p.astype(BF16)))
        p_sum = p[:, 0:tq]
        for h in range(1, NSA_HPG):
            p_sum = p_sum + p[:, h * tq:(h + 1) * tq]
        p_hi = p_sum.astype(BF16)
        p_lo = (p_sum - p_hi.astype(F32)).astype(BF16)
        imp_g = _dot(ovt_ref[...], p_hi) + _dot(ovt_ref[...], p_lo)
        imp.append(jnp.where(blk <= cur, jnp.where(forced, FORCED_SCORE, imp_g), NEG_INF))

    SUBL = 8

    def block_bias(n):
        out = []
        for g in G:
            groups = [imp[g][r:r + SUBL] for r in range(0, n, SUBL)]
            ranks = [jnp.zeros((SUBL, tq), F32) for _ in groups]
            for i in range(n):
                row = imp[g][i:i + 1, :]
                for gidx, grp in enumerate(groups):
                    r = gidx * SUBL
                    ge = jnp.where(row >= grp, 1.0, 0.0)
                    gt = jnp.where(row > grp, 1.0, 0.0)
                    if r > i:
                        inc = ge
                    elif r + SUBL - 1 <= i:
                        inc = gt
                    else:
                        inc = jnp.where(blk[r:r + SUBL] > i, ge, gt)
                    ranks[gidx] = ranks[gidx] + inc
            bias = jnp.where(jnp.concatenate(ranks, axis=0) < float(min(SLC_TOPK, n_slc)), 0.0, NEG_INF)
            out.append(jnp.concatenate([bias, jnp.zeros((NSA_DH - n, tq), F32)], axis=0) if n < NSA_DH else bias)
        return tuple(out)

    quarter = n_slc // 4
    visible = (q0 + tq - 1) // SLC_BLOCK + 1
    bias = lax.cond(
        visible <= 2 * quarter,
        lambda: lax.cond(visible <= quarter, lambda: block_bias(quarter), lambda: block_bias(2 * quarter)),
        lambda: lax.cond(visible <= 3 * quarter, lambda: block_bias(3 * quarter), lambda: block_bias(n_slc)))
    q_aug = [jnp.concatenate([q_all[g], per_head(bias[g]).astype(BF16)], axis=0) for g in G]

    kpos_l = lax.broadcasted_iota(jnp.int32, (tk, 1), 0)
    n_tiles = q0 // tk + 1
    last_tile = ks_ref.shape[2] // tk - 1

    def tile_start(jt):
        return pl.multiple_of(jnp.minimum(jt, last_tile) * tk, tk)

    def scores_into(s_ref, jt):
        causal = per_head(jnp.where(jt * tk + kpos_l <= t_lane, 0.0, NEG_INF))
        for g in G:
            s_ref[g] = _dot(ks_ref[0, g, pl.ds(tile_start(jt), tk), :], q_aug[g]) + causal

    def consume(s_ref, jt, carry):
        return tuple(_softmax_step(s_ref[g], *carry[g], vs_ref[0, g, :, pl.ds(tile_start(jt), tk)]) for g in G)

    def slc_pair(i, carry):
        scores_into(sb_ref, 2 * i + 1)
        carry = consume(sa_ref, 2 * i, carry)
        scores_into(sa_ref, 2 * i + 2)
        return consume(sb_ref, 2 * i + 1, carry)

    init = tuple((jnp.full((1, W), NEG_INF, F32), jnp.zeros((NSA_DH + NSA_ONES, W), F32)) for g in G)
    scores_into(sa_ref, 0)

    acc_w = []
    for g in G:
        sw = sw_ref[g]
        ew = jnp.exp2(sw - jnp.max(sw, axis=0, keepdims=True)).astype(BF16)
        acc_w.append(_dot(_with_ones(vw_ref[0, g, :, pl.ds(w0, wk)]), ew))

    slc = lax.fori_loop(0, (n_tiles + 1) // 2, slc_pair, init)

    for g in G:
        def gate(j):
            return jnp.concatenate([ng_ref[0, (g * NSA_HPG + h) * 3 + j:(g * NSA_HPG + h) * 3 + j + 1, :]
                                    for h in range(NSA_HPG)], axis=1)

        acc_s = slc[g][1]
        l_s, l_w = acc_s[NSA_DH:NSA_DH + 1], acc_w[g][NSA_DH:NSA_DH + 1]
        out_t = (gate(0) * o_cmp[g] + (gate(1) * (1.0 / l_s)) * acc_s[:NSA_DH]
                 + (gate(2) * (1.0 / l_w)) * acc_w[g][:NSA_DH])
        out_t = jnp.concatenate([out_t[:, h * tq:(h + 1) * tq] for h in range(NSA_HPG)], axis=0)
        o_ref[0, :, g * NSA_HPG * NSA_DH:(g + 1) * NSA_HPG * NSA_DH] = out_t.T.astype(o_ref.dtype)


def _nsa_attention(q, kc, vc, ks, vs, kw, vw, ng):
    B, H, dh, S = q.shape
    tq = min(ATT_TQ, S)
    tk = min(ATT_TK, S)
    assert S % tk == 0 and tk % tq == 0 and S >= WINDOW + tq and S // SLC_BLOCK <= dh
    n_cmp = kc.shape[2]
    n_slc = S // SLC_BLOCK
    c0 = CMP_STRIDE * np.arange(n_cmp)[None, :]
    s0 = SLC_BLOCK * np.arange(n_slc)[:, None]
    ov = np.clip(np.minimum(c0 + CMP_BLOCK, s0 + SLC_BLOCK) - np.maximum(c0, s0), 0, None) / CMP_BLOCK
    ovt = jnp.asarray(ov, BF16)
    G = NSA_KV_GROUPS
    grid = (B, S // tq)
    k_spec = lambda n, w: pl.BlockSpec((1, G, n, w), lambda b, i: (b, 0, 0, 0))
    vt_spec = lambda n: pl.BlockSpec((1, G, dh, n), lambda b, i: (b, 0, 0, 0))
    return pl.pallas_call(
        functools.partial(_nsa_groups_body, tq=tq, tk=tk, n_slc=n_slc),
        grid=grid,
        in_specs=[
            pl.BlockSpec((1, H, dh, tq), lambda b, i: (b, 0, 0, i)),
            k_spec(n_cmp, dh), vt_spec(n_cmp),
            k_spec(S, 2 * dh), vt_spec(S),
            k_spec(S, dh), vt_spec(S),
            pl.BlockSpec((1, LANES, tq), lambda b, i: (b, 0, i)),
            pl.BlockSpec(ovt.shape, lambda b, i: (0, 0)),
        ],
        out_specs=pl.BlockSpec((1, tq, H * dh), lambda b, i: (b, i, 0)),
        out_shape=jax.ShapeDtypeStruct((B, S, H * dh), BF16),
        scratch_shapes=[pltpu.VMEM((G, tk, NSA_HPG * tq), F32),
                        pltpu.VMEM((G, tk, NSA_HPG * tq), F32),
                        pltpu.VMEM((G, WINDOW + tq, NSA_HPG * tq), F32)],
        compiler_params=pltpu.CompilerParams(
            dimension_semantics=("parallel", "arbitrary"), vmem_limit_bytes=VMEM_LIMIT),
        name="nsa_attn",
    )(q, kc, vc, ks, vs, kw, vw, ng, ovt)


def _dot_3pass(a, b):
    a1 = a.astype(BF16)
    a2 = (a - a1.astype(F32)).astype(BF16)
    b1 = b.astype(BF16)
    b2 = (b - b1.astype(F32)).astype(BF16)
    return _dot(a1, b1) + (_dot(a1, b2) + _dot(a2, b1))


def _gla_pair_body(q_ref, k_ref, v_ref, vt_ref, b_ref, r_ref, ng_ref,
                   o_ref, state_ref, kp_ref, bp_ref, vp_ref, plain_ref, *, n_chunks):
    C, SUB, dk, dv, hp = GLA_C, GLA_SUB, GLA_DK, GLA_DV, GLA_HP
    W = hp * dk
    lane = lax.broadcasted_iota(jnp.int32, (1, W), 1)
    of_head = [lane // dk == hh for hh in range(hp)]
    t_loc = lax.broadcasted_iota(jnp.int32, (C, 1), 0)
    state_ref[...] = jnp.zeros_like(state_ref)
    kp_ref[0:SUB, :] = jnp.zeros((SUB, W), F32)
    bp_ref[0:SUB, :] = jnp.zeros((SUB, W), F32)
    vp_ref[:, 0:SUB, :] = jnp.zeros((hp, SUB, dv), F32)

    def decay_chunk(c, all_plain):
        b_last = b_ref[0, pl.ds(c * C + (C - 1), 1), :]
        plain = (jnp.min(b_last) > -GLA_PLAIN_DECAY).astype(jnp.int32)
        plain_ref[c] = plain
        return jnp.minimum(all_plain, plain)

    all_plain = lax.fori_loop(0, n_chunks, decay_chunk, jnp.int32(1), unroll=GLA_UNROLL_DECAY)

    def chunk(c, check_decay):
        c0 = pl.multiple_of(c * C, C)
        b = b_ref[0, pl.ds(c0, C), :]
        q = q_ref[0, pl.ds(c0, C), :]
        k = k_ref[0, pl.ds(c0, C), :]
        v = [v_ref[0, pl.ds(c0, C), hh * dv:(hh + 1) * dv] for hh in range(hp)]
        vt = [vt_ref[0, hh * dv:(hh + 1) * dv, pl.ds(c0, C)] for hh in range(hp)]
        b_last = b[C - 1:C, :]
        st = state_ref[...]
        st_b = st.astype(BF16)
        qg = q * jnp.exp(b)
        qg_h = [jnp.where(of_head[hh], qg, 0.0).astype(BF16) for hh in range(hp)]
        o_inter = [_dot_nt(qg_h[hh], st_b) for hh in range(hp)]

        def intra_plain():
            ke = (k * jnp.exp(-b)).astype(BF16)
            row = lax.broadcasted_iota(jnp.int32, (C, C), 0)
            col = lax.broadcasted_iota(jnp.int32, (C, C), 1)
            return tuple(_dot(jnp.where(row >= col, _dot_nt(qg_h[hh], ke), 0.0).astype(BF16), v[hh])
                         for hh in range(hp))

        def intra_strong_decay():
            far = [[jnp.zeros((SUB, dv), F32)] for _ in range(hp)]
            for i in range(1, C // SUB):
                r0 = i * SUB
                b_first = b[r0:r0 + 1, :]
                qt = q[r0:r0 + SUB] * jnp.exp(b[r0:r0 + SUB] - b_first)
                kt = (k[:r0] * jnp.exp(b_first - b[:r0])).astype(BF16)
                lt = lax.broadcasted_iota(jnp.int32, (SUB, r0), 0)
                ls = lax.broadcasted_iota(jnp.int32, (SUB, r0), 1)
                for hh in range(hp):
                    a = _dot_nt(jnp.where(of_head[hh], qt, 0.0).astype(BF16), kt)
                    a = jnp.where(lt + (r0 - SUB) >= ls, a, 0.0)
                    far[hh].append(_dot(a.astype(BF16), v[hh][:r0]))
            acc = [jnp.concatenate(far[hh], axis=0) for hh in range(hp)]
            kp_ref[SUB:SUB + C, :] = k
            bp_ref[SUB:SUB + C, :] = b
            for hh in range(hp):
                vp_ref[hh, SUB:SUB + C, :] = v[hh].astype(F32)
            for d in range(SUB):
                kd = kp_ref[SUB - d:SUB - d + C, :]
                bd = bp_ref[SUB - d:SUB - d + C, :]
                valid = t_loc >= d
                w = jnp.exp(jnp.where(valid, b - bd, 0.0))
                x = jnp.where(valid, q * kd * w, 0.0)
                for hh in range(hp):
                    a_d = jnp.sum(jnp.where(of_head[hh], x, 0.0), axis=-1, keepdims=True)
                    acc[hh] = acc[hh] + a_d * vp_ref[hh, SUB - d:SUB - d + C, :]
            return tuple(acc)

        if check_decay:
            o_intra = lax.cond(plain_ref[c] > 0, intra_plain, intra_strong_decay)
        else:
            o_intra = intra_plain()
        k_dec = (k * jnp.exp(b_last - b)).astype(BF16)
        upd = _dot(vt[hp - 1], k_dec)
        for hh in range(hp - 2, -1, -1):
            upd = jnp.where(of_head[hh], _dot(vt[hh], k_dec), upd)
        state_ref[...] = st * jnp.exp(b_last) + upd
        for hh in range(hp):
            o = o_inter[hh] + o_intra[hh]
            o = o * lax.rsqrt(jnp.mean(o * o, axis=-1, keepdims=True) + EPS) * ng_ref[hh]
            gate = jax.nn.silu(r_ref[0, pl.ds(c0, C), hh * dv:(hh + 1) * dv])
            o_ref[0, pl.ds(c0, C), hh * dv:(hh + 1) * dv] = (o * gate).astype(o_ref.dtype)
        return 0

    @pl.when(all_plain > 0)
    def _():
        lax.fori_loop(0, n_chunks, lambda c, _: chunk(c, False), 0, unroll=GLA_UNROLL_PLAIN)

    @pl.when(all_plain <= 0)
    def _():
        lax.fori_loop(0, n_chunks, lambda c, _: chunk(c, True), 0)


def _gla_pairs(gq, gk, gv, gvt, gb, gr, norm_g):
    B, S, _ = gq.shape
    H, dk, dv, hp = GLA_HEADS, GLA_DK, GLA_DV, GLA_HP
    W = hp * dk
    assert W == LANES and H % hp == 0
    ng = norm_g.reshape(H, 1, dv)
    n_chunks = S // GLA_C
    tok_spec = lambda w: pl.BlockSpec((1, S, hp * w), lambda b, h: (b, 0, h))
    return pl.pallas_call(
        functools.partial(_gla_pair_body, n_chunks=n_chunks),
        grid=(B, H // hp),
        in_specs=[tok_spec(dk), tok_spec(dk), tok_spec(dv),
                  pl.BlockSpec((1, hp * dv, S), lambda b, h: (b, h, 0)),
                  tok_spec(dk),
                  tok_spec(dv),
                  pl.BlockSpec((hp, 1, dv), lambda b, h: (h, 0, 0))],
        out_specs=tok_spec(dv),
        out_shape=jax.ShapeDtypeStruct((B, S, H * dv), BF16),
        scratch_shapes=[pltpu.VMEM((dv, W), F32),
                        pltpu.VMEM((GLA_SUB + GLA_C, W), F32),
                        pltpu.VMEM((GLA_SUB + GLA_C, W), F32),
                        pltpu.VMEM((hp, GLA_SUB + GLA_C, dv), F32),
                        pltpu.SMEM((n_chunks,), jnp.int32)],
        compiler_params=pltpu.CompilerParams(
            dimension_semantics=("parallel", "parallel"), vmem_limit_bytes=VMEM_LIMIT),
        name="gla",
    )(gq, gk, gv, gvt, gb, gr, ng)


def _merge_body(x_ref, ya_ref, yb_ref, ma_ref, mb_ref, wpa_ref, wpb_ref, wo_ref, gf_ref, wr_ref, br_ref,
                h_ref, v_ref, comb_ref):
    rows = x_ref.shape[0] // MERGE_PARTS
    part = lambda p: slice(p * rows, (p + 1) * rows)
    wr = wr_ref[...]
    w_hi = wr.astype(BF16)
    w_both = jnp.concatenate([w_hi, (wr - w_hi.astype(F32)).astype(BF16)], axis=1)

    def mix(p):
        sl = part(p)
        y_a = _dot(ya_ref[sl, :], wpa_ref[...])
        y_b = _dot(yb_ref[sl, :], wpb_ref[...])
        mixed = ma_ref[sl, :] * y_a + mb_ref[sl, :] * y_b
        h = x_ref[sl, :] + _dot(mixed.astype(BF16), wo_ref[...])
        h_ref[sl, :] = h
        return h

    def route(p, h):
        sl = part(p)
        v = h * lax.rsqrt(jnp.mean(h * h, axis=-1, keepdims=True) + EPS) * gf_ref[...]
        v_ref[sl, :] = v.astype(BF16)
        v_hi = v.astype(BF16)
        v_lo = (v - v_hi.astype(F32)).astype(BF16)
        both = _dot(v_hi, w_both)
        logits = (both[:, :LANES] + both[:, LANES:]) + _dot(v_lo, w_hi) + br_ref[...]
        comb_ref[sl, :] = _route_weights(logits)

    h_prev = mix(0)
    for p in range(1, MERGE_PARTS):
        h_next = mix(p)
        route(p - 1, h_prev)
        h_prev = h_next
    route(MERGE_PARTS - 1, h_prev)


def _route_weights(logits):
    lane = lax.broadcasted_iota(jnp.int32, logits.shape, 1)
    is_grp = jnp.logical_and(lane >= MOE_EXPERTS, lane < MOE_EXPERTS + MOE_GROUPS)
    lg = jnp.where(is_grp, logits, NEG_INF)
    eg = jnp.where(is_grp, jnp.exp(lg - jnp.max(lg, axis=-1, keepdims=True)), 0.0)
    pg = eg / jnp.sum(eg, axis=-1, keepdims=True)
    p_grp = jnp.max(pg, axis=-1, keepdims=True)
    g_sel = jnp.min(jnp.where(jnp.logical_and(is_grp, pg == p_grp), lane, 2 * LANES),
                    axis=-1, keepdims=True) - MOE_EXPERTS
    in_grp = jnp.logical_and(lane < MOE_EXPERTS, lane // MOE_EPG == g_sel)
    le = jnp.where(in_grp, logits, NEG_INF)
    ee = jnp.where(in_grp, jnp.exp(le - jnp.max(le, axis=-1, keepdims=True)), 0.0)
    pin = ee / jnp.sum(ee, axis=-1, keepdims=True)
    p1 = jnp.max(jnp.where(in_grp, pin, -1.0), axis=-1, keepdims=True)
    i1 = jnp.min(jnp.where(jnp.logical_and(in_grp, pin == p1), lane, 2 * LANES), axis=-1, keepdims=True)
    rest = jnp.logical_and(in_grp, lane != i1)
    p2 = jnp.max(jnp.where(rest, pin, -1.0), axis=-1, keepdims=True)
    i2 = jnp.min(jnp.where(jnp.logical_and(rest, pin == p2), lane, 2 * LANES), axis=-1, keepdims=True)
    tot = p1 + p2
    return jnp.where(lane == i1, p_grp * p1 / tot, 0.0) + jnp.where(lane == i2, p_grp * p2 / tot, 0.0)


def _merge(x2, ya, yb, ma, mb, w_proj_nsa, w_proj_gla, w_out, g_ffn, w_grp, b_grp, w_exp, b_exp):
    T, D = x2.shape
    tm = min(MERGE_TM, T)
    wr = jnp.pad(jnp.concatenate([w_exp, w_grp], axis=1), ((0, 0), (0, LANES - MOE_EXPERTS - MOE_GROUPS)))
    br = jnp.pad(jnp.concatenate([b_exp, b_grp]), (0, LANES - MOE_EXPERTS - MOE_GROUPS)).reshape(1, LANES)
    tok = lambda w: pl.BlockSpec((tm, w), lambda i: (i, 0))
    full = lambda a: pl.BlockSpec(a.shape, lambda i: (0, 0))
    wpa, wpb, wo = w_proj_nsa.astype(BF16), w_proj_gla.astype(BF16), w_out.astype(BF16)
    gf = g_ffn.reshape(1, D)
    sds = jax.ShapeDtypeStruct
    return pl.pallas_call(
        _merge_body,
        grid=(T // tm,),
        in_specs=[tok(D), tok(ya.shape[1]), tok(yb.shape[1]), tok(D), tok(D),
                  full(wpa), full(wpb), full(wo), full(gf), full(wr), full(br)],
        out_specs=(tok(D), tok(D), tok(LANES)),
        out_shape=(sds((T, D), F32), sds((T, D), BF16), sds((T, LANES), F32)),
        compiler_params=pltpu.CompilerParams(
            dimension_semantics=("parallel",), vmem_limit_bytes=VMEM_LIMIT),
        name="merge",
    )(x2, ya, yb, ma, mb, wpa, wpb, wo, gf, wr, br)


def _moe_rows(tw):
    rows = 2 * tw + MOE_EXPERTS * (MOE_ALIGN - 1) + MOE_RT
    return -(-rows // MOE_RT) * MOE_RT


def _moe_window(e, phase, v_ref, comb_ref, tri_ref, y_ref, xs_ref, z_ref, meta_ref, pos_ref, tw, rmax):
    no_row = -1.0

    @pl.when(e == 0 if phase == "route" else False)
    def _route():
        comb = comb_ref[...]
        assigned = comb > 0.0
        a = jnp.where(assigned, 1.0, 0.0)
        tri = tri_ref[...]
        run = jnp.zeros((1, LANES), F32)
        ranks = []
        for b in range(tw // MOE_RT):
            ab = a[b * MOE_RT:(b + 1) * MOE_RT]
            ranks.append(_dot(tri, ab.astype(BF16)) + run)
            run = run + jnp.sum(ab, axis=0, keepdims=True)
        rank = jnp.concatenate(ranks, axis=0)
        cnt_pad = jnp.floor((run + (MOE_ALIGN - 1)) * (1.0 / MOE_ALIGN)) * MOE_ALIGN
        incl = jnp.broadcast_to(cnt_pad, (8, LANES))
        lane8 = lax.broadcasted_iota(jnp.int32, (8, LANES), 1)
        shift = 1
        while shift < MOE_EXPERTS:
            incl = incl + jnp.where(lane8 >= shift, pltpu.roll(incl, shift, 1), 0.0)
            shift *= 2
        offs = incl[0:1] - cnt_pad
        meta_ref[0:1, :] = offs
        meta_ref[1:2, :] = run
        row_of = offs + rank
        pos_a = jnp.min(jnp.where(assigned, row_of, 1e9), axis=-1, keepdims=True)
        pos_b = jnp.max(jnp.where(assigned, row_of, no_row), axis=-1, keepdims=True)
        pos_a = jnp.where(pos_a > 1e8, no_row, pos_a)
        pos_b = jnp.where(pos_b == pos_a, no_row, pos_b)
        w_a = jnp.sum(jnp.where(jnp.logical_and(assigned, row_of == pos_a), comb, 0.0), axis=-1, keepdims=True)
        w_b = jnp.sum(jnp.where(jnp.logical_and(assigned, row_of == pos_b), comb, 0.0), axis=-1, keepdims=True)
        lane_t = lax.broadcasted_iota(jnp.int32, (tw, LANES), 1)
        pos_ref[...] = jnp.where(lane_t == 0, pos_a, jnp.where(lane_t == 1, pos_b, jnp.where(
            lane_t == 2, w_a, jnp.where(lane_t == 3, w_b, no_row))))
        pos_t = pos_ref[...].T
        pa, pb = pos_t[0:1], pos_t[1:2]
        v = v_ref[...]
        for rt in range(rmax // MOE_RT):
            r = (rt * MOE_RT + lax.broadcasted_iota(jnp.int32, (MOE_RT, 1), 0)).astype(F32)
            p = jnp.where(r == pa, 1.0, jnp.where(r == pb, 1.0, 0.0)).astype(BF16)
            xs_ref[rt * MOE_RT:(rt + 1) * MOE_RT, :] = _dot(p, v).astype(BF16)
        z_ref[...] = jnp.zeros_like(z_ref)

    @pl.when(e == pl.num_programs(1) - 1 if phase == "combine" else False)
    def _combine():
        r = lax.broadcasted_iota(jnp.int32, (1, rmax), 1).astype(F32)
        z = z_ref[...]
        for tt in range(tw // MOE_RT):
            rows = slice(tt * MOE_RT, (tt + 1) * MOE_RT)
            pa, pb = pos_ref[rows, 0:1], pos_ref[rows, 1:2]
            w_a, w_b = pos_ref[rows, 2:3], pos_ref[rows, 3:4]
            q = jnp.where(r == pa, w_a, jnp.where(r == pb, w_b, 0.0)).astype(BF16)
            y_ref[rows, :] = _dot(q, z).astype(y_ref.dtype)


def _moe_body(v_ref, comb_ref, wg_ref, wu_ref, wd_ref, tri_ref, y_ref,
              xs_ref, z_ref, meta_ref, pos_ref, *, tw, rmax, n_win):
    e = pl.program_id(1)
    lane = lax.broadcasted_iota(jnp.int32, (1, LANES), 1)

    def window_phase(phase):
        for w in range(n_win):
            rows = pl.ds(w * tw, tw)
            _moe_window(e, phase, v_ref.at[rows], comb_ref.at[rows], tri_ref, y_ref.at[rows],
                        xs_ref.at[w], z_ref.at[w], meta_ref.at[w], pos_ref.at[w], tw, rmax)

    window_phase("route")

    pick = lambda w, row: jnp.sum(jnp.where(lane == e, meta_ref[w, row:row + 1, :], 0.0)).astype(jnp.int32)
    offs = [pick(w, 0) for w in range(n_win)]
    tiles = [(pick(w, 1) + MOE_ET - 1) // MOE_ET for w in range(n_win)]

    def expert(x):
        hdn = jax.nn.silu(_dot(x, wg_ref[0])) * _dot(x, wu_ref[0])
        return _dot(hdn.astype(BF16), wd_ref[0]).astype(BF16)

    def joint_tile(i, _):
        r0 = [pl.multiple_of(offs[w] + i * MOE_ET, MOE_ALIGN) for w in range(n_win)]
        z = expert(jnp.concatenate([xs_ref[w, pl.ds(r0[w], MOE_ET), :] for w in range(n_win)], axis=0))
        for w in range(n_win):
            z_ref[w, pl.ds(r0[w], MOE_ET), :] = z[w * MOE_ET:(w + 1) * MOE_ET]
        return 0

    n_joint = functools.reduce(jnp.minimum, tiles)
    lax.fori_loop(0, n_joint, joint_tile, 0)
    for w in range(n_win):
        def own_tile(i, _, w=w):
            r0 = pl.multiple_of(offs[w] + i * MOE_ET, MOE_ALIGN)
            z_ref[w, pl.ds(r0, MOE_ET), :] = expert(xs_ref[w, pl.ds(r0, MOE_ET), :])
            return 0

        lax.fori_loop(n_joint, tiles[w], own_tile, 0)

    window_phase("combine")


def _moe(v, comb, w_gate, w_up, w_down):
    T, D = v.shape
    tw = min(MOE_TW, T)
    n_win = min(MOE_WINDOWS, T // tw)
    rmax = _moe_rows(tw)
    E, _, F = w_gate.shape
    wg, wu, wd = w_gate.astype(BF16), w_up.astype(BF16), w_down.astype(BF16)
    tri = jnp.asarray(np.tril(np.ones((MOE_RT, MOE_RT)), -1), BF16)
    tok = lambda w: pl.BlockSpec((n_win * tw, w), lambda i, e: (i, 0))
    tok_in = lambda w: pl.BlockSpec((n_win * tw, w), lambda i, e: (i, 0), pipeline_mode=pl.Buffered(1))
    return pl.pallas_call(
        functools.partial(_moe_body, tw=tw, rmax=rmax, n_win=n_win),
        grid=(T // (n_win * tw), E),
        in_specs=[tok_in(D), tok_in(LANES),
                  pl.BlockSpec((1, D, F), lambda i, e: (e, 0, 0)),
                  pl.BlockSpec((1, D, F), lambda i, e: (e, 0, 0)),
                  pl.BlockSpec((1, F, D), lambda i, e: (e, 0, 0)),
                  pl.BlockSpec(tri.shape, lambda i, e: (0, 0))],
        out_specs=tok(D),
        out_shape=jax.ShapeDtypeStruct((T, D), BF16),
        scratch_shapes=[pltpu.VMEM((n_win, rmax, D), BF16),
                        pltpu.VMEM((n_win, rmax, D), BF16),
                        pltpu.VMEM((n_win, 8, LANES), F32),
                        pltpu.VMEM((n_win, tw, LANES), F32)],
        compiler_params=pltpu.CompilerParams(
            dimension_semantics=("parallel", "arbitrary"), vmem_limit_bytes=VMEM_LIMIT),
        name="moe",
    )(v, comb, wg, wu, wd, tri)


def _final_body(h_ref, y_ref, g_ref, o_ref):
    h = h_ref[...] + y_ref[...].astype(F32)
    o_ref[...] = h * lax.rsqrt(jnp.mean(h * h, axis=-1, keepdims=True) + EPS) * g_ref[...]


def _final_norm(h, y, g_final):
    T, D = h.shape
    tm = min(FINAL_TM, T)
    tok = pl.BlockSpec((tm, D), lambda i: (i, 0))
    return pl.pallas_call(
        _final_body,
        grid=(T // tm,),
        in_specs=[tok, tok, pl.BlockSpec((1, D), lambda i: (0, 0))],
        out_specs=tok,
        out_shape=jax.ShapeDtypeStruct((T, D), F32),
        compiler_params=pltpu.CompilerParams(
            dimension_semantics=("parallel",), vmem_limit_bytes=VMEM_LIMIT),
        name="final_norm",
    )(h, y, g_final.reshape(1, D))


def _rope_inv_freq():
    half = NSA_DH // 2
    return 1.0 / (ROPE_THETA ** (jnp.arange(half, dtype=F32) / half))


def _rope_tables(positions):
    ang = positions.astype(F32)[..., None] * _rope_inv_freq()
    cos, sin = jnp.cos(ang), jnp.sin(ang)
    cs = jnp.concatenate([cos, cos, cos, cos], axis=-1)
    sn = jnp.concatenate([-sin, sin, -sin, sin], axis=-1)
    return cs, sn


def _layer(h, positions, g_mix, w_in, cmp_pos_k, cmp_w1_k, cmp_b1_k, cmp_w2_k, cmp_b2_k,
           cmp_pos_v, cmp_w1_v, cmp_b1_v, cmp_w2_v, cmp_b2_v, gla_w_a2, gla_b_a, gla_norm_g,
           w_proj_nsa, w_proj_gla, w_out, g_ffn, w_grp, b_grp, w_exp, b_exp, w_gate, w_up, w_down, g_out):
    B, S, D = h.shape
    n_chunks = S // CMP_STRIDE
    cmp_end = jnp.minimum(CMP_STRIDE * jnp.arange(n_chunks) + CMP_BLOCK - 1, S - 1)
    cs_c, sn_c = _rope_tables(jnp.take(positions, cmp_end, axis=1))
    (q, kvc, ks, vs, kw, vw, gq, gk, gv, gvt, gr, ma, mb, ng, gb) = _in_proj(h, g_mix, w_in, positions,
                                                                             gla_w_a2, gla_b_a)
    kc, vc = _compress(kvc, cs_c, sn_c, cmp_pos_k, cmp_w1_k, cmp_b1_k, cmp_w2_k, cmp_b2_k,
                       cmp_pos_v, cmp_w1_v, cmp_b1_v, cmp_w2_v, cmp_b2_v)
    ya = _nsa_attention(q, kc, vc, ks, vs, kw, vw, ng)
    yb = _gla_pairs(gq, gk, gv, gvt, gb, gr, gla_norm_g)
    T = B * S
    h1, v, comb = _merge(h.reshape(T, D), ya.reshape(T, -1), yb.reshape(T, -1), ma.reshape(T, D), mb.reshape(T, D),
                         w_proj_nsa, w_proj_gla, w_out, g_ffn, w_grp, b_grp, w_exp, b_exp)
    y = _moe(v, comb, w_gate, w_up, w_down)
    return _final_norm(h1, y, g_out).reshape(B, S, D)


def kernel(x, positions, g_mix, w_in, cmp_pos_k, cmp_w1_k, cmp_b1_k, cmp_w2_k, cmp_b2_k, cmp_pos_v, cmp_w1_v,
           cmp_b1_v, cmp_w2_v, cmp_b2_v, gla_w_a2, gla_b_a, gla_norm_g, w_proj_nsa, w_proj_gla, w_out, g_ffn,
           w_grp, b_grp, w_exp, b_exp, w_gate, w_up, w_down, g_final):
    depth = g_mix.shape[0]
    assert depth == 1, "the final norm closes the single layer"
    return _layer(x, positions, g_mix[0], w_in[0], cmp_pos_k[0], cmp_w1_k[0], cmp_b1_k[0], cmp_w2_k[0],
                  cmp_b2_k[0], cmp_pos_v[0], cmp_w1_v[0], cmp_b1_v[0], cmp_w2_v[0], cmp_b2_v[0],
                  gla_w_a2[0], gla_b_a[0], gla_norm_g[0], w_proj_nsa[0], w_proj_gla[0], w_out[0],
                  g_ffn[0], w_grp[0], b_grp[0], w_exp[0], b_exp[0], w_gate[0], w_up[0], w_down[0], g_final)
```

```python
import functools

import numpy as np
import jax
import jax.numpy as jnp
from jax import lax
from jax.experimental import pallas as pl
from jax.experimental.pallas import tpu as pltpu

F32 = jnp.float32
BF16 = jnp.bfloat16

NSA_HEADS = 8
NSA_KV_GROUPS = 2
NSA_HPG = NSA_HEADS // NSA_KV_GROUPS
NSA_DH = 64
CMP_BLOCK = 32
CMP_STRIDE = 16
CMP_HIDDEN = 256
SLC_BLOCK = 64
SLC_TOPK = 16
WINDOW = 512
GLA_HEADS = 4
GLA_DK = 64
GLA_DV = 128
GLA_RANK = 16
GLA_TAU = 16.0
MOE_GROUPS = 4
MOE_EPG = 8
MOE_EXPERTS = MOE_GROUPS * MOE_EPG
MOE_DFF = 512
ROPE_THETA = 10000.0
EPS = 1e-6
NEG_INF = -1e30
FORCED_SCORE = 1e4
LOG2E = 1.4426950408889634

LANES = 128
VMEM_LIMIT = 56 * 1024 * 1024

IN_TM = 512
ATT_TQ = 128
ATT_TK = 256
GLA_C = 128
GLA_SUB = 16
GLA_HP = 2
GLA_UNROLL_DECAY = 4
GLA_UNROLL_PLAIN = 4
GLA_PLAIN_DECAY = 60.0
MERGE_TM = 512
MERGE_PARTS = 1
MOE_TW = 1024
MOE_WINDOWS = 2
FINAL_TM = 1024
MOE_RT = 256
MOE_ET = 80
MOE_ALIGN = 16


def _dot(a, b):
    return jnp.dot(a, b, preferred_element_type=F32)


def _dot_nt(a, b):
    return lax.dot_general(a, b, (((1,), (1,)), ((), ())), preferred_element_type=F32)


def _split3(x):
    x1 = x.astype(BF16)
    r1 = x - x1.astype(F32)
    x2 = r1.astype(BF16)
    r2 = r1 - x2.astype(F32)
    x3 = r2.astype(BF16)
    return x1, x2, x3


def _dot_exact_lhs(a_bf16, x):
    x1, x2, x3 = _split3(x)
    return _dot(a_bf16, x1) + _dot(a_bf16, x2) + _dot(a_bf16, x3)


def _dot_f32(a, b):
    a1, a2, a3 = _split3(a)
    b1, b2, b3 = _split3(b)
    return (_dot(a1, b1) + (_dot(a1, b2) + _dot(a2, b1))
            + (_dot(a1, b3) + _dot(a2, b2) + _dot(a3, b1)))


def _rope_lanes(z, cs, sn):
    w = z.shape[-1]
    lane = lax.broadcasted_iota(jnp.int32, z.shape, 1)
    first_half = (lane % NSA_DH) < (NSA_DH // 2)
    rot = jnp.where(first_half, pltpu.roll(z, w - NSA_DH // 2, 1), pltpu.roll(z, NSA_DH // 2, 1))
    reps = w // LANES
    if reps > 1:
        cs = jnp.concatenate([cs] * reps, axis=1)
        sn = jnp.concatenate([sn] * reps, axis=1)
    return z * cs + rot * sn


_SEC = {}
_off = 0
for _name, _w in (("q", 512), ("kvc", 256), ("ks", 128), ("vs", 128), ("kw", 128), ("vw", 128),
                  ("gq", 256), ("gk", 256), ("gv", 512), ("gr", 512), ("ma", 1024), ("mb", 1024),
                  ("nga", 128)):
    _SEC[_name] = (_off, _off + _w)
    _off += _w
IN_NW = _off


def _in_proj_body(x_ref, g_ref, w_ref, pos_ref, inv_ref, wa_ref, ba_ref, tri_ref,
                  q_ref, kvc_ref, ks_ref, vs_ref, kw_ref, vw_ref,
                  gq_ref, gk_ref, gv_ref, gvt_ref, gr_ref, ma_ref, mb_ref, ng_ref, gb_ref):
    x = x_ref[0]
    var = jnp.mean(x * x, axis=-1, keepdims=True)
    u = (x * lax.rsqrt(var + EPS) * g_ref[...]).astype(BF16)
    ang = pos_ref[0].astype(F32) * inv_ref[...]
    lane_r = lax.broadcasted_iota(jnp.int32, ang.shape, 1)
    cs = jnp.cos(ang)
    sn = jnp.where((lane_r % NSA_DH) < (NSA_DH // 2), -1.0, 1.0) * jnp.sin(ang)

    def proj(name):
        a, b = _SEC[name]
        return _dot(u, w_ref[:, a:b])

    zq_t = (_rope_lanes(proj("q"), cs, sn) * (NSA_DH ** -0.5 * LOG2E)).T
    for h in range(NSA_HEADS):
        q_ref[0, h] = zq_t[h * NSA_DH:(h + 1) * NSA_DH].astype(BF16)
    kvc_ref[0] = proj("kvc")
    zks = _rope_lanes(proj("ks"), cs, sn)
    zkw = _rope_lanes(proj("kw"), cs, sn)
    zvs_t = proj("vs").T
    zvw_t = proj("vw").T
    tm = zks.shape[0]
    lane = lax.broadcasted_iota(jnp.int32, (tm, LANES), 1)
    blk = (pl.program_id(1) * tm + lax.broadcasted_iota(jnp.int32, (tm, LANES), 0)) // SLC_BLOCK
    onehot = jnp.where(lane - NSA_DH == blk, 1.0, 0.0)
    for g in range(NSA_KV_GROUPS):
        sl = slice(g * NSA_DH, (g + 1) * NSA_DH)
        k_front = zks if g == 0 else pltpu.roll(zks, LANES - g * NSA_DH, 1)
        ks_ref[0, g] = jnp.where(lane < NSA_DH, k_front, onehot).astype(BF16)
        kw_ref[0, g] = zkw[:, sl].astype(BF16)
        vs_ref[0, g] = zvs_t[sl].astype(BF16)
        vw_ref[0, g] = zvw_t[sl].astype(BF16)
    gq_ref[0] = proj("gq") * (GLA_DK ** -0.5)
    gk_ref[0] = proj("gk")
    zgv = proj("gv")
    gv_ref[0] = zgv.astype(BF16)
    gvt_ref[0] = zgv.T.astype(BF16)
    gr_ref[0] = proj("gr")
    ma_ref[0] = jax.nn.sigmoid(proj("ma")).astype(ma_ref.dtype)
    mb_ref[0] = jax.nn.sigmoid(proj("mb")).astype(mb_ref.dtype)
    z_nga = proj("nga")
    ng_ref[0] = jax.nn.sigmoid(z_nga).T
    la = jax.nn.log_sigmoid(_dot_3pass(z_nga, wa_ref[...]) + ba_ref[...]) / GLA_TAU
    tri = tri_ref[...]
    gb_ref[0] = jnp.concatenate([_dot_exact_lhs(tri, la[c * GLA_C:(c + 1) * GLA_C])
                                 for c in range(la.shape[0] // GLA_C)], axis=0)


def _in_proj(x, g_mix, w_in, positions, gla_w_a2, gla_b_a):
    B, S, D = x.shape
    inv = jnp.tile(_rope_inv_freq(), LANES // (NSA_DH // 2)).reshape(1, LANES)
    tm = min(IN_TM, S)
    assert tm % GLA_C == 0
    n_gates = 3 * NSA_HEADS
    wa = jnp.pad(gla_w_a2, ((n_gates, LANES - n_gates - GLA_RANK), (0, 0)))
    ba = gla_b_a.reshape(1, -1)
    tri = jnp.asarray(np.tril(np.ones((GLA_C, GLA_C))), BF16)
    splits = np.cumsum((512,) + (128,) * 6 + (24, 256, 256, 512, 16, 512, 1024, 1024))
    (wq, wkc, wvc, wks, wvs, wkw, wvw, wng, wgq, wgk, wgv, wga, wgr, wma, wmb) = jnp.split(
        w_in, splits[:-1].tolist(), axis=1)
    pad = lambda w: jnp.pad(w, ((0, 0), (0, LANES - w.shape[1])))
    w_all = jnp.concatenate([wq, wkc, wvc, wks, wvs, wkw, wvw, wgq, wgk, wgv, wgr, wma, wmb,
                             pad(jnp.concatenate([wng, wga], axis=1))], axis=1).astype(BF16)
    assert w_all.shape[1] == IN_NW
    grid = (B, S // tm)
    tok = lambda w: pl.BlockSpec((1, tm, w), lambda b, i: (b, i, 0))
    head = lambda n, w: pl.BlockSpec((1, n, tm, w), lambda b, i: (b, 0, i, 0))
    head_t = lambda n, w: pl.BlockSpec((1, n, w, tm), lambda b, i: (b, 0, 0, i))
    sds = jax.ShapeDtypeStruct
    out_shape = (
        sds((B, NSA_HEADS, NSA_DH, S), BF16),
        sds((B, S, 256), F32),
        sds((B, NSA_KV_GROUPS, S, LANES), BF16),
        sds((B, NSA_KV_GROUPS, NSA_DH, S), BF16),
        sds((B, NSA_KV_GROUPS, S, NSA_DH), BF16),
        sds((B, NSA_KV_GROUPS, NSA_DH, S), BF16),
        sds((B, S, GLA_HEADS * GLA_DK), F32),
        sds((B, S, GLA_HEADS * GLA_DK), F32),
        sds((B, S, GLA_HEADS * GLA_DV), BF16),
        sds((B, GLA_HEADS * GLA_DV, S), BF16),
        sds((B, S, GLA_HEADS * GLA_DV), F32),
        sds((B, S, D), BF16),
        sds((B, S, D), BF16),
        sds((B, LANES, S), F32),
        sds((B, S, GLA_HEADS * GLA_DK), F32),
    )
    out_specs = (
        head_t(NSA_HEADS, NSA_DH), tok(256),
        head(NSA_KV_GROUPS, LANES), head_t(NSA_KV_GROUPS, NSA_DH),
        head(NSA_KV_GROUPS, NSA_DH), head_t(NSA_KV_GROUPS, NSA_DH),
        tok(GLA_HEADS * GLA_DK), tok(GLA_HEADS * GLA_DK),
        tok(GLA_HEADS * GLA_DV),
        pl.BlockSpec((1, GLA_HEADS * GLA_DV, tm), lambda b, i: (b, 0, i)),
        tok(GLA_HEADS * GLA_DV), tok(D), tok(D),
        pl.BlockSpec((1, LANES, tm), lambda b, i: (b, 0, i)), tok(GLA_HEADS * GLA_DK),
    )
    return pl.pallas_call(
        _in_proj_body,
        grid=grid,
        in_specs=[
            tok(D),
            pl.BlockSpec((1, D), lambda b, i: (0, 0)),
            pl.BlockSpec((D, IN_NW), lambda b, i: (0, 0), pipeline_mode=pl.Buffered(1)),
            tok(1),
            pl.BlockSpec((1, LANES), lambda b, i: (0, 0)),
            pl.BlockSpec(wa.shape, lambda b, i: (0, 0)),
            pl.BlockSpec(ba.shape, lambda b, i: (0, 0)),
            pl.BlockSpec(tri.shape, lambda b, i: (0, 0)),
        ],
        out_specs=out_specs,
        out_shape=out_shape,
        compiler_params=pltpu.CompilerParams(
            dimension_semantics=("parallel", "parallel"), vmem_limit_bytes=VMEM_LIMIT),
        name="in_proj",
    )(x, g_mix.reshape(1, D), w_all, positions.reshape(B, S, 1), inv, wa, ba, tri)


def _compress_body(x_ref, p0_ref, p1_ref, w0_ref, w1_ref, b1_ref, w2_ref, b2_ref, cs_ref, sn_ref,
                   kc_ref, vc_ref, xk_ref, xv_ref):
    n = x_ref.shape[1] // CMP_STRIDE
    half_refs = (xk_ref, xv_ref)
    for s, half in enumerate(half_refs):
        half[...] = x_ref[0, :, s * LANES:(s + 1) * LANES]
    hid = w0_ref.shape[1] // len(half_refs)
    y0 = [None, None]
    y1 = [None, None]
    for t in range(CMP_STRIDE):
        for s, half in enumerate(half_refs):
            xt = half[pl.ds(t, n, stride=CMP_STRIDE), :]
            rows = slice((2 * t + s) * LANES, (2 * t + s + 1) * LANES)
            cols = slice(s * hid, (s + 1) * hid)
            lanes = slice(s * LANES, (s + 1) * LANES)
            d0 = _dot((xt + p0_ref[t:t + 1, lanes]).astype(BF16), w0_ref[rows, cols])
            d1 = _dot((xt + p1_ref[t:t + 1, lanes]).astype(BF16), w1_ref[rows, cols])
            y0[s] = d0 if y0[s] is None else y0[s] + d0
            y1[s] = d1 if y1[s] is None else y1[s] + d1
    y0 = jnp.concatenate(y0, axis=1)
    y1 = jnp.concatenate(y1, axis=1)
    h = jax.nn.gelu(y0 + pltpu.roll(y1, n - 1, 0) + b1_ref[...])
    o = _dot(h.astype(BF16), w2_ref[...]) + b2_ref[...]
    k = _rope_lanes(o[:, :LANES], cs_ref[0], sn_ref[0])
    v_t = o[:, LANES:].T
    for g in range(NSA_KV_GROUPS):
        sl = slice(g * NSA_DH, (g + 1) * NSA_DH)
        kc_ref[0, g] = k[:, sl].astype(BF16)
        vc_ref[0, g] = v_t[sl].astype(BF16)


def _compress(kvc, cs_c, sn_c, pos_k, w1_k, b1_k, w2_k, b2_k, pos_v, w1_v, b1_v, w2_v, b2_v):
    B, S, _ = kvc.shape
    n = S // CMP_STRIDE
    ns = 2 * NSA_KV_GROUPS
    eye = jnp.eye(ns, dtype=F32)
    w1s = jnp.stack([w1_k, w1_k, w1_v, w1_v])
    poss = jnp.stack([pos_k, pos_k, pos_v, pos_v])
    wbig, pbig = [], []
    for m in range(CMP_BLOCK // CMP_STRIDE):
        wm = w1s.reshape(ns, CMP_BLOCK, NSA_DH, CMP_HIDDEN)[:, m * CMP_STRIDE:(m + 1) * CMP_STRIDE]
        wb = jnp.einsum('ctdj,ce->tcdej', wm, eye).reshape(CMP_STRIDE * ns * NSA_DH, ns * CMP_HIDDEN)
        wbig.append(wb.astype(BF16))
        pm = poss[:, m * CMP_STRIDE:(m + 1) * CMP_STRIDE]
        pbig.append(pm.transpose(1, 0, 2).reshape(CMP_STRIDE, ns * NSA_DH))
    b1 = jnp.concatenate([b1_k, b1_k, b1_v, b1_v]).reshape(1, ns * CMP_HIDDEN)
    w2s = jnp.stack([w2_k, w2_k, w2_v, w2_v])
    w2big = jnp.einsum('cjd,ce->cjed', w2s, eye).reshape(ns * CMP_HIDDEN, ns * NSA_DH).astype(BF16)
    b2 = jnp.concatenate([b2_k, b2_k, b2_v, b2_v]).reshape(1, ns * NSA_DH)
    full = lambda a: pl.BlockSpec(a.shape, lambda b: (0,) * a.ndim)
    sds = jax.ShapeDtypeStruct
    return pl.pallas_call(
        _compress_body,
        grid=(B,),
        in_specs=[pl.BlockSpec((1, S, ns * NSA_DH), lambda b: (b, 0, 0)),
                  full(pbig[0]), full(pbig[1]), full(wbig[0]), full(wbig[1]), full(b1), full(w2big), full(b2),
                  pl.BlockSpec((1, n, LANES), lambda b: (b, 0, 0)),
                  pl.BlockSpec((1, n, LANES), lambda b: (b, 0, 0))],
        out_specs=(pl.BlockSpec((1, NSA_KV_GROUPS, n, NSA_DH), lambda b: (b, 0, 0, 0)),
                   pl.BlockSpec((1, NSA_KV_GROUPS, NSA_DH, n), lambda b: (b, 0, 0, 0))),
        out_shape=(sds((B, NSA_KV_GROUPS, n, NSA_DH), BF16), sds((B, NSA_KV_GROUPS, NSA_DH, n), BF16)),
        scratch_shapes=[pltpu.VMEM((S, LANES), F32), pltpu.VMEM((S, LANES), F32)],
        compiler_params=pltpu.CompilerParams(
            dimension_semantics=("parallel",), vmem_limit_bytes=VMEM_LIMIT),
        name="compress",
    )(kvc, pbig[0], pbig[1], wbig[0], wbig[1], b1, w2big, b2, cs_c, sn_c)


NSA_ONES = 16


def _with_ones(v_t):
    return jnp.concatenate([v_t, jnp.ones((NSA_ONES, v_t.shape[1]), v_t.dtype)], axis=0)


def _softmax_step(s, m, acc, v_t):
    m_new = jnp.maximum(m, jnp.max(s, axis=0, keepdims=True))
    p = jnp.exp2(s - m_new).astype(BF16)
    acc = jnp.exp2(m - m_new) * acc + _dot(_with_ones(v_t), p)
    return m_new, acc


def _nsa_groups_body(q_ref, kc_ref, vc_ref, ks_ref, vs_ref, kw_ref, vw_ref, ng_ref, ovt_ref,
                     o_ref, sa_ref, sb_ref, sw_ref, *, tq, tk, n_slc):
    qi = pl.program_id(1)
    q0 = qi * tq
    n_cmp = kc_ref.shape[2]
    W = NSA_HPG * tq
    G = range(NSA_KV_GROUPS)
    per_head = lambda x: jnp.concatenate([x] * NSA_HPG, axis=1)
    q_all = [jnp.concatenate([q_ref[0, g * NSA_HPG + h] for h in range(NSA_HPG)], axis=1) for g in G]
    t_lane = q0 + lax.broadcasted_iota(jnp.int32, (1, tq), 1)

    cmp_end = CMP_STRIDE * lax.broadcasted_iota(jnp.int32, (n_cmp, 1), 0) + (CMP_BLOCK - 1)
    ok = cmp_end <= t_lane
    cmp_bias = per_head(jnp.where(ok, 0.0, NEG_INF))
    cmp_keep = per_head(jnp.where(ok, 1.0, 0.0))
    sm = [_dot(kc_ref[0, g], q_all[g]) + cmp_bias for g in G]
    wk = WINDOW + tq
    w0 = pl.multiple_of(jnp.maximum(q0 - WINDOW, 0), tq)
    kpos = w0 + lax.broadcasted_iota(jnp.int32, (wk, 1), 0)
    win_bias = per_head(jnp.where(jnp.logical_and(kpos <= t_lane, kpos > t_lane - WINDOW), 0.0, NEG_INF))
    for g in G:
        sw_ref[g] = _dot(kw_ref[0, g, pl.ds(w0, wk), :], q_all[g]) + win_bias
    blk = lax.broadcasted_iota(jnp.int32, (n_slc, tq), 0)
    cur = (q0 + lax.broadcasted_iota(jnp.int32, (n_slc, tq), 1)) // SLC_BLOCK
    forced = (blk == 0) | (blk == cur) | (blk == cur - 1)
    o_cmp, imp = [], []
    for g in G:
        e = jnp.exp2(sm[g] - jnp.max(sm[g], axis=0, keepdims=True))
        p = e * (1.0 / jnp.sum(e, axis=0, keepdims=True)) * cmp_keep
        o_cmp.append(_dot(vc_ref[0, g], p.astype(BF16)))
        p_sum = p[:, 0:tq]
        for h in range(1, NSA_HPG):
            p_sum = p_sum + p[:, h * tq:(h + 1) * tq]
        p_hi = p_sum.astype(BF16)
        p_lo = (p_sum - p_hi.astype(F32)).astype(BF16)
        imp_g = _dot(ovt_ref[...], p_hi) + _dot(ovt_ref[...], p_lo)
        imp.append(jnp.where(blk <= cur, jnp.where(forced, FORCED_SCORE, imp_g), NEG_INF))

    SUBL = 8

    def block_bias(n):
        out = []
        for g in G:
            groups = [imp[g][r:r + SUBL] for r in range(0, n, SUBL)]
            ranks = [jnp.zeros((SUBL, tq), F32) for _ in groups]
            for i in range(n):
                row = imp[g][i:i + 1, :]
                for gidx, grp in enumerate(groups):
                    r = gidx * SUBL
                    ge = jnp.where(row >= grp, 1.0, 0.0)
                    gt = jnp.where(row > grp, 1.0, 0.0)
                    if r > i:
                        inc = ge
                    elif r + SUBL - 1 <= i:
                        inc = gt
                    else:
                        inc = jnp.where(blk[r:r + SUBL] > i, ge, gt)
                    ranks[gidx] = ranks[gidx] + inc
            bias = jnp.where(jnp.concatenate(ranks, axis=0) < float(min(SLC_TOPK, n_slc)), 0.0, NEG_INF)
            out.append(jnp.concatenate([bias, jnp.zeros((NSA_DH - n, tq), F32)], axis=0) if n < NSA_DH else bias)
        return tuple(out)

    quarter = n_slc // 4
    visible = (q0 + tq - 1) // SLC_BLOCK + 1
    bias = lax.cond(
        visible <= 2 * quarter,
        lambda: lax.cond(visible <= quarter, lambda: block_bias(quarter), lambda: block_bias(2 * quarter)),
        lambda: lax.cond(visible <= 3 * quarter, lambda: block_bias(3 * quarter), lambda: block_bias(n_slc)))
    q_aug = [jnp.concatenate([q_all[g], per_head(bias[g]).astype(BF16)], axis=0) for g in G]

    kpos_l = lax.broadcasted_iota(jnp.int32, (tk, 1), 0)
    n_tiles = q0 // tk + 1
    last_tile = ks_ref.shape[2] // tk - 1

    def tile_start(jt):
        return pl.multiple_of(jnp.minimum(jt, last_tile) * tk, tk)

    def scores_into(s_ref, jt):
        causal = per_head(jnp.where(jt * tk + kpos_l <= t_lane, 0.0, NEG_INF))
        for g in G:
            s_ref[g] = _dot(ks_ref[0, g, pl.ds(tile_start(jt), tk), :], q_aug[g]) + causal

    def consume(s_ref, jt, carry):
        return tuple(_softmax_step(s_ref[g], *carry[g], vs_ref[0, g, :, pl.ds(tile_start(jt), tk)]) for g in G)

    def slc_pair(i, carry):
        scores_into(sb_ref, 2 * i + 1)
        carry = consume(sa_ref, 2 * i, carry)
        scores_into(sa_ref, 2 * i + 2)
        return consume(sb_ref, 2 * i + 1, carry)

    init = tuple((jnp.full((1, W), NEG_INF, F32), jnp.zeros((NSA_DH + NSA_ONES, W), F32)) for g in G)
    scores_into(sa_ref, 0)

    acc_w = []
    for g in G:
        sw = sw_ref[g]
        ew = jnp.exp2(sw - jnp.max(sw, axis=0, keepdims=True)).astype(BF16)
        acc_w.append(_dot(_with_ones(vw_ref[0, g, :, pl.ds(w0, wk)]), ew))

    slc = lax.fori_loop(0, (n_tiles + 1) // 2, slc_pair, init)

    for g in G:
        def gate(j):
            return jnp.concatenate([ng_ref[0, (g * NSA_HPG + h) * 3 + j:(g * NSA_HPG + h) * 3 + j + 1, :]
                                    for h in range(NSA_HPG)], axis=1)

        acc_s = slc[g][1]
        l_s, l_w = acc_s[NSA_DH:NSA_DH + 1], acc_w[g][NSA_DH:NSA_DH + 1]
        out_t = (gate(0) * o_cmp[g] + (gate(1) * (1.0 / l_s)) * acc_s[:NSA_DH]
                 + (gate(2) * (1.0 / l_w)) * acc_w[g][:NSA_DH])
        out_t = jnp.concatenate([out_t[:, h * tq:(h + 1) * tq] for h in range(NSA_HPG)], axis=0)
        o_ref[0, :, g * NSA_HPG * NSA_DH:(g + 1) * NSA_HPG * NSA_DH] = out_t.T.astype(o_ref.dtype)


def _nsa_attention(q, kc, vc, ks, vs, kw, vw, ng):
    B, H, dh, S = q.shape
    tq = min(ATT_TQ, S)
    tk = min(ATT_TK, S)
    assert S % tk == 0 and tk % tq == 0 and S >= WINDOW + tq and S // SLC_BLOCK <= dh
    n_cmp = kc.shape[2]
    n_slc = S // SLC_BLOCK
    c0 = CMP_STRIDE * np.arange(n_cmp)[None, :]
    s0 = SLC_BLOCK * np.arange(n_slc)[:, None]
    ov = np.clip(np.minimum(c0 + CMP_BLOCK, s0 + SLC_BLOCK) - np.maximum(c0, s0), 0, None) / CMP_BLOCK
    ovt = jnp.asarray(ov, BF16)
    G = NSA_KV_GROUPS
    grid = (B, S // tq)
    k_spec = lambda n, w: pl.BlockSpec((1, G, n, w), lambda b, i: (b, 0, 0, 0))
    vt_spec = lambda n: pl.BlockSpec((1, G, dh, n), lambda b, i: (b, 0, 0, 0))
    return pl.pallas_call(
        functools.partial(_nsa_groups_body, tq=tq, tk=tk, n_slc=n_slc),
        grid=grid,
        in_specs=[
            pl.BlockSpec((1, H, dh, tq), lambda b, i: (b, 0, 0, i)),
            k_spec(n_cmp, dh), vt_spec(n_cmp),
            k_spec(S, 2 * dh), vt_spec(S),
            k_spec(S, dh), vt_spec(S),
            pl.BlockSpec((1, LANES, tq), lambda b, i: (b, 0, i)),
            pl.BlockSpec(ovt.shape, lambda b, i: (0, 0)),
        ],
        out_specs=pl.BlockSpec((1, tq, H * dh), lambda b, i: (b, i, 0)),
        out_shape=jax.ShapeDtypeStruct((B, S, H * dh), BF16),
        scratch_shapes=[pltpu.VMEM((G, tk, NSA_HPG * tq), F32),
                        pltpu.VMEM((G, tk, NSA_HPG * tq), F32),
                        pltpu.VMEM((G, WINDOW + tq, NSA_HPG * tq), F32)],
        compiler_params=pltpu.CompilerParams(
            dimension_semantics=("parallel", "arbitrary"), vmem_limit_bytes=VMEM_LIMIT),
        name="nsa_attn",
    )(q, kc, vc, ks, vs, kw, vw, ng, ovt)


def _dot_3pass(a, b):
    a1 = a.astype(BF16)
    a2 = (a - a1.astype(F32)).astype(BF16)
    b1 = b.astype(BF16)
    b2 = (b - b1.astype(F32)).astype(BF16)
    return _dot(a1, b1) + (_dot(a1, b2) + _dot(a2, b1))


def _gla_pair_body(q_ref, k_ref, v_ref, vt_ref, b_ref, r_ref, ng_ref,
                   o_ref, state_ref, kp_ref, bp_ref, vp_ref, plain_ref, *, n_chunks):
    C, SUB, dk, dv, hp = GLA_C, GLA_SUB, GLA_DK, GLA_DV, GLA_HP
    W = hp * dk
    lane = lax.broadcasted_iota(jnp.int32, (1, W), 1)
    of_head = [lane // dk == hh for hh in range(hp)]
    t_loc = lax.broadcasted_iota(jnp.int32, (C, 1), 0)
    state_ref[...] = jnp.zeros_like(state_ref)
    kp_ref[0:SUB, :] = jnp.zeros((SUB, W), F32)
    bp_ref[0:SUB, :] = jnp.zeros((SUB, W), F32)
    vp_ref[:, 0:SUB, :] = jnp.zeros((hp, SUB, dv), F32)

    def decay_chunk(c, all_plain):
        b_last = b_ref[0, pl.ds(c * C + (C - 1), 1), :]
        plain = (jnp.min(b_last) > -GLA_PLAIN_DECAY).astype(jnp.int32)
        plain_ref[c] = plain
        return jnp.minimum(all_plain, plain)

    all_plain = lax.fori_loop(0, n_chunks, decay_chunk, jnp.int32(1), unroll=GLA_UNROLL_DECAY)

    def chunk(c, check_decay):
        c0 = pl.multiple_of(c * C, C)
        b = b_ref[0, pl.ds(c0, C), :]
        q = q_ref[0, pl.ds(c0, C), :]
        k = k_ref[0, pl.ds(c0, C), :]
        v = [v_ref[0, pl.ds(c0, C), hh * dv:(hh + 1) * dv] for hh in range(hp)]
        vt = [vt_ref[0, hh * dv:(hh + 1) * dv, pl.ds(c0, C)] for hh in range(hp)]
        b_last = b[C - 1:C, :]
        st = state_ref[...]
        st_b = st.astype(BF16)
        qg = q * jnp.exp(b)
        qg_h = [jnp.where(of_head[hh], qg, 0.0).astype(BF16) for hh in range(hp)]
        o_inter = [_dot_nt(qg_h[hh], st_b) for hh in range(hp)]

        def intra_plain():
            ke = (k * jnp.exp(-b)).astype(BF16)
            row = lax.broadcasted_iota(jnp.int32, (C, C), 0)
            col = lax.broadcasted_iota(jnp.int32, (C, C), 1)
            return tuple(_dot(jnp.where(row >= col, _dot_nt(qg_h[hh], ke), 0.0).astype(BF16), v[hh])
                         for hh in range(hp))

        def intra_strong_decay():
            far = [[jnp.zeros((SUB, dv), F32)] for _ in range(hp)]
            for i in range(1, C // SUB):
                r0 = i * SUB
                b_first = b[r0:r0 + 1, :]
                qt = q[r0:r0 + SUB] * jnp.exp(b[r0:r0 + SUB] - b_first)
                kt = (k[:r0] * jnp.exp(b_first - b[:r0])).astype(BF16)
                lt = lax.broadcasted_iota(jnp.int32, (SUB, r0), 0)
                ls = lax.broadcasted_iota(jnp.int32, (SUB, r0), 1)
                for hh in range(hp):
                    a = _dot_nt(jnp.where(of_head[hh], qt, 0.0).astype(BF16), kt)
                    a = jnp.where(lt + (r0 - SUB) >= ls, a, 0.0)
                    far[hh].append(_dot(a.astype(BF16), v[hh][:r0]))
            acc = [jnp.concatenate(far[hh], axis=0) for hh in range(hp)]
            kp_ref[SUB:SUB + C, :] = k
            bp_ref[SUB:SUB + C, :] = b
            for hh in range(hp):
                vp_ref[hh, SUB:SUB + C, :] = v[hh].astype(F32)
            for d in range(SUB):
                kd = kp_ref[SUB - d:SUB - d + C, :]
                bd = bp_ref[SUB - d:SUB - d + C, :]
                valid = t_loc >= d
                w = jnp.exp(jnp.where(valid, b - bd, 0.0))
                x = jnp.where(valid, q * kd * w, 0.0)
                for hh in range(hp):
                    a_d = jnp.sum(jnp.where(of_head[hh], x, 0.0), axis=-1, keepdims=True)
                    acc[hh] = acc[hh] + a_d * vp_ref[hh, SUB - d:SUB - d + C, :]
            return tuple(acc)

        if check_decay:
            o_intra = lax.cond(plain_ref[c] > 0, intra_plain, intra_strong_decay)
        else:
            o_intra = intra_plain()
        k_dec = (k * jnp.exp(b_last - b)).astype(BF16)
        upd = _dot(vt[hp - 1], k_dec)
        for hh in range(hp - 2, -1, -1):
            upd = jnp.where(of_head[hh], _dot(vt[hh], k_dec), upd)
        state_ref[...] = st * jnp.exp(b_last) + upd
        for hh in range(hp):
            o = o_inter[hh] + o_intra[hh]
            o = o * lax.rsqrt(jnp.mean(o * o, axis=-1, keepdims=True) + EPS) * ng_ref[hh]
            gate = jax.nn.silu(r_ref[0, pl.ds(c0, C), hh * dv:(hh + 1) * dv])
            o_ref[0, pl.ds(c0, C), hh * dv:(hh + 1) * dv] = (o * gate).astype(o_ref.dtype)
        return 0

    @pl.when(all_plain > 0)
    def _():
        lax.fori_loop(0, n_chunks, lambda c, _: chunk(c, False), 0, unroll=GLA_UNROLL_PLAIN)

    @pl.when(all_plain <= 0)
    def _():
        lax.fori_loop(0, n_chunks, lambda c, _: chunk(c, True), 0)


def _gla_pairs(gq, gk, gv, gvt, gb, gr, norm_g):
    B, S, _ = gq.shape
    H, dk, dv, hp = GLA_HEADS, GLA_DK, GLA_DV, GLA_HP
    W = hp * dk
    assert W == LANES and H % hp == 0
    ng = norm_g.reshape(H, 1, dv)
    n_chunks = S // GLA_C
    tok_spec = lambda w: pl.BlockSpec((1, S, hp * w), lambda b, h: (b, 0, h))
    return pl.pallas_call(
        functools.partial(_gla_pair_body, n_chunks=n_chunks),
        grid=(B, H // hp),
        in_specs=[tok_spec(dk), tok_spec(dk), tok_spec(dv),
                  pl.BlockSpec((1, hp * dv, S), lambda b, h: (b, h, 0)),
                  tok_spec(dk),
                  tok_spec(dv),
                  pl.BlockSpec((hp, 1, dv), lambda b, h: (h, 0, 0))],
        out_specs=tok_spec(dv),
        out_shape=jax.ShapeDtypeStruct((B, S, H * dv), BF16),
        scratch_shapes=[pltpu.VMEM((dv, W), F32),
                        pltpu.VMEM((GLA_SUB + GLA_C, W), F32),
                        pltpu.VMEM((GLA_SUB + GLA_C, W), F32),
                        pltpu.VMEM((hp, GLA_SUB + GLA_C, dv), F32),
                        pltpu.SMEM((n_chunks,), jnp.int32)],
        compiler_params=pltpu.CompilerParams(
            dimension_semantics=("parallel", "parallel"), vmem_limit_bytes=VMEM_LIMIT),
        name="gla",
    )(gq, gk, gv, gvt, gb, gr, ng)


def _merge_body(x_ref, ya_ref, yb_ref, ma_ref, mb_ref, wpa_ref, wpb_ref, wo_ref, gf_ref, wr_ref, br_ref,
                h_ref, v_ref, comb_ref):
    rows = x_ref.shape[0] // MERGE_PARTS
    part = lambda p: slice(p * rows, (p + 1) * rows)
    wr = wr_ref[...]
    w_hi = wr.astype(BF16)
    w_both = jnp.concatenate([w_hi, (wr - w_hi.astype(F32)).astype(BF16)], axis=1)

    def mix(p):
        sl = part(p)
        y_a = _dot(ya_ref[sl, :], wpa_ref[...])
        y_b = _dot(yb_ref[sl, :], wpb_ref[...])
        mixed = ma_ref[sl, :] * y_a + mb_ref[sl, :] * y_b
        h = x_ref[sl, :] + _dot(mixed.astype(BF16), wo_ref[...])
        h_ref[sl, :] = h
        return h

    def route(p, h):
        sl = part(p)
        v = h * lax.rsqrt(jnp.mean(h * h, axis=-1, keepdims=True) + EPS) * gf_ref[...]
        v_ref[sl, :] = v.astype(BF16)
        v_hi = v.astype(BF16)
        v_lo = (v - v_hi.astype(F32)).astype(BF16)
        both = _dot(v_hi, w_both)
        logits = (both[:, :LANES] + both[:, LANES:]) + _dot(v_lo, w_hi) + br_ref[...]
        comb_ref[sl, :] = _route_weights(logits)

    h_prev = mix(0)
    for p in range(1, MERGE_PARTS):
        h_next = mix(p)
        route(p - 1, h_prev)
        h_prev = h_next
    route(MERGE_PARTS - 1, h_prev)


def _route_weights(logits):
    lane = lax.broadcasted_iota(jnp.int32, logits.shape, 1)
    is_grp = jnp.logical_and(lane >= MOE_EXPERTS, lane < MOE_EXPERTS + MOE_GROUPS)
    lg = jnp.where(is_grp, logits, NEG_INF)
    eg = jnp.where(is_grp, jnp.exp(lg - jnp.max(lg, axis=-1, keepdims=True)), 0.0)
    pg = eg / jnp.sum(eg, axis=-1, keepdims=True)
    p_grp = jnp.max(pg, axis=-1, keepdims=True)
    g_sel = jnp.min(jnp.where(jnp.logical_and(is_grp, pg == p_grp), lane, 2 * LANES),
                    axis=-1, keepdims=True) - MOE_EXPERTS
    in_grp = jnp.logical_and(lane < MOE_EXPERTS, lane // MOE_EPG == g_sel)
    le = jnp.where(in_grp, logits, NEG_INF)
    ee = jnp.where(in_grp, jnp.exp(le - jnp.max(le, axis=-1, keepdims=True)), 0.0)
    pin = ee / jnp.sum(ee, axis=-1, keepdims=True)
    p1 = jnp.max(jnp.where(in_grp, pin, -1.0), axis=-1, keepdims=True)
    i1 = jnp.min(jnp.where(jnp.logical_and(in_grp, pin == p1), lane, 2 * LANES), axis=-1, keepdims=True)
    rest = jnp.logical_and(in_grp, lane != i1)
    p2 = jnp.max(jnp.where(rest, pin, -1.0), axis=-1, keepdims=True)
    i2 = jnp.min(jnp.where(jnp.logical_and(rest, pin == p2), lane, 2 * LANES), axis=-1, keepdims=True)
    tot = p1 + p2
    return jnp.where(lane == i1, p_grp * p1 / tot, 0.0) + jnp.where(lane == i2, p_grp * p2 / tot, 0.0)


def _merge(x2, ya, yb, ma, mb, w_proj_nsa, w_proj_gla, w_out, g_ffn, w_grp, b_grp, w_exp, b_exp):
    T, D = x2.shape
    tm = min(MERGE_TM, T)
    wr = jnp.pad(jnp.concatenate([w_exp, w_grp], axis=1), ((0, 0), (0, LANES - MOE_EXPERTS - MOE_GROUPS)))
    br = jnp.pad(jnp.concatenate([b_exp, b_grp]), (0, LANES - MOE_EXPERTS - MOE_GROUPS)).reshape(1, LANES)
    tok = lambda w: pl.BlockSpec((tm, w), lambda i: (i, 0))
    full = lambda a: pl.BlockSpec(a.shape, lambda i: (0, 0))
    wpa, wpb, wo = w_proj_nsa.astype(BF16), w_proj_gla.astype(BF16), w_out.astype(BF16)
    gf = g_ffn.reshape(1, D)
    sds = jax.ShapeDtypeStruct
    return pl.pallas_call(
        _merge_body,
        grid=(T // tm,),
        in_specs=[tok(D), tok(ya.shape[1]), tok(yb.shape[1]), tok(D), tok(D),
                  full(wpa), full(wpb), full(wo), full(gf), full(wr), full(br)],
        out_specs=(tok(D), tok(D), tok(LANES)),
        out_shape=(sds((T, D), F32), sds((T, D), BF16), sds((T, LANES), F32)),
        compiler_params=pltpu.CompilerParams(
            dimension_semantics=("parallel",), vmem_limit_bytes=VMEM_LIMIT),
        name="merge",
    )(x2, ya, yb, ma, mb, wpa, wpb, wo, gf, wr, br)


def _moe_rows(tw):
    rows = 2 * tw + MOE_EXPERTS * (MOE_ALIGN - 1) + MOE_RT
    return -(-rows // MOE_RT) * MOE_RT


def _moe_window(e, phase, v_ref, comb_ref, tri_ref, y_ref, xs_ref, z_ref, meta_ref, pos_ref, tw, rmax):
    no_row = -1.0

    @pl.when(e == 0 if phase == "route" else False)
    def _route():
        comb = comb_ref[...]
        assigned = comb > 0.0
        a = jnp.where(assigned, 1.0, 0.0)
        tri = tri_ref[...]
        run = jnp.zeros((1, LANES), F32)
        ranks = []
        for b in range(tw // MOE_RT):
            ab = a[b * MOE_RT:(b + 1) * MOE_RT]
            ranks.append(_dot(tri, ab.astype(BF16)) + run)
            run = run + jnp.sum(ab, axis=0, keepdims=True)
        rank = jnp.concatenate(ranks, axis=0)
        cnt_pad = jnp.floor((run + (MOE_ALIGN - 1)) * (1.0 / MOE_ALIGN)) * MOE_ALIGN
        incl = jnp.broadcast_to(cnt_pad, (8, LANES))
        lane8 = lax.broadcasted_iota(jnp.int32, (8, LANES), 1)
        shift = 1
        while shift < MOE_EXPERTS:
            incl = incl + jnp.where(lane8 >= shift, pltpu.roll(incl, shift, 1), 0.0)
            shift *= 2
        offs = incl[0:1] - cnt_pad
        meta_ref[0:1, :] = offs
        meta_ref[1:2, :] = run
        row_of = offs + rank
        pos_a = jnp.min(jnp.where(assigned, row_of, 1e9), axis=-1, keepdims=True)
        pos_b = jnp.max(jnp.where(assigned, row_of, no_row), axis=-1, keepdims=True)
        pos_a = jnp.where(pos_a > 1e8, no_row, pos_a)
        pos_b = jnp.where(pos_b == pos_a, no_row, pos_b)
        w_a = jnp.sum(jnp.where(jnp.logical_and(assigned, row_of == pos_a), comb, 0.0), axis=-1, keepdims=True)
        w_b = jnp.sum(jnp.where(jnp.logical_and(assigned, row_of == pos_b), comb, 0.0), axis=-1, keepdims=True)
        lane_t = lax.broadcasted_iota(jnp.int32, (tw, LANES), 1)
        pos_ref[...] = jnp.where(lane_t == 0, pos_a, jnp.where(lane_t == 1, pos_b, jnp.where(
            lane_t == 2, w_a, jnp.where(lane_t == 3, w_b, no_row))))
        pos_t = pos_ref[...].T
        pa, pb = pos_t[0:1], pos_t[1:2]
        v = v_ref[...]
        for rt in range(rmax // MOE_RT):
            r = (rt * MOE_RT + lax.broadcasted_iota(jnp.int32, (MOE_RT, 1), 0)).astype(F32)
            p = jnp.where(r == pa, 1.0, jnp.where(r == pb, 1.0, 0.0)).astype(BF16)
            xs_ref[rt * MOE_RT:(rt + 1) * MOE_RT, :] = _dot(p, v).astype(BF16)
        z_ref[...] = jnp.zeros_like(z_ref)

    @pl.when(e == pl.num_programs(1) - 1 if phase == "combine" else False)
    def _combine():
        r = lax.broadcasted_iota(jnp.int32, (1, rmax), 1).astype(F32)
        z = z_ref[...]
        for tt in range(tw // MOE_RT):
            rows = slice(tt * MOE_RT, (tt + 1) * MOE_RT)
            pa, pb = pos_ref[rows, 0:1], pos_ref[rows, 1:2]
            w_a, w_b = pos_ref[rows, 2:3], pos_ref[rows, 3:4]
            q = jnp.where(r == pa, w_a, jnp.where(r == pb, w_b, 0.0)).astype(BF16)
            y_ref[rows, :] = _dot(q, z).astype(y_ref.dtype)


def _moe_body(v_ref, comb_ref, wg_ref, wu_ref, wd_ref, tri_ref, y_ref,
              xs_ref, z_ref, meta_ref, pos_ref, *, tw, rmax, n_win):
    e = pl.program_id(1)
    lane = lax.broadcasted_iota(jnp.int32, (1, LANES), 1)

    def window_phase(phase):
        for w in range(n_win):
            rows = pl.ds(w * tw, tw)
            _moe_window(e, phase, v_ref.at[rows], comb_ref.at[rows], tri_ref, y_ref.at[rows],
                        xs_ref.at[w], z_ref.at[w], meta_ref.at[w], pos_ref.at[w], tw, rmax)

    window_phase("route")

    pick = lambda w, row: jnp.sum(jnp.where(lane == e, meta_ref[w, row:row + 1, :], 0.0)).astype(jnp.int32)
    offs = [pick(w, 0) for w in range(n_win)]
    tiles = [(pick(w, 1) + MOE_ET - 1) // MOE_ET for w in range(n_win)]

    def expert(x):
        hdn = jax.nn.silu(_dot(x, wg_ref[0])) * _dot(x, wu_ref[0])
        return _dot(hdn.astype(BF16), wd_ref[0]).astype(BF16)

    def joint_tile(i, _):
        r0 = [pl.multiple_of(offs[w] + i * MOE_ET, MOE_ALIGN) for w in range(n_win)]
        z = expert(jnp.concatenate([xs_ref[w, pl.ds(r0[w], MOE_ET), :] for w in range(n_win)], axis=0))
        for w in range(n_win):
            z_ref[w, pl.ds(r0[w], MOE_ET), :] = z[w * MOE_ET:(w + 1) * MOE_ET]
        return 0

    n_joint = functools.reduce(jnp.minimum, tiles)
    lax.fori_loop(0, n_joint, joint_tile, 0)
    for w in range(n_win):
        def own_tile(i, _, w=w):
            r0 = pl.multiple_of(offs[w] + i * MOE_ET, MOE_ALIGN)
            z_ref[w, pl.ds(r0, MOE_ET), :] = expert(xs_ref[w, pl.ds(r0, MOE_ET), :])
            return 0

        lax.fori_loop(n_joint, tiles[w], own_tile, 0)

    window_phase("combine")


def _moe(v, comb, w_gate, w_up, w_down):
    T, D = v.shape
    tw = min(MOE_TW, T)
    n_win = min(MOE_WINDOWS, T // tw)
    rmax = _moe_rows(tw)
    E, _, F = w_gate.shape
    wg, wu, wd = w_gate.astype(BF16), w_up.astype(BF16), w_down.astype(BF16)
    tri = jnp.asarray(np.tril(np.ones((MOE_RT, MOE_RT)), -1), BF16)
    tok = lambda w: pl.BlockSpec((n_win * tw, w), lambda i, e: (i, 0))
    tok_in = lambda w: pl.BlockSpec((n_win * tw, w), lambda i, e: (i, 0), pipeline_mode=pl.Buffered(1))
    return pl.pallas_call(
        functools.partial(_moe_body, tw=tw, rmax=rmax, n_win=n_win),
        grid=(T // (n_win * tw), E),
        in_specs=[tok_in(D), tok_in(LANES),
                  pl.BlockSpec((1, D, F), lambda i, e: (e, 0, 0)),
                  pl.BlockSpec((1, D, F), lambda i, e: (e, 0, 0)),
                  pl.BlockSpec((1, F, D), lambda i, e: (e, 0, 0)),
                  pl.BlockSpec(tri.shape, lambda i, e: (0, 0))],
        out_specs=tok(D),
        out_shape=jax.ShapeDtypeStruct((T, D), BF16),
        scratch_shapes=[pltpu.VMEM((n_win, rmax, D), BF16),
                        pltpu.VMEM((n_win, rmax, D), BF16),
                        pltpu.VMEM((n_win, 8, LANES), F32),
                        pltpu.VMEM((n_win, tw, LANES), F32)],
        compiler_params=pltpu.CompilerParams(
            dimension_semantics=("parallel", "arbitrary"), vmem_limit_bytes=VMEM_LIMIT),
        name="moe",
    )(v, comb, wg, wu, wd, tri)


def _final_body(h_ref, y_ref, g_ref, o_ref):
    h = h_ref[...] + y_ref[...].astype(F32)
    o_ref[...] = h * lax.rsqrt(jnp.mean(h * h, axis=-1, keepdims=True) + EPS) * g_ref[...]


def _final_norm(h, y, g_final):
    T, D = h.shape
    tm = min(FINAL_TM, T)
    tok = pl.BlockSpec((tm, D), lambda i: (i, 0))
    return pl.pallas_call(
        _final_body,
        grid=(T // tm,),
        in_specs=[tok, tok, pl.BlockSpec((1, D), lambda i: (0, 0))],
        out_specs=tok,
        out_shape=jax.ShapeDtypeStruct((T, D), F32),
        compiler_params=pltpu.CompilerParams(
            dimension_semantics=("parallel",), vmem_limit_bytes=VMEM_LIMIT),
        name="final_norm",
    )(h, y, g_final.reshape(1, D))


def _rope_inv_freq():
    half = NSA_DH // 2
    return 1.0 / (ROPE_THETA ** (jnp.arange(half, dtype=F32) / half))


def _rope_tables(positions):
    ang = positions.astype(F32)[..., None] * _rope_inv_freq()
    cos, sin = jnp.cos(ang), jnp.sin(ang)
    cs = jnp.concatenate([cos, cos, cos, cos], axis=-1)
    sn = jnp.concatenate([-sin, sin, -sin, sin], axis=-1)
    return cs, sn


def _layer(h, positions, g_mix, w_in, cmp_pos_k, cmp_w1_k, cmp_b1_k, cmp_w2_k, cmp_b2_k,
           cmp_pos_v, cmp_w1_v, cmp_b1_v, cmp_w2_v, cmp_b2_v, gla_w_a2, gla_b_a, gla_norm_g,
           w_proj_nsa, w_proj_gla, w_out, g_ffn, w_grp, b_grp, w_exp, b_exp, w_gate, w_up, w_down, g_out):
    B, S, D = h.shape
    n_chunks = S // CMP_STRIDE
    cmp_end = jnp.minimum(CMP_STRIDE * jnp.arange(n_chunks) + CMP_BLOCK - 1, S - 1)
    cs_c, sn_c = _rope_tables(jnp.take(positions, cmp_end, axis=1))
    (q, kvc, ks, vs, kw, vw, gq, gk, gv, gvt, gr, ma, mb, ng, gb) = _in_proj(h, g_mix, w_in, positions,
                                                                             gla_w_a2, gla_b_a)
    kc, vc = _compress(kvc, cs_c, sn_c, cmp_pos_k, cmp_w1_k, cmp_b1_k, cmp_w2_k, cmp_b2_k,
                       cmp_pos_v, cmp_w1_v, cmp_b1_v, cmp_w2_v, cmp_b2_v)
    ya = _nsa_attention(q, kc, vc, ks, vs, kw, vw, ng)
    yb = _gla_pairs(gq, gk, gv, gvt, gb, gr, gla_norm_g)
    T = B * S
    h1, v, comb = _merge(h.reshape(T, D), ya.reshape(T, -1), yb.reshape(T, -1), ma.reshape(T, D), mb.reshape(T, D),
                         w_proj_nsa, w_proj_gla, w_out, g_ffn, w_grp, b_grp, w_exp, b_exp)
    y = _moe(v, comb, w_gate, w_up, w_down)
    return _final_norm(h1, y, g_out).reshape(B, S, D)


def kernel(x, positions, g_mix, w_in, cmp_pos_k, cmp_w1_k, cmp_b1_k, cmp_w2_k, cmp_b2_k, cmp_pos_v, cmp_w1_v,
           cmp_b1_v, cmp_w2_v, cmp_b2_v, gla_w_a2, gla_b_a, gla_norm_g, w_proj_nsa, w_proj_gla, w_out, g_ffn,
           w_grp, b_grp, w_exp, b_exp, w_gate, w_up, w_down, g_final):
    depth = g_mix.shape[0]
    assert depth == 1, "the final norm closes the single layer"
    return _layer(x, positions, g_mix[0], w_in[0], cmp_pos_k[0], cmp_w1_k[0], cmp_b1_k[0], cmp_w2_k[0],
                  cmp_b2_k[0], cmp_pos_v[0], cmp_w1_v[0], cmp_b1_v[0], cmp_w2_v[0], cmp_b2_v[0],
                  gla_w_a2[0], gla_b_a[0], gla_norm_g[0], w_proj_nsa[0], w_proj_gla[0], w_out[0],
                  g_ffn[0], w_grp[0], b_grp[0], w_exp[0], b_exp[0], w_gate[0], w_up[0], w_down[0], g_final)
```

```python
import functools

import numpy as np
import jax
import jax.numpy as jnp
from jax import lax
from jax.experimental import pallas as pl
from jax.experimental.pallas import tpu as pltpu

F32 = jnp.float32
BF16 = jnp.bfloat16

NSA_HEADS = 8
NSA_KV_GROUPS = 2
NSA_HPG = NSA_HEADS // NSA_KV_GROUPS
NSA_DH = 64
CMP_BLOCK = 32
CMP_STRIDE = 16
CMP_HIDDEN = 256
SLC_BLOCK = 64
SLC_TOPK = 16
WINDOW = 512
GLA_HEADS = 4
GLA_DK = 64
GLA_DV = 128
GLA_RANK = 16
GLA_TAU = 16.0
MOE_GROUPS = 4
MOE_EPG = 8
MOE_EXPERTS = MOE_GROUPS * MOE_EPG
ROPE_THETA = 10000.0
EPS = 1e-6
NEG_INF = -1e30
FORCED_SCORE = 1e4
LOG2E = 1.4426950408889634

LANES = 128
VMEM_LIMIT = 56 * 1024 * 1024

IN_TM = 512
ATT_TQ = 128
ATT_TK = 256
GLA_C = 128
GLA_SUB = 16
GLA_HP = 2
GLA_UNROLL_DECAY = 4
GLA_UNROLL_PLAIN = 8
GLA_PLAIN_DECAY = 60.0
MERGE_TM = 512
MOE_TW = 1024
MOE_WINDOWS = 2
FINAL_TM = 1024
MOE_RT = 256
MOE_ET = 80
MOE_ALIGN = 16


def _dot(a, b):
    return jnp.dot(a, b, preferred_element_type=F32)


def _dot_nt(a, b):
    return lax.dot_general(a, b, (((1,), (1,)), ((), ())), preferred_element_type=F32)


def _split3(x):
    x1 = x.astype(BF16)
    r1 = x - x1.astype(F32)
    x2 = r1.astype(BF16)
    r2 = r1 - x2.astype(F32)
    x3 = r2.astype(BF16)
    return x1, x2, x3


def _dot_exact_lhs(a_bf16, x):
    x1, x2, x3 = _split3(x)
    return _dot(a_bf16, x1) + _dot(a_bf16, x2) + _dot(a_bf16, x3)


def _rope_lanes(z, cs, sn):
    w = z.shape[-1]
    lane = lax.broadcasted_iota(jnp.int32, z.shape, 1)
    first_half = (lane % NSA_DH) < (NSA_DH // 2)
    rot = jnp.where(first_half, pltpu.roll(z, w - NSA_DH // 2, 1), pltpu.roll(z, NSA_DH // 2, 1))
    reps = w // LANES
    if reps > 1:
        cs = jnp.concatenate([cs] * reps, axis=1)
        sn = jnp.concatenate([sn] * reps, axis=1)
    return z * cs + rot * sn


_SEC = {}
_off = 0
for _name, _w in (("q", 512), ("kvc", 256), ("ks", 128), ("vs", 128), ("kw", 128), ("vw", 128),
                  ("gq", 256), ("gk", 256), ("gv", 512), ("gr", 512), ("ma", 1024), ("mb", 1024),
                  ("nga", 128)):
    _SEC[_name] = (_off, _off + _w)
    _off += _w
IN_NW = _off


def _in_proj_body(x_ref, g_ref, w_ref, pos_ref, inv_ref, wa_ref, ba_ref, tri_ref,
                  q_ref, kvc_ref, ks_ref, vs_ref, kw_ref, vw_ref,
                  gq_ref, gk_ref, gv_ref, gvt_ref, gr_ref, ma_ref, mb_ref, ng_ref, gb_ref):
    x = x_ref[0]
    var = jnp.mean(x * x, axis=-1, keepdims=True)
    u = (x * lax.rsqrt(var + EPS) * g_ref[...]).astype(BF16)
    ang = pos_ref[0].astype(F32) * inv_ref[...]
    lane_r = lax.broadcasted_iota(jnp.int32, ang.shape, 1)
    cs = jnp.cos(ang)
    sn = jnp.where((lane_r % NSA_DH) < (NSA_DH // 2), -1.0, 1.0) * jnp.sin(ang)

    def proj(name):
        a, b = _SEC[name]
        return _dot(u, w_ref[:, a:b])

    zq_t = (_rope_lanes(proj("q"), cs, sn) * (NSA_DH ** -0.5 * LOG2E)).T
    for h in range(NSA_HEADS):
        q_ref[0, h] = zq_t[h * NSA_DH:(h + 1) * NSA_DH].astype(BF16)
    kvc_ref[0] = proj("kvc")
    zks = _rope_lanes(proj("ks"), cs, sn)
    zkw = _rope_lanes(proj("kw"), cs, sn)
    zvs_t = proj("vs").T
    zvw_t = proj("vw").T
    tm = zks.shape[0]
    lane = lax.broadcasted_iota(jnp.int32, (tm, LANES), 1)
    blk = (pl.program_id(1) * tm + lax.broadcasted_iota(jnp.int32, (tm, LANES), 0)) // SLC_BLOCK
    onehot = jnp.where(lane - NSA_DH == blk, 1.0, 0.0)
    for g in range(NSA_KV_GROUPS):
        sl = slice(g * NSA_DH, (g + 1) * NSA_DH)
        k_front = zks if g == 0 else pltpu.roll(zks, LANES - g * NSA_DH, 1)
        ks_ref[0, g] = jnp.where(lane < NSA_DH, k_front, onehot).astype(BF16)
        kw_ref[0, g] = zkw[:, sl].astype(BF16)
        vs_ref[0, g] = zvs_t[sl].astype(BF16)
        vw_ref[0, g] = zvw_t[sl].astype(BF16)
    gq_ref[0] = proj("gq") * (GLA_DK ** -0.5)
    gk_ref[0] = proj("gk")
    zgv = proj("gv")
    gv_ref[0] = zgv.astype(BF16)
    gvt_ref[0] = zgv.T.astype(BF16)
    gr_ref[0] = proj("gr")
    ma_ref[0] = jax.nn.sigmoid(proj("ma")).astype(ma_ref.dtype)
    mb_ref[0] = jax.nn.sigmoid(proj("mb")).astype(mb_ref.dtype)
    z_nga = proj("nga")
    ng_ref[0] = jax.nn.sigmoid(z_nga).T
    la = jax.nn.log_sigmoid(_dot_3pass(z_nga, wa_ref[...]) + ba_ref[...]) / GLA_TAU
    tri = tri_ref[...]
    gb_ref[0] = jnp.concatenate([_dot_exact_lhs(tri, la[c * GLA_C:(c + 1) * GLA_C])
                                 for c in range(la.shape[0] // GLA_C)], axis=0)


def _in_proj(x, g_mix, w_in, positions, gla_w_a2, gla_b_a):
    B, S, D = x.shape
    inv = jnp.tile(_rope_inv_freq(), LANES // (NSA_DH // 2)).reshape(1, LANES)
    tm = min(IN_TM, S)
    assert tm % GLA_C == 0
    n_gates = 3 * NSA_HEADS
    wa = jnp.pad(gla_w_a2, ((n_gates, LANES - n_gates - GLA_RANK), (0, 0)))
    ba = gla_b_a.reshape(1, -1)
    tri = jnp.asarray(np.tril(np.ones((GLA_C, GLA_C))), BF16)
    splits = np.cumsum((512,) + (128,) * 6 + (24, 256, 256, 512, 16, 512, 1024, 1024))
    (wq, wkc, wvc, wks, wvs, wkw, wvw, wng, wgq, wgk, wgv, wga, wgr, wma, wmb) = jnp.split(
        w_in, splits[:-1].tolist(), axis=1)
    pad = lambda w: jnp.pad(w, ((0, 0), (0, LANES - w.shape[1])))
    w_all = jnp.concatenate([wq, wkc, wvc, wks, wvs, wkw, wvw, wgq, wgk, wgv, wgr, wma, wmb,
                             pad(jnp.concatenate([wng, wga], axis=1))], axis=1).astype(BF16)
    assert w_all.shape[1] == IN_NW
    grid = (B, S // tm)
    tok = lambda w: pl.BlockSpec((1, tm, w), lambda b, i: (b, i, 0))
    head = lambda n, w: pl.BlockSpec((1, n, tm, w), lambda b, i: (b, 0, i, 0))
    head_t = lambda n, w: pl.BlockSpec((1, n, w, tm), lambda b, i: (b, 0, 0, i))
    sds = jax.ShapeDtypeStruct
    out_shape = (
        sds((B, NSA_HEADS, NSA_DH, S), BF16),
        sds((B, S, 256), F32),
        sds((B, NSA_KV_GROUPS, S, LANES), BF16),
        sds((B, NSA_KV_GROUPS, NSA_DH, S), BF16),
        sds((B, NSA_KV_GROUPS, S, NSA_DH), BF16),
        sds((B, NSA_KV_GROUPS, NSA_DH, S), BF16),
        sds((B, S, GLA_HEADS * GLA_DK), F32),
        sds((B, S, GLA_HEADS * GLA_DK), F32),
        sds((B, S, GLA_HEADS * GLA_DV), BF16),
        sds((B, GLA_HEADS * GLA_DV, S), BF16),
        sds((B, S, GLA_HEADS * GLA_DV), F32),
        sds((B, S, D), BF16),
        sds((B, S, D), BF16),
        sds((B, LANES, S), F32),
        sds((B, S, GLA_HEADS * GLA_DK), F32),
    )
    out_specs = (
        head_t(NSA_HEADS, NSA_DH), tok(256),
        head(NSA_KV_GROUPS, LANES), head_t(NSA_KV_GROUPS, NSA_DH),
        head(NSA_KV_GROUPS, NSA_DH), head_t(NSA_KV_GROUPS, NSA_DH),
        tok(GLA_HEADS * GLA_DK), tok(GLA_HEADS * GLA_DK),
        tok(GLA_HEADS * GLA_DV),
        pl.BlockSpec((1, GLA_HEADS * GLA_DV, tm), lambda b, i: (b, 0, i)),
        tok(GLA_HEADS * GLA_DV), tok(D), tok(D),
        pl.BlockSpec((1, LANES, tm), lambda b, i: (b, 0, i)), tok(GLA_HEADS * GLA_DK),
    )
    return pl.pallas_call(
        _in_proj_body,
        grid=grid,
        in_specs=[
            tok(D),
            pl.BlockSpec((1, D), lambda b, i: (0, 0)),
            pl.BlockSpec((D, IN_NW), lambda b, i: (0, 0), pipeline_mode=pl.Buffered(1)),
            tok(1),
            pl.BlockSpec((1, LANES), lambda b, i: (0, 0)),
            pl.BlockSpec(wa.shape, lambda b, i: (0, 0)),
            pl.BlockSpec(ba.shape, lambda b, i: (0, 0)),
            pl.BlockSpec(tri.shape, lambda b, i: (0, 0)),
        ],
        out_specs=out_specs,
        out_shape=out_shape,
        compiler_params=pltpu.CompilerParams(
            dimension_semantics=("parallel", "parallel"), vmem_limit_bytes=VMEM_LIMIT),
        name="in_proj",
    )(x, g_mix.reshape(1, D), w_all, positions.reshape(B, S, 1), inv, wa, ba, tri)


def _compress_body(x_ref, p0_ref, p1_ref, w0_ref, w1_ref, b1_ref, w2_ref, b2_ref, cs_ref, sn_ref,
                   kc_ref, vc_ref, xk_ref, xv_ref):
    n = x_ref.shape[1] // CMP_STRIDE
    half_refs = (xk_ref, xv_ref)
    for s, half in enumerate(half_refs):
        half[...] = x_ref[0, :, s * LANES:(s + 1) * LANES]
    hid = w0_ref.shape[1] // len(half_refs)
    y0 = [None, None]
    y1 = [None, None]
    for t in range(CMP_STRIDE):
        for s, half in enumerate(half_refs):
            xt = half[pl.ds(t, n, stride=CMP_STRIDE), :]
            rows = slice((2 * t + s) * LANES, (2 * t + s + 1) * LANES)
            cols = slice(s * hid, (s + 1) * hid)
            lanes = slice(s * LANES, (s + 1) * LANES)
            d0 = _dot((xt + p0_ref[t:t + 1, lanes]).astype(BF16), w0_ref[rows, cols])
            d1 = _dot((xt + p1_ref[t:t + 1, lanes]).astype(BF16), w1_ref[rows, cols])
            y0[s] = d0 if y0[s] is None else y0[s] + d0
            y1[s] = d1 if y1[s] is None else y1[s] + d1
    y0 = jnp.concatenate(y0, axis=1)
    y1 = jnp.concatenate(y1, axis=1)
    h = jax.nn.gelu(y0 + pltpu.roll(y1, n - 1, 0) + b1_ref[...])
    o = _dot(h.astype(BF16), w2_ref[...]) + b2_ref[...]
    k = _rope_lanes(o[:, :LANES], cs_ref[0], sn_ref[0])
    v_t = o[:, LANES:].T
    for g in range(NSA_KV_GROUPS):
        sl = slice(g * NSA_DH, (g + 1) * NSA_DH)
        kc_ref[0, g] = k[:, sl].astype(BF16)
        vc_ref[0, g] = v_t[sl].astype(BF16)


def _compress(kvc, cs_c, sn_c, pos_k, w1_k, b1_k, w2_k, b2_k, pos_v, w1_v, b1_v, w2_v, b2_v):
    B, S, _ = kvc.shape
    n = S // CMP_STRIDE
    ns = 2 * NSA_KV_GROUPS
    eye = jnp.eye(ns, dtype=F32)
    w1s = jnp.stack([w1_k, w1_k, w1_v, w1_v])
    poss = jnp.stack([pos_k, pos_k, pos_v, pos_v])
    wbig, pbig = [], []
    for m in range(CMP_BLOCK // CMP_STRIDE):
        wm = w1s.reshape(ns, CMP_BLOCK, NSA_DH, CMP_HIDDEN)[:, m * CMP_STRIDE:(m + 1) * CMP_STRIDE]
        wb = jnp.einsum('ctdj,ce->tcdej', wm, eye).reshape(CMP_STRIDE * ns * NSA_DH, ns * CMP_HIDDEN)
        wbig.append(wb.astype(BF16))
        pm = poss[:, m * CMP_STRIDE:(m + 1) * CMP_STRIDE]
        pbig.append(pm.transpose(1, 0, 2).reshape(CMP_STRIDE, ns * NSA_DH))
    b1 = jnp.concatenate([b1_k, b1_k, b1_v, b1_v]).reshape(1, ns * CMP_HIDDEN)
    w2s = jnp.stack([w2_k, w2_k, w2_v, w2_v])
    w2big = jnp.einsum('cjd,ce->cjed', w2s, eye).reshape(ns * CMP_HIDDEN, ns * NSA_DH).astype(BF16)
    b2 = jnp.concatenate([b2_k, b2_k, b2_v, b2_v]).reshape(1, ns * NSA_DH)
    full = lambda a: pl.BlockSpec(a.shape, lambda b: (0,) * a.ndim)
    sds = jax.ShapeDtypeStruct
    return pl.pallas_call(
        _compress_body,
        grid=(B,),
        in_specs=[pl.BlockSpec((1, S, ns * NSA_DH), lambda b: (b, 0, 0)),
                  full(pbig[0]), full(pbig[1]), full(wbig[0]), full(wbig[1]), full(b1), full(w2big), full(b2),
                  pl.BlockSpec((1, n, LANES), lambda b: (b, 0, 0)),
                  pl.BlockSpec((1, n, LANES), lambda b: (b, 0, 0))],
        out_specs=(pl.BlockSpec((1, NSA_KV_GROUPS, n, NSA_DH), lambda b: (b, 0, 0, 0)),
                   pl.BlockSpec((1, NSA_KV_GROUPS, NSA_DH, n), lambda b: (b, 0, 0, 0))),
        out_shape=(sds((B, NSA_KV_GROUPS, n, NSA_DH), BF16), sds((B, NSA_KV_GROUPS, NSA_DH, n), BF16)),
        scratch_shapes=[pltpu.VMEM((S, LANES), F32), pltpu.VMEM((S, LANES), F32)],
        compiler_params=pltpu.CompilerParams(
            dimension_semantics=("parallel",), vmem_limit_bytes=VMEM_LIMIT),
        name="compress",
    )(kvc, pbig[0], pbig[1], wbig[0], wbig[1], b1, w2big, b2, cs_c, sn_c)


NSA_ONES = 16


def _with_ones(v_t):
    return jnp.concatenate([v_t, jnp.ones((NSA_ONES, v_t.shape[1]), v_t.dtype)], axis=0)


def _softmax_step(s, m, acc, v_t):
    m_new = jnp.maximum(m, jnp.max(s, axis=0, keepdims=True))
    p = jnp.exp2(s - m_new).astype(BF16)
    acc = jnp.exp2(m - m_new) * acc + _dot(_with_ones(v_t), p)
    return m_new, acc


def _nsa_groups_body(q_ref, kc_ref, vc_ref, ks_ref, vs_ref, kw_ref, vw_ref, ng_ref, ovt_ref,
                     o_ref, sa_ref, sb_ref, sw_ref, *, tq, tk, n_slc):
    qi = pl.program_id(1)
    q0 = qi * tq
    n_cmp = kc_ref.shape[2]
    W = NSA_HPG * tq
    G = range(NSA_KV_GROUPS)
    per_head = lambda x: jnp.concatenate([x] * NSA_HPG, axis=1)
    q_all = [jnp.concatenate([q_ref[0, g * NSA_HPG + h] for h in range(NSA_HPG)], axis=1) for g in G]
    t_lane = q0 + lax.broadcasted_iota(jnp.int32, (1, tq), 1)

    cmp_end = CMP_STRIDE * lax.broadcasted_iota(jnp.int32, (n_cmp, 1), 0) + (CMP_BLOCK - 1)
    ok = cmp_end <= t_lane
    cmp_bias = per_head(jnp.where(ok, 0.0, NEG_INF))
    cmp_keep = per_head(jnp.where(ok, 1.0, 0.0))
    sm = [_dot(kc_ref[0, g], q_all[g]) + cmp_bias for g in G]
    wk = WINDOW + tq
    w0 = pl.multiple_of(jnp.maximum(q0 - WINDOW, 0), tq)
    kpos = w0 + lax.broadcasted_iota(jnp.int32, (wk, 1), 0)
    win_bias = per_head(jnp.where(jnp.logical_and(kpos <= t_lane, kpos > t_lane - WINDOW), 0.0, NEG_INF))
    for g in G:
        sw_ref[g] = _dot(kw_ref[0, g, pl.ds(w0, wk), :], q_all[g]) + win_bias
    blk = lax.broadcasted_iota(jnp.int32, (n_slc, tq), 0)
    cur = (q0 + lax.broadcasted_iota(jnp.int32, (n_slc, tq), 1)) // SLC_BLOCK
    forced = (blk == 0) | (blk == cur) | (blk == cur - 1)
    o_cmp, imp = [], []
    for g in G:
        e = jnp.exp2(sm[g] - jnp.max(sm[g], axis=0, keepdims=True))
        p = e * (1.0 / jnp.sum(e, axis=0, keepdims=True)) * cmp_keep
        o_cmp.append(_dot(vc_ref[0, g], p.astype(BF16)))
        p_sum = p[:, 0:tq]
        for h in range(1, NSA_HPG):
            p_sum = p_sum + p[:, h * tq:(h + 1) * tq]
        p_hi = p_sum.astype(BF16)
        p_lo = (p_sum - p_hi.astype(F32)).astype(BF16)
        imp_g = _dot(ovt_ref[...], p_hi) + _dot(ovt_ref[...], p_lo)
        imp.append(jnp.where(blk <= cur, jnp.where(forced, FORCED_SCORE, imp_g), NEG_INF))

    SUBL = 8

    def block_bias(n):
        out = []
        for g in G:
            groups = [imp[g][r:r + SUBL] for r in range(0, n, SUBL)]
            ranks = [jnp.zeros((SUBL, tq), F32) for _ in groups]
            for i in range(n):
                row = imp[g][i:i + 1, :]
                for gidx, grp in enumerate(groups):
                    r = gidx * SUBL
                    ge = jnp.where(row >= grp, 1.0, 0.0)
                    gt = jnp.where(row > grp, 1.0, 0.0)
                    if r > i:
                        inc = ge
                    elif r + SUBL - 1 <= i:
                        inc = gt
                    else:
                        inc = jnp.where(blk[r:r + SUBL] > i, ge, gt)
                    ranks[gidx] = ranks[gidx] + inc
            bias = jnp.where(jnp.concatenate(ranks, axis=0) < float(min(SLC_TOPK, n_slc)), 0.0, NEG_INF)
            out.append(jnp.concatenate([bias, jnp.zeros((NSA_DH - n, tq), F32)], axis=0) if n < NSA_DH else bias)
        return tuple(out)

    quarter = n_slc // 4
    visible = (q0 + tq - 1) // SLC_BLOCK + 1
    bias = lax.cond(
        visible <= 2 * quarter,
        lambda: lax.cond(visible <= quarter, lambda: block_bias(quarter), lambda: block_bias(2 * quarter)),
        lambda: lax.cond(visible <= 3 * quarter, lambda: block_bias(3 * quarter), lambda: block_bias(n_slc)))
    q_aug = [jnp.concatenate([q_all[g], per_head(bias[g]).astype(BF16)], axis=0) for g in G]

    kpos_l = lax.broadcasted_iota(jnp.int32, (tk, 1), 0)
    n_tiles = q0 // tk + 1
    last_tile = ks_ref.shape[2] // tk - 1

    def tile_start(jt):
        return pl.multiple_of(jnp.minimum(jt, last_tile) * tk, tk)

    def scores_into(s_ref, jt):
        causal = per_head(jnp.where(jt * tk + kpos_l <= t_lane, 0.0, NEG_INF))
        for g in G:
            s_ref[g] = _dot(ks_ref[0, g, pl.ds(tile_start(jt), tk), :], q_aug[g]) + causal

    def consume(s_ref, jt, carry):
        return tuple(_softmax_step(s_ref[g], *carry[g], vs_ref[0, g, :, pl.ds(tile_start(jt), tk)]) for g in G)

    def slc_pair(i, carry):
        scores_into(sb_ref, 2 * i + 1)
        carry = consume(sa_ref, 2 * i, carry)
        scores_into(sa_ref, 2 * i + 2)
        return consume(sb_ref, 2 * i + 1, carry)

    init = tuple((jnp.full((1, W), NEG_INF, F32), jnp.zeros((NSA_DH + NSA_ONES, W), F32)) for g in G)
    scores_into(sa_ref, 0)

    acc_w = []
    for g in G:
        sw = sw_ref[g]
        ew = jnp.exp2(sw - jnp.max(sw, axis=0, keepdims=True)).astype(BF16)
        acc_w.append(_dot(_with_ones(vw_ref[0, g, :, pl.ds(w0, wk)]), ew))

    slc = lax.fori_loop(0, (n_tiles + 1) // 2, slc_pair, init)

    for g in G:
        def gate(j):
            return jnp.concatenate([ng_ref[0, (g * NSA_HPG + h) * 3 + j:(g * NSA_HPG + h) * 3 + j + 1, :]
                                    for h in range(NSA_HPG)], axis=1)

        acc_s = slc[g][1]
        l_s, l_w = acc_s[NSA_DH:NSA_DH + 1], acc_w[g][NSA_DH:NSA_DH + 1]
        out_t = (gate(0) * o_cmp[g] + (gate(1) * (1.0 / l_s)) * acc_s[:NSA_DH]
                 + (gate(2) * (1.0 / l_w)) * acc_w[g][:NSA_DH])
        out_t = jnp.concatenate([out_t[:, h * tq:(h + 1) * tq] for h in range(NSA_HPG)], axis=0)
        o_ref[0, :, g * NSA_HPG * NSA_DH:(g + 1) * NSA_HPG * NSA_DH] = out_t.T.astype(o_ref.dtype)


def _nsa_attention(q, kc, vc, ks, vs, kw, vw, ng):
    B, H, dh, S = q.shape
    tq = min(ATT_TQ, S)
    tk = min(ATT_TK, S)
    assert S % tk == 0 and tk % tq == 0 and S >= WINDOW + tq and S // SLC_BLOCK <= dh
    n_cmp = kc.shape[2]
    n_slc = S // SLC_BLOCK
    c0 = CMP_STRIDE * np.arange(n_cmp)[None, :]
    s0 = SLC_BLOCK * np.arange(n_slc)[:, None]
    ov = np.clip(np.minimum(c0 + CMP_BLOCK, s0 + SLC_BLOCK) - np.maximum(c0, s0), 0, None) / CMP_BLOCK
    ovt = jnp.asarray(ov, BF16)
    G = NSA_KV_GROUPS
    grid = (B, S // tq)
    k_spec = lambda n, w: pl.BlockSpec((1, G, n, w), lambda b, i: (b, 0, 0, 0))
    vt_spec = lambda n: pl.BlockSpec((1, G, dh, n), lambda b, i: (b, 0, 0, 0))
    return pl.pallas_call(
        functools.partial(_nsa_groups_body, tq=tq, tk=tk, n_slc=n_slc),
        grid=grid,
        in_specs=[
            pl.BlockSpec((1, H, dh, tq), lambda b, i: (b, 0, 0, i)),
            k_spec(n_cmp, dh), vt_spec(n_cmp),
            k_spec(S, 2 * dh), vt_spec(S),
            k_spec(S, dh), vt_spec(S),
            pl.BlockSpec((1, LANES, tq), lambda b, i: (b, 0, i)),
            pl.BlockSpec(ovt.shape, lambda b, i: (0, 0)),
        ],
        out_specs=pl.BlockSpec((1, tq, H * dh), lambda b, i: (b, i, 0)),
        out_shape=jax.ShapeDtypeStruct((B, S, H * dh), BF16),
        scratch_shapes=[pltpu.VMEM((G, tk, NSA_HPG * tq), F32),
                        pltpu.VMEM((G, tk, NSA_HPG * tq), F32),
                        pltpu.VMEM((G, WINDOW + tq, NSA_HPG * tq), F32)],
        compiler_params=pltpu.CompilerParams(
            dimension_semantics=("parallel", "arbitrary"), vmem_limit_bytes=VMEM_LIMIT),
        name="nsa_attn",
    )(q, kc, vc, ks, vs, kw, vw, ng, ovt)


def _dot_3pass(a, b):
    a1 = a.astype(BF16)
    a2 = (a - a1.astype(F32)).astype(BF16)
    b1 = b.astype(BF16)
    b2 = (b - b1.astype(F32)).astype(BF16)
    return _dot(a1, b1) + (_dot(a1, b2) + _dot(a2, b1))


def _gla_pair_body(q_ref, k_ref, v_ref, vt_ref, b_ref, r_ref, ng_ref,
                   o_ref, state_ref, kp_ref, bp_ref, vp_ref, plain_ref, *, n_chunks):
    C, SUB, dk, dv, hp = GLA_C, GLA_SUB, GLA_DK, GLA_DV, GLA_HP
    W = hp * dk
    lane = lax.broadcasted_iota(jnp.int32, (1, W), 1)
    of_head = [lane // dk == hh for hh in range(hp)]
    t_loc = lax.broadcasted_iota(jnp.int32, (C, 1), 0)
    state_ref[...] = jnp.zeros_like(state_ref)
    kp_ref[0:SUB, :] = jnp.zeros((SUB, W), F32)
    bp_ref[0:SUB, :] = jnp.zeros((SUB, W), F32)
    vp_ref[:, 0:SUB, :] = jnp.zeros((hp, SUB, dv), F32)

    def decay_chunk(c, all_plain):
        b_last = b_ref[0, pl.ds(c * C + (C - 1), 1), :]
        plain = (jnp.min(b_last) > -GLA_PLAIN_DECAY).astype(jnp.int32)
        plain_ref[c] = plain
        return jnp.minimum(all_plain, plain)

    all_plain = lax.fori_loop(0, n_chunks, decay_chunk, jnp.int32(1), unroll=GLA_UNROLL_DECAY)

    def chunk(c, check_decay):
        c0 = pl.multiple_of(c * C, C)
        b = b_ref[0, pl.ds(c0, C), :]
        q = q_ref[0, pl.ds(c0, C), :]
        k = k_ref[0, pl.ds(c0, C), :]
        v = [v_ref[0, pl.ds(c0, C), hh * dv:(hh + 1) * dv] for hh in range(hp)]
        vt = [vt_ref[0, hh * dv:(hh + 1) * dv, pl.ds(c0, C)] for hh in range(hp)]
        b_last = b[C - 1:C, :]
        st = state_ref[...]
        st_b = st.astype(BF16)
        qg = q * jnp.exp(b)
        qg_h = [jnp.where(of_head[hh], qg, 0.0).astype(BF16) for hh in range(hp)]
        o_inter = [_dot_nt(qg_h[hh], st_b) for hh in range(hp)]

        def intra_plain():
            ke = (k * jnp.exp(-b)).astype(BF16)
            row = lax.broadcasted_iota(jnp.int32, (C, C), 0)
            col = lax.broadcasted_iota(jnp.int32, (C, C), 1)
            return tuple(_dot(jnp.where(row >= col, _dot_nt(qg_h[hh], ke), 0.0).astype(BF16), v[hh])
                         for hh in range(hp))

        def intra_strong_decay():
            far = [[jnp.zeros((SUB, dv), F32)] for _ in range(hp)]
            for i in range(1, C // SUB):
                r0 = i * SUB
                b_first = b[r0:r0 + 1, :]
                qt = q[r0:r0 + SUB] * jnp.exp(b[r0:r0 + SUB] - b_first)
                kt = (k[:r0] * jnp.exp(b_first - b[:r0])).astype(BF16)
                lt = lax.broadcasted_iota(jnp.int32, (SUB, r0), 0)
                ls = lax.broadcasted_iota(jnp.int32, (SUB, r0), 1)
                for hh in range(hp):
                    a = _dot_nt(jnp.where(of_head[hh], qt, 0.0).astype(BF16), kt)
                    a = jnp.where(lt + (r0 - SUB) >= ls, a, 0.0)
                    far[hh].append(_dot(a.astype(BF16), v[hh][:r0]))
            acc = [jnp.concatenate(far[hh], axis=0) for hh in range(hp)]
            kp_ref[SUB:SUB + C, :] = k
            bp_ref[SUB:SUB + C, :] = b
            for hh in range(hp):
                vp_ref[hh, SUB:SUB + C, :] = v[hh].astype(F32)
            for d in range(SUB):
                kd = kp_ref[SUB - d:SUB - d + C, :]
                bd = bp_ref[SUB - d:SUB - d + C, :]
                valid = t_loc >= d
                w = jnp.exp(jnp.where(valid, b - bd, 0.0))
                x = jnp.where(valid, q * kd * w, 0.0)
                for hh in range(hp):
                    a_d = jnp.sum(jnp.where(of_head[hh], x, 0.0), axis=-1, keepdims=True)
                    acc[hh] = acc[hh] + a_d * vp_ref[hh, SUB - d:SUB - d + C, :]
            return tuple(acc)

        if check_decay:
            o_intra = lax.cond(plain_ref[c] > 0, intra_plain, intra_strong_decay)
        else:
            o_intra = intra_plain()
        k_dec = (k * jnp.exp(b_last - b)).astype(BF16)
        upd = _dot(vt[hp - 1], k_dec)
        for hh in range(hp - 2, -1, -1):
            upd = jnp.where(of_head[hh], _dot(vt[hh], k_dec), upd)
        state_ref[...] = st * jnp.exp(b_last) + upd
        for hh in range(hp):
            o = o_inter[hh] + o_intra[hh]
            o = o * lax.rsqrt(jnp.mean(o * o, axis=-1, keepdims=True) + EPS) * ng_ref[hh]
            gate = jax.nn.silu(r_ref[0, pl.ds(c0, C), hh * dv:(hh + 1) * dv])
            o_ref[0, pl.ds(c0, C), hh * dv:(hh + 1) * dv] = (o * gate).astype(o_ref.dtype)
        return 0

    @pl.when(all_plain > 0)
    def _():
        lax.fori_loop(0, n_chunks, lambda c, _: chunk(c, False), 0, unroll=GLA_UNROLL_PLAIN)

    @pl.when(all_plain <= 0)
    def _():
        lax.fori_loop(0, n_chunks, lambda c, _: chunk(c, True), 0)


def _gla_pairs(gq, gk, gv, gvt, gb, gr, norm_g):
    B, S, _ = gq.shape
    H, dk, dv, hp = GLA_HEADS, GLA_DK, GLA_DV, GLA_HP
    W = hp * dk
    assert W == LANES and H % hp == 0
    ng = norm_g.reshape(H, 1, dv)
    n_chunks = S // GLA_C
    tok_spec = lambda w: pl.BlockSpec((1, S, hp * w), lambda b, h: (b, 0, h))
    return pl.pallas_call(
        functools.partial(_gla_pair_body, n_chunks=n_chunks),
        grid=(B, H // hp),
        in_specs=[tok_spec(dk), tok_spec(dk), tok_spec(dv),
                  pl.BlockSpec((1, hp * dv, S), lambda b, h: (b, h, 0)),
                  tok_spec(dk),
                  tok_spec(dv),
                  pl.BlockSpec((hp, 1, dv), lambda b, h: (h, 0, 0))],
        out_specs=tok_spec(dv),
        out_shape=jax.ShapeDtypeStruct((B, S, H * dv), BF16),
        scratch_shapes=[pltpu.VMEM((dv, W), F32),
                        pltpu.VMEM((GLA_SUB + GLA_C, W), F32),
                        pltpu.VMEM((GLA_SUB + GLA_C, W), F32),
                        pltpu.VMEM((hp, GLA_SUB + GLA_C, dv), F32),
                        pltpu.SMEM((n_chunks,), jnp.int32)],
        compiler_params=pltpu.CompilerParams(
            dimension_semantics=("parallel", "parallel"), vmem_limit_bytes=VMEM_LIMIT),
        name="gla",
    )(gq, gk, gv, gvt, gb, gr, ng)


def _merge_body(x_ref, ya_ref, yb_ref, ma_ref, mb_ref, wpa_ref, wpb_ref, wo_ref, gf_ref, wr_ref, br_ref,
                h_ref, v_ref, comb_ref):
    y_a = _dot(ya_ref[...], wpa_ref[...])
    y_b = _dot(yb_ref[...], wpb_ref[...])
    mixed = ma_ref[...] * y_a + mb_ref[...] * y_b
    h = x_ref[...] + _dot(mixed.astype(BF16), wo_ref[...])
    h_ref[...] = h
    v = h * lax.rsqrt(jnp.mean(h * h, axis=-1, keepdims=True) + EPS) * gf_ref[...]
    v_ref[...] = v.astype(BF16)
    wr = wr_ref[...]
    w_hi = wr.astype(BF16)
    w_both = jnp.concatenate([w_hi, (wr - w_hi.astype(F32)).astype(BF16)], axis=1)
    v_hi = v.astype(BF16)
    v_lo = (v - v_hi.astype(F32)).astype(BF16)
    both = _dot(v_hi, w_both)
    logits = (both[:, :LANES] + both[:, LANES:]) + _dot(v_lo, w_hi) + br_ref[...]
    comb_ref[...] = _route_weights(logits)


def _route_weights(logits):
    lane = lax.broadcasted_iota(jnp.int32, logits.shape, 1)
    is_grp = jnp.logical_and(lane >= MOE_EXPERTS, lane < MOE_EXPERTS + MOE_GROUPS)
    lg = jnp.where(is_grp, logits, NEG_INF)
    eg = jnp.where(is_grp, jnp.exp(lg - jnp.max(lg, axis=-1, keepdims=True)), 0.0)
    pg = eg / jnp.sum(eg, axis=-1, keepdims=True)
    p_grp = jnp.max(pg, axis=-1, keepdims=True)
    g_sel = jnp.min(jnp.where(jnp.logical_and(is_grp, pg == p_grp), lane, 2 * LANES),
                    axis=-1, keepdims=True) - MOE_EXPERTS
    in_grp = jnp.logical_and(lane < MOE_EXPERTS, lane // MOE_EPG == g_sel)
    le = jnp.where(in_grp, logits, NEG_INF)
    ee = jnp.where(in_grp, jnp.exp(le - jnp.max(le, axis=-1, keepdims=True)), 0.0)
    pin = ee / jnp.sum(ee, axis=-1, keepdims=True)
    p1 = jnp.max(jnp.where(in_grp, pin, -1.0), axis=-1, keepdims=True)
    i1 = jnp.min(jnp.where(jnp.logical_and(in_grp, pin == p1), lane, 2 * LANES), axis=-1, keepdims=True)
    rest = jnp.logical_and(in_grp, lane != i1)
    p2 = jnp.max(jnp.where(rest, pin, -1.0), axis=-1, keepdims=True)
    i2 = jnp.min(jnp.where(jnp.logical_and(rest, pin == p2), lane, 2 * LANES), axis=-1, keepdims=True)
    tot = p1 + p2
    return jnp.where(lane == i1, p_grp * p1 / tot, 0.0) + jnp.where(lane == i2, p_grp * p2 / tot, 0.0)


def _merge(x2, ya, yb, ma, mb, w_proj_nsa, w_proj_gla, w_out, g_ffn, w_grp, b_grp, w_exp, b_exp):
    T, D = x2.shape
    tm = min(MERGE_TM, T)
    wr = jnp.pad(jnp.concatenate([w_exp, w_grp], axis=1), ((0, 0), (0, LANES - MOE_EXPERTS - MOE_GROUPS)))
    br = jnp.pad(jnp.concatenate([b_exp, b_grp]), (0, LANES - MOE_EXPERTS - MOE_GROUPS)).reshape(1, LANES)
    tok = lambda w: pl.BlockSpec((tm, w), lambda i: (i, 0))
    full = lambda a: pl.BlockSpec(a.shape, lambda i: (0, 0))
    wpa, wpb, wo = w_proj_nsa.astype(BF16), w_proj_gla.astype(BF16), w_out.astype(BF16)
    gf = g_ffn.reshape(1, D)
    sds = jax.ShapeDtypeStruct
    return pl.pallas_call(
        _merge_body,
        grid=(T // tm,),
        in_specs=[tok(D), tok(ya.shape[1]), tok(yb.shape[1]), tok(D), tok(D),
                  full(wpa), full(wpb), full(wo), full(gf), full(wr), full(br)],
        out_specs=(tok(D), tok(D), tok(LANES)),
        out_shape=(sds((T, D), F32), sds((T, D), BF16), sds((T, LANES), F32)),
        compiler_params=pltpu.CompilerParams(
            dimension_semantics=("parallel",), vmem_limit_bytes=VMEM_LIMIT),
        name="merge",
    )(x2, ya, yb, ma, mb, wpa, wpb, wo, gf, wr, br)


def _moe_rows(tw):
    rows = 2 * tw + MOE_EXPERTS * (MOE_ALIGN - 1) + MOE_RT
    return -(-rows // MOE_RT) * MOE_RT


def _moe_window(e, phase, v_ref, comb_ref, tri_ref, y_ref, xs_ref, z_ref, meta_ref, pos_ref, tw, rmax):
    no_row = -1.0

    @pl.when(e == 0 if phase == "route" else False)
    def _route():
        comb = comb_ref[...]
        assigned = comb > 0.0
        a = jnp.where(assigned, 1.0, 0.0)
        tri = tri_ref[...]
        run = jnp.zeros((1, LANES), F32)
        ranks = []
        for b in range(tw // MOE_RT):
            ab = a[b * MOE_RT:(b + 1) * MOE_RT]
            ranks.append(_dot(tri, ab.astype(BF16)) + run)
            run = run + jnp.sum(ab, axis=0, keepdims=True)
        rank = jnp.concatenate(ranks, axis=0)
        cnt_pad = jnp.floor((run + (MOE_ALIGN - 1)) * (1.0 / MOE_ALIGN)) * MOE_ALIGN
        incl = jnp.broadcast_to(cnt_pad, (8, LANES))
        lane8 = lax.broadcasted_iota(jnp.int32, (8, LANES), 1)
        shift = 1
        while shift < MOE_EXPERTS:
            incl = incl + jnp.where(lane8 >= shift, pltpu.roll(incl, shift, 1), 0.0)
            shift *= 2
        offs = incl[0:1] - cnt_pad
        meta_ref[0:1, :] = offs
        meta_ref[1:2, :] = run
        row_of = offs + rank
        pos_a = jnp.min(jnp.where(assigned, row_of, 1e9), axis=-1, keepdims=True)
        pos_b = jnp.max(jnp.where(assigned, row_of, no_row), axis=-1, keepdims=True)
        pos_a = jnp.where(pos_a > 1e8, no_row, pos_a)
        pos_b = jnp.where(pos_b == pos_a, no_row, pos_b)
        w_a = jnp.sum(jnp.where(jnp.logical_and(assigned, row_of == pos_a), comb, 0.0), axis=-1, keepdims=True)
        w_b = jnp.sum(jnp.where(jnp.logical_and(assigned, row_of == pos_b), comb, 0.0), axis=-1, keepdims=True)
        lane_t = lax.broadcasted_iota(jnp.int32, (tw, LANES), 1)
        pos_ref[...] = jnp.where(lane_t == 0, pos_a, jnp.where(lane_t == 1, pos_b, jnp.where(
            lane_t == 2, w_a, jnp.where(lane_t == 3, w_b, no_row))))
        pos_t = pos_ref[...].T
        pa, pb = pos_t[0:1], pos_t[1:2]
        v = v_ref[...]
        for rt in range(rmax // MOE_RT):
            r = (rt * MOE_RT + lax.broadcasted_iota(jnp.int32, (MOE_RT, 1), 0)).astype(F32)
            p = jnp.where(r == pa, 1.0, jnp.where(r == pb, 1.0, 0.0)).astype(BF16)
            xs_ref[rt * MOE_RT:(rt + 1) * MOE_RT, :] = _dot(p, v).astype(BF16)
        z_ref[...] = jnp.zeros_like(z_ref)

    @pl.when(e == pl.num_programs(1) - 1 if phase == "combine" else False)
    def _combine():
        r = lax.broadcasted_iota(jnp.int32, (1, rmax), 1).astype(F32)
        z = z_ref[...]
        for tt in range(tw // MOE_RT):
            rows = slice(tt * MOE_RT, (tt + 1) * MOE_RT)
            pa, pb = pos_ref[rows, 0:1], pos_ref[rows, 1:2]
            w_a, w_b = pos_ref[rows, 2:3], pos_ref[rows, 3:4]
            q = jnp.where(r == pa, w_a, jnp.where(r == pb, w_b, 0.0)).astype(BF16)
            y_ref[rows, :] = _dot(q, z).astype(y_ref.dtype)


def _moe_body(v_ref, comb_ref, wg_ref, wu_ref, wd_ref, tri_ref, y_ref,
              xs_ref, z_ref, meta_ref, pos_ref, *, tw, rmax, n_win):
    e = pl.program_id(1)
    lane = lax.broadcasted_iota(jnp.int32, (1, LANES), 1)

    def window_phase(phase):
        for w in range(n_win):
            rows = pl.ds(w * tw, tw)
            _moe_window(e, phase, v_ref.at[rows], comb_ref.at[rows], tri_ref, y_ref.at[rows],
                        xs_ref.at[w], z_ref.at[w], meta_ref.at[w], pos_ref.at[w], tw, rmax)

    window_phase("route")

    pick = lambda w, row: jnp.sum(jnp.where(lane == e, meta_ref[w, row:row + 1, :], 0.0)).astype(jnp.int32)
    offs = [pick(w, 0) for w in range(n_win)]
    tiles = [(pick(w, 1) + MOE_ET - 1) // MOE_ET for w in range(n_win)]

    def expert(x):
        hdn = jax.nn.silu(_dot(x, wg_ref[0])) * _dot(x, wu_ref[0])
        return _dot(hdn.astype(BF16), wd_ref[0]).astype(BF16)

    def joint_tile(i, _):
        r0 = [pl.multiple_of(offs[w] + i * MOE_ET, MOE_ALIGN) for w in range(n_win)]
        z = expert(jnp.concatenate([xs_ref[w, pl.ds(r0[w], MOE_ET), :] for w in range(n_win)], axis=0))
        for w in range(n_win):
            z_ref[w, pl.ds(r0[w], MOE_ET), :] = z[w * MOE_ET:(w + 1) * MOE_ET]
        return 0

    n_joint = functools.reduce(jnp.minimum, tiles)
    lax.fori_loop(0, n_joint, joint_tile, 0)
    for w in range(n_win):
        def own_tile(i, _, w=w):
            r0 = pl.multiple_of(offs[w] + i * MOE_ET, MOE_ALIGN)
            z_ref[w, pl.ds(r0, MOE_ET), :] = expert(xs_ref[w, pl.ds(r0, MOE_ET), :])
            return 0

        lax.fori_loop(n_joint, tiles[w], own_tile, 0)

    window_phase("combine")


def _moe(v, comb, w_gate, w_up, w_down):
    T, D = v.shape
    tw = min(MOE_TW, T)
    n_win = min(MOE_WINDOWS, T // tw)
    rmax = _moe_rows(tw)
    E, _, F = w_gate.shape
    wg, wu, wd = w_gate.astype(BF16), w_up.astype(BF16), w_down.astype(BF16)
    tri = jnp.asarray(np.tril(np.ones((MOE_RT, MOE_RT)), -1), BF16)
    tok = lambda w: pl.BlockSpec((n_win * tw, w), lambda i, e: (i, 0))
    tok_in = lambda w: pl.BlockSpec((n_win * tw, w), lambda i, e: (i, 0), pipeline_mode=pl.Buffered(1))
    return pl.pallas_call(
        functools.partial(_moe_body, tw=tw, rmax=rmax, n_win=n_win),
        grid=(T // (n_win * tw), E),
        in_specs=[tok_in(D), tok_in(LANES),
                  pl.BlockSpec((1, D, F), lambda i, e: (e, 0, 0)),
                  pl.BlockSpec((1, D, F), lambda i, e: (e, 0, 0)),
                  pl.BlockSpec((1, F, D), lambda i, e: (e, 0, 0)),
                  pl.BlockSpec(tri.shape, lambda i, e: (0, 0))],
        out_specs=tok(D),
        out_shape=jax.ShapeDtypeStruct((T, D), BF16),
        scratch_shapes=[pltpu.VMEM((n_win, rmax, D), BF16),
                        pltpu.VMEM((n_win, rmax, D), BF16),
                        pltpu.VMEM((n_win, 8, LANES), F32),
                        pltpu.VMEM((n_win, tw, LANES), F32)],
        compiler_params=pltpu.CompilerParams(
            dimension_semantics=("parallel", "arbitrary"), vmem_limit_bytes=VMEM_LIMIT),
        name="moe",
    )(v, comb, wg, wu, wd, tri)


def _final_body(h_ref, y_ref, g_ref, o_ref):
    h = h_ref[...] + y_ref[...].astype(F32)
    o_ref[...] = h * lax.rsqrt(jnp.mean(h * h, axis=-1, keepdims=True) + EPS) * g_ref[...]


def _final_norm(h, y, g_final):
    T, D = h.shape
    tm = min(FINAL_TM, T)
    tok = pl.BlockSpec((tm, D), lambda i: (i, 0))
    return pl.pallas_call(
        _final_body,
        grid=(T // tm,),
        in_specs=[tok, tok, pl.BlockSpec((1, D), lambda i: (0, 0))],
        out_specs=tok,
        out_shape=jax.ShapeDtypeStruct((T, D), F32),
        compiler_params=pltpu.CompilerParams(
            dimension_semantics=("parallel",), vmem_limit_bytes=VMEM_LIMIT),
        name="final_norm",
    )(h, y, g_final.reshape(1, D))


def _rope_inv_freq():
    half = NSA_DH // 2
    return 1.0 / (ROPE_THETA ** (jnp.arange(half, dtype=F32) / half))


def _rope_tables(positions):
    ang = positions.astype(F32)[..., None] * _rope_inv_freq()
    cos, sin = jnp.cos(ang), jnp.sin(ang)
    cs = jnp.concatenate([cos, cos, cos, cos], axis=-1)
    sn = jnp.concatenate([-sin, sin, -sin, sin], axis=-1)
    return cs, sn


def _layer(h, positions, g_mix, w_in, cmp_pos_k, cmp_w1_k, cmp_b1_k, cmp_w2_k, cmp_b2_k,
           cmp_pos_v, cmp_w1_v, cmp_b1_v, cmp_w2_v, cmp_b2_v, gla_w_a2, gla_b_a, gla_norm_g,
           w_proj_nsa, w_proj_gla, w_out, g_ffn, w_grp, b_grp, w_exp, b_exp, w_gate, w_up, w_down, g_out):
    B, S, D = h.shape
    n_chunks = S // CMP_STRIDE
    cmp_end = jnp.minimum(CMP_STRIDE * jnp.arange(n_chunks) + CMP_BLOCK - 1, S - 1)
    cs_c, sn_c = _rope_tables(jnp.take(positions, cmp_end, axis=1))
    (q, kvc, ks, vs, kw, vw, gq, gk, gv, gvt, gr, ma, mb, ng, gb) = _in_proj(h, g_mix, w_in, positions,
                                                                             gla_w_a2, gla_b_a)
    kc, vc = _compress(kvc, cs_c, sn_c, cmp_pos_k, cmp_w1_k, cmp_b1_k, cmp_w2_k, cmp_b2_k,
                       cmp_pos_v, cmp_w1_v, cmp_b1_v, cmp_w2_v, cmp_b2_v)
    ya = _nsa_attention(q, kc, vc, ks, vs, kw, vw, ng)
    yb = _gla_pairs(gq, gk, gv, gvt, gb, gr, gla_norm_g)
    T = B * S
    h1, v, comb = _merge(h.reshape(T, D), ya.reshape(T, -1), yb.reshape(T, -1), ma.reshape(T, D), mb.reshape(T, D),
                         w_proj_nsa, w_proj_gla, w_out, g_ffn, w_grp, b_grp, w_exp, b_exp)
    y = _moe(v, comb, w_gate, w_up, w_down)
    return _final_norm(h1, y, g_out).reshape(B, S, D)


def kernel(x, positions, g_mix, w_in, cmp_pos_k, cmp_w1_k, cmp_b1_k, cmp_w2_k, cmp_b2_k, cmp_pos_v, cmp_w1_v,
           cmp_b1_v, cmp_w2_v, cmp_b2_v, gla_w_a2, gla_b_a, gla_norm_g, w_proj_nsa, w_proj_gla, w_out, g_ffn,
           w_grp, b_grp, w_exp, b_exp, w_gate, w_up, w_down, g_final):
    depth = g_mix.shape[0]
    assert depth == 1, "the final norm closes the single layer"
    return _layer(x, positions, g_mix[0], w_in[0], cmp_pos_k[0], cmp_w1_k[0], cmp_b1_k[0], cmp_w2_k[0],
                  cmp_b2_k[0], cmp_pos_v[0], cmp_w1_v[0], cmp_b1_v[0], cmp_w2_v[0], cmp_b2_v[0],
                  gla_w_a2[0], gla_b_a[0], gla_norm_g[0], w_proj_nsa[0], w_proj_gla[0], w_out[0],
                  g_ffn[0], w_grp[0], b_grp[0], w_exp[0], b_exp[0], w_gate[0], w_up[0], w_down[0], g_final)
```

```python
import functools

import numpy as np
import jax
import jax.numpy as jnp
from jax import lax
from jax.experimental import pallas as pl
from jax.experimental.pallas import tpu as pltpu

F32 = jnp.float32
BF16 = jnp.bfloat16

NSA_HEADS = 8
NSA_KV_GROUPS = 2
NSA_HPG = NSA_HEADS // NSA_KV_GROUPS
NSA_DH = 64
CMP_BLOCK = 32
CMP_STRIDE = 16
CMP_HIDDEN = 256
SLC_BLOCK = 64
SLC_TOPK = 16
WINDOW = 512
GLA_HEADS = 4
GLA_DK = 64
GLA_DV = 128
GLA_RANK = 16
GLA_TAU = 16.0
MOE_GROUPS = 4
MOE_EPG = 8
MOE_EXPERTS = MOE_GROUPS * MOE_EPG
ROPE_THETA = 10000.0
EPS = 1e-6
NEG_INF = -1e30
FORCED_SCORE = 1e4
LOG2E = 1.4426950408889634

LANES = 128
VMEM_LIMIT = 56 * 1024 * 1024

IN_TM = 512
ATT_TQ = 128
ATT_TK = 256
GLA_C = 128
GLA_SUB = 16
GLA_HP = 2
GLA_UNROLL_DECAY = 4
GLA_UNROLL_PLAIN = 8
GLA_PLAIN_DECAY = 60.0
MERGE_TM = 512
MOE_TW = 512
MOE_WINDOWS = 4
MOE_PACK = 4096
FINAL_TM = 1024
MOE_RT = 256
MOE_ET = 48
MOE_ALIGN = 16


def _dot(a, b):
    return jnp.dot(a, b, preferred_element_type=F32)


def _dot_nt(a, b):
    return lax.dot_general(a, b, (((1,), (1,)), ((), ())), preferred_element_type=F32)


def _split3(x):
    x1 = x.astype(BF16)
    r1 = x - x1.astype(F32)
    x2 = r1.astype(BF16)
    r2 = r1 - x2.astype(F32)
    x3 = r2.astype(BF16)
    return x1, x2, x3


def _dot_exact_lhs(a_bf16, x):
    x1, x2, x3 = _split3(x)
    return _dot(a_bf16, x1) + _dot(a_bf16, x2) + _dot(a_bf16, x3)


def _rope_lanes(z, cs, sn):
    w = z.shape[-1]
    lane = lax.broadcasted_iota(jnp.int32, z.shape, 1)
    first_half = (lane % NSA_DH) < (NSA_DH // 2)
    rot = jnp.where(first_half, pltpu.roll(z, w - NSA_DH // 2, 1), pltpu.roll(z, NSA_DH // 2, 1))
    reps = w // LANES
    if reps > 1:
        cs = jnp.concatenate([cs] * reps, axis=1)
        sn = jnp.concatenate([sn] * reps, axis=1)
    return z * cs + rot * sn


_SEC = {}
_off = 0
for _name, _w in (("q", 512), ("kvc", 256), ("ks", 128), ("vs", 128), ("kw", 128), ("vw", 128),
                  ("gq", 256), ("gk", 256), ("gv", 512), ("gr", 512), ("ma", 1024), ("mb", 1024),
                  ("nga", 128)):
    _SEC[_name] = (_off, _off + _w)
    _off += _w
IN_NW = _off


def _in_proj_body(x_ref, g_ref, w_ref, pos_ref, inv_ref, wa_ref, ba_ref, tri_ref,
                  q_ref, kvc_ref, ks_ref, vs_ref, kw_ref, vw_ref,
                  gq_ref, gk_ref, gv_ref, gvt_ref, gr_ref, ma_ref, mb_ref, ng_ref, gb_ref):
    x = x_ref[0]
    var = jnp.mean(x * x, axis=-1, keepdims=True)
    u = (x * lax.rsqrt(var + EPS) * g_ref[...]).astype(BF16)
    ang = pos_ref[0].astype(F32) * inv_ref[...]
    lane_r = lax.broadcasted_iota(jnp.int32, ang.shape, 1)
    cs = jnp.cos(ang)
    sn = jnp.where((lane_r % NSA_DH) < (NSA_DH // 2), -1.0, 1.0) * jnp.sin(ang)

    def proj(name):
        a, b = _SEC[name]
        return _dot(u, w_ref[:, a:b])

    zq_t = (_rope_lanes(proj("q"), cs, sn) * (NSA_DH ** -0.5 * LOG2E)).T
    for h in range(NSA_HEADS):
        q_ref[0, h] = zq_t[h * NSA_DH:(h + 1) * NSA_DH].astype(BF16)
    kvc_ref[0] = proj("kvc")
    zks = _rope_lanes(proj("ks"), cs, sn)
    zkw = _rope_lanes(proj("kw"), cs, sn)
    zvs_t = proj("vs").T
    zvw_t = proj("vw").T
    tm = zks.shape[0]
    lane = lax.broadcasted_iota(jnp.int32, (tm, LANES), 1)
    blk = (pl.program_id(1) * tm + lax.broadcasted_iota(jnp.int32, (tm, LANES), 0)) // SLC_BLOCK
    onehot = jnp.where(lane - NSA_DH == blk, 1.0, 0.0)
    for g in range(NSA_KV_GROUPS):
        sl = slice(g * NSA_DH, (g + 1) * NSA_DH)
        k_front = zks if g == 0 else pltpu.roll(zks, LANES - g * NSA_DH, 1)
        ks_ref[0, g] = jnp.where(lane < NSA_DH, k_front, onehot).astype(BF16)
        kw_ref[0, g] = zkw[:, sl].astype(BF16)
        vs_ref[0, g] = zvs_t[sl].astype(BF16)
        vw_ref[0, g] = zvw_t[sl].astype(BF16)
    gq_ref[0] = proj("gq") * (GLA_DK ** -0.5)
    gk_ref[0] = proj("gk")
    zgv = proj("gv")
    gv_ref[0] = zgv.astype(BF16)
    gvt_ref[0] = zgv.T.astype(BF16)
    gr_ref[0] = proj("gr")
    ma_ref[0] = jax.nn.sigmoid(proj("ma")).astype(ma_ref.dtype)
    mb_ref[0] = jax.nn.sigmoid(proj("mb")).astype(mb_ref.dtype)
    z_nga = proj("nga")
    ng_ref[0] = jax.nn.sigmoid(z_nga).T
    la = jax.nn.log_sigmoid(_dot_3pass(z_nga, wa_ref[...]) + ba_ref[...]) / GLA_TAU
    tri = tri_ref[...]
    gb_ref[0] = jnp.concatenate([_dot_exact_lhs(tri, la[c * GLA_C:(c + 1) * GLA_C])
                                 for c in range(la.shape[0] // GLA_C)], axis=0)


def _in_proj(x, g_mix, w_in, positions, gla_w_a2, gla_b_a):
    B, S, D = x.shape
    inv = jnp.tile(_rope_inv_freq(), LANES // (NSA_DH // 2)).reshape(1, LANES)
    tm = min(IN_TM, S)
    assert tm % GLA_C == 0
    n_gates = 3 * NSA_HEADS
    wa = jnp.pad(gla_w_a2, ((n_gates, LANES - n_gates - GLA_RANK), (0, 0)))
    ba = gla_b_a.reshape(1, -1)
    tri = jnp.asarray(np.tril(np.ones((GLA_C, GLA_C))), BF16)
    splits = np.cumsum((512,) + (128,) * 6 + (24, 256, 256, 512, 16, 512, 1024, 1024))
    (wq, wkc, wvc, wks, wvs, wkw, wvw, wng, wgq, wgk, wgv, wga, wgr, wma, wmb) = jnp.split(
        w_in, splits[:-1].tolist(), axis=1)
    pad = lambda w: jnp.pad(w, ((0, 0), (0, LANES - w.shape[1])))
    w_all = jnp.concatenate([wq, wkc, wvc, wks, wvs, wkw, wvw, wgq, wgk, wgv, wgr, wma, wmb,
                             pad(jnp.concatenate([wng, wga], axis=1))], axis=1).astype(BF16)
    assert w_all.shape[1] == IN_NW
    grid = (B, S // tm)
    tok = lambda w: pl.BlockSpec((1, tm, w), lambda b, i: (b, i, 0))
    head = lambda n, w: pl.BlockSpec((1, n, tm, w), lambda b, i: (b, 0, i, 0))
    head_t = lambda n, w: pl.BlockSpec((1, n, w, tm), lambda b, i: (b, 0, 0, i))
    sds = jax.ShapeDtypeStruct
    out_shape = (
        sds((B, NSA_HEADS, NSA_DH, S), BF16),
        sds((B, S, 256), F32),
        sds((B, NSA_KV_GROUPS, S, LANES), BF16),
        sds((B, NSA_KV_GROUPS, NSA_DH, S), BF16),
        sds((B, NSA_KV_GROUPS, S, NSA_DH), BF16),
        sds((B, NSA_KV_GROUPS, NSA_DH, S), BF16),
        sds((B, S, GLA_HEADS * GLA_DK), F32),
        sds((B, S, GLA_HEADS * GLA_DK), F32),
        sds((B, S, GLA_HEADS * GLA_DV), BF16),
        sds((B, GLA_HEADS * GLA_DV, S), BF16),
        sds((B, S, GLA_HEADS * GLA_DV), F32),
        sds((B, S, D), BF16),
        sds((B, S, D), BF16),
        sds((B, LANES, S), F32),
        sds((B, S, GLA_HEADS * GLA_DK), F32),
    )
    out_specs = (
        head_t(NSA_HEADS, NSA_DH), tok(256),
        head(NSA_KV_GROUPS, LANES), head_t(NSA_KV_GROUPS, NSA_DH),
        head(NSA_KV_GROUPS, NSA_DH), head_t(NSA_KV_GROUPS, NSA_DH),
        tok(GLA_HEADS * GLA_DK), tok(GLA_HEADS * GLA_DK),
        tok(GLA_HEADS * GLA_DV),
        pl.BlockSpec((1, GLA_HEADS * GLA_DV, tm), lambda b, i: (b, 0, i)),
        tok(GLA_HEADS * GLA_DV), tok(D), tok(D),
        pl.BlockSpec((1, LANES, tm), lambda b, i: (b, 0, i)), tok(GLA_HEADS * GLA_DK),
    )
    return pl.pallas_call(
        _in_proj_body,
        grid=grid,
        in_specs=[
            tok(D),
            pl.BlockSpec((1, D), lambda b, i: (0, 0)),
            pl.BlockSpec((D, IN_NW), lambda b, i: (0, 0), pipeline_mode=pl.Buffered(1)),
            tok(1),
            pl.BlockSpec((1, LANES), lambda b, i: (0, 0)),
            pl.BlockSpec(wa.shape, lambda b, i: (0, 0)),
            pl.BlockSpec(ba.shape, lambda b, i: (0, 0)),
            pl.BlockSpec(tri.shape, lambda b, i: (0, 0)),
        ],
        out_specs=out_specs,
        out_shape=out_shape,
        compiler_params=pltpu.CompilerParams(
            dimension_semantics=("parallel", "parallel"), vmem_limit_bytes=VMEM_LIMIT),
        name="in_proj",
    )(x, g_mix.reshape(1, D), w_all, positions.reshape(B, S, 1), inv, wa, ba, tri)


def _compress_body(x_ref, p0_ref, p1_ref, w0_ref, w1_ref, b1_ref, w2_ref, b2_ref, cs_ref, sn_ref,
                   kc_ref, vc_ref, xk_ref, xv_ref):
    n = x_ref.shape[1] // CMP_STRIDE
    half_refs = (xk_ref, xv_ref)
    for s, half in enumerate(half_refs):
        half[...] = x_ref[0, :, s * LANES:(s + 1) * LANES]
    hid = w0_ref.shape[1] // len(half_refs)
    y0 = [None, None]
    y1 = [None, None]
    for t in range(CMP_STRIDE):
        for s, half in enumerate(half_refs):
            xt = half[pl.ds(t, n, stride=CMP_STRIDE), :]
            rows = slice((2 * t + s) * LANES, (2 * t + s + 1) * LANES)
            cols = slice(s * hid, (s + 1) * hid)
            lanes = slice(s * LANES, (s + 1) * LANES)
            d0 = _dot((xt + p0_ref[t:t + 1, lanes]).astype(BF16), w0_ref[rows, cols])
            d1 = _dot((xt + p1_ref[t:t + 1, lanes]).astype(BF16), w1_ref[rows, cols])
            y0[s] = d0 if y0[s] is None else y0[s] + d0
            y1[s] = d1 if y1[s] is None else y1[s] + d1
    y0 = jnp.concatenate(y0, axis=1)
    y1 = jnp.concatenate(y1, axis=1)
    h = jax.nn.gelu(y0 + pltpu.roll(y1, n - 1, 0) + b1_ref[...])
    o = _dot(h.astype(BF16), w2_ref[...]) + b2_ref[...]
    k = _rope_lanes(o[:, :LANES], cs_ref[0], sn_ref[0])
    v_t = o[:, LANES:].T
    for g in range(NSA_KV_GROUPS):
        sl = slice(g * NSA_DH, (g + 1) * NSA_DH)
        kc_ref[0, g] = k[:, sl].astype(BF16)
        vc_ref[0, g] = v_t[sl].astype(BF16)


def _compress(kvc, cs_c, sn_c, pos_k, w1_k, b1_k, w2_k, b2_k, pos_v, w1_v, b1_v, w2_v, b2_v):
    B, S, _ = kvc.shape
    n = S // CMP_STRIDE
    ns = 2 * NSA_KV_GROUPS
    eye = jnp.eye(ns, dtype=F32)
    w1s = jnp.stack([w1_k, w1_k, w1_v, w1_v])
    poss = jnp.stack([pos_k, pos_k, pos_v, pos_v])
    wbig, pbig = [], []
    for m in range(CMP_BLOCK // CMP_STRIDE):
        wm = w1s.reshape(ns, CMP_BLOCK, NSA_DH, CMP_HIDDEN)[:, m * CMP_STRIDE:(m + 1) * CMP_STRIDE]
        wb = jnp.einsum('ctdj,ce->tcdej', wm, eye).reshape(CMP_STRIDE * ns * NSA_DH, ns * CMP_HIDDEN)
        wbig.append(wb.astype(BF16))
        pm = poss[:, m * CMP_STRIDE:(m + 1) * CMP_STRIDE]
        pbig.append(pm.transpose(1, 0, 2).reshape(CMP_STRIDE, ns * NSA_DH))
    b1 = jnp.concatenate([b1_k, b1_k, b1_v, b1_v]).reshape(1, ns * CMP_HIDDEN)
    w2s = jnp.stack([w2_k, w2_k, w2_v, w2_v])
    w2big = jnp.einsum('cjd,ce->cjed', w2s, eye).reshape(ns * CMP_HIDDEN, ns * NSA_DH).astype(BF16)
    b2 = jnp.concatenate([b2_k, b2_k, b2_v, b2_v]).reshape(1, ns * NSA_DH)
    full = lambda a: pl.BlockSpec(a.shape, lambda b: (0,) * a.ndim)
    sds = jax.ShapeDtypeStruct
    return pl.pallas_call(
        _compress_body,
        grid=(B,),
        in_specs=[pl.BlockSpec((1, S, ns * NSA_DH), lambda b: (b, 0, 0)),
                  full(pbig[0]), full(pbig[1]), full(wbig[0]), full(wbig[1]), full(b1), full(w2big), full(b2),
                  pl.BlockSpec((1, n, LANES), lambda b: (b, 0, 0)),
                  pl.BlockSpec((1, n, LANES), lambda b: (b, 0, 0))],
        out_specs=(pl.BlockSpec((1, NSA_KV_GROUPS, n, NSA_DH), lambda b: (b, 0, 0, 0)),
                   pl.BlockSpec((1, NSA_KV_GROUPS, NSA_DH, n), lambda b: (b, 0, 0, 0))),
        out_shape=(sds((B, NSA_KV_GROUPS, n, NSA_DH), BF16), sds((B, NSA_KV_GROUPS, NSA_DH, n), BF16)),
        scratch_shapes=[pltpu.VMEM((S, LANES), F32), pltpu.VMEM((S, LANES), F32)],
        compiler_params=pltpu.CompilerParams(
            dimension_semantics=("parallel",), vmem_limit_bytes=VMEM_LIMIT),
        name="compress",
    )(kvc, pbig[0], pbig[1], wbig[0], wbig[1], b1, w2big, b2, cs_c, sn_c)


NSA_ONES = 16


def _with_ones(v_t):
    return jnp.concatenate([v_t, jnp.ones((NSA_ONES, v_t.shape[1]), v_t.dtype)], axis=0)


def _softmax_step(s, m, acc, v_t):
    m_new = jnp.maximum(m, jnp.max(s, axis=0, keepdims=True))
    p = jnp.exp2(s - m_new).astype(BF16)
    acc = jnp.exp2(m - m_new) * acc + _dot(_with_ones(v_t), p)
    return m_new, acc


def _nsa_groups_body(q_ref, kc_ref, vc_ref, ks_ref, vs_ref, kw_ref, vw_ref, ng_ref, ovt_ref,
                     o_ref, sa_ref, sb_ref, sw_ref, *, tq, tk, n_slc):
    qi = pl.program_id(1)
    q0 = qi * tq
    n_cmp = kc_ref.shape[2]
    W = NSA_HPG * tq
    G = range(NSA_KV_GROUPS)
    per_head = lambda x: jnp.concatenate([x] * NSA_HPG, axis=1)
    q_all = [jnp.concatenate([q_ref[0, g * NSA_HPG + h] for h in range(NSA_HPG)], axis=1) for g in G]
    t_lane = q0 + lax.broadcasted_iota(jnp.int32, (1, tq), 1)

    cmp_end = CMP_STRIDE * lax.broadcasted_iota(jnp.int32, (n_cmp, 1), 0) + (CMP_BLOCK - 1)
    ok = cmp_end <= t_lane
    cmp_bias = per_head(jnp.where(ok, 0.0, NEG_INF))
    cmp_keep = per_head(jnp.where(ok, 1.0, 0.0))
    sm = [_dot(kc_ref[0, g], q_all[g]) + cmp_bias for g in G]
    wk = WINDOW + tq
    w0 = pl.multiple_of(jnp.maximum(q0 - WINDOW, 0), tq)
    kpos = w0 + lax.broadcasted_iota(jnp.int32, (wk, 1), 0)
    win_bias = per_head(jnp.where(jnp.logical_and(kpos <= t_lane, kpos > t_lane - WINDOW), 0.0, NEG_INF))
    for g in G:
        sw_ref[g] = _dot(kw_ref[0, g, pl.ds(w0, wk), :], q_all[g]) + win_bias
    blk = lax.broadcasted_iota(jnp.int32, (n_slc, tq), 0)
    cur = (q0 + lax.broadcasted_iota(jnp.int32, (n_slc, tq), 1)) // SLC_BLOCK
    forced = (blk == 0) | (blk == cur) | (blk == cur - 1)
    o_cmp, imp = [], []
    for g in G:
        e = jnp.exp2(sm[g] - jnp.max(sm[g], axis=0, keepdims=True))
        p = e * (1.0 / jnp.sum(e, axis=0, keepdims=True)) * cmp_keep
        o_cmp.append(_dot(vc_ref[0, g], p.astype(BF16)))
        p_sum = p[:, 0:tq]
        for h in range(1, NSA_HPG):
            p_sum = p_sum + p[:, h * tq:(h + 1) * tq]
        p_hi = p_sum.astype(BF16)
        p_lo = (p_sum - p_hi.astype(F32)).astype(BF16)
        imp_g = _dot(ovt_ref[...], p_hi) + _dot(ovt_ref[...], p_lo)
        imp.append(jnp.where(blk <= cur, jnp.where(forced, FORCED_SCORE, imp_g), NEG_INF))

    SUBL = 8

    def block_bias(n):
        out = []
        for g in G:
            groups = [imp[g][r:r + SUBL] for r in range(0, n, SUBL)]
            ranks = [jnp.zeros((SUBL, tq), F32) for _ in groups]
            for i in range(n):
                row = imp[g][i:i + 1, :]
                for gidx, grp in enumerate(groups):
                    r = gidx * SUBL
                    ge = jnp.where(row >= grp, 1.0, 0.0)
                    gt = jnp.where(row > grp, 1.0, 0.0)
                    if r > i:
                        inc = ge
                    elif r + SUBL - 1 <= i:
                        inc = gt
                    else:
                        inc = jnp.where(blk[r:r + SUBL] > i, ge, gt)
                    ranks[gidx] = ranks[gidx] + inc
            bias = jnp.where(jnp.concatenate(ranks, axis=0) < float(min(SLC_TOPK, n_slc)), 0.0, NEG_INF)
            out.append(jnp.concatenate([bias, jnp.zeros((NSA_DH - n, tq), F32)], axis=0) if n < NSA_DH else bias)
        return tuple(out)

    quarter = n_slc // 4
    visible = (q0 + tq - 1) // SLC_BLOCK + 1
    bias = lax.cond(
        visible <= 2 * quarter,
        lambda: lax.cond(visible <= quarter, lambda: block_bias(quarter), lambda: block_bias(2 * quarter)),
        lambda: lax.cond(visible <= 3 * quarter, lambda: block_bias(3 * quarter), lambda: block_bias(n_slc)))
    q_aug = [jnp.concatenate([q_all[g], per_head(bias[g]).astype(BF16)], axis=0) for g in G]

    kpos_l = lax.broadcasted_iota(jnp.int32, (tk, 1), 0)
    n_tiles = q0 // tk + 1
    last_tile = ks_ref.shape[2] // tk - 1

    def tile_start(jt):
        return pl.multiple_of(jnp.minimum(jt, last_tile) * tk, tk)

    def scores_into(s_ref, jt):
        causal = per_head(jnp.where(jt * tk + kpos_l <= t_lane, 0.0, NEG_INF))
        for g in G:
            s_ref[g] = _dot(ks_ref[0, g, pl.ds(tile_start(jt), tk), :], q_aug[g]) + causal

    def consume(s_ref, jt, carry):
        return tuple(_softmax_step(s_ref[g], *carry[g], vs_ref[0, g, :, pl.ds(tile_start(jt), tk)]) for g in G)

    def slc_pair(i, carry):
        scores_into(sb_ref, 2 * i + 1)
        carry = consume(sa_ref, 2 * i, carry)
        scores_into(sa_ref, 2 * i + 2)
        return consume(sb_ref, 2 * i + 1, carry)

    init = tuple((jnp.full((1, W), NEG_INF, F32), jnp.zeros((NSA_DH + NSA_ONES, W), F32)) for g in G)
    scores_into(sa_ref, 0)

    acc_w = []
    for g in G:
        sw = sw_ref[g]
        ew = jnp.exp2(sw - jnp.max(sw, axis=0, keepdims=True)).astype(BF16)
        acc_w.append(_dot(_with_ones(vw_ref[0, g, :, pl.ds(w0, wk)]), ew))

    slc = lax.fori_loop(0, (n_tiles + 1) // 2, slc_pair, init)

    for g in G:
        def gate(j):
            return jnp.concatenate([ng_ref[0, (g * NSA_HPG + h) * 3 + j:(g * NSA_HPG + h) * 3 + j + 1, :]
                                    for h in range(NSA_HPG)], axis=1)

        acc_s = slc[g][1]
        l_s, l_w = acc_s[NSA_DH:NSA_DH + 1], acc_w[g][NSA_DH:NSA_DH + 1]
        out_t = (gate(0) * o_cmp[g] + (gate(1) * (1.0 / l_s)) * acc_s[:NSA_DH]
                 + (gate(2) * (1.0 / l_w)) * acc_w[g][:NSA_DH])
        out_t = jnp.concatenate([out_t[:, h * tq:(h + 1) * tq] for h in range(NSA_HPG)], axis=0)
        o_ref[0, :, g * NSA_HPG * NSA_DH:(g + 1) * NSA_HPG * NSA_DH] = out_t.T.astype(o_ref.dtype)


def _nsa_attention(q, kc, vc, ks, vs, kw, vw, ng):
    B, H, dh, S = q.shape
    tq = min(ATT_TQ, S)
    tk = min(ATT_TK, S)
    assert S % tk == 0 and tk % tq == 0 and S >= WINDOW + tq and S // SLC_BLOCK <= dh
    n_cmp = kc.shape[2]
    n_slc = S // SLC_BLOCK
    c0 = CMP_STRIDE * np.arange(n_cmp)[None, :]
    s0 = SLC_BLOCK * np.arange(n_slc)[:, None]
    ov = np.clip(np.minimum(c0 + CMP_BLOCK, s0 + SLC_BLOCK) - np.maximum(c0, s0), 0, None) / CMP_BLOCK
    ovt = jnp.asarray(ov, BF16)
    G = NSA_KV_GROUPS
    grid = (B, S // tq)
    k_spec = lambda n, w: pl.BlockSpec((1, G, n, w), lambda b, i: (b, 0, 0, 0))
    vt_spec = lambda n: pl.BlockSpec((1, G, dh, n), lambda b, i: (b, 0, 0, 0))
    return pl.pallas_call(
        functools.partial(_nsa_groups_body, tq=tq, tk=tk, n_slc=n_slc),
        grid=grid,
        in_specs=[
            pl.BlockSpec((1, H, dh, tq), lambda b, i: (b, 0, 0, i)),
            k_spec(n_cmp, dh), vt_spec(n_cmp),
            k_spec(S, 2 * dh), vt_spec(S),
            k_spec(S, dh), vt_spec(S),
            pl.BlockSpec((1, LANES, tq), lambda b, i: (b, 0, i)),
            pl.BlockSpec(ovt.shape, lambda b, i: (0, 0)),
        ],
        out_specs=pl.BlockSpec((1, tq, H * dh), lambda b, i: (b, i, 0)),
        out_shape=jax.ShapeDtypeStruct((B, S, H * dh), BF16),
        scratch_shapes=[pltpu.VMEM((G, tk, NSA_HPG * tq), F32),
                        pltpu.VMEM((G, tk, NSA_HPG * tq), F32),
                        pltpu.VMEM((G, WINDOW + tq, NSA_HPG * tq), F32)],
        compiler_params=pltpu.CompilerParams(
            dimension_semantics=("parallel", "arbitrary"), vmem_limit_bytes=VMEM_LIMIT),
        name="nsa_attn",
    )(q, kc, vc, ks, vs, kw, vw, ng, ovt)


def _dot_3pass(a, b):
    a1 = a.astype(BF16)
    a2 = (a - a1.astype(F32)).astype(BF16)
    b1 = b.astype(BF16)
    b2 = (b - b1.astype(F32)).astype(BF16)
    return _dot(a1, b1) + (_dot(a1, b2) + _dot(a2, b1))


def _gla_pair_body(q_ref, k_ref, v_ref, vt_ref, b_ref, r_ref, ng_ref,
                   o_ref, state_ref, kp_ref, bp_ref, vp_ref, plain_ref, *, n_chunks):
    C, SUB, dk, dv, hp = GLA_C, GLA_SUB, GLA_DK, GLA_DV, GLA_HP
    W = hp * dk
    lane = lax.broadcasted_iota(jnp.int32, (1, W), 1)
    of_head = [lane // dk == hh for hh in range(hp)]
    t_loc = lax.broadcasted_iota(jnp.int32, (C, 1), 0)
    state_ref[...] = jnp.zeros_like(state_ref)
    kp_ref[0:SUB, :] = jnp.zeros((SUB, W), F32)
    bp_ref[0:SUB, :] = jnp.zeros((SUB, W), F32)
    vp_ref[:, 0:SUB, :] = jnp.zeros((hp, SUB, dv), F32)

    def decay_chunk(c, all_plain):
        b_last = b_ref[0, pl.ds(c * C + (C - 1), 1), :]
        plain = (jnp.min(b_last) > -GLA_PLAIN_DECAY).astype(jnp.int32)
        plain_ref[c] = plain
        return jnp.minimum(all_plain, plain)

    all_plain = lax.fori_loop(0, n_chunks, decay_chunk, jnp.int32(1), unroll=GLA_UNROLL_DECAY)

    def chunk(c, check_decay):
        c0 = pl.multiple_of(c * C, C)
        b = b_ref[0, pl.ds(c0, C), :]
        q = q_ref[0, pl.ds(c0, C), :]
        k = k_ref[0, pl.ds(c0, C), :]
        v = [v_ref[0, pl.ds(c0, C), hh * dv:(hh + 1) * dv] for hh in range(hp)]
        vt = [vt_ref[0, hh * dv:(hh + 1) * dv, pl.ds(c0, C)] for hh in range(hp)]
        b_last = b[C - 1:C, :]
        st = state_ref[...]
        st_b = st.astype(BF16)
        qg = q * jnp.exp(b)
        qg_h = [jnp.where(of_head[hh], qg, 0.0).astype(BF16) for hh in range(hp)]
        o_inter = [_dot_nt(qg_h[hh], st_b) for hh in range(hp)]

        def intra_plain():
            ke = (k * jnp.exp(-b)).astype(BF16)
            row = lax.broadcasted_iota(jnp.int32, (C, C), 0)
            col = lax.broadcasted_iota(jnp.int32, (C, C), 1)
            return tuple(_dot(jnp.where(row >= col, _dot_nt(qg_h[hh], ke), 0.0).astype(BF16), v[hh])
                         for hh in range(hp))

        def intra_strong_decay():
            far = [[jnp.zeros((SUB, dv), F32)] for _ in range(hp)]
            for i in range(1, C // SUB):
                r0 = i * SUB
                b_first = b[r0:r0 + 1, :]
                qt = q[r0:r0 + SUB] * jnp.exp(b[r0:r0 + SUB] - b_first)
                kt = (k[:r0] * jnp.exp(b_first - b[:r0])).astype(BF16)
                lt = lax.broadcasted_iota(jnp.int32, (SUB, r0), 0)
                ls = lax.broadcasted_iota(jnp.int32, (SUB, r0), 1)
                for hh in range(hp):
                    a = _dot_nt(jnp.where(of_head[hh], qt, 0.0).astype(BF16), kt)
                    a = jnp.where(lt + (r0 - SUB) >= ls, a, 0.0)
                    far[hh].append(_dot(a.astype(BF16), v[hh][:r0]))
            acc = [jnp.concatenate(far[hh], axis=0) for hh in range(hp)]
            kp_ref[SUB:SUB + C, :] = k
            bp_ref[SUB:SUB + C, :] = b
            for hh in range(hp):
                vp_ref[hh, SUB:SUB + C, :] = v[hh].astype(F32)
            for d in range(SUB):
                kd = kp_ref[SUB - d:SUB - d + C, :]
                bd = bp_ref[SUB - d:SUB - d + C, :]
                valid = t_loc >= d
                w = jnp.exp(jnp.where(valid, b - bd, 0.0))
                x = jnp.where(valid, q * kd * w, 0.0)
                for hh in range(hp):
                    a_d = jnp.sum(jnp.where(of_head[hh], x, 0.0), axis=-1, keepdims=True)
                    acc[hh] = acc[hh] + a_d * vp_ref[hh, SUB - d:SUB - d + C, :]
            return tuple(acc)

        if check_decay:
            o_intra = lax.cond(plain_ref[c] > 0, intra_plain, intra_strong_decay)
        else:
            o_intra = intra_plain()
        k_dec = (k * jnp.exp(b_last - b)).astype(BF16)
        upd = _dot(vt[hp - 1], k_dec)
        for hh in range(hp - 2, -1, -1):
            upd = jnp.where(of_head[hh], _dot(vt[hh], k_dec), upd)
        state_ref[...] = st * jnp.exp(b_last) + upd
        for hh in range(hp):
            o = o_inter[hh] + o_intra[hh]
            o = o * lax.rsqrt(jnp.mean(o * o, axis=-1, keepdims=True) + EPS) * ng_ref[hh]
            gate = jax.nn.silu(r_ref[0, pl.ds(c0, C), hh * dv:(hh + 1) * dv])
            o_ref[0, pl.ds(c0, C), hh * dv:(hh + 1) * dv] = (o * gate).astype(o_ref.dtype)
        return 0

    @pl.when(all_plain > 0)
    def _():
        lax.fori_loop(0, n_chunks, lambda c, _: chunk(c, False), 0, unroll=GLA_UNROLL_PLAIN)

    @pl.when(all_plain <= 0)
    def _():
        lax.fori_loop(0, n_chunks, lambda c, _: chunk(c, True), 0)


def _gla_pairs(gq, gk, gv, gvt, gb, gr, norm_g):
    B, S, _ = gq.shape
    H, dk, dv, hp = GLA_HEADS, GLA_DK, GLA_DV, GLA_HP
    W = hp * dk
    assert W == LANES and H % hp == 0
    ng = norm_g.reshape(H, 1, dv)
    n_chunks = S // GLA_C
    tok_spec = lambda w: pl.BlockSpec((1, S, hp * w), lambda b, h: (b, 0, h))
    return pl.pallas_call(
        functools.partial(_gla_pair_body, n_chunks=n_chunks),
        grid=(B, H // hp),
        in_specs=[tok_spec(dk), tok_spec(dk), tok_spec(dv),
                  pl.BlockSpec((1, hp * dv, S), lambda b, h: (b, h, 0)),
                  tok_spec(dk),
                  tok_spec(dv),
                  pl.BlockSpec((hp, 1, dv), lambda b, h: (h, 0, 0))],
        out_specs=tok_spec(dv),
        out_shape=jax.ShapeDtypeStruct((B, S, H * dv), BF16),
        scratch_shapes=[pltpu.VMEM((dv, W), F32),
                        pltpu.VMEM((GLA_SUB + GLA_C, W), F32),
                        pltpu.VMEM((GLA_SUB + GLA_C, W), F32),
                        pltpu.VMEM((hp, GLA_SUB + GLA_C, dv), F32),
                        pltpu.SMEM((n_chunks,), jnp.int32)],
        compiler_params=pltpu.CompilerParams(
            dimension_semantics=("parallel", "parallel"), vmem_limit_bytes=VMEM_LIMIT),
        name="gla",
    )(gq, gk, gv, gvt, gb, gr, ng)


def _merge_body(x_ref, ya_ref, yb_ref, ma_ref, mb_ref, wpa_ref, wpb_ref, wo_ref, gf_ref, wr_ref, br_ref,
                h_ref, v_ref, comb_ref):
    y_a = _dot(ya_ref[...], wpa_ref[...])
    y_b = _dot(yb_ref[...], wpb_ref[...])
    mixed = ma_ref[...] * y_a + mb_ref[...] * y_b
    h = x_ref[...] + _dot(mixed.astype(BF16), wo_ref[...])
    h_ref[...] = h
    v = h * lax.rsqrt(jnp.mean(h * h, axis=-1, keepdims=True) + EPS) * gf_ref[...]
    v_ref[...] = v.astype(BF16)
    wr = wr_ref[...]
    w_hi = wr.astype(BF16)
    w_both = jnp.concatenate([w_hi, (wr - w_hi.astype(F32)).astype(BF16)], axis=1)
    v_hi = v.astype(BF16)
    v_lo = (v - v_hi.astype(F32)).astype(BF16)
    both = _dot(v_hi, w_both)
    logits = (both[:, :LANES] + both[:, LANES:]) + _dot(v_lo, w_hi) + br_ref[...]
    comb_ref[...] = _route_weights(logits)


def _route_weights(logits):
    lane = lax.broadcasted_iota(jnp.int32, logits.shape, 1)
    is_grp = jnp.logical_and(lane >= MOE_EXPERTS, lane < MOE_EXPERTS + MOE_GROUPS)
    lg = jnp.where(is_grp, logits, NEG_INF)
    eg = jnp.where(is_grp, jnp.exp(lg - jnp.max(lg, axis=-1, keepdims=True)), 0.0)
    pg = eg / jnp.sum(eg, axis=-1, keepdims=True)
    p_grp = jnp.max(pg, axis=-1, keepdims=True)
    g_sel = jnp.min(jnp.where(jnp.logical_and(is_grp, pg == p_grp), lane, 2 * LANES),
                    axis=-1, keepdims=True) - MOE_EXPERTS
    in_grp = jnp.logical_and(lane < MOE_EXPERTS, lane // MOE_EPG == g_sel)
    le = jnp.where(in_grp, logits, NEG_INF)
    ee = jnp.where(in_grp, jnp.exp(le - jnp.max(le, axis=-1, keepdims=True)), 0.0)
    pin = ee / jnp.sum(ee, axis=-1, keepdims=True)
    p1 = jnp.max(jnp.where(in_grp, pin, -1.0), axis=-1, keepdims=True)
    i1 = jnp.min(jnp.where(jnp.logical_and(in_grp, pin == p1), lane, 2 * LANES), axis=-1, keepdims=True)
    rest = jnp.logical_and(in_grp, lane != i1)
    p2 = jnp.max(jnp.where(rest, pin, -1.0), axis=-1, keepdims=True)
    i2 = jnp.min(jnp.where(jnp.logical_and(rest, pin == p2), lane, 2 * LANES), axis=-1, keepdims=True)
    tot = p1 + p2
    return jnp.where(lane == i1, p_grp * p1 / tot, 0.0) + jnp.where(lane == i2, p_grp * p2 / tot, 0.0)


def _merge(x2, ya, yb, ma, mb, w_proj_nsa, w_proj_gla, w_out, g_ffn, w_grp, b_grp, w_exp, b_exp):
    T, D = x2.shape
    tm = min(MERGE_TM, T)
    wr = jnp.pad(jnp.concatenate([w_exp, w_grp], axis=1), ((0, 0), (0, LANES - MOE_EXPERTS - MOE_GROUPS)))
    br = jnp.pad(jnp.concatenate([b_exp, b_grp]), (0, LANES - MOE_EXPERTS - MOE_GROUPS)).reshape(1, LANES)
    tok = lambda w: pl.BlockSpec((tm, w), lambda i: (i, 0))
    full = lambda a: pl.BlockSpec(a.shape, lambda i: (0, 0))
    wpa, wpb, wo = w_proj_nsa.astype(BF16), w_proj_gla.astype(BF16), w_out.astype(BF16)
    gf = g_ffn.reshape(1, D)
    sds = jax.ShapeDtypeStruct
    return pl.pallas_call(
        _merge_body,
        grid=(T // tm,),
        in_specs=[tok(D), tok(ya.shape[1]), tok(yb.shape[1]), tok(D), tok(D),
                  full(wpa), full(wpb), full(wo), full(gf), full(wr), full(br)],
        out_specs=(tok(D), tok(D), tok(LANES)),
        out_shape=(sds((T, D), F32), sds((T, D), BF16), sds((T, LANES), F32)),
        compiler_params=pltpu.CompilerParams(
            dimension_semantics=("parallel",), vmem_limit_bytes=VMEM_LIMIT),
        name="merge",
    )(x2, ya, yb, ma, mb, wpa, wpb, wo, gf, wr, br)


def _moe_rows(tw):
    rows = 2 * tw + MOE_EXPERTS * (MOE_ALIGN - 1) + MOE_RT
    return -(-rows // MOE_RT) * MOE_RT


def _moe_window(e, phase, v_ref, comb_ref, tri_ref, y_ref, xs_ref, z_ref, meta_ref, pos_ref, tw, rmax):
    no_row = -1.0

    @pl.when(e == 0 if phase == "route" else False)
    def _route():
        comb = comb_ref[...]
        assigned = comb > 0.0
        a = jnp.where(assigned, 1.0, 0.0)
        tri = tri_ref[...]
        run = jnp.zeros((1, LANES), F32)
        ranks = []
        for b in range(tw // MOE_RT):
            ab = a[b * MOE_RT:(b + 1) * MOE_RT]
            ranks.append(_dot(tri, ab.astype(BF16)) + run)
            run = run + jnp.sum(ab, axis=0, keepdims=True)
        rank = jnp.concatenate(ranks, axis=0)
        cnt_pad = jnp.floor((run + (MOE_ALIGN - 1)) * (1.0 / MOE_ALIGN)) * MOE_ALIGN
        incl = jnp.broadcast_to(cnt_pad, (8, LANES))
        lane8 = lax.broadcasted_iota(jnp.int32, (8, LANES), 1)
        shift = 1
        while shift < MOE_EXPERTS:
            incl = incl + jnp.where(lane8 >= shift, pltpu.roll(incl, shift, 1), 0.0)
            shift *= 2
        offs = incl[0:1] - cnt_pad
        meta_ref[0:1, :] = offs * MOE_PACK + run
        row_of = offs + rank
        pos_a = jnp.min(jnp.where(assigned, row_of, 1e9), axis=-1, keepdims=True)
        pos_b = jnp.max(jnp.where(assigned, row_of, no_row), axis=-1, keepdims=True)
        pos_a = jnp.where(pos_a > 1e8, no_row, pos_a)
        pos_b = jnp.where(pos_b == pos_a, no_row, pos_b)
        w_a = jnp.sum(jnp.where(jnp.logical_and(assigned, row_of == pos_a), comb, 0.0), axis=-1, keepdims=True)
        w_b = jnp.sum(jnp.where(jnp.logical_and(assigned, row_of == pos_b), comb, 0.0), axis=-1, keepdims=True)
        lane_t = lax.broadcasted_iota(jnp.int32, (tw, LANES), 1)
        pos_ref[...] = jnp.where(lane_t == 0, pos_a, jnp.where(lane_t == 1, pos_b, jnp.where(
            lane_t == 2, w_a, jnp.where(lane_t == 3, w_b, no_row))))
        pos_t = pos_ref[...].T
        pa, pb = pos_t[0:1], pos_t[1:2]
        v = v_ref[...]
        for rt in range(rmax // MOE_RT):
            r = (rt * MOE_RT + lax.broadcasted_iota(jnp.int32, (MOE_RT, 1), 0)).astype(F32)
            p = jnp.where(r == pa, 1.0, jnp.where(r == pb, 1.0, 0.0)).astype(BF16)
            xs_ref[rt * MOE_RT:(rt + 1) * MOE_RT, :] = _dot(p, v).astype(BF16)
        z_ref[...] = jnp.zeros_like(z_ref)

    @pl.when(e == pl.num_programs(1) - 1 if phase == "combine" else False)
    def _combine():
        r = lax.broadcasted_iota(jnp.int32, (1, rmax), 1).astype(F32)
        z = z_ref[...]
        for tt in range(tw // MOE_RT):
            rows = slice(tt * MOE_RT, (tt + 1) * MOE_RT)
            pa, pb = pos_ref[rows, 0:1], pos_ref[rows, 1:2]
            w_a, w_b = pos_ref[rows, 2:3], pos_ref[rows, 3:4]
            q = jnp.where(r == pa, w_a, jnp.where(r == pb, w_b, 0.0)).astype(BF16)
            y_ref[rows, :] = _dot(q, z).astype(y_ref.dtype)


def _moe_body(v_ref, comb_ref, wg_ref, wu_ref, wd_ref, tri_ref, y_ref,
              xs_ref, z_ref, meta_ref, pos_ref, *, tw, rmax, n_win):
    e = pl.program_id(1)
    lane = lax.broadcasted_iota(jnp.int32, (1, LANES), 1)

    def window_phase(phase):
        for w in range(n_win):
            rows = pl.ds(w * tw, tw)
            _moe_window(e, phase, v_ref.at[rows], comb_ref.at[rows], tri_ref, y_ref.at[rows],
                        xs_ref.at[w], z_ref.at[w], meta_ref.at[w], pos_ref.at[w], tw, rmax)

    window_phase("route")

    packed = [jnp.sum(jnp.where(lane == e, meta_ref[w, 0:1, :], 0.0)).astype(jnp.int32) for w in range(n_win)]
    offs = [p // MOE_PACK for p in packed]
    tiles = [(p % MOE_PACK + MOE_ET - 1) // MOE_ET for p in packed]

    def expert(x):
        hdn = jax.nn.silu(_dot(x, wg_ref[0])) * _dot(x, wu_ref[0])
        return _dot(hdn.astype(BF16), wd_ref[0]).astype(BF16)

    def joint_tile(i, _):
        r0 = [pl.multiple_of(offs[w] + i * MOE_ET, MOE_ALIGN) for w in range(n_win)]
        z = expert(jnp.concatenate([xs_ref[w, pl.ds(r0[w], MOE_ET), :] for w in range(n_win)], axis=0))
        for w in range(n_win):
            z_ref[w, pl.ds(r0[w], MOE_ET), :] = z[w * MOE_ET:(w + 1) * MOE_ET]
        return 0

    n_joint = functools.reduce(jnp.minimum, tiles)
    lax.fori_loop(0, n_joint, joint_tile, 0)
    for w in range(n_win):
        def own_tile(i, _, w=w):
            r0 = pl.multiple_of(offs[w] + i * MOE_ET, MOE_ALIGN)
            z_ref[w, pl.ds(r0, MOE_ET), :] = expert(xs_ref[w, pl.ds(r0, MOE_ET), :])
            return 0

        lax.fori_loop(n_joint, tiles[w], own_tile, 0)

    window_phase("combine")


def _moe(v, comb, w_gate, w_up, w_down):
    T, D = v.shape
    tw = min(MOE_TW, T)
    n_win = min(MOE_WINDOWS, T // tw)
    rmax = _moe_rows(tw)
    E, _, F = w_gate.shape
    wg, wu, wd = w_gate.astype(BF16), w_up.astype(BF16), w_down.astype(BF16)
    tri = jnp.asarray(np.tril(np.ones((MOE_RT, MOE_RT)), -1), BF16)
    tok = lambda w: pl.BlockSpec((n_win * tw, w), lambda i, e: (i, 0))
    tok_in = lambda w: pl.BlockSpec((n_win * tw, w), lambda i, e: (i, 0), pipeline_mode=pl.Buffered(1))
    return pl.pallas_call(
        functools.partial(_moe_body, tw=tw, rmax=rmax, n_win=n_win),
        grid=(T // (n_win * tw), E),
        in_specs=[tok_in(D), tok_in(LANES),
                  pl.BlockSpec((1, D, F), lambda i, e: (e, 0, 0)),
                  pl.BlockSpec((1, D, F), lambda i, e: (e, 0, 0)),
                  pl.BlockSpec((1, F, D), lambda i, e: (e, 0, 0)),
                  pl.BlockSpec(tri.shape, lambda i, e: (0, 0))],
        out_specs=tok(D),
        out_shape=jax.ShapeDtypeStruct((T, D), BF16),
        scratch_shapes=[pltpu.VMEM((n_win, rmax, D), BF16),
                        pltpu.VMEM((n_win, rmax, D), BF16),
                        pltpu.VMEM((n_win, 8, LANES), F32),
                        pltpu.VMEM((n_win, tw, LANES), F32)],
        compiler_params=pltpu.CompilerParams(
            dimension_semantics=("parallel", "arbitrary"), vmem_limit_bytes=VMEM_LIMIT),
        name="moe",
    )(v, comb, wg, wu, wd, tri)


def _final_body(h_ref, y_ref, g_ref, o_ref):
    h = h_ref[...] + y_ref[...].astype(F32)
    o_ref[...] = h * lax.rsqrt(jnp.mean(h * h, axis=-1, keepdims=True) + EPS) * g_ref[...]


def _final_norm(h, y, g_final):
    T, D = h.shape
    tm = min(FINAL_TM, T)
    tok = pl.BlockSpec((tm, D), lambda i: (i, 0))
    return pl.pallas_call(
        _final_body,
        grid=(T // tm,),
        in_specs=[tok, tok, pl.BlockSpec((1, D), lambda i: (0, 0))],
        out_specs=tok,
        out_shape=jax.ShapeDtypeStruct((T, D), F32),
        compiler_params=pltpu.CompilerParams(
            dimension_semantics=("parallel",), vmem_limit_bytes=VMEM_LIMIT),
        name="final_norm",
    )(h, y, g_final.reshape(1, D))


def _rope_inv_freq():
    half = NSA_DH // 2
    return 1.0 / (ROPE_THETA ** (jnp.arange(half, dtype=F32) / half))


def _rope_tables(positions):
    ang = positions.astype(F32)[..., None] * _rope_inv_freq()
    cos, sin = jnp.cos(ang), jnp.sin(ang)
    cs = jnp.concatenate([cos, cos, cos, cos], axis=-1)
    sn = jnp.concatenate([-sin, sin, -sin, sin], axis=-1)
    return cs, sn


def _layer(h, positions, g_mix, w_in, cmp_pos_k, cmp_w1_k, cmp_b1_k, cmp_w2_k, cmp_b2_k,
           cmp_pos_v, cmp_w1_v, cmp_b1_v, cmp_w2_v, cmp_b2_v, gla_w_a2, gla_b_a, gla_norm_g,
           w_proj_nsa, w_proj_gla, w_out, g_ffn, w_grp, b_grp, w_exp, b_exp, w_gate, w_up, w_down, g_out):
    B, S, D = h.shape
    n_chunks = S // CMP_STRIDE
    cmp_end = jnp.minimum(CMP_STRIDE * jnp.arange(n_chunks) + CMP_BLOCK - 1, S - 1)
    cs_c, sn_c = _rope_tables(jnp.take(positions, cmp_end, axis=1))
    (q, kvc, ks, vs, kw, vw, gq, gk, gv, gvt, gr, ma, mb, ng, gb) = _in_proj(h, g_mix, w_in, positions,
                                                                             gla_w_a2, gla_b_a)
    kc, vc = _compress(kvc, cs_c, sn_c, cmp_pos_k, cmp_w1_k, cmp_b1_k, cmp_w2_k, cmp_b2_k,
                       cmp_pos_v, cmp_w1_v, cmp_b1_v, cmp_w2_v, cmp_b2_v)
    ya = _nsa_attention(q, kc, vc, ks, vs, kw, vw, ng)
    yb = _gla_pairs(gq, gk, gv, gvt, gb, gr, gla_norm_g)
    T = B * S
    h1, v, comb = _merge(h.reshape(T, D), ya.reshape(T, -1), yb.reshape(T, -1), ma.reshape(T, D), mb.reshape(T, D),
                         w_proj_nsa, w_proj_gla, w_out, g_ffn, w_grp, b_grp, w_exp, b_exp)
    y = _moe(v, comb, w_gate, w_up, w_down)
    return _final_norm(h1, y, g_out).reshape(B, S, D)


def kernel(x, positions, g_mix, w_in, cmp_pos_k, cmp_w1_k, cmp_b1_k, cmp_w2_k, cmp_b2_k, cmp_pos_v, cmp_w1_v,
           cmp_b1_v, cmp_w2_v, cmp_b2_v, gla_w_a2, gla_b_a, gla_norm_g, w_proj_nsa, w_proj_gla, w_out, g_ffn,
           w_grp, b_grp, w_exp, b_exp, w_gate, w_up, w_down, g_final):
    depth = g_mix.shape[0]
    assert depth == 1, "the final norm closes the single layer"
    return _layer(x, positions, g_mix[0], w_in[0], cmp_pos_k[0], cmp_w1_k[0], cmp_b1_k[0], cmp_w2_k[0],
                  cmp_b2_k[0], cmp_pos_v[0], cmp_w1_v[0], cmp_b1_v[0], cmp_w2_v[0], cmp_b2_v[0],
                  gla_w_a2[0], gla_b_a[0], gla_norm_g[0], w_proj_nsa[0], w_proj_gla[0], w_out[0],
                  g_ffn[0], w_grp[0], b_grp[0], w_exp[0], b_exp[0], w_gate[0], w_up[0], w_down[0], g_final)
```

```python
import functools

import numpy as np
import jax
import jax.numpy as jnp
from jax import lax
from jax.experimental import pallas as pl
from jax.experimental.pallas import tpu as pltpu

F32 = jnp.float32
BF16 = jnp.bfloat16

NSA_HEADS = 8
NSA_KV_GROUPS = 2
NSA_HPG = NSA_HEADS // NSA_KV_GROUPS
NSA_DH = 64
CMP_BLOCK = 32
CMP_STRIDE = 16
CMP_HIDDEN = 256
SLC_BLOCK = 64
SLC_TOPK = 16
WINDOW = 512
GLA_HEADS = 4
GLA_DK = 64
GLA_DV = 128
GLA_RANK = 16
GLA_TAU = 16.0
MOE_GROUPS = 4
MOE_EPG = 8
MOE_EXPERTS = MOE_GROUPS * MOE_EPG
ROPE_THETA = 10000.0
EPS = 1e-6
NEG_INF = -1e30
FORCED_SCORE = 1e4
LOG2E = 1.4426950408889634

LANES = 128
VMEM_LIMIT = 56 * 1024 * 1024

IN_TM = 512
ATT_TQ = 128
ATT_TK = 256
GLA_C = 128
GLA_SUB = 16
GLA_HP = 2
GLA_UNROLL_DECAY = 4
GLA_UNROLL_PLAIN = 8
GLA_PLAIN_DECAY = 60.0
MERGE_TM = 1024
MOE_TW = 512
MOE_WINDOWS = 4
MOE_PACK = 4096
FINAL_TM = 1024
MOE_RT = 256
MOE_ET = 48
MOE_ALIGN = 16


def _dot(a, b):
    return jnp.dot(a, b, preferred_element_type=F32)


def _dot_nt(a, b):
    return lax.dot_general(a, b, (((1,), (1,)), ((), ())), preferred_element_type=F32)


def _split3(x):
    x1 = x.astype(BF16)
    r1 = x - x1.astype(F32)
    x2 = r1.astype(BF16)
    r2 = r1 - x2.astype(F32)
    x3 = r2.astype(BF16)
    return x1, x2, x3


def _dot_exact_lhs(a_bf16, x):
    x1, x2, x3 = _split3(x)
    return _dot(a_bf16, x1) + _dot(a_bf16, x2) + _dot(a_bf16, x3)


def _rope_lanes(z, cs, sn):
    w = z.shape[-1]
    lane = lax.broadcasted_iota(jnp.int32, z.shape, 1)
    first_half = (lane % NSA_DH) < (NSA_DH // 2)
    rot = jnp.where(first_half, pltpu.roll(z, w - NSA_DH // 2, 1), pltpu.roll(z, NSA_DH // 2, 1))
    reps = w // LANES
    if reps > 1:
        cs = jnp.concatenate([cs] * reps, axis=1)
        sn = jnp.concatenate([sn] * reps, axis=1)
    return z * cs + rot * sn


_SEC = {}
_off = 0
for _name, _w in (("q", 512), ("kvc", 256), ("ks", 128), ("vs", 128), ("kw", 128), ("vw", 128),
                  ("gq", 256), ("gk", 256), ("gv", 512), ("gr", 512), ("ma", 1024), ("mb", 1024),
                  ("nga", 128)):
    _SEC[_name] = (_off, _off + _w)
    _off += _w
IN_NW = _off


def _in_proj_body(x_ref, g_ref, w_ref, pos_ref, inv_ref, wa_ref, ba_ref, tri_ref,
                  q_ref, kvc_ref, ks_ref, vs_ref, kw_ref, vw_ref,
                  gq_ref, gk_ref, gv_ref, gvt_ref, gr_ref, ma_ref, mb_ref, ng_ref, gb_ref):
    x = x_ref[0]
    var = jnp.mean(x * x, axis=-1, keepdims=True)
    u = (x * lax.rsqrt(var + EPS) * g_ref[...]).astype(BF16)
    ang = pos_ref[0].astype(F32) * inv_ref[...]
    lane_r = lax.broadcasted_iota(jnp.int32, ang.shape, 1)
    cs = jnp.cos(ang)
    sn = jnp.where((lane_r % NSA_DH) < (NSA_DH // 2), -1.0, 1.0) * jnp.sin(ang)

    def proj(name):
        a, b = _SEC[name]
        return _dot(u, w_ref[:, a:b])

    zq_t = (_rope_lanes(proj("q"), cs, sn) * (NSA_DH ** -0.5 * LOG2E)).T
    for h in range(NSA_HEADS):
        q_ref[0, h] = zq_t[h * NSA_DH:(h + 1) * NSA_DH].astype(BF16)
    kvc_ref[0] = proj("kvc")
    zks = _rope_lanes(proj("ks"), cs, sn)
    zkw = _rope_lanes(proj("kw"), cs, sn)
    zvs_t = proj("vs").T
    zvw_t = proj("vw").T
    tm = zks.shape[0]
    lane = lax.broadcasted_iota(jnp.int32, (tm, LANES), 1)
    blk = (pl.program_id(1) * tm + lax.broadcasted_iota(jnp.int32, (tm, LANES), 0)) // SLC_BLOCK
    onehot = jnp.where(lane - NSA_DH == blk, 1.0, 0.0)
    for g in range(NSA_KV_GROUPS):
        sl = slice(g * NSA_DH, (g + 1) * NSA_DH)
        k_front = zks if g == 0 else pltpu.roll(zks, LANES - g * NSA_DH, 1)
        ks_ref[0, g] = jnp.where(lane < NSA_DH, k_front, onehot).astype(BF16)
        kw_ref[0, g] = zkw[:, sl].astype(BF16)
        vs_ref[0, g] = zvs_t[sl].astype(BF16)
        vw_ref[0, g] = zvw_t[sl].astype(BF16)
    gq_ref[0] = proj("gq") * (GLA_DK ** -0.5)
    gk_ref[0] = proj("gk")
    zgv = proj("gv")
    gv_ref[0] = zgv.astype(BF16)
    gvt_ref[0] = zgv.T.astype(BF16)
    gr_ref[0] = proj("gr")
    ma_ref[0] = jax.nn.sigmoid(proj("ma")).astype(ma_ref.dtype)
    mb_ref[0] = jax.nn.sigmoid(proj("mb")).astype(mb_ref.dtype)
    z_nga = proj("nga")
    ng_ref[0] = jax.nn.sigmoid(z_nga).T
    la = jax.nn.log_sigmoid(_dot_3pass(z_nga, wa_ref[...]) + ba_ref[...]) / GLA_TAU
    tri = tri_ref[...]
    gb_ref[0] = jnp.concatenate([_dot_exact_lhs(tri, la[c * GLA_C:(c + 1) * GLA_C])
                                 for c in range(la.shape[0] // GLA_C)], axis=0)


def _in_proj(x, g_mix, w_in, positions, gla_w_a2, gla_b_a):
    B, S, D = x.shape
    inv = jnp.tile(_rope_inv_freq(), LANES // (NSA_DH // 2)).reshape(1, LANES)
    tm = min(IN_TM, S)
    assert tm % GLA_C == 0
    n_gates = 3 * NSA_HEADS
    wa = jnp.pad(gla_w_a2, ((n_gates, LANES - n_gates - GLA_RANK), (0, 0)))
    ba = gla_b_a.reshape(1, -1)
    tri = jnp.asarray(np.tril(np.ones((GLA_C, GLA_C))), BF16)
    splits = np.cumsum((512,) + (128,) * 6 + (24, 256, 256, 512, 16, 512, 1024, 1024))
    (wq, wkc, wvc, wks, wvs, wkw, wvw, wng, wgq, wgk, wgv, wga, wgr, wma, wmb) = jnp.split(
        w_in, splits[:-1].tolist(), axis=1)
    pad = lambda w: jnp.pad(w, ((0, 0), (0, LANES - w.shape[1])))
    w_all = jnp.concatenate([wq, wkc, wvc, wks, wvs, wkw, wvw, wgq, wgk, wgv, wgr, wma, wmb,
                             pad(jnp.concatenate([wng, wga], axis=1))], axis=1).astype(BF16)
    assert w_all.shape[1] == IN_NW
    grid = (B, S // tm)
    tok = lambda w: pl.BlockSpec((1, tm, w), lambda b, i: (b, i, 0))
    head = lambda n, w: pl.BlockSpec((1, n, tm, w), lambda b, i: (b, 0, i, 0))
    head_t = lambda n, w: pl.BlockSpec((1, n, w, tm), lambda b, i: (b, 0, 0, i))
    sds = jax.ShapeDtypeStruct
    out_shape = (
        sds((B, NSA_HEADS, NSA_DH, S), BF16),
        sds((B, S, 256), F32),
        sds((B, NSA_KV_GROUPS, S, LANES), BF16),
        sds((B, NSA_KV_GROUPS, NSA_DH, S), BF16),
        sds((B, NSA_KV_GROUPS, S, NSA_DH), BF16),
        sds((B, NSA_KV_GROUPS, NSA_DH, S), BF16),
        sds((B, S, GLA_HEADS * GLA_DK), F32),
        sds((B, S, GLA_HEADS * GLA_DK), F32),
        sds((B, S, GLA_HEADS * GLA_DV), BF16),
        sds((B, GLA_HEADS * GLA_DV, S), BF16),
        sds((B, S, GLA_HEADS * GLA_DV), F32),
        sds((B, S, D), BF16),
        sds((B, S, D), BF16),
        sds((B, LANES, S), F32),
        sds((B, S, GLA_HEADS * GLA_DK), F32),
    )
    out_specs = (
        head_t(NSA_HEADS, NSA_DH), tok(256),
        head(NSA_KV_GROUPS, LANES), head_t(NSA_KV_GROUPS, NSA_DH),
        head(NSA_KV_GROUPS, NSA_DH), head_t(NSA_KV_GROUPS, NSA_DH),
        tok(GLA_HEADS * GLA_DK), tok(GLA_HEADS * GLA_DK),
        tok(GLA_HEADS * GLA_DV),
        pl.BlockSpec((1, GLA_HEADS * GLA_DV, tm), lambda b, i: (b, 0, i)),
        tok(GLA_HEADS * GLA_DV), tok(D), tok(D),
        pl.BlockSpec((1, LANES, tm), lambda b, i: (b, 0, i)), tok(GLA_HEADS * GLA_DK),
    )
    return pl.pallas_call(
        _in_proj_body,
        grid=grid,
        in_specs=[
            tok(D),
            pl.BlockSpec((1, D), lambda b, i: (0, 0)),
            pl.BlockSpec((D, IN_NW), lambda b, i: (0, 0), pipeline_mode=pl.Buffered(1)),
            tok(1),
            pl.BlockSpec((1, LANES), lambda b, i: (0, 0)),
            pl.BlockSpec(wa.shape, lambda b, i: (0, 0)),
            pl.BlockSpec(ba.shape, lambda b, i: (0, 0)),
            pl.BlockSpec(tri.shape, lambda b, i: (0, 0)),
        ],
        out_specs=out_specs,
        out_shape=out_shape,
        compiler_params=pltpu.CompilerParams(
            dimension_semantics=("parallel", "parallel"), vmem_limit_bytes=VMEM_LIMIT),
        name="in_proj",
    )(x, g_mix.reshape(1, D), w_all, positions.reshape(B, S, 1), inv, wa, ba, tri)


def _compress_body(x_ref, p0_ref, p1_ref, w0_ref, w1_ref, b1_ref, w2_ref, b2_ref, cs_ref, sn_ref,
                   kc_ref, vc_ref, xk_ref, xv_ref):
    n = x_ref.shape[1] // CMP_STRIDE
    half_refs = (xk_ref, xv_ref)
    for s, half in enumerate(half_refs):
        half[...] = x_ref[0, :, s * LANES:(s + 1) * LANES]
    hid = w0_ref.shape[1] // len(half_refs)
    y0 = [None, None]
    y1 = [None, None]
    for t in range(CMP_STRIDE):
        for s, half in enumerate(half_refs):
            xt = half[pl.ds(t, n, stride=CMP_STRIDE), :]
            rows = slice((2 * t + s) * LANES, (2 * t + s + 1) * LANES)
            cols = slice(s * hid, (s + 1) * hid)
            lanes = slice(s * LANES, (s + 1) * LANES)
            d0 = _dot((xt + p0_ref[t:t + 1, lanes]).astype(BF16), w0_ref[rows, cols])
            d1 = _dot((xt + p1_ref[t:t + 1, lanes]).astype(BF16), w1_ref[rows, cols])
            y0[s] = d0 if y0[s] is None else y0[s] + d0
            y1[s] = d1 if y1[s] is None else y1[s] + d1
    y0 = jnp.concatenate(y0, axis=1)
    y1 = jnp.concatenate(y1, axis=1)
    h = jax.nn.gelu(y0 + pltpu.roll(y1, n - 1, 0) + b1_ref[...])
    o = _dot(h.astype(BF16), w2_ref[...]) + b2_ref[...]
    k = _rope_lanes(o[:, :LANES], cs_ref[0], sn_ref[0])
    v_t = o[:, LANES:].T
    for g in range(NSA_KV_GROUPS):
        sl = slice(g * NSA_DH, (g + 1) * NSA_DH)
        kc_ref[0, g] = k[:, sl].astype(BF16)
        vc_ref[0, g] = v_t[sl].astype(BF16)


def _compress(kvc, cs_c, sn_c, pos_k, w1_k, b1_k, w2_k, b2_k, pos_v, w1_v, b1_v, w2_v, b2_v):
    B, S, _ = kvc.shape
    n = S // CMP_STRIDE
    ns = 2 * NSA_KV_GROUPS
    eye = jnp.eye(ns, dtype=F32)
    w1s = jnp.stack([w1_k, w1_k, w1_v, w1_v])
    poss = jnp.stack([pos_k, pos_k, pos_v, pos_v])
    wbig, pbig = [], []
    for m in range(CMP_BLOCK // CMP_STRIDE):
        wm = w1s.reshape(ns, CMP_BLOCK, NSA_DH, CMP_HIDDEN)[:, m * CMP_STRIDE:(m + 1) * CMP_STRIDE]
        wb = jnp.einsum('ctdj,ce->tcdej', wm, eye).reshape(CMP_STRIDE * ns * NSA_DH, ns * CMP_HIDDEN)
        wbig.append(wb.astype(BF16))
        pm = poss[:, m * CMP_STRIDE:(m + 1) * CMP_STRIDE]
        pbig.append(pm.transpose(1, 0, 2).reshape(CMP_STRIDE, ns * NSA_DH))
    b1 = jnp.concatenate([b1_k, b1_k, b1_v, b1_v]).reshape(1, ns * CMP_HIDDEN)
    w2s = jnp.stack([w2_k, w2_k, w2_v, w2_v])
    w2big = jnp.einsum('cjd,ce->cjed', w2s, eye).reshape(ns * CMP_HIDDEN, ns * NSA_DH).astype(BF16)
    b2 = jnp.concatenate([b2_k, b2_k, b2_v, b2_v]).reshape(1, ns * NSA_DH)
    full = lambda a: pl.BlockSpec(a.shape, lambda b: (0,) * a.ndim)
    sds = jax.ShapeDtypeStruct
    return pl.pallas_call(
        _compress_body,
        grid=(B,),
        in_specs=[pl.BlockSpec((1, S, ns * NSA_DH), lambda b: (b, 0, 0)),
                  full(pbig[0]), full(pbig[1]), full(wbig[0]), full(wbig[1]), full(b1), full(w2big), full(b2),
                  pl.BlockSpec((1, n, LANES), lambda b: (b, 0, 0)),
                  pl.BlockSpec((1, n, LANES), lambda b: (b, 0, 0))],
        out_specs=(pl.BlockSpec((1, NSA_KV_GROUPS, n, NSA_DH), lambda b: (b, 0, 0, 0)),
                   pl.BlockSpec((1, NSA_KV_GROUPS, NSA_DH, n), lambda b: (b, 0, 0, 0))),
        out_shape=(sds((B, NSA_KV_GROUPS, n, NSA_DH), BF16), sds((B, NSA_KV_GROUPS, NSA_DH, n), BF16)),
        scratch_shapes=[pltpu.VMEM((S, LANES), F32), pltpu.VMEM((S, LANES), F32)],
        compiler_params=pltpu.CompilerParams(
            dimension_semantics=("parallel",), vmem_limit_bytes=VMEM_LIMIT),
        name="compress",
    )(kvc, pbig[0], pbig[1], wbig[0], wbig[1], b1, w2big, b2, cs_c, sn_c)


NSA_ONES = 16


def _with_ones(v_t):
    return jnp.concatenate([v_t, jnp.ones((NSA_ONES, v_t.shape[1]), v_t.dtype)], axis=0)


def _softmax_step(s, m, acc, v_t):
    m_new = jnp.maximum(m, jnp.max(s, axis=0, keepdims=True))
    p = jnp.exp2(s - m_new).astype(BF16)
    acc = jnp.exp2(m - m_new) * acc + _dot(_with_ones(v_t), p)
    return m_new, acc


def _nsa_groups_body(q_ref, kc_ref, vc_ref, ks_ref, vs_ref, kw_ref, vw_ref, ng_ref, ovt_ref,
                     o_ref, sa_ref, sb_ref, sw_ref, *, tq, tk, n_slc):
    qi = pl.program_id(1)
    q0 = qi * tq
    n_cmp = kc_ref.shape[2]
    W = NSA_HPG * tq
    G = range(NSA_KV_GROUPS)
    per_head = lambda x: jnp.concatenate([x] * NSA_HPG, axis=1)
    q_all = [jnp.concatenate([q_ref[0, g * NSA_HPG + h] for h in range(NSA_HPG)], axis=1) for g in G]
    t_lane = q0 + lax.broadcasted_iota(jnp.int32, (1, tq), 1)

    cmp_end = CMP_STRIDE * lax.broadcasted_iota(jnp.int32, (n_cmp, 1), 0) + (CMP_BLOCK - 1)
    ok = cmp_end <= t_lane
    cmp_bias = per_head(jnp.where(ok, 0.0, NEG_INF))
    cmp_keep = per_head(jnp.where(ok, 1.0, 0.0))
    sm = [_dot(kc_ref[0, g], q_all[g]) + cmp_bias for g in G]
    wk = WINDOW + tq
    w0 = pl.multiple_of(jnp.maximum(q0 - WINDOW, 0), tq)
    kpos = w0 + lax.broadcasted_iota(jnp.int32, (wk, 1), 0)
    win_bias = per_head(jnp.where(jnp.logical_and(kpos <= t_lane, kpos > t_lane - WINDOW), 0.0, NEG_INF))
    for g in G:
        sw_ref[g] = _dot(kw_ref[0, g, pl.ds(w0, wk), :], q_all[g]) + win_bias
    blk = lax.broadcasted_iota(jnp.int32, (n_slc, tq), 0)
    cur = (q0 + lax.broadcasted_iota(jnp.int32, (n_slc, tq), 1)) // SLC_BLOCK
    forced = (blk == 0) | (blk == cur) | (blk == cur - 1)
    o_cmp, imp = [], []
    for g in G:
        e = jnp.exp2(sm[g] - jnp.max(sm[g], axis=0, keepdims=True))
        p = e * (1.0 / jnp.sum(e, axis=0, keepdims=True)) * cmp_keep
        o_cmp.append(_dot(vc_ref[0, g], p.astype(BF16)))
        p_sum = p[:, 0:tq]
        for h in range(1, NSA_HPG):
            p_sum = p_sum + p[:, h * tq:(h + 1) * tq]
        p_hi = p_sum.astype(BF16)
        p_lo = (p_sum - p_hi.astype(F32)).astype(BF16)
        imp_g = _dot(ovt_ref[...], p_hi) + _dot(ovt_ref[...], p_lo)
        imp.append(jnp.where(blk <= cur, jnp.where(forced, FORCED_SCORE, imp_g), NEG_INF))

    SUBL = 8

    def block_bias(n):
        out = []
        for g in G:
            groups = [imp[g][r:r + SUBL] for r in range(0, n, SUBL)]
            ranks = [jnp.zeros((SUBL, tq), F32) for _ in groups]
            for i in range(n):
                row = imp[g][i:i + 1, :]
                for gidx, grp in enumerate(groups):
                    r = gidx * SUBL
                    ge = jnp.where(row >= grp, 1.0, 0.0)
                    gt = jnp.where(row > grp, 1.0, 0.0)
                    if r > i:
                        inc = ge
                    elif r + SUBL - 1 <= i:
                        inc = gt
                    else:
                        inc = jnp.where(blk[r:r + SUBL] > i, ge, gt)
                    ranks[gidx] = ranks[gidx] + inc
            bias = jnp.where(jnp.concatenate(ranks, axis=0) < float(min(SLC_TOPK, n_slc)), 0.0, NEG_INF)
            out.append(jnp.concatenate([bias, jnp.zeros((NSA_DH - n, tq), F32)], axis=0) if n < NSA_DH else bias)
        return tuple(out)

    quarter = n_slc // 4
    visible = (q0 + tq - 1) // SLC_BLOCK + 1
    bias = lax.cond(
        visible <= 2 * quarter,
        lambda: lax.cond(visible <= quarter, lambda: block_bias(quarter), lambda: block_bias(2 * quarter)),
        lambda: lax.cond(visible <= 3 * quarter, lambda: block_bias(3 * quarter), lambda: block_bias(n_slc)))
    q_aug = [jnp.concatenate([q_all[g], per_head(bias[g]).astype(BF16)], axis=0) for g in G]

    kpos_l = lax.broadcasted_iota(jnp.int32, (tk, 1), 0)
    n_tiles = q0 // tk + 1
    last_tile = ks_ref.shape[2] // tk - 1

    def tile_start(jt):
        return pl.multiple_of(jnp.minimum(jt, last_tile) * tk, tk)

    def scores_into(s_ref, jt):
        causal = per_head(jnp.where(jt * tk + kpos_l <= t_lane, 0.0, NEG_INF))
        for g in G:
            s_ref[g] = _dot(ks_ref[0, g, pl.ds(tile_start(jt), tk), :], q_aug[g]) + causal

    def consume(s_ref, jt, carry):
        return tuple(_softmax_step(s_ref[g], *carry[g], vs_ref[0, g, :, pl.ds(tile_start(jt), tk)]) for g in G)

    def slc_pair(i, carry):
        scores_into(sb_ref, 2 * i + 1)
        carry = consume(sa_ref, 2 * i, carry)
        scores_into(sa_ref, 2 * i + 2)
        return consume(sb_ref, 2 * i + 1, carry)

    init = tuple((jnp.full((1, W), NEG_INF, F32), jnp.zeros((NSA_DH + NSA_ONES, W), F32)) for g in G)
    scores_into(sa_ref, 0)

    acc_w = []
    for g in G:
        sw = sw_ref[g]
        ew = jnp.exp2(sw - jnp.max(sw, axis=0, keepdims=True)).astype(BF16)
        acc_w.append(_dot(_with_ones(vw_ref[0, g, :, pl.ds(w0, wk)]), ew))

    slc = lax.fori_loop(0, (n_tiles + 1) // 2, slc_pair, init)

    for g in G:
        def gate(j):
            return jnp.concatenate([ng_ref[0, (g * NSA_HPG + h) * 3 + j:(g * NSA_HPG + h) * 3 + j + 1, :]
                                    for h in range(NSA_HPG)], axis=1)

        acc_s = slc[g][1]
        l_s, l_w = acc_s[NSA_DH:NSA_DH + 1], acc_w[g][NSA_DH:NSA_DH + 1]
        out_t = (gate(0) * o_cmp[g] + (gate(1) * (1.0 / l_s)) * acc_s[:NSA_DH]
                 + (gate(2) * (1.0 / l_w)) * acc_w[g][:NSA_DH])
        out_t = jnp.concatenate([out_t[:, h * tq:(h + 1) * tq] for h in range(NSA_HPG)], axis=0)
        o_ref[0, :, g * NSA_HPG * NSA_DH:(g + 1) * NSA_HPG * NSA_DH] = out_t.T.astype(o_ref.dtype)


def _nsa_attention(q, kc, vc, ks, vs, kw, vw, ng):
    B, H, dh, S = q.shape
    tq = min(ATT_TQ, S)
    tk = min(ATT_TK, S)
    assert S % tk == 0 and tk % tq == 0 and S >= WINDOW + tq and S // SLC_BLOCK <= dh
    n_cmp = kc.shape[2]
    n_slc = S // SLC_BLOCK
    c0 = CMP_STRIDE * np.arange(n_cmp)[None, :]
    s0 = SLC_BLOCK * np.arange(n_slc)[:, None]
    ov = np.clip(np.minimum(c0 + CMP_BLOCK, s0 + SLC_BLOCK) - np.maximum(c0, s0), 0, None) / CMP_BLOCK
    ovt = jnp.asarray(ov, BF16)
    G = NSA_KV_GROUPS
    grid = (B, S // tq)
    k_spec = lambda n, w: pl.BlockSpec((1, G, n, w), lambda b, i: (b, 0, 0, 0))
    vt_spec = lambda n: pl.BlockSpec((1, G, dh, n), lambda b, i: (b, 0, 0, 0))
    return pl.pallas_call(
        functools.partial(_nsa_groups_body, tq=tq, tk=tk, n_slc=n_slc),
        grid=grid,
        in_specs=[
            pl.BlockSpec((1, H, dh, tq), lambda b, i: (b, 0, 0, i)),
            k_spec(n_cmp, dh), vt_spec(n_cmp),
            k_spec(S, 2 * dh), vt_spec(S),
            k_spec(S, dh), vt_spec(S),
            pl.BlockSpec((1, LANES, tq), lambda b, i: (b, 0, i)),
            pl.BlockSpec(ovt.shape, lambda b, i: (0, 0)),
        ],
        out_specs=pl.BlockSpec((1, tq, H * dh), lambda b, i: (b, i, 0)),
        out_shape=jax.ShapeDtypeStruct((B, S, H * dh), BF16),
        scratch_shapes=[pltpu.VMEM((G, tk, NSA_HPG * tq), F32),
                        pltpu.VMEM((G, tk, NSA_HPG * tq), F32),
                        pltpu.VMEM((G, WINDOW + tq, NSA_HPG * tq), F32)],
        compiler_params=pltpu.CompilerParams(
            dimension_semantics=("parallel", "arbitrary"), vmem_limit_bytes=VMEM_LIMIT),
        name="nsa_attn",
    )(q, kc, vc, ks, vs, kw, vw, ng, ovt)


def _dot_3pass(a, b):
    a1 = a.astype(BF16)
    a2 = (a - a1.astype(F32)).astype(BF16)
    b1 = b.astype(BF16)
    b2 = (b - b1.astype(F32)).astype(BF16)
    return _dot(a1, b1) + (_dot(a1, b2) + _dot(a2, b1))


def _gla_pair_body(q_ref, k_ref, v_ref, vt_ref, b_ref, r_ref, ng_ref,
                   o_ref, state_ref, kp_ref, bp_ref, vp_ref, plain_ref, *, n_chunks):
    C, SUB, dk, dv, hp = GLA_C, GLA_SUB, GLA_DK, GLA_DV, GLA_HP
    W = hp * dk
    lane = lax.broadcasted_iota(jnp.int32, (1, W), 1)
    of_head = [lane // dk == hh for hh in range(hp)]
    t_loc = lax.broadcasted_iota(jnp.int32, (C, 1), 0)
    state_ref[...] = jnp.zeros_like(state_ref)
    kp_ref[0:SUB, :] = jnp.zeros((SUB, W), F32)
    bp_ref[0:SUB, :] = jnp.zeros((SUB, W), F32)
    vp_ref[:, 0:SUB, :] = jnp.zeros((hp, SUB, dv), F32)

    def decay_chunk(c, all_plain):
        b_last = b_ref[0, pl.ds(c * C + (C - 1), 1), :]
        plain = (jnp.min(b_last) > -GLA_PLAIN_DECAY).astype(jnp.int32)
        plain_ref[c] = plain
        return jnp.minimum(all_plain, plain)

    all_plain = lax.fori_loop(0, n_chunks, decay_chunk, jnp.int32(1), unroll=GLA_UNROLL_DECAY)

    def chunk(c, check_decay):
        c0 = pl.multiple_of(c * C, C)
        b = b_ref[0, pl.ds(c0, C), :]
        q = q_ref[0, pl.ds(c0, C), :]
        k = k_ref[0, pl.ds(c0, C), :]
        v = [v_ref[0, pl.ds(c0, C), hh * dv:(hh + 1) * dv] for hh in range(hp)]
        vt = [vt_ref[0, hh * dv:(hh + 1) * dv, pl.ds(c0, C)] for hh in range(hp)]
        b_last = b[C - 1:C, :]
        st = state_ref[...]
        st_b = st.astype(BF16)
        qg = q * jnp.exp(b)
        qg_h = [jnp.where(of_head[hh], qg, 0.0).astype(BF16) for hh in range(hp)]
        o_inter = [_dot_nt(qg_h[hh], st_b) for hh in range(hp)]

        def intra_plain():
            ke = (k * jnp.exp(-b)).astype(BF16)
            row = lax.broadcasted_iota(jnp.int32, (C, C), 0)
            col = lax.broadcasted_iota(jnp.int32, (C, C), 1)
            return tuple(_dot(jnp.where(row >= col, _dot_nt(qg_h[hh], ke), 0.0).astype(BF16), v[hh])
                         for hh in range(hp))

        def intra_strong_decay():
            far = [[jnp.zeros((SUB, dv), F32)] for _ in range(hp)]
            for i in range(1, C // SUB):
                r0 = i * SUB
                b_first = b[r0:r0 + 1, :]
                qt = q[r0:r0 + SUB] * jnp.exp(b[r0:r0 + SUB] - b_first)
                kt = (k[:r0] * jnp.exp(b_first - b[:r0])).astype(BF16)
                lt = lax.broadcasted_iota(jnp.int32, (SUB, r0), 0)
                ls = lax.broadcasted_iota(jnp.int32, (SUB, r0), 1)
                for hh in range(hp):
                    a = _dot_nt(jnp.where(of_head[hh], qt, 0.0).astype(BF16), kt)
                    a = jnp.where(lt + (r0 - SUB) >= ls, a, 0.0)
                    far[hh].append(_dot(a.astype(BF16), v[hh][:r0]))
            acc = [jnp.concatenate(far[hh], axis=0) for hh in range(hp)]
            kp_ref[SUB:SUB + C, :] = k
            bp_ref[SUB:SUB + C, :] = b
            for hh in range(hp):
                vp_ref[hh, SUB:SUB + C, :] = v[hh].astype(F32)
            for d in range(SUB):
                kd = kp_ref[SUB - d:SUB - d + C, :]
                bd = bp_ref[SUB - d:SUB - d + C, :]
                valid = t_loc >= d
                w = jnp.exp(jnp.where(valid, b - bd, 0.0))
                x = jnp.where(valid, q * kd * w, 0.0)
                for hh in range(hp):
                    a_d = jnp.sum(jnp.where(of_head[hh], x, 0.0), axis=-1, keepdims=True)
                    acc[hh] = acc[hh] + a_d * vp_ref[hh, SUB - d:SUB - d + C, :]
            return tuple(acc)

        if check_decay:
            o_intra = lax.cond(plain_ref[c] > 0, intra_plain, intra_strong_decay)
        else:
            o_intra = intra_plain()
        k_dec = (k * jnp.exp(b_last - b)).astype(BF16)
        upd = _dot(vt[hp - 1], k_dec)
        for hh in range(hp - 2, -1, -1):
            upd = jnp.where(of_head[hh], _dot(vt[hh], k_dec), upd)
        state_ref[...] = st * jnp.exp(b_last) + upd
        for hh in range(hp):
            o = o_inter[hh] + o_intra[hh]
            o = o * lax.rsqrt(jnp.mean(o * o, axis=-1, keepdims=True) + EPS) * ng_ref[hh]
            gate = jax.nn.silu(r_ref[0, pl.ds(c0, C), hh * dv:(hh + 1) * dv])
            o_ref[0, pl.ds(c0, C), hh * dv:(hh + 1) * dv] = (o * gate).astype(o_ref.dtype)
        return 0

    @pl.when(all_plain > 0)
    def _():
        lax.fori_loop(0, n_chunks, lambda c, _: chunk(c, False), 0, unroll=GLA_UNROLL_PLAIN)

    @pl.when(all_plain <= 0)
    def _():
        lax.fori_loop(0, n_chunks, lambda c, _: chunk(c, True), 0)


def _gla_pairs(gq, gk, gv, gvt, gb, gr, norm_g):
    B, S, _ = gq.shape
    H, dk, dv, hp = GLA_HEADS, GLA_DK, GLA_DV, GLA_HP
    W = hp * dk
    assert W == LANES and H % hp == 0
    ng = norm_g.reshape(H, 1, dv)
    n_chunks = S // GLA_C
    tok_spec = lambda w: pl.BlockSpec((1, S, hp * w), lambda b, h: (b, 0, h))
    return pl.pallas_call(
        functools.partial(_gla_pair_body, n_chunks=n_chunks),
        grid=(B, H // hp),
        in_specs=[tok_spec(dk), tok_spec(dk), tok_spec(dv),
                  pl.BlockSpec((1, hp * dv, S), lambda b, h: (b, h, 0)),
                  tok_spec(dk),
                  tok_spec(dv),
                  pl.BlockSpec((hp, 1, dv), lambda b, h: (h, 0, 0))],
        out_specs=tok_spec(dv),
        out_shape=jax.ShapeDtypeStruct((B, S, H * dv), BF16),
        scratch_shapes=[pltpu.VMEM((dv, W), F32),
                        pltpu.VMEM((GLA_SUB + GLA_C, W), F32),
                        pltpu.VMEM((GLA_SUB + GLA_C, W), F32),
                        pltpu.VMEM((hp, GLA_SUB + GLA_C, dv), F32),
                        pltpu.SMEM((n_chunks,), jnp.int32)],
        compiler_params=pltpu.CompilerParams(
            dimension_semantics=("parallel", "parallel"), vmem_limit_bytes=VMEM_LIMIT),
        name="gla",
    )(gq, gk, gv, gvt, gb, gr, ng)


def _merge_body(x_ref, ya_ref, yb_ref, ma_ref, mb_ref, wpa_ref, wpb_ref, wo_ref, gf_ref, wr_ref, br_ref,
                h_ref, v_ref, comb_ref):
    y_a = _dot(ya_ref[...], wpa_ref[...])
    y_b = _dot(yb_ref[...], wpb_ref[...])
    mixed = ma_ref[...] * y_a + mb_ref[...] * y_b
    h = x_ref[...] + _dot(mixed.astype(BF16), wo_ref[...])
    h_ref[...] = h
    v = h * lax.rsqrt(jnp.mean(h * h, axis=-1, keepdims=True) + EPS) * gf_ref[...]
    v_ref[...] = v.astype(BF16)
    wr = wr_ref[...]
    w_hi = wr.astype(BF16)
    w_both = jnp.concatenate([w_hi, (wr - w_hi.astype(F32)).astype(BF16)], axis=1)
    v_hi = v.astype(BF16)
    v_lo = (v - v_hi.astype(F32)).astype(BF16)
    both = _dot(v_hi, w_both)
    logits = (both[:, :LANES] + both[:, LANES:]) + _dot(v_lo, w_hi) + br_ref[...]
    comb_ref[...] = _route_weights(logits)


def _route_weights(logits):
    lane = lax.broadcasted_iota(jnp.int32, logits.shape, 1)
    is_grp = jnp.logical_and(lane >= MOE_EXPERTS, lane < MOE_EXPERTS + MOE_GROUPS)
    lg = jnp.where(is_grp, logits, NEG_INF)
    eg = jnp.where(is_grp, jnp.exp(lg - jnp.max(lg, axis=-1, keepdims=True)), 0.0)
    pg = eg / jnp.sum(eg, axis=-1, keepdims=True)
    p_grp = jnp.max(pg, axis=-1, keepdims=True)
    g_sel = jnp.min(jnp.where(jnp.logical_and(is_grp, pg == p_grp), lane, 2 * LANES),
                    axis=-1, keepdims=True) - MOE_EXPERTS
    in_grp = jnp.logical_and(lane < MOE_EXPERTS, lane // MOE_EPG == g_sel)
    le = jnp.where(in_grp, logits, NEG_INF)
    ee = jnp.where(in_grp, jnp.exp(le - jnp.max(le, axis=-1, keepdims=True)), 0.0)
    pin = ee / jnp.sum(ee, axis=-1, keepdims=True)
    p1 = jnp.max(jnp.where(in_grp, pin, -1.0), axis=-1, keepdims=True)
    i1 = jnp.min(jnp.where(jnp.logical_and(in_grp, pin == p1), lane, 2 * LANES), axis=-1, keepdims=True)
    rest = jnp.logical_and(in_grp, lane != i1)
    p2 = jnp.max(jnp.where(rest, pin, -1.0), axis=-1, keepdims=True)
    i2 = jnp.min(jnp.where(jnp.logical_and(rest, pin == p2), lane, 2 * LANES), axis=-1, keepdims=True)
    tot = p1 + p2
    return jnp.where(lane == i1, p_grp * p1 / tot, 0.0) + jnp.where(lane == i2, p_grp * p2 / tot, 0.0)


def _merge(x2, ya, yb, ma, mb, w_proj_nsa, w_proj_gla, w_out, g_ffn, w_grp, b_grp, w_exp, b_exp):
    T, D = x2.shape
    tm = min(MERGE_TM, T)
    wr = jnp.pad(jnp.concatenate([w_exp, w_grp], axis=1), ((0, 0), (0, LANES - MOE_EXPERTS - MOE_GROUPS)))
    br = jnp.pad(jnp.concatenate([b_exp, b_grp]), (0, LANES - MOE_EXPERTS - MOE_GROUPS)).reshape(1, LANES)
    tok = lambda w: pl.BlockSpec((tm, w), lambda i: (i, 0))
    full = lambda a: pl.BlockSpec(a.shape, lambda i: (0, 0))
    wpa, wpb, wo = w_proj_nsa.astype(BF16), w_proj_gla.astype(BF16), w_out.astype(BF16)
    gf = g_ffn.reshape(1, D)
    sds = jax.ShapeDtypeStruct
    return pl.pallas_call(
        _merge_body,
        grid=(T // tm,),
        in_specs=[tok(D), tok(ya.shape[1]), tok(yb.shape[1]), tok(D), tok(D),
                  full(wpa), full(wpb), full(wo), full(gf), full(wr), full(br)],
        out_specs=(tok(D), tok(D), tok(LANES)),
        out_shape=(sds((T, D), F32), sds((T, D), BF16), sds((T, LANES), F32)),
        compiler_params=pltpu.CompilerParams(
            dimension_semantics=("parallel",), vmem_limit_bytes=VMEM_LIMIT),
        name="merge",
    )(x2, ya, yb, ma, mb, wpa, wpb, wo, gf, wr, br)


def _moe_rows(tw):
    rows = 2 * tw + MOE_EXPERTS * (MOE_ALIGN - 1) + MOE_RT
    return -(-rows // MOE_RT) * MOE_RT


def _moe_window(e, phase, v_ref, comb_ref, tri_ref, y_ref, xs_ref, z_ref, meta_ref, pos_ref, tw, rmax):
    no_row = -1.0

    @pl.when(e == 0 if phase == "route" else False)
    def _route():
        comb = comb_ref[...]
        assigned = comb > 0.0
        a = jnp.where(assigned, 1.0, 0.0)
        tri = tri_ref[...]
        run = jnp.zeros((1, LANES), F32)
        ranks = []
        for b in range(tw // MOE_RT):
            ab = a[b * MOE_RT:(b + 1) * MOE_RT]
            ranks.append(_dot(tri, ab.astype(BF16)) + run)
            run = run + jnp.sum(ab, axis=0, keepdims=True)
        rank = jnp.concatenate(ranks, axis=0)
        cnt_pad = jnp.floor((run + (MOE_ALIGN - 1)) * (1.0 / MOE_ALIGN)) * MOE_ALIGN
        incl = jnp.broadcast_to(cnt_pad, (8, LANES))
        lane8 = lax.broadcasted_iota(jnp.int32, (8, LANES), 1)
        shift = 1
        while shift < MOE_EXPERTS:
            incl = incl + jnp.where(lane8 >= shift, pltpu.roll(incl, shift, 1), 0.0)
            shift *= 2
        offs = incl[0:1] - cnt_pad
        meta_ref[0:1, :] = offs * MOE_PACK + run
        row_of = offs + rank
        pos_a = jnp.min(jnp.where(assigned, row_of, 1e9), axis=-1, keepdims=True)
        pos_b = jnp.max(jnp.where(assigned, row_of, no_row), axis=-1, keepdims=True)
        pos_a = jnp.where(pos_a > 1e8, no_row, pos_a)
        pos_b = jnp.where(pos_b == pos_a, no_row, pos_b)
        w_a = jnp.sum(jnp.where(jnp.logical_and(assigned, row_of == pos_a), comb, 0.0), axis=-1, keepdims=True)
        w_b = jnp.sum(jnp.where(jnp.logical_and(assigned, row_of == pos_b), comb, 0.0), axis=-1, keepdims=True)
        lane_t = lax.broadcasted_iota(jnp.int32, (tw, LANES), 1)
        pos_ref[...] = jnp.where(lane_t == 0, pos_a, jnp.where(lane_t == 1, pos_b, jnp.where(
            lane_t == 2, w_a, jnp.where(lane_t == 3, w_b, no_row))))
        pos_t = pos_ref[...].T
        pa, pb = pos_t[0:1], pos_t[1:2]
        v = v_ref[...]
        for rt in range(rmax // MOE_RT):
            r = (rt * MOE_RT + lax.broadcasted_iota(jnp.int32, (MOE_RT, 1), 0)).astype(F32)
            p = jnp.where(r == pa, 1.0, jnp.where(r == pb, 1.0, 0.0)).astype(BF16)
            xs_ref[rt * MOE_RT:(rt + 1) * MOE_RT, :] = _dot(p, v).astype(BF16)
        z_ref[...] = jnp.zeros_like(z_ref)

    @pl.when(e == pl.num_programs(1) - 1 if phase == "combine" else False)
    def _combine():
        r = lax.broadcasted_iota(jnp.int32, (1, rmax), 1).astype(F32)
        z = z_ref[...]
        for tt in range(tw // MOE_RT):
            rows = slice(tt * MOE_RT, (tt + 1) * MOE_RT)
            pa, pb = pos_ref[rows, 0:1], pos_ref[rows, 1:2]
            w_a, w_b = pos_ref[rows, 2:3], pos_ref[rows, 3:4]
            q = jnp.where(r == pa, w_a, jnp.where(r == pb, w_b, 0.0)).astype(BF16)
            y_ref[rows, :] = _dot(q, z).astype(y_ref.dtype)


def _moe_body(v_ref, comb_ref, wg_ref, wu_ref, wd_ref, tri_ref, y_ref,
              xs_ref, z_ref, meta_ref, pos_ref, *, tw, rmax, n_win):
    e = pl.program_id(1)
    lane = lax.broadcasted_iota(jnp.int32, (1, LANES), 1)

    def window_phase(phase):
        for w in range(n_win):
            rows = pl.ds(w * tw, tw)
            _moe_window(e, phase, v_ref.at[rows], comb_ref.at[rows], tri_ref, y_ref.at[rows],
                        xs_ref.at[w], z_ref.at[w], meta_ref.at[w], pos_ref.at[w], tw, rmax)

    window_phase("route")

    packed = [jnp.sum(jnp.where(lane == e, meta_ref[w, 0:1, :], 0.0)).astype(jnp.int32) for w in range(n_win)]
    offs = [p // MOE_PACK for p in packed]
    tiles = [(p % MOE_PACK + MOE_ET - 1) // MOE_ET for p in packed]

    def expert(x):
        hdn = jax.nn.silu(_dot(x, wg_ref[0])) * _dot(x, wu_ref[0])
        return _dot(hdn.astype(BF16), wd_ref[0]).astype(BF16)

    def joint_tile(i, _):
        r0 = [pl.multiple_of(offs[w] + i * MOE_ET, MOE_ALIGN) for w in range(n_win)]
        z = expert(jnp.concatenate([xs_ref[w, pl.ds(r0[w], MOE_ET), :] for w in range(n_win)], axis=0))
        for w in range(n_win):
            z_ref[w, pl.ds(r0[w], MOE_ET), :] = z[w * MOE_ET:(w + 1) * MOE_ET]
        return 0

    n_joint = functools.reduce(jnp.minimum, tiles)
    lax.fori_loop(0, n_joint, joint_tile, 0)
    for w in range(n_win):
        def own_tile(i, _, w=w):
            r0 = pl.multiple_of(offs[w] + i * MOE_ET, MOE_ALIGN)
            z_ref[w, pl.ds(r0, MOE_ET), :] = expert(xs_ref[w, pl.ds(r0, MOE_ET), :])
            return 0

        lax.fori_loop(n_joint, tiles[w], own_tile, 0)

    window_phase("combine")


def _moe(v, comb, w_gate, w_up, w_down):
    T, D = v.shape
    tw = min(MOE_TW, T)
    n_win = min(MOE_WINDOWS, T // tw)
    rmax = _moe_rows(tw)
    E, _, F = w_gate.shape
    wg, wu, wd = w_gate.astype(BF16), w_up.astype(BF16), w_down.astype(BF16)
    tri = jnp.asarray(np.tril(np.ones((MOE_RT, MOE_RT)), -1), BF16)
    tok = lambda w: pl.BlockSpec((n_win * tw, w), lambda i, e: (i, 0))
    tok_in = lambda w: pl.BlockSpec((n_win * tw, w), lambda i, e: (i, 0), pipeline_mode=pl.Buffered(1))
    return pl.pallas_call(
        functools.partial(_moe_body, tw=tw, rmax=rmax, n_win=n_win),
        grid=(T // (n_win * tw), E),
        in_specs=[tok_in(D), tok_in(LANES),
                  pl.BlockSpec((1, D, F), lambda i, e: (e, 0, 0)),
                  pl.BlockSpec((1, D, F), lambda i, e: (e, 0, 0)),
                  pl.BlockSpec((1, F, D), lambda i, e: (e, 0, 0)),
                  pl.BlockSpec(tri.shape, lambda i, e: (0, 0))],
        out_specs=tok(D),
        out_shape=jax.ShapeDtypeStruct((T, D), BF16),
        scratch_shapes=[pltpu.VMEM((n_win, rmax, D), BF16),
                        pltpu.VMEM((n_win, rmax, D), BF16),
                        pltpu.VMEM((n_win, 8, LANES), F32),
                        pltpu.VMEM((n_win, tw, LANES), F32)],
        compiler_params=pltpu.CompilerParams(
            dimension_semantics=("parallel", "arbitrary"), vmem_limit_bytes=VMEM_LIMIT),
        name="moe",
    )(v, comb, wg, wu, wd, tri)


def _final_body(h_ref, y_ref, g_ref, o_ref):
    h = h_ref[...] + y_ref[...].astype(F32)
    o_ref[...] = h * lax.rsqrt(jnp.mean(h * h, axis=-1, keepdims=True) + EPS) * g_ref[...]


def _final_norm(h, y, g_final):
    T, D = h.shape
    tm = min(FINAL_TM, T)
    tok = pl.BlockSpec((tm, D), lambda i: (i, 0))
    return pl.pallas_call(
        _final_body,
        grid=(T // tm,),
        in_specs=[tok, tok, pl.BlockSpec((1, D), lambda i: (0, 0))],
        out_specs=tok,
        out_shape=jax.ShapeDtypeStruct((T, D), F32),
        compiler_params=pltpu.CompilerParams(
            dimension_semantics=("parallel",), vmem_limit_bytes=VMEM_LIMIT),
        name="final_norm",
    )(h, y, g_final.reshape(1, D))


def _rope_inv_freq():
    half = NSA_DH // 2
    return 1.0 / (ROPE_THETA ** (jnp.arange(half, dtype=F32) / half))


def _rope_tables(positions):
    ang = positions.astype(F32)[..., None] * _rope_inv_freq()
    cos, sin = jnp.cos(ang), jnp.sin(ang)
    cs = jnp.concatenate([cos, cos, cos, cos], axis=-1)
    sn = jnp.concatenate([-sin, sin, -sin, sin], axis=-1)
    return cs, sn


def _layer(h, positions, g_mix, w_in, cmp_pos_k, cmp_w1_k, cmp_b1_k, cmp_w2_k, cmp_b2_k,
           cmp_pos_v, cmp_w1_v, cmp_b1_v, cmp_w2_v, cmp_b2_v, gla_w_a2, gla_b_a, gla_norm_g,
           w_proj_nsa, w_proj_gla, w_out, g_ffn, w_grp, b_grp, w_exp, b_exp, w_gate, w_up, w_down, g_out):
    B, S, D = h.shape
    n_chunks = S // CMP_STRIDE
    cmp_end = jnp.minimum(CMP_STRIDE * jnp.arange(n_chunks) + CMP_BLOCK - 1, S - 1)
    cs_c, sn_c = _rope_tables(jnp.take(positions, cmp_end, axis=1))
    (q, kvc, ks, vs, kw, vw, gq, gk, gv, gvt, gr, ma, mb, ng, gb) = _in_proj(h, g_mix, w_in, positions,
                                                                             gla_w_a2, gla_b_a)
    kc, vc = _compress(kvc, cs_c, sn_c, cmp_pos_k, cmp_w1_k, cmp_b1_k, cmp_w2_k, cmp_b2_k,
                       cmp_pos_v, cmp_w1_v, cmp_b1_v, cmp_w2_v, cmp_b2_v)
    ya = _nsa_attention(q, kc, vc, ks, vs, kw, vw, ng)
    yb = _gla_pairs(gq, gk, gv, gvt, gb, gr, gla_norm_g)
    T = B * S
    h1, v, comb = _merge(h.reshape(T, D), ya.reshape(T, -1), yb.reshape(T, -1), ma.reshape(T, D), mb.reshape(T, D),
                         w_proj_nsa, w_proj_gla, w_out, g_ffn, w_grp, b_grp, w_exp, b_exp)
    y = _moe(v, comb, w_gate, w_up, w_down)
    return _final_norm(h1, y, g_out).reshape(B, S, D)


def kernel(x, positions, g_mix, w_in, cmp_pos_k, cmp_w1_k, cmp_b1_k, cmp_w2_k, cmp_b2_k, cmp_pos_v, cmp_w1_v,
           cmp_b1_v, cmp_w2_v, cmp_b2_v, gla_w_a2, gla_b_a, gla_norm_g, w_proj_nsa, w_proj_gla, w_out, g_ffn,
           w_grp, b_grp, w_exp, b_exp, w_gate, w_up, w_down, g_final):
    depth = g_mix.shape[0]
    assert depth == 1, "the final norm closes the single layer"
    return _layer(x, positions, g_mix[0], w_in[0], cmp_pos_k[0], cmp_w1_k[0], cmp_b1_k[0], cmp_w2_k[0],
                  cmp_b2_k[0], cmp_pos_v[0], cmp_w1_v[0], cmp_b1_v[0], cmp_w2_v[0], cmp_b2_v[0],
                  gla_w_a2[0], gla_b_a[0], gla_norm_g[0], w_proj_nsa[0], w_proj_gla[0], w_out[0],
                  g_ffn[0], w_grp[0], b_grp[0], w_exp[0], b_exp[0], w_gate[0], w_up[0], w_down[0], g_final)
```
